```python
import math
import jax, jax.numpy as jnp
from jax import lax
import numpy as np

D_MODEL = 4096
BATCH = 4
SEQ = 2048
DEPTH = 1
DEC_BATCH = 32
DEC_SEQ = 1
PAST_LEN = 8192
PAGE_SIZE = 128

ATT_HEADS = 16
ATT_KV_HEADS = 8
ATT_HEAD_DIM = 128
ROPE_THETA = 500000.0
ROPE_FRAC = 4
IDX_HEADS = 32
IDX_DIM = 64
TOPK_MAX = 256
Q_BLOCK = 128
RET_HEADS = 8
RET_DK = 256
RET_DV = 512
RET_THETA = 10000.0
RET_CHUNK = 128
D_FF = 4 * D_MODEL
NORM_EPS = 1e-6

ATT_Q_W = ATT_HEADS * ATT_HEAD_DIM
ATT_KV_W = ATT_KV_HEADS * ATT_HEAD_DIM
IDX_Q_W = IDX_HEADS * IDX_DIM
RET_QK_W = RET_HEADS * RET_DK
RET_V_W = RET_HEADS * RET_DV
SPLITS = (ATT_Q_W, ATT_KV_W, ATT_KV_W, IDX_Q_W, IDX_DIM, IDX_HEADS,
          RET_QK_W, RET_QK_W, RET_V_W, RET_V_W, D_MODEL, D_MODEL)
IN_W = sum(SPLITS)

kernel_name = 'dsa_retention_gated_hybrid_step'


def _rms_norm(x, w):
    xf = x.astype(jnp.float32)
    y = xf * lax.rsqrt(jnp.mean(xf * xf, axis=-1, keepdims=True) + NORM_EPS)
    return (y * w.astype(jnp.float32)).astype(x.dtype)


def _rope(x, pos, rot_dim, theta):
    half = rot_dim // 2
    inv_freq = jnp.exp(-math.log(theta) * jnp.arange(half, dtype=jnp.float32) / half)
    ang = pos.astype(jnp.float32)[:, None] * inv_freq[None, :]
    ang = ang.reshape((ang.shape[0],) + (1,) * (x.ndim - 3) + (half,))
    cos = jnp.cos(ang).astype(x.dtype)
    sin = jnp.sin(ang).astype(x.dtype)
    x1 = x[..., :half]
    x2 = x[..., half:rot_dim]
    return jnp.concatenate([x1 * cos - x2 * sin, x2 * cos + x1 * sin, x[..., rot_dim:]], axis=-1)


def _project(xn, pos, w_in):
    B, T = xn.shape[:2]
    proj = xn @ w_in
    offsets = np.cumsum(SPLITS)[:-1].tolist()
    q, k, v, iq, ik, iw, rq, rk, rv, rg, ga, gr = jnp.split(proj, offsets, axis=-1)
    att_rot = ATT_HEAD_DIM // ROPE_FRAC
    idx_rot = IDX_DIM // ROPE_FRAC
    q = _rope(q.reshape(B, T, ATT_HEADS, ATT_HEAD_DIM), pos, att_rot, ROPE_THETA)
    k = _rope(k.reshape(B, T, ATT_KV_HEADS, ATT_HEAD_DIM), pos, att_rot, ROPE_THETA)
    v = v.reshape(B, T, ATT_KV_HEADS, ATT_HEAD_DIM)
    iq = _rope(iq.reshape(B, T, IDX_HEADS, IDX_DIM), pos, idx_rot, ROPE_THETA)
    ik = _rope(ik, pos, idx_rot, ROPE_THETA)
    iw = iw * (IDX_HEADS ** -0.5)
    rq = _rope(rq.reshape(B, T, RET_HEADS, RET_DK), pos, RET_DK, RET_THETA)
    rk = _rope(rk.reshape(B, T, RET_HEADS, RET_DK), pos, RET_DK, RET_THETA) * (RET_DK ** -0.5)
    rv = rv.reshape(B, T, RET_HEADS, RET_DV)
    return q, k, v, iq, ik, iw, rq, rk, rv, rg, ga, gr


def _indexer_scores(iq, iw, ik):
    dots = jnp.einsum('bthd,bsd->bths', iq, ik).astype(jnp.float32) * (IDX_DIM ** -0.5)
    return jnp.einsum('bth,bths->bts', iw.astype(jnp.float32), jax.nn.relu(dots))


def _select_keys(scores, q_pos, topk):
    s_pos = jnp.arange(scores.shape[-1], dtype=jnp.int32)
    visible = s_pos[None, None, :] <= q_pos[None, :, None]
    _, idx = lax.top_k(jnp.where(visible, scores, -jnp.inf), topk)
    valid = idx <= q_pos[None, :, None]
    return idx, valid


def _sparse_attend(q, k_sel, v_sel, valid):
    B, Tq = q.shape[:2]
    qg = q.reshape(B, Tq, ATT_KV_HEADS, ATT_HEADS // ATT_KV_HEADS, ATT_HEAD_DIM)
    logits = jnp.einsum('btkgd,btskd->btkgs', qg, k_sel).astype(jnp.float32) * (ATT_HEAD_DIM ** -0.5)
    logits = jnp.where(valid[:, :, None, None, :], logits, -jnp.inf)
    p = jax.nn.softmax(logits, axis=-1).astype(v_sel.dtype)
    o = jnp.einsum('btkgs,btskd->btkgd', p, v_sel)
    return o.reshape(B, Tq, ATT_Q_W)


def _gather_rows(a, idx):
    return jax.vmap(lambda ab, ib: ab[ib])(a, idx)


def _prompt_attention(q, k, v, iq, ik, iw):
    B, T = q.shape[:2]
    nb = T // Q_BLOCK
    topk = min(TOPK_MAX, T // 4)

    def to_blocks(a):
        return jnp.moveaxis(a.reshape((B, nb, Q_BLOCK) + a.shape[2:]), 1, 0)

    def block(args):
        qb, iqb, iwb, pos_b = args
        idx, valid = _select_keys(_indexer_scores(iqb, iwb, ik), pos_b, topk)
        return _sparse_attend(qb, _gather_rows(k, idx), _gather_rows(v, idx), valid)

    pos_blocks = jnp.arange(T, dtype=jnp.int32).reshape(nb, Q_BLOCK)
    out = lax.map(block, (to_blocks(q), to_blocks(iq), to_blocks(iw), pos_blocks))
    return jnp.moveaxis(out, 0, 1).reshape(B, T, ATT_Q_W)


def _sample_attention(q, k_new, v_new, iq, ik_new, iw, cache_k_l, cache_v_l, cache_idx_k_l, page_table):
    Bd, Ts = q.shape[:2]
    n_pages = PAST_LEN // PAGE_SIZE
    L = PAST_LEN + Ts
    topk = min(TOPK_MAX, L // 4)
    ik_past = cache_idx_k_l[page_table].reshape(Bd, n_pages * PAGE_SIZE, IDX_DIM)
    ik_all = jnp.concatenate([ik_past, ik_new.astype(ik_past.dtype)], axis=1)
    q_pos = PAST_LEN + jnp.arange(Ts, dtype=jnp.int32)
    idx, valid = _select_keys(_indexer_scores(iq, iw, ik_all), q_pos, topk)
    from_past = (idx < PAST_LEN)[..., None, None]
    p_past = jnp.minimum(idx, PAST_LEN - 1)
    phys = jax.vmap(lambda pt, ib: pt[ib])(page_table, p_past // PAGE_SIZE)
    off = p_past % PAGE_SIZE
    p_new = jnp.clip(idx - PAST_LEN, 0, Ts - 1)
    k_sel = jnp.where(from_past, cache_k_l[phys, off].astype(k_new.dtype), _gather_rows(k_new, p_new))
    v_sel = jnp.where(from_past, cache_v_l[phys, off].astype(v_new.dtype), _gather_rows(v_new, p_new))
    return _sparse_attend(q, k_sel, v_sel, valid)


def _retention_chunk(S, q, k, v, log_gamma):
    C = q.shape[1]
    i = jnp.arange(C, dtype=jnp.float32)
    diff = i[:, None] - i[None, :]
    decay = jnp.where(diff >= 0, jnp.exp(log_gamma[:, None, None] * jnp.maximum(diff, 0.0)), 0.0).astype(q.dtype)
    q_dec = jnp.exp(log_gamma[None, :] * (i[:, None] + 1.0)).astype(q.dtype)
    k_dec = jnp.exp(log_gamma[None, :] * (C - 1.0 - i[:, None])).astype(k.dtype)
    chunk_dec = jnp.exp(log_gamma * C).astype(S.dtype)
    scores = jnp.einsum('bihd,bjhd->bhij', q, k) * decay[None]
    o = (jnp.einsum('bhij,bjhv->bihv', scores, v)
         + jnp.einsum('bihd,bhdv->bihv', q * q_dec[None, :, :, None], S.astype(q.dtype)))
    S_new = (chunk_dec[None, :, None, None] * S
             + jnp.einsum('bjhd,bjhv->bhdv', k * k_dec[None, :, :, None], v).astype(S.dtype))
    return o, S_new


def _retention_prompt(q, k, v, log_gamma):
    B, T, H, DK = q.shape
    DV = v.shape[-1]
    nc = T // RET_CHUNK

    def to_blocks(a):
        return jnp.moveaxis(a.reshape((B, nc, RET_CHUNK) + a.shape[2:]), 1, 0)

    def step(S, args):
        o, S_next = _retention_chunk(S, args[0], args[1], args[2], log_gamma)
        return S_next, o

    S0 = jnp.zeros((B, H, DK, DV), v.dtype)
    S_final, o = lax.scan(step, S0, (to_blocks(q), to_blocks(k), to_blocks(v)))
    return jnp.moveaxis(o, 0, 1).reshape(B, T, H, DV), S_final


def _retention_output(o, rg, gn_w):
    B, T = o.shape[:2]
    of = o.astype(jnp.float32)
    of = of * lax.rsqrt(jnp.mean(of * of, axis=-1, keepdims=True) + NORM_EPS)
    of = of * gn_w.astype(jnp.float32).reshape(RET_HEADS, RET_DV)
    return of.astype(o.dtype).reshape(B, T, RET_V_W) * jax.nn.silu(rg)


def _merge_and_finish(x, att, ret, ga, gr, w_att_proj, w_ret_proj, w_out, n_attn_post,
                      n_mlp_pre, w_mlp_up, w_mlp_down, n_mlp_post):
    m = jax.nn.sigmoid(ga) * (att @ w_att_proj) + jax.nn.sigmoid(gr) * (ret @ w_ret_proj)
    h = x + _rms_norm(m @ w_out, n_attn_post)
    u = jax.nn.relu(_rms_norm(h, n_mlp_pre) @ w_mlp_up)
    return h + _rms_norm((u * u) @ w_mlp_down, n_mlp_post)


def setup_inputs(seed: int = 0) -> dict:
    key = jax.random.key(seed)
    ks = jax.random.split(key, 20)
    n_pages = PAST_LEN // PAGE_SIZE
    n_phys = (DEC_BATCH * n_pages * 5) // 4
    f32 = jnp.float32

    def nrm(k, shape, scale):
        return jax.random.normal(k, shape, f32) * scale

    def gain(k, shape):
        return 1.0 + 0.1 * jax.random.normal(k, shape, f32)

    page_table = jax.random.permutation(ks[6], n_phys)[:DEC_BATCH * n_pages]
    page_table = page_table.reshape(DEC_BATCH, n_pages).astype(jnp.int32)
    return {
        'x_prompt': nrm(ks[0], (BATCH, SEQ, D_MODEL), 1.0),
        'x_sample': nrm(ks[1], (DEC_BATCH, DEC_SEQ, D_MODEL), 1.0),
        'cache_k': nrm(ks[2], (DEPTH, n_phys, PAGE_SIZE, ATT_KV_HEADS, ATT_HEAD_DIM), 1.0),
        'cache_v': nrm(ks[3], (DEPTH, n_phys, PAGE_SIZE, ATT_KV_HEADS, ATT_HEAD_DIM), 1.0),
        'cache_idx_k': nrm(ks[4], (DEPTH, n_phys, PAGE_SIZE, IDX_DIM), 1.0),
        'state_ret': nrm(ks[5], (DEPTH, DEC_BATCH, RET_HEADS, RET_DK, RET_DV), 0.5),
        'page_table': page_table,
        'norm_attn_pre': gain(ks[7], (DEPTH, D_MODEL)),
        'norm_attn_post': gain(ks[8], (DEPTH, D_MODEL)),
        'w_in': nrm(ks[9], (DEPTH, D_MODEL, IN_W), D_MODEL ** -0.5),
        'ret_gn_w': gain(ks[10], (DEPTH, RET_V_W)),
        'w_att_proj': nrm(ks[11], (DEPTH, ATT_Q_W, D_MODEL), ATT_Q_W ** -0.5),
        'w_ret_proj': nrm(ks[12], (DEPTH, RET_V_W, D_MODEL), RET_V_W ** -0.5),
        'w_out': nrm(ks[13], (DEPTH, D_MODEL, D_MODEL), D_MODEL ** -0.5),
        'norm_mlp_pre': gain(ks[14], (DEPTH, D_MODEL)),
        'w_mlp_up': nrm(ks[15], (DEPTH, D_MODEL, D_FF), D_MODEL ** -0.5),
        'w_mlp_down': nrm(ks[16], (DEPTH, D_FF, D_MODEL), D_FF ** -0.5),
        'norm_mlp_post': gain(ks[17], (DEPTH, D_MODEL)),
    }


def reference(x_prompt, x_sample, cache_k, cache_v, cache_idx_k, state_ret, page_table,
              norm_attn_pre, norm_attn_post, w_in, ret_gn_w, w_att_proj, w_ret_proj, w_out,
              norm_mlp_pre, w_mlp_up, w_mlp_down, norm_mlp_post):
    pos_p = jnp.arange(x_prompt.shape[1], dtype=jnp.int32)
    pos_s = PAST_LEN + jnp.arange(x_sample.shape[1], dtype=jnp.int32)
    log_gamma = jnp.log1p(-jnp.exp2(-5.0 - jnp.arange(RET_HEADS, dtype=jnp.float32)))
    xp, xs = x_prompt, x_sample
    kp_l, vp_l, ikp_l, sp_l = [], [], [], []
    ks_l, vs_l, iks_l, ss_l = [], [], [], []
    for l in range(DEPTH):
        xn = _rms_norm(xp, norm_attn_pre[l])
        q, k, v, iq, ik, iw, rq, rk, rv, rg, ga, gr = _project(xn, pos_p, w_in[l])
        att = _prompt_attention(q, k, v, iq, ik, iw)
        o, S_p = _retention_prompt(rq, rk, rv, log_gamma)
        ret = _retention_output(o, rg, ret_gn_w[l])
        xp = _merge_and_finish(xp, att, ret, ga, gr, w_att_proj[l], w_ret_proj[l], w_out[l],
                               norm_attn_post[l], norm_mlp_pre[l], w_mlp_up[l], w_mlp_down[l],
                               norm_mlp_post[l])
        kp_l.append(k)
        vp_l.append(v)
        ikp_l.append(ik)
        sp_l.append(S_p)
        xn = _rms_norm(xs, norm_attn_pre[l])
        q, k, v, iq, ik, iw, rq, rk, rv, rg, ga, gr = _project(xn, pos_s, w_in[l])
        att = _sample_attention(q, k, v, iq, ik, iw, cache_k[l], cache_v[l], cache_idx_k[l], page_table)
        o, S_s = _retention_chunk(state_ret[l], rq, rk, rv, log_gamma)
        ret = _retention_output(o, rg, ret_gn_w[l])
        xs = _merge_and_finish(xs, att, ret, ga, gr, w_att_proj[l], w_ret_proj[l], w_out[l],
                               norm_attn_post[l], norm_mlp_pre[l], w_mlp_up[l], w_mlp_down[l],
                               norm_mlp_post[l])
        ks_l.append(k)
        vs_l.append(v)
        iks_l.append(ik)
        ss_l.append(S_s)
    y_prompt = xp
    y_sample = xs
    new_k_prompt = jnp.stack(kp_l, axis=0)
    new_v_prompt = jnp.stack(vp_l, axis=0)
    new_idx_k_prompt = jnp.stack(ikp_l, axis=0)
    new_state_ret_prompt = jnp.stack(sp_l, axis=0)
    new_k_sample = jnp.stack(ks_l, axis=0)
    new_v_sample = jnp.stack(vs_l, axis=0)
    new_idx_k_sample = jnp.stack(iks_l, axis=0)
    new_state_ret_sample = jnp.stack(ss_l, axis=0)
    return (y_prompt, y_sample, new_k_prompt, new_v_prompt, new_idx_k_prompt, new_state_ret_prompt,
            new_k_sample, new_v_sample, new_idx_k_sample, new_state_ret_sample)
```

```python
import functools
import math

import jax
import jax.numpy as jnp
import numpy as np
from jax import lax
from jax.experimental import pallas as pl
from jax.experimental.pallas import tpu as pltpu

F32 = jnp.float32
BF16 = jnp.bfloat16
I32 = jnp.int32

D_MODEL = 4096
BATCH = 4
SEQ = 2048
DEC_BATCH = 32
PAST_LEN = 8192
PAGE_SIZE = 128
N_PAGES = PAST_LEN // PAGE_SIZE
ATT_HEADS = 16
ATT_KV_HEADS = 8
ATT_HEAD_DIM = 128
ROPE_THETA = 500000.0
IDX_HEADS = 32
IDX_DIM = 64
TOPK = 256
RET_HEADS = 8
RET_DK = 256
RET_DV = 512
RET_THETA = 10000.0
RET_CHUNK = 128
D_FF = 4 * D_MODEL
NORM_EPS = 1e-6

ATT_Q_W = ATT_HEADS * ATT_HEAD_DIM
ATT_KV_W = ATT_KV_HEADS * ATT_HEAD_DIM
IDX_Q_W = IDX_HEADS * IDX_DIM
RET_QK_W = RET_HEADS * RET_DK
RET_V_W = RET_HEADS * RET_DV
SPLITS = (ATT_Q_W, ATT_KV_W, ATT_KV_W, IDX_Q_W, IDX_DIM, IDX_HEADS,
          RET_QK_W, RET_QK_W, RET_V_W, RET_V_W, D_MODEL, D_MODEL)
OFFS = tuple(int(v) for v in np.concatenate([[0], np.cumsum(SPLITS)]))

LANES = 128
Q_BLOCK = 128
VMEM_LIMIT = 56 * 1024 * 1024

INT_MIN = -2 ** 31
NEG_BIG = -1e30
ATT_SCALE = ATT_HEAD_DIM ** -0.5
IDX_SCALE = (IDX_DIM ** -0.5) * (IDX_HEADS ** -0.5)

_NT = (((1,), (1,)), ((), ()))


def _params(sem):
    return pltpu.CompilerParams(dimension_semantics=sem, vmem_limit_bytes=VMEM_LIMIT)


def _sigmoid(x):
    return 1.0 / (1.0 + jnp.exp(-x))


def _rmsnorm_cast_kernel(x_ref, w_ref, o_ref):
    x = x_ref[...]
    y = x * lax.rsqrt(jnp.mean(x * x, axis=-1, keepdims=True) + NORM_EPS)
    o_ref[...] = (y * w_ref[...]).astype(o_ref.dtype)


def _rmsnorm_cast(x, w, tr):
    m, d = x.shape
    return pl.pallas_call(
        _rmsnorm_cast_kernel,
        grid=(m // tr,),
        in_specs=[pl.BlockSpec((tr, d), lambda i: (i, 0)), pl.BlockSpec((1, d), lambda i: (0, 0))],
        out_specs=pl.BlockSpec((tr, d), lambda i: (i, 0)),
        out_shape=jax.ShapeDtypeStruct((m, d), BF16),
        compiler_params=_params(("parallel",)),
        name="rmsnorm_cast",
    )(x, w.reshape(1, d))


def _post_attn_kernel(x_ref, y_ref, w1_ref, w2_ref, h_ref, hn_ref):
    y = y_ref[...]
    yn = y * lax.rsqrt(jnp.mean(y * y, axis=-1, keepdims=True) + NORM_EPS) * w1_ref[...]
    h = x_ref[...] + yn
    h_ref[...] = h
    hn = h * lax.rsqrt(jnp.mean(h * h, axis=-1, keepdims=True) + NORM_EPS) * w2_ref[...]
    hn_ref[...] = hn.astype(hn_ref.dtype)


def _post_attn(x, y, w1, w2, tr):
    m, d = x.shape
    row = pl.BlockSpec((tr, d), lambda i: (i, 0))
    vec = pl.BlockSpec((1, d), lambda i: (0, 0))
    return pl.pallas_call(
        _post_attn_kernel,
        grid=(m // tr,),
        in_specs=[row, row, vec, vec],
        out_specs=[row, row],
        out_shape=[jax.ShapeDtypeStruct((m, d), F32), jax.ShapeDtypeStruct((m, d), BF16)],
        compiler_params=_params(("parallel",)),
        name="post_attn_norm",
    )(x, y, w1.reshape(1, d), w2.reshape(1, d))


def _post_mlp_kernel(h_ref, d_ref, w_ref, o_ref):
    d = d_ref[...]
    dn = d * lax.rsqrt(jnp.mean(d * d, axis=-1, keepdims=True) + NORM_EPS) * w_ref[...]
    o_ref[...] = h_ref[...] + dn


def _post_mlp(h, d, w, tr):
    m, dm = h.shape
    row = pl.BlockSpec((tr, dm), lambda i: (i, 0))
    return pl.pallas_call(
        _post_mlp_kernel,
        grid=(m // tr,),
        in_specs=[row, row, pl.BlockSpec((1, dm), lambda i: (0, 0))],
        out_specs=row,
        out_shape=jax.ShapeDtypeStruct((m, dm), F32),
        compiler_params=_params(("parallel",)),
        name="post_mlp_norm",
    )(h, d, w.reshape(1, dm))


def _mm_kernel(*refs, n_extra, n_out, nk, epilogue):
    a_ref, b_ref = refs[0], refs[1]
    extra = refs[2:2 + n_extra]
    outs = refs[2 + n_extra:2 + n_extra + n_out]
    if nk == 1:
        epilogue(jnp.dot(a_ref[...], b_ref[...], preferred_element_type=F32), extra, outs)
        return
    acc_ref = refs[-1]
    k = pl.program_id(2)

    @pl.when(k == 0)
    def _():
        acc_ref[...] = jnp.zeros_like(acc_ref)

    acc_ref[...] += jnp.dot(a_ref[...], b_ref[...], preferred_element_type=F32)

    @pl.when(k == nk - 1)
    def _():
        epilogue(acc_ref[...], extra, outs)


def _matmul(a, b, epilogue, extras, outs, *, tm, tn, tk=None, name):
    m, kd = a.shape
    n = b.shape[1]
    tk = kd if tk is None else tk
    nk = kd // tk
    grid = (m // tm, n // tn, nk)

    def lift(f):
        return lambda i, j, k: f(i, j)

    in_specs = [pl.BlockSpec((tm, tk), lambda i, j, k: (i, k)), pl.BlockSpec((tk, tn), lambda i, j, k: (k, j))]
    in_specs += [pl.BlockSpec(bs, lift(im)) for _, bs, im in extras]
    out_specs = [pl.BlockSpec(bs, lift(im)) for _, _, bs, im in outs]
    out_shape = [jax.ShapeDtypeStruct(s, dt) for s, dt, _, _ in outs]
    scratch = [pltpu.VMEM((tm, tn), F32)] if nk > 1 else []
    res = pl.pallas_call(
        functools.partial(_mm_kernel, n_extra=len(extras), n_out=len(outs), nk=nk, epilogue=epilogue),
        grid=grid,
        in_specs=in_specs,
        out_specs=out_specs,
        out_shape=out_shape,
        scratch_shapes=scratch,
        compiler_params=_params(("parallel", "parallel", "arbitrary")),
        name=name,
    )(a, b, *[e[0] for e in extras])
    return res


def _nat(m, n, dt, tm, tn):
    return ((m, n), dt, (tm, tn), lambda i, j: (i, j))


def _rope_lanes(y, c, sm, sp, half):
    n = y.shape[-1]
    return y * c + pltpu.roll(y, n - half, 1) * sm + pltpu.roll(y, half, 1) * sp


def _ep_plain(acc, extra, outs):
    for o in outs:
        o[...] = acc.astype(o.dtype)


def _ep_rope_lanes(acc, extra, outs, *, half, blocked):
    c, sm, sp = extra[0][...], extra[1][...], extra[2][...]
    tm, tn = acc.shape
    for jj in range(tn // LANES):
        y = _rope_lanes(acc[:, jj * LANES:(jj + 1) * LANES], c, sm, sp, half)
        if blocked:
            for r in range(tm // Q_BLOCK):
                outs[0][r, jj] = y[r * Q_BLOCK:(r + 1) * Q_BLOCK].astype(outs[0].dtype)
        else:
            outs[0][:, jj * LANES:(jj + 1) * LANES] = y.astype(outs[0].dtype)
        if len(outs) > 1:
            outs[1][jj] = y.astype(outs[1].dtype)


def _ep_ret_qk(acc, extra, outs, *, scale):
    cos, sin, dec = extra[0][...], extra[1][...], extra[2]
    tn = acc.shape[1]
    for hh in range(tn // RET_DK):
        lo = hh * RET_DK
        x1 = acc[:, lo:lo + LANES]
        x2 = acc[:, lo + LANES:lo + 2 * LANES]
        o1 = x1 * cos - x2 * sin
        o2 = x2 * cos + x1 * sin
        if scale != 1.0:
            o1 = o1 * scale
            o2 = o2 * scale
        d = dec[:, hh * LANES:(hh + 1) * LANES]
        outs[0][:, lo:lo + LANES] = o1.astype(outs[0].dtype)
        outs[0][:, lo + LANES:lo + 2 * LANES] = o2.astype(outs[0].dtype)
        outs[1][:, lo:lo + LANES] = (o1 * d).astype(outs[1].dtype)
        outs[1][:, lo + LANES:lo + 2 * LANES] = (o2 * d).astype(outs[1].dtype)


def _merge_kernel(att_ref, ret_ref, wa_ref, wr_ref, ga_ref, gr_ref, o_ref):
    a = jnp.dot(att_ref[...], wa_ref[...], preferred_element_type=F32)
    r = jnp.dot(ret_ref[...], wr_ref[...], preferred_element_type=F32)
    o_ref[...] = (_sigmoid(ga_ref[...]) * a + _sigmoid(gr_ref[...]) * r).astype(o_ref.dtype)


def _merge(att, ret, wa, wr, gates, tm, tn):
    m = att.shape[0]
    nb = D_MODEL // tn
    return pl.pallas_call(
        _merge_kernel,
        grid=(m // tm, nb),
        in_specs=[
            pl.BlockSpec((tm, ATT_Q_W), lambda i, j: (i, 0)),
            pl.BlockSpec((tm, RET_V_W), lambda i, j: (i, 0)),
            pl.BlockSpec((ATT_Q_W, tn), lambda i, j: (0, j)),
            pl.BlockSpec((RET_V_W, tn), lambda i, j: (0, j)),
            pl.BlockSpec((tm, tn), lambda i, j: (i, nb + j)),
            pl.BlockSpec((tm, tn), lambda i, j: (i, 2 * nb + j)),
        ],
        out_specs=pl.BlockSpec((tm, tn), lambda i, j: (i, j)),
        out_shape=jax.ShapeDtypeStruct((m, D_MODEL), BF16),
        compiler_params=_params(("parallel", "parallel")),
        name="merge_proj",
    )(att, ret, wa, wr, gates, gates)


def _ep_relu2(acc, extra, outs):
    u = jnp.maximum(acc, 0.0)
    outs[0][...] = (u * u).astype(outs[0].dtype)


def _rope_lane_tables(pos, head_w, rot, theta):
    half = rot // 2
    inv_freq = jnp.exp(-math.log(theta) * jnp.arange(half, dtype=F32) / half)
    ang = pos.astype(F32)[:, None] * inv_freq[None, :]
    cos, sin = jnp.cos(ang), jnp.sin(ang)
    n = pos.shape[0]
    z_half = jnp.zeros((n, half), F32)
    rest1 = jnp.ones((n, head_w - rot), F32)
    rest0 = jnp.zeros((n, head_w - rot), F32)
    c = jnp.concatenate([cos, cos, rest1], axis=1)
    sm = jnp.concatenate([-sin, z_half, rest0], axis=1)
    sp = jnp.concatenate([z_half, sin, rest0], axis=1)
    rep = LANES // head_w
    return [jnp.tile(t, (1, rep)) for t in (c, sm, sp)]


def _ret_tables(pos, chunk_pos, chunk_len, log_gamma):
    half = RET_DK // 2
    inv_freq = jnp.exp(-math.log(RET_THETA) * jnp.arange(half, dtype=F32) / half)
    ang = pos.astype(F32)[:, None] * inv_freq[None, :]
    i = chunk_pos.astype(F32)[:, None]
    q_dec = jnp.exp(log_gamma[None, :] * (i + 1.0))
    k_dec = jnp.exp(log_gamma[None, :] * (chunk_len - 1.0 - i))
    return jnp.cos(ang), jnp.sin(ang), jnp.repeat(q_dec, LANES, axis=1), jnp.repeat(k_dec, LANES, axis=1)


def _project(xn, w_in, pos, chunk_pos, chunk_len, log_gamma, *, tm, prompt):
    m = xn.shape[0]
    nrep = max(pos.shape[0] // tm, 1)
    wdt = BF16 if prompt else F32

    def wseg(a, b):
        return w_in[:, OFFS[a]:OFFS[b]].astype(BF16)

    def tab(t, width=LANES, by_col=False):
        if by_col:
            return (t, (tm, width), lambda i, j: (i % nrep, j))
        return (t, (tm, width), lambda i, j: (i % nrep, 0))

    att_t = [tab(t) for t in _rope_lane_tables(pos, ATT_HEAD_DIM, ATT_HEAD_DIM // 4, ROPE_THETA)]
    idx_t = [tab(t) for t in _rope_lane_tables(pos, IDX_DIM, IDX_DIM // 4, ROPE_THETA)]
    r_cos, r_sin, q_dec, k_dec = _ret_tables(pos, chunk_pos, chunk_len, log_gamma)
    tn = 512
    nqb = m // Q_BLOCK
    out = {}

    if prompt:
        o = [((nqb, ATT_HEADS, Q_BLOCK, LANES), BF16, (tm // Q_BLOCK, tn // LANES, Q_BLOCK, LANES),
              lambda i, j: (i, j, 0, 0))]
    else:
        o = [_nat(m, ATT_Q_W, F32, tm, tn)]
    out["q"], = _matmul(xn, wseg(0, 1), functools.partial(_ep_rope_lanes, half=16, blocked=prompt), att_t, o,
                        tm=tm, tn=tn, name="proj_q")
    o = [_nat(m, ATT_KV_W, F32, tm, tn)]
    if prompt:
        per_b = SEQ // tm
        o.append(((BATCH, ATT_KV_HEADS, SEQ, LANES), BF16, (None, tn // LANES, tm, LANES),
                  lambda i, j: (i // per_b, j, i % per_b, 0)))
    res = _matmul(xn, wseg(1, 2), functools.partial(_ep_rope_lanes, half=16, blocked=False), att_t, o,
                  tm=tm, tn=tn, name="proj_k")
    out["k"] = res[0]
    if prompt:
        out["k_heads"] = res[1]
    o = [_nat(m, ATT_KV_W, F32, tm, tn)]
    if prompt:
        o.append(_nat(m, ATT_KV_W, BF16, tm, tn))
    res = _matmul(xn, wseg(2, 3), _ep_plain, [], o, tm=tm, tn=tn, name="proj_v")
    out["v"] = res[0]
    if prompt:
        out["v_bf"] = res[1]
    if prompt:
        o = [((nqb, IDX_Q_W // LANES, Q_BLOCK, LANES), F32, (tm // Q_BLOCK, tn // LANES, Q_BLOCK, LANES),
              lambda i, j: (i, j, 0, 0))]
    else:
        o = [_nat(m, IDX_Q_W, F32, tm, tn)]
    out["iq"], = _matmul(xn, wseg(3, 4), functools.partial(_ep_rope_lanes, half=8, blocked=prompt), idx_t, o,
                         tm=tm, tn=tn, name="proj_iq")
    w_ikw = jnp.pad(w_in[:, OFFS[4]:OFFS[6]], ((0, 0), (0, LANES - IDX_DIM - IDX_HEADS))).astype(BF16)
    ikw_t = [tab(t) for t in _rope_lane_tables(pos, LANES, IDX_DIM // 4, ROPE_THETA)]
    out["ikw"], = _matmul(xn, w_ikw, functools.partial(_ep_rope_lanes, half=8, blocked=False), ikw_t,
                          [_nat(m, LANES, F32, tm, LANES)], tm=tm, tn=LANES, name="proj_ikw")
    for nm, seg, dec, scale in (("rq", 6, q_dec, 1.0), ("rk", 7, k_dec, RET_DK ** -0.5)):
        res = _matmul(xn, wseg(seg, seg + 1), functools.partial(_ep_ret_qk, scale=scale),
                      [tab(r_cos), tab(r_sin), tab(dec, tn // 2, by_col=True)],
                      [_nat(m, RET_QK_W, wdt, tm, tn), _nat(m, RET_QK_W, wdt, tm, tn)],
                      tm=tm, tn=tn, name="proj_" + nm)
        out[nm], out[nm + "d"] = res
    out["rv"], = _matmul(xn, wseg(8, 9), _ep_plain, [], [_nat(m, RET_V_W, wdt, tm, tn)], tm=tm, tn=tn,
                         name="proj_rv")
    out["gates"], = _matmul(xn, wseg(9, 12), _ep_plain, [], [_nat(m, 3 * D_MODEL, F32, tm, tn)], tm=tm, tn=tn,
                            name="proj_gates")
    return out


def _sortable_key(score):
    kb = lax.bitcast_convert_type(score, I32)
    kb = jnp.where(kb == INT_MIN, 0, kb)
    return jnp.where(kb < 0, kb ^ 0x7FFFFFFF, kb)


def _kth_largest(key_ref, k):
    def count_ge(c):
        return jnp.sum(jnp.where(key_ref[...] >= c, 1.0, 0.0), axis=0, keepdims=True)

    t0 = jnp.where(count_ge(0) >= k, 0, INT_MIN).astype(I32)

    def body(i, t):
        cand = t | lax.shift_left(jnp.int32(1), jnp.int32(30) - i)
        return jnp.where(count_ge(cand) >= k, cand, t)

    return lax.fori_loop(0, 31, body, t0)


def _topk_mask(sc_ref, key_ref, bias_ref, q_pos):
    n_keys = key_ref.shape[0]
    row = lax.broadcasted_iota(I32, (n_keys, LANES), 0)
    vis = row <= q_pos
    key_ref[...] = jnp.where(vis, _sortable_key(sc_ref[...]), INT_MIN)
    thr = _kth_largest(key_ref, TOPK)

    key = key_ref[...]
    gt = key > thr
    eqv = vis & (key == thr)
    n_gt = jnp.sum(jnp.where(gt, 1.0, 0.0), axis=0, keepdims=True)
    n_eq = jnp.sum(jnp.where(eqv, 1.0, 0.0), axis=0, keepdims=True)
    need = TOPK - n_gt
    bias_ref[...] = jnp.where(gt, 0.0, jnp.where(eqv, 0.0, NEG_BIG))

    @pl.when(jnp.max(n_eq - need) > 0)
    def _():
        r_i = lax.broadcasted_iota(I32, (LANES, LANES), 0)
        c_i = lax.broadcasted_iota(I32, (LANES, LANES), 1)
        tri = jnp.where(c_i < r_i, 1.0, 0.0).astype(BF16)

        def chunk(c, off):
            r0 = pl.multiple_of(c * LANES, LANES)
            kc = key_ref[pl.ds(r0, LANES), :]
            rc = r0 + lax.broadcasted_iota(I32, (LANES, LANES), 0)
            e = (rc <= q_pos) & (kc == thr)
            ef = jnp.where(e, 1.0, 0.0)
            before = jnp.dot(tri, ef.astype(BF16), preferred_element_type=F32) + off
            keep = e & (before < need)
            bias_ref[pl.ds(r0, LANES), :] = jnp.where(kc > thr, 0.0, jnp.where(keep, 0.0, NEG_BIG))
            return off + jnp.sum(ef, axis=0, keepdims=True)

        lax.fori_loop(0, n_keys // LANES, chunk, jnp.zeros((1, LANES), F32))


def _split_bf16(x):
    hi = x.astype(BF16).astype(F32)
    return hi, x - hi


def _dsa_prompt_kernel(iq_ref, ikw_all_ref, ikw_q_ref, q_ref, k_ref, vt_ref, att_ref,
                       lhs_ref, wt_ref, sc_ref, key_ref, bias_ref):
    qb = pl.program_id(1)
    lane = lax.broadcasted_iota(I32, (1, LANES), 1)
    low = lane < IDX_DIM

    @pl.when(qb == 0)
    def _():
        x = jnp.where(low, ikw_all_ref[...], 0.0)
        hi, lo = _split_bf16(x)
        lhs_ref[:, :LANES] = (hi + pltpu.roll(lo, IDX_DIM, 1)).astype(BF16)
        lhs_ref[:, LANES:] = hi.astype(BF16)

    wt_ref[...] = ikw_q_ref[...].T * IDX_SCALE
    sc_ref[...] = jnp.zeros_like(sc_ref)

    def pair_body(p, carry):
        hi, lo = _split_bf16(iq_ref[p])
        rhi = pltpu.roll(hi, IDX_DIM, 1)
        rlo = pltpu.roll(lo, IDX_DIM, 1)
        ra = jnp.concatenate([jnp.where(low, hi, rhi), jnp.where(low, lo, 0.0)], axis=1)
        rb = jnp.concatenate([jnp.where(low, rhi, hi), jnp.where(low, rlo, 0.0)], axis=1)
        rhs_t = jnp.concatenate([ra, rb], axis=0).astype(BF16)
        d = lax.dot_general(lhs_ref[...], rhs_t, _NT, preferred_element_type=F32)
        wa = wt_ref[pl.ds(IDX_DIM + 2 * p, 1), :]
        wb = wt_ref[pl.ds(IDX_DIM + 2 * p + 1, 1), :]
        sc_ref[...] += wa * jnp.maximum(d[:, :LANES], 0.0) + wb * jnp.maximum(d[:, LANES:], 0.0)
        return carry

    lax.fori_loop(0, IDX_HEADS // 2, pair_body, 0)

    _topk_mask(sc_ref, key_ref, bias_ref, qb * Q_BLOCK + lane)

    def group_body(g, carry):
        qq = jnp.concatenate([q_ref[2 * g], q_ref[2 * g + 1]], axis=0)
        lg = lax.dot_general(k_ref[g], qq, _NT, preferred_element_type=F32) * ATT_SCALE
        bias = bias_ref[...]
        ps = []
        sums = []
        for hh in range(2):
            l = lg[:, hh * LANES:(hh + 1) * LANES] + bias
            p = jnp.exp(l - jnp.max(l, axis=0, keepdims=True))
            sums.append(jnp.sum(p, axis=0, keepdims=True))
            ps.append(p)
        pt = jnp.concatenate(ps, axis=1).astype(BF16)
        ot = jnp.dot(vt_ref[g], pt, preferred_element_type=F32)
        for hh in range(2):
            o = ot[:, hh * LANES:(hh + 1) * LANES] / sums[hh]
            att_ref[2 * g + hh] = o.T.astype(att_ref.dtype)
        return carry

    lax.fori_loop(0, ATT_KV_HEADS, group_body, 0)


def _dsa_prompt(iq, ikw, q, k_heads, vt):
    nqb = SEQ // Q_BLOCK
    n_pair = IDX_Q_W // LANES
    return pl.pallas_call(
        _dsa_prompt_kernel,
        grid=(BATCH, nqb),
        in_specs=[
            pl.BlockSpec((None, n_pair, Q_BLOCK, LANES), lambda b, i: (b * nqb + i, 0, 0, 0)),
            pl.BlockSpec((SEQ, LANES), lambda b, i: (b, 0)),
            pl.BlockSpec((Q_BLOCK, LANES), lambda b, i: (b * nqb + i, 0)),
            pl.BlockSpec((None, ATT_HEADS, Q_BLOCK, LANES), lambda b, i: (b * nqb + i, 0, 0, 0)),
            pl.BlockSpec((None, ATT_KV_HEADS, SEQ, LANES), lambda b, i: (b, 0, 0, 0)),
            pl.BlockSpec((None, ATT_KV_HEADS, LANES, SEQ), lambda b, i: (b, 0, 0, 0)),
        ],
        out_specs=pl.BlockSpec((None, ATT_HEADS, Q_BLOCK, LANES), lambda b, i: (b * nqb + i, 0, 0, 0)),
        out_shape=jax.ShapeDtypeStruct((BATCH * nqb, ATT_HEADS, Q_BLOCK, LANES), BF16),
        scratch_shapes=[
            pltpu.VMEM((SEQ, 2 * LANES), BF16),
            pltpu.VMEM((LANES, LANES), F32),
            pltpu.VMEM((SEQ, LANES), F32),
            pltpu.VMEM((SEQ, LANES), I32),
            pltpu.VMEM((SEQ, LANES), F32),
        ],
        compiler_params=_params(("parallel", "arbitrary")),
        name="dsa_prompt",
    )(iq, ikw, ikw, q, k_heads, vt)


def _group_norm_gate(o, gn_w, rg):
    of = o * lax.rsqrt(jnp.mean(o * o, axis=-1, keepdims=True) + NORM_EPS) * gn_w
    return of * (rg * _sigmoid(rg))


def _ret_prompt_kernel(q_ref, qd_ref, k_ref, kd_ref, v_ref, rg_ref, decay_ref, cdec_ref, gnw_ref,
                       ret_ref, s_ref):
    c = pl.program_id(2)

    @pl.when(c == 0)
    def _():
        s_ref[...] = jnp.zeros_like(s_ref)

    s = s_ref[...]
    v = v_ref[...]
    scores = lax.dot_general(q_ref[...], k_ref[...], _NT, preferred_element_type=F32) * decay_ref[...]
    o = (jnp.dot(scores.astype(BF16), v, preferred_element_type=F32)
         + jnp.dot(qd_ref[...], s.astype(BF16), preferred_element_type=F32))
    kv = lax.dot_general(kd_ref[...], v, (((0,), (0,)), ((), ())), preferred_element_type=F32)
    s_ref[...] = cdec_ref[...] * s + kv
    ret_ref[...] = _group_norm_gate(o, gnw_ref[...], rg_ref[...]).astype(ret_ref.dtype)


def _ret_prompt(pr, decay, cdec, gn_w):
    nc = SEQ // RET_CHUNK
    qk = pl.BlockSpec((RET_CHUNK, RET_DK), lambda b, h, c: (b * nc + c, h))
    vv = pl.BlockSpec((RET_CHUNK, RET_DV), lambda b, h, c: (b * nc + c, h))
    return pl.pallas_call(
        _ret_prompt_kernel,
        grid=(BATCH, RET_HEADS, nc),
        in_specs=[qk, qk, qk, qk, vv, vv,
                  pl.BlockSpec((None, RET_CHUNK, RET_CHUNK), lambda b, h, c: (h, 0, 0)),
                  pl.BlockSpec((None, 1, RET_DV), lambda b, h, c: (h, 0, 0)),
                  pl.BlockSpec((1, RET_DV), lambda b, h, c: (0, h))],
        out_specs=[vv, pl.BlockSpec((None, None, RET_DK, RET_DV), lambda b, h, c: (b, h, 0, 0))],
        out_shape=[jax.ShapeDtypeStruct((BATCH * SEQ, RET_V_W), BF16),
                   jax.ShapeDtypeStruct((BATCH, RET_HEADS, RET_DK, RET_DV), F32)],
        compiler_params=_params(("parallel", "parallel", "arbitrary")),
        name="retention_prompt",
    )(pr["rq"], pr["rqd"], pr["rk"], pr["rkd"], pr["rv"], pr["gates"], decay, cdec, gn_w.reshape(1, RET_V_W))


def _ret_sample_kernel(q_ref, qd_ref, kt_ref, k_ref, v_ref, rg_ref, cdec_ref, gnw_ref, s_ref, ret_ref, so_ref):
    b = pl.program_id(1)
    nb = kt_ref.shape[1]
    s = s_ref[...]
    v = v_ref[pl.ds(b, 1), :]
    q = q_ref[pl.ds(b, 1), :]
    k = k_ref[pl.ds(b, 1), :]
    qk = jnp.sum(q.astype(BF16).astype(F32) * k.astype(BF16).astype(F32), axis=-1, keepdims=True)
    qd8 = jnp.broadcast_to(qd_ref[pl.ds(b, 1), :], (16, RET_DK)).astype(BF16)
    o = qk.astype(BF16).astype(F32) * v.astype(BF16).astype(F32) \
        + jnp.dot(qd8, s.astype(BF16), preferred_element_type=F32)[0:1]
    onehot = lax.broadcasted_iota(I32, (1, nb), 1) == b
    k_col = jnp.sum(jnp.where(onehot, kt_ref[...], 0.0), axis=1, keepdims=True)
    so_ref[...] = cdec_ref[...] * s + k_col * v
    ret_ref[...] = _group_norm_gate(o, gnw_ref[...], rg_ref[pl.ds(b, 1), :])


def _ret_sample(ps, state, cdec, gn_w):
    nb = DEC_BATCH
    qk = pl.BlockSpec((nb, RET_DK), lambda h, b: (0, h))
    vv = pl.BlockSpec((nb, RET_DV), lambda h, b: (0, h))
    st = pl.BlockSpec((None, None, RET_DK, RET_DV), lambda h, b: (b, h, 0, 0))
    return pl.pallas_call(
        _ret_sample_kernel,
        grid=(RET_HEADS, nb),
        in_specs=[qk, qk, pl.BlockSpec((RET_DK, nb), lambda h, b: (h, 0)), qk, vv, vv,
                  pl.BlockSpec((None, 1, RET_DV), lambda h, b: (h, 0, 0)),
                  pl.BlockSpec((1, RET_DV), lambda h, b: (0, h)), st],
        out_specs=[pl.BlockSpec((None, 1, RET_DV), lambda h, b: (b, 0, h)), st],
        out_shape=[jax.ShapeDtypeStruct((nb, 1, RET_V_W), F32),
                   jax.ShapeDtypeStruct((nb, RET_HEADS, RET_DK, RET_DV), F32)],
        compiler_params=_params(("parallel", "arbitrary")),
        name="retention_sample",
    )(ps["rq"], ps["rqd"], ps["rkd"].T, ps["rk"], ps["rv"], ps["gates"], cdec, gn_w.reshape(1, RET_V_W), state)


PAGES_PER_STEP = 8


def _idx_sample_kernel(pt_ref, iq_ref, w_ref, new_ref, *rest):
    pages = rest[:PAGES_PER_STEP]
    sc_ref, scn_ref = rest[PAGES_PER_STEP], rest[PAGES_PER_STEP + 1]
    j = pl.program_id(1)
    hi, lo = _split_bf16(iq_ref[...])
    lhs = jnp.concatenate([hi + pltpu.roll(hi, IDX_DIM, 1), lo], axis=1).astype(BF16)
    w = w_ref[...] * IDX_SCALE

    def page_scores(x):
        xp = jnp.concatenate([x, jnp.zeros_like(x)], axis=1)
        khi, klo = _split_bf16(xp)
        rhs = jnp.concatenate([khi + pltpu.roll(klo, IDX_DIM, 1), khi], axis=1).astype(BF16)
        d = lax.dot_general(lhs, rhs, _NT, preferred_element_type=F32)
        return jnp.sum(w * jnp.maximum(d, 0.0), axis=0, keepdims=True)

    for i in range(PAGES_PER_STEP):
        sc_ref[:, i * PAGE_SIZE:(i + 1) * PAGE_SIZE] = page_scores(pages[i][...])

    @pl.when(j == pl.num_programs(1) - 1)
    def _():
        scn_ref[...] = page_scores(new_ref[...])


def _idx_sample(page_table, iq3, w3, new_pages, cache_idx):
    nsteps = N_PAGES // PAGES_PER_STEP

    def page_spec(i):
        return pl.BlockSpec((None, PAGE_SIZE, IDX_DIM),
                            lambda b, j, pt: (pt[b, j * PAGES_PER_STEP + i], 0, 0))

    grid_spec = pltpu.PrefetchScalarGridSpec(
        num_scalar_prefetch=1,
        grid=(DEC_BATCH, nsteps),
        in_specs=[pl.BlockSpec((None, IDX_HEADS, LANES), lambda b, j, pt: (b, 0, 0)),
                  pl.BlockSpec((None, IDX_HEADS, 1), lambda b, j, pt: (b, 0, 0)),
                  pl.BlockSpec((None, PAGE_SIZE, IDX_DIM), lambda b, j, pt: (b, 0, 0))]
                 + [page_spec(i) for i in range(PAGES_PER_STEP)],
        out_specs=[pl.BlockSpec((None, 1, PAGES_PER_STEP * PAGE_SIZE), lambda b, j, pt: (b, 0, j)),
                   pl.BlockSpec((None, 1, PAGE_SIZE), lambda b, j, pt: (b, 0, 0))],
    )
    return pl.pallas_call(
        _idx_sample_kernel,
        grid_spec=grid_spec,
        out_shape=[jax.ShapeDtypeStruct((DEC_BATCH, 1, PAST_LEN), F32),
                   jax.ShapeDtypeStruct((DEC_BATCH, 1, PAGE_SIZE), F32)],
        compiler_params=_params(("parallel", "arbitrary")),
        name="indexer_sample",
    )(page_table, iq3, w3, new_pages, *([cache_idx] * PAGES_PER_STEP))


def _select_sample_kernel(sc_ref, bias_ref, key_ref):
    _topk_mask(sc_ref, key_ref, bias_ref, jnp.full((1, LANES), PAST_LEN, I32))


def _select_sample(scores_t):
    return pl.pallas_call(
        _select_sample_kernel,
        out_shape=jax.ShapeDtypeStruct(scores_t.shape, F32),
        scratch_shapes=[pltpu.VMEM(scores_t.shape, I32)],
        compiler_params=pltpu.CompilerParams(vmem_limit_bytes=VMEM_LIMIT),
        name="select_sample",
    )(scores_t)


def _attn_sample_kernel(pt_ref, q_ref, bias_ref, biasn_ref, kn_ref, vn_ref, *rest):
    kp = rest[:PAGES_PER_STEP]
    vp = rest[PAGES_PER_STEP:2 * PAGES_PER_STEP]
    o_ref = rest[2 * PAGES_PER_STEP]
    m_ref, l_ref, acc_ref = rest[2 * PAGES_PER_STEP + 1:]
    j = pl.program_id(1)
    group = lax.broadcasted_iota(I32, (ATT_HEADS, LANES), 0) // (ATT_HEADS // ATT_KV_HEADS)

    q = q_ref[...]
    qbd = jnp.concatenate([jnp.where(group == g, q, 0.0) for g in range(ATT_KV_HEADS)], axis=1).astype(BF16)

    @pl.when(j == 0)
    def _():
        m_ref[...] = jnp.full_like(m_ref, NEG_BIG)
        l_ref[...] = jnp.zeros_like(l_ref)
        acc_ref[...] = jnp.zeros_like(acc_ref)

    logits = []
    for i in range(PAGES_PER_STEP):
        kb = kp[i][...].astype(BF16)
        lg = lax.dot_general(qbd, kb, _NT, preferred_element_type=F32) * ATT_SCALE
        logits.append(lg + bias_ref[:, i * PAGE_SIZE:(i + 1) * PAGE_SIZE])
    m_old = m_ref[...]
    m_new = m_old
    for lg in logits:
        m_new = jnp.maximum(m_new, jnp.max(lg, axis=1, keepdims=True))
    alpha = jnp.exp(m_old - m_new)
    l_new = alpha * l_ref[...]
    acc = alpha * acc_ref[...]
    for i in range(PAGES_PER_STEP):
        p = jnp.exp(logits[i] - m_new)
        l_new = l_new + jnp.sum(p, axis=1, keepdims=True)
        acc = acc + jnp.dot(p.astype(BF16), vp[i][...].astype(BF16), preferred_element_type=F32)
    m_ref[...] = m_new
    l_ref[...] = l_new
    acc_ref[...] = acc

    @pl.when(j == pl.num_programs(1) - 1)
    def _():
        kn = kn_ref[...].astype(BF16).astype(F32)
        vn = vn_ref[...].astype(BF16).astype(F32)
        lgn = jnp.sum(qbd.astype(F32) * kn, axis=1, keepdims=True) * ATT_SCALE + biasn_ref[:, 0:1]
        m_f = jnp.maximum(m_new, lgn)
        a = jnp.exp(m_new - m_f)
        pn = jnp.exp(lgn - m_f)
        l_f = a * l_new + pn
        acc_f = a * acc + pn.astype(BF16).astype(F32) * vn
        o = jnp.zeros((ATT_HEADS, LANES), F32)
        for g in range(ATT_KV_HEADS):
            o = o + jnp.where(group == g, acc_f[:, g * LANES:(g + 1) * LANES], 0.0)
        o_ref[...] = o / l_f


def _attn_sample(page_table, q3, bias, bias_new, k_new, v_new, cache_k, cache_v):
    nsteps = N_PAGES // PAGES_PER_STEP

    def page_spec(i):
        return pl.BlockSpec((None, PAGE_SIZE, ATT_KV_W),
                            lambda b, j, pt: (pt[b, j * PAGES_PER_STEP + i], 0, 0))

    row = pl.BlockSpec((None, 1, ATT_KV_W), lambda b, j, pt: (b, 0, 0))
    grid_spec = pltpu.PrefetchScalarGridSpec(
        num_scalar_prefetch=1,
        grid=(DEC_BATCH, nsteps),
        in_specs=[pl.BlockSpec((None, ATT_HEADS, LANES), lambda b, j, pt: (b, 0, 0)),
                  pl.BlockSpec((None, 1, PAGES_PER_STEP * PAGE_SIZE), lambda b, j, pt: (b, 0, j)),
                  pl.BlockSpec((None, 1, PAGE_SIZE), lambda b, j, pt: (b, 0, 0)),
                  row, row]
                 + [page_spec(i) for i in range(PAGES_PER_STEP)] * 2,
        out_specs=pl.BlockSpec((None, ATT_HEADS, LANES), lambda b, j, pt: (b, 0, 0)),
        scratch_shapes=[pltpu.VMEM((ATT_HEADS, 1), F32), pltpu.VMEM((ATT_HEADS, 1), F32),
                        pltpu.VMEM((ATT_HEADS, ATT_KV_W), F32)],
    )
    return pl.pallas_call(
        _attn_sample_kernel,
        grid_spec=grid_spec,
        out_shape=jax.ShapeDtypeStruct((DEC_BATCH, ATT_HEADS, LANES), F32),
        compiler_params=_params(("parallel", "arbitrary")),
        name="attention_sample",
    )(page_table, q3, bias, bias_new, k_new, v_new,
      *([cache_k] * PAGES_PER_STEP), *([cache_v] * PAGES_PER_STEP))


def _finish(x, att, ret, gates, w, *, tm, tr):
    m = x.shape[0]
    mg = _merge(att, ret, w["att_proj"], w["ret_proj"], gates, tm, 512)
    y, = _matmul(mg, w["out"], _ep_plain, [], [_nat(m, D_MODEL, F32, tm, 512)], tm=tm, tn=512, name="w_out")
    h, hn = _post_attn(x, y, w["n_attn_post"], w["n_mlp_pre"], tr)
    u, = _matmul(hn, w["mlp_up"], _ep_relu2, [], [_nat(m, D_FF, BF16, tm, 512)], tm=tm, tn=512, name="mlp_up")
    d, = _matmul(u, w["mlp_down"], _ep_plain, [], [_nat(m, D_MODEL, F32, tm, 512)], tm=tm, tn=512, tk=2048,
                 name="mlp_down")
    return _post_mlp(h, d, w["n_mlp_post"], tr)


def kernel(x_prompt, x_sample, cache_k, cache_v, cache_idx_k, state_ret, page_table, norm_attn_pre,
           norm_attn_post, w_in, ret_gn_w, w_att_proj, w_ret_proj, w_out, norm_mlp_pre, w_mlp_up,
           w_mlp_down, norm_mlp_post):
    log_gamma = jnp.log1p(-jnp.exp2(-5.0 - jnp.arange(RET_HEADS, dtype=F32)))
    w_in0 = w_in[0]
    w = {
        "att_proj": w_att_proj[0].astype(BF16), "ret_proj": w_ret_proj[0].astype(BF16),
        "out": w_out[0].astype(BF16), "mlp_up": w_mlp_up[0].astype(BF16), "mlp_down": w_mlp_down[0].astype(BF16),
        "n_attn_post": norm_attn_post[0], "n_mlp_pre": norm_mlp_pre[0], "n_mlp_post": norm_mlp_post[0],
    }
    gn_w = ret_gn_w[0]

    m_p = BATCH * SEQ
    xp = x_prompt.reshape(m_p, D_MODEL)
    xn = _rmsnorm_cast(xp, norm_attn_pre[0], 256)
    pos_p = jnp.arange(SEQ, dtype=I32)
    pr = _project(xn, w_in0, pos_p, pos_p % RET_CHUNK, float(RET_CHUNK), log_gamma, tm=1024, prompt=True)
    vt = pr["v_bf"].reshape(BATCH, SEQ, ATT_KV_HEADS, ATT_HEAD_DIM).transpose(0, 2, 3, 1)
    att4 = _dsa_prompt(pr["iq"], pr["ikw"], pr["q"], pr["k_heads"], vt)
    att = att4.reshape(m_p // Q_BLOCK, ATT_HEADS, Q_BLOCK, ATT_HEAD_DIM).transpose(0, 2, 1, 3).reshape(m_p, ATT_Q_W)
    ci = jnp.arange(RET_CHUNK, dtype=F32)
    diff = ci[:, None] - ci[None, :]
    decay = jnp.where(diff >= 0, jnp.exp(log_gamma[:, None, None] * jnp.maximum(diff, 0.0)), 0.0)
    cdec_p = jnp.broadcast_to(jnp.exp(log_gamma * RET_CHUNK)[:, None, None], (RET_HEADS, 1, RET_DV))
    ret, s_prompt = _ret_prompt(pr, decay, cdec_p, gn_w)
    y_prompt = _finish(xp, att, ret, pr["gates"], w, tm=1024, tr=256).reshape(BATCH, SEQ, D_MODEL)

    nb = DEC_BATCH
    xs = x_sample.reshape(nb, D_MODEL)
    xns = _rmsnorm_cast(xs, norm_attn_pre[0], nb)
    pos_s = jnp.full((nb,), PAST_LEN, I32)
    ps = _project(xns, w_in0, pos_s, jnp.zeros((nb,), I32), 1.0, log_gamma, tm=nb, prompt=False)
    ik_new = ps["ikw"][:, :IDX_DIM]
    iq3 = jnp.pad(ps["iq"].reshape(nb, IDX_HEADS, IDX_DIM), ((0, 0), (0, 0), (0, LANES - IDX_DIM)))
    w3 = ps["ikw"][:, IDX_DIM:IDX_DIM + IDX_HEADS].reshape(nb, IDX_HEADS, 1)
    new_pages = jnp.pad(ik_new[:, None, :], ((0, 0), (0, PAGE_SIZE - 1), (0, 0)))
    sc_past, sc_new = _idx_sample(page_table, iq3, w3, new_pages, cache_idx_k[0])
    scores = jnp.concatenate([sc_past.reshape(nb, PAST_LEN), sc_new.reshape(nb, PAGE_SIZE)], axis=1)
    scores_t = jnp.pad(scores.T, ((0, 0), (0, LANES - nb)))
    bias = _select_sample(scores_t)[:, :nb].T
    n_phys = cache_k.shape[1]
    att_s = _attn_sample(page_table, ps["q"].reshape(nb, ATT_HEADS, ATT_HEAD_DIM),
                         bias[:, :PAST_LEN].reshape(nb, 1, PAST_LEN), bias[:, PAST_LEN:].reshape(nb, 1, PAGE_SIZE),
                         ps["k"].reshape(nb, 1, ATT_KV_W), ps["v"].reshape(nb, 1, ATT_KV_W),
                         cache_k[0].reshape(n_phys, PAGE_SIZE, ATT_KV_W), cache_v[0].reshape(n_phys, PAGE_SIZE, ATT_KV_W))
    cdec_s = jnp.broadcast_to(jnp.exp(log_gamma)[:, None, None], (RET_HEADS, 1, RET_DV))
    ret_s, s_sample = _ret_sample(ps, state_ret[0], cdec_s, gn_w)
    y_sample = _finish(xs, att_s.reshape(nb, ATT_Q_W).astype(BF16), ret_s.reshape(nb, RET_V_W).astype(BF16),
                       ps["gates"], w, tm=nb, tr=nb).reshape(nb, 1, D_MODEL)

    return (y_prompt, y_sample,
            pr["k"].reshape(1, BATCH, SEQ, ATT_KV_HEADS, ATT_HEAD_DIM),
            pr["v"].reshape(1, BATCH, SEQ, ATT_KV_HEADS, ATT_HEAD_DIM),
            pr["ikw"][:, :IDX_DIM].reshape(1, BATCH, SEQ, IDX_DIM),
            s_prompt[None],
            ps["k"].reshape(1, nb, 1, ATT_KV_HEADS, ATT_HEAD_DIM),
            ps["v"].reshape(1, nb, 1, ATT_KV_HEADS, ATT_HEAD_DIM),
            ik_new.reshape(1, nb, 1, IDX_DIM),
            s_sample[None])
```

```python
import functools
import math

import jax
import jax.numpy as jnp
import numpy as np
from jax import lax
from jax.experimental import pallas as pl
from jax.experimental.pallas import tpu as pltpu

F32 = jnp.float32
BF16 = jnp.bfloat16
I32 = jnp.int32

D_MODEL = 4096
BATCH = 4
SEQ = 2048
DEC_BATCH = 32
PAST_LEN = 8192
PAGE_SIZE = 128
N_PAGES = PAST_LEN // PAGE_SIZE
ATT_HEADS = 16
ATT_KV_HEADS = 8
ATT_HEAD_DIM = 128
ROPE_THETA = 500000.0
IDX_HEADS = 32
IDX_DIM = 64
TOPK = 256
RET_HEADS = 8
RET_DK = 256
RET_DV = 512
RET_THETA = 10000.0
RET_CHUNK = 128
D_FF = 4 * D_MODEL
NORM_EPS = 1e-6

ATT_Q_W = ATT_HEADS * ATT_HEAD_DIM
ATT_KV_W = ATT_KV_HEADS * ATT_HEAD_DIM
IDX_Q_W = IDX_HEADS * IDX_DIM
RET_QK_W = RET_HEADS * RET_DK
RET_V_W = RET_HEADS * RET_DV
SPLITS = (ATT_Q_W, ATT_KV_W, ATT_KV_W, IDX_Q_W, IDX_DIM, IDX_HEADS,
          RET_QK_W, RET_QK_W, RET_V_W, RET_V_W, D_MODEL, D_MODEL)
OFFS = tuple(int(v) for v in np.concatenate([[0], np.cumsum(SPLITS)]))

LANES = 128
SUBLANES = 8
Q_BLOCK = 128
KEY_CHUNK = 512
PAIRS_PER_STEP = 4
GROUPS_PER_STEP = 4
VMEM_LIMIT = 56 * 1024 * 1024

INT_MIN = -2 ** 31
NEG_BIG = -1e30
ATT_SCALE = ATT_HEAD_DIM ** -0.5
IDX_SCALE = (IDX_DIM ** -0.5) * (IDX_HEADS ** -0.5)

_NT = (((1,), (1,)), ((), ()))


def _params(sem):
    return pltpu.CompilerParams(dimension_semantics=sem, vmem_limit_bytes=VMEM_LIMIT)


def _sigmoid(x):
    return 1.0 / (1.0 + jnp.exp(-x))


def _rmsnorm_cast_kernel(x_ref, w_ref, o_ref):
    x = x_ref[...]
    y = x * lax.rsqrt(jnp.mean(x * x, axis=-1, keepdims=True) + NORM_EPS)
    o_ref[...] = (y * w_ref[...]).astype(o_ref.dtype)


def _rmsnorm_cast(x, w, tr):
    m, d = x.shape
    return pl.pallas_call(
        _rmsnorm_cast_kernel,
        grid=(m // tr,),
        in_specs=[pl.BlockSpec((tr, d), lambda i: (i, 0)), pl.BlockSpec((1, d), lambda i: (0, 0))],
        out_specs=pl.BlockSpec((tr, d), lambda i: (i, 0)),
        out_shape=jax.ShapeDtypeStruct((m, d), BF16),
        compiler_params=_params(("parallel",)),
        name="rmsnorm_cast",
    )(x, w.reshape(1, d))


def _post_attn_kernel(x_ref, y_ref, w1_ref, w2_ref, h_ref, hn_ref):
    y = y_ref[...]
    yn = y * lax.rsqrt(jnp.mean(y * y, axis=-1, keepdims=True) + NORM_EPS) * w1_ref[...]
    h = x_ref[...] + yn
    h_ref[...] = h
    hn = h * lax.rsqrt(jnp.mean(h * h, axis=-1, keepdims=True) + NORM_EPS) * w2_ref[...]
    hn_ref[...] = hn.astype(hn_ref.dtype)


def _post_attn(x, y, w1, w2, tr):
    m, d = x.shape
    row = pl.BlockSpec((tr, d), lambda i: (i, 0))
    vec = pl.BlockSpec((1, d), lambda i: (0, 0))
    return pl.pallas_call(
        _post_attn_kernel,
        grid=(m // tr,),
        in_specs=[row, row, vec, vec],
        out_specs=[row, row],
        out_shape=[jax.ShapeDtypeStruct((m, d), F32), jax.ShapeDtypeStruct((m, d), BF16)],
        compiler_params=_params(("parallel",)),
        name="post_attn_norm",
    )(x, y, w1.reshape(1, d), w2.reshape(1, d))


def _post_mlp_kernel(h_ref, d_ref, w_ref, o_ref):
    d = d_ref[...]
    dn = d * lax.rsqrt(jnp.mean(d * d, axis=-1, keepdims=True) + NORM_EPS) * w_ref[...]
    o_ref[...] = h_ref[...] + dn


def _post_mlp(h, d, w, tr):
    m, dm = h.shape
    row = pl.BlockSpec((tr, dm), lambda i: (i, 0))
    return pl.pallas_call(
        _post_mlp_kernel,
        grid=(m // tr,),
        in_specs=[row, row, pl.BlockSpec((1, dm), lambda i: (0, 0))],
        out_specs=row,
        out_shape=jax.ShapeDtypeStruct((m, dm), F32),
        compiler_params=_params(("parallel",)),
        name="post_mlp_norm",
    )(h, d, w.reshape(1, dm))


def _mm_kernel(*refs, n_extra, n_out, nk, epilogue):
    a_ref, b_ref = refs[0], refs[1]
    extra = refs[2:2 + n_extra]
    outs = refs[2 + n_extra:2 + n_extra + n_out]
    if nk == 1:
        epilogue(jnp.dot(a_ref[...], b_ref[...], preferred_element_type=F32), extra, outs)
        return
    acc_ref = refs[-1]
    k = pl.program_id(2)
    d = jnp.dot(a_ref[...], b_ref[...], preferred_element_type=F32)

    @pl.when(k == 0)
    def _():
        acc_ref[...] = d

    @pl.when((k > 0) & (k < nk - 1))
    def _():
        acc_ref[...] += d

    @pl.when(k == nk - 1)
    def _():
        epilogue(acc_ref[...] + d, extra, outs)


def _matmul(a, b, epilogue, extras, outs, *, tm, tn, tk=None, n=None, col0=0, name):
    m, kd = a.shape
    n = b.shape[1] if n is None else n
    tk = kd if tk is None else tk
    nk = kd // tk
    grid = (m // tm, n // tn, nk)

    def lift(f):
        return lambda i, j, k: f(i, j)

    in_specs = [pl.BlockSpec((tm, tk), lambda i, j, k: (i, k)),
                pl.BlockSpec((tk, tn), lambda i, j, k: (k, col0 + j))]
    in_specs += [pl.BlockSpec(bs, lift(im)) for _, bs, im in extras]
    out_specs = [pl.BlockSpec(bs, lift(im)) for _, _, bs, im in outs]
    out_shape = [jax.ShapeDtypeStruct(s, dt) for s, dt, _, _ in outs]
    scratch = [pltpu.VMEM((tm, tn), F32)] if nk > 1 else []
    res = pl.pallas_call(
        functools.partial(_mm_kernel, n_extra=len(extras), n_out=len(outs), nk=nk, epilogue=epilogue),
        grid=grid,
        in_specs=in_specs,
        out_specs=out_specs,
        out_shape=out_shape,
        scratch_shapes=scratch,
        compiler_params=_params(("parallel", "parallel", "arbitrary")),
        name=name,
    )(a, b, *[e[0] for e in extras])
    return res


def _nat(m, n, dt, tm, tn):
    return ((m, n), dt, (tm, tn), lambda i, j: (i, j))


def _rope_lanes(y, c, sm, sp, half):
    n = y.shape[-1]
    return y * c + pltpu.roll(y, n - half, 1) * sm + pltpu.roll(y, half, 1) * sp


def _ep_plain(acc, extra, outs):
    for o in outs:
        o[...] = acc.astype(o.dtype)


def _ep_rope_lanes(acc, extra, outs, *, half, blocked):
    c, sm, sp = extra[0][...], extra[1][...], extra[2][...]
    tm, tn = acc.shape
    for jj in range(tn // LANES):
        y = _rope_lanes(acc[:, jj * LANES:(jj + 1) * LANES], c, sm, sp, half)
        if blocked:
            for r in range(tm // Q_BLOCK):
                outs[0][r, jj] = y[r * Q_BLOCK:(r + 1) * Q_BLOCK].astype(outs[0].dtype)
        else:
            outs[0][:, jj * LANES:(jj + 1) * LANES] = y.astype(outs[0].dtype)
        if len(outs) > 1:
            outs[1][jj] = y.astype(outs[1].dtype)


def _ep_ret_qk(acc, extra, outs, *, scale):
    cos, sin, dec = extra[0][...], extra[1][...], extra[2]
    tn = acc.shape[1]
    for hh in range(tn // RET_DK):
        lo = hh * RET_DK
        x1 = acc[:, lo:lo + LANES]
        x2 = acc[:, lo + LANES:lo + 2 * LANES]
        o1 = x1 * cos - x2 * sin
        o2 = x2 * cos + x1 * sin
        if scale != 1.0:
            o1 = o1 * scale
            o2 = o2 * scale
        d = dec[:, hh * LANES:(hh + 1) * LANES]
        outs[0][:, lo:lo + LANES] = o1.astype(outs[0].dtype)
        outs[0][:, lo + LANES:lo + 2 * LANES] = o2.astype(outs[0].dtype)
        outs[1][:, lo:lo + LANES] = (o1 * d).astype(outs[1].dtype)
        outs[1][:, lo + LANES:lo + 2 * LANES] = (o2 * d).astype(outs[1].dtype)


def _merge_kernel(att_ref, ret_ref, wa_ref, wr_ref, ga_ref, gr_ref, o_ref):
    a = jnp.dot(att_ref[...], wa_ref[...], preferred_element_type=F32)
    r = jnp.dot(ret_ref[...], wr_ref[...], preferred_element_type=F32)
    o_ref[...] = (_sigmoid(ga_ref[...]) * a + _sigmoid(gr_ref[...]) * r).astype(o_ref.dtype)


def _merge(att, ret, wa, wr, gates, tm, tn):
    m = att.shape[0]
    nb = D_MODEL // tn
    return pl.pallas_call(
        _merge_kernel,
        grid=(m // tm, nb),
        in_specs=[
            pl.BlockSpec((tm, ATT_Q_W), lambda i, j: (i, 0)),
            pl.BlockSpec((tm, RET_V_W), lambda i, j: (i, 0)),
            pl.BlockSpec((ATT_Q_W, tn), lambda i, j: (0, j)),
            pl.BlockSpec((RET_V_W, tn), lambda i, j: (0, j)),
            pl.BlockSpec((tm, tn), lambda i, j: (i, nb + j)),
            pl.BlockSpec((tm, tn), lambda i, j: (i, 2 * nb + j)),
        ],
        out_specs=pl.BlockSpec((tm, tn), lambda i, j: (i, j)),
        out_shape=jax.ShapeDtypeStruct((m, D_MODEL), BF16),
        compiler_params=_params(("parallel", "parallel")),
        name="merge_proj",
    )(att, ret, wa, wr, gates, gates)


def _ep_relu2(acc, extra, outs):
    u = jnp.maximum(acc, 0.0)
    outs[0][...] = (u * u).astype(outs[0].dtype)


def _rope_lane_tables(pos, head_w, rot, theta):
    half = rot // 2
    inv_freq = jnp.exp(-math.log(theta) * jnp.arange(half, dtype=F32) / half)
    ang = pos.astype(F32)[:, None] * inv_freq[None, :]
    cos, sin = jnp.cos(ang), jnp.sin(ang)
    n = pos.shape[0]
    z_half = jnp.zeros((n, half), F32)
    rest1 = jnp.ones((n, head_w - rot), F32)
    rest0 = jnp.zeros((n, head_w - rot), F32)
    c = jnp.concatenate([cos, cos, rest1], axis=1)
    sm = jnp.concatenate([-sin, z_half, rest0], axis=1)
    sp = jnp.concatenate([z_half, sin, rest0], axis=1)
    rep = LANES // head_w
    return [jnp.tile(t, (1, rep)) for t in (c, sm, sp)]


def _ret_tables(pos, chunk_pos, chunk_len, log_gamma):
    half = RET_DK // 2
    inv_freq = jnp.exp(-math.log(RET_THETA) * jnp.arange(half, dtype=F32) / half)
    ang = pos.astype(F32)[:, None] * inv_freq[None, :]
    i = chunk_pos.astype(F32)[:, None]
    q_dec = jnp.exp(log_gamma[None, :] * (i + 1.0))
    k_dec = jnp.exp(log_gamma[None, :] * (chunk_len - 1.0 - i))
    return jnp.cos(ang), jnp.sin(ang), jnp.repeat(q_dec, LANES, axis=1), jnp.repeat(k_dec, LANES, axis=1)


PROJ_TN = 512
IKW_W = IDX_DIM + IDX_HEADS
W_PAD = PROJ_TN - IKW_W
PAD_OFFS = tuple(o if i <= 5 else o + W_PAD for i, o in enumerate(OFFS))


def _pad_cast_w_in(w_in):
    zeros = jnp.zeros((w_in.shape[0], W_PAD), w_in.dtype)
    return jnp.concatenate([w_in[:, :OFFS[6]], zeros, w_in[:, OFFS[6]:]], axis=1).astype(BF16)


def _project(xn, w_pad, pos, chunk_pos, chunk_len, log_gamma, *, tm, prompt):
    m = xn.shape[0]
    nrep = max(pos.shape[0] // tm, 1)
    wdt = BF16 if prompt else F32

    def seg(a, b):
        return dict(n=PAD_OFFS[b] - PAD_OFFS[a], col0=PAD_OFFS[a] // PROJ_TN, tn=PROJ_TN)

    def tab(t, width=LANES, by_col=False):
        if by_col:
            return (t, (tm, width), lambda i, j: (i % nrep, j))
        return (t, (tm, width), lambda i, j: (i % nrep, 0))

    att_t = [tab(t) for t in _rope_lane_tables(pos, ATT_HEAD_DIM, ATT_HEAD_DIM // 4, ROPE_THETA)]
    idx_t = [tab(t) for t in _rope_lane_tables(pos, IDX_DIM, IDX_DIM // 4, ROPE_THETA)]
    r_cos, r_sin, q_dec, k_dec = _ret_tables(pos, chunk_pos, chunk_len, log_gamma)
    tn = PROJ_TN
    nqb = m // Q_BLOCK
    out = {}

    if prompt:
        o = [((nqb, ATT_HEADS, Q_BLOCK, LANES), BF16, (tm // Q_BLOCK, tn // LANES, Q_BLOCK, LANES),
              lambda i, j: (i, j, 0, 0))]
    else:
        o = [_nat(m, ATT_Q_W, F32, tm, tn)]
    out["q"], = _matmul(xn, w_pad, functools.partial(_ep_rope_lanes, half=16, blocked=prompt), att_t, o,
                        tm=tm, name="proj_q", **seg(0, 1))
    o = [_nat(m, ATT_KV_W, F32, tm, tn)]
    if prompt:
        per_b = SEQ // tm
        o.append(((BATCH, ATT_KV_HEADS, SEQ, LANES), BF16, (None, tn // LANES, tm, LANES),
                  lambda i, j: (i // per_b, j, i % per_b, 0)))
    res = _matmul(xn, w_pad, functools.partial(_ep_rope_lanes, half=16, blocked=False), att_t, o,
                  tm=tm, name="proj_k", **seg(1, 2))
    out["k"] = res[0]
    if prompt:
        out["k_heads"] = res[1]
    o = [_nat(m, ATT_KV_W, F32, tm, tn)]
    if prompt:
        o.append(_nat(m, ATT_KV_W, BF16, tm, tn))
    res = _matmul(xn, w_pad, _ep_plain, [], o, tm=tm, name="proj_v", **seg(2, 3))
    out["v"] = res[0]
    if prompt:
        out["v_bf"] = res[1]
    if prompt:
        o = [((nqb, IDX_Q_W // LANES, Q_BLOCK, LANES), F32, (tm // Q_BLOCK, tn // LANES, Q_BLOCK, LANES),
              lambda i, j: (i, j, 0, 0))]
    else:
        o = [_nat(m, IDX_Q_W, F32, tm, tn)]
    out["iq"], = _matmul(xn, w_pad, functools.partial(_ep_rope_lanes, half=8, blocked=prompt), idx_t, o,
                         tm=tm, name="proj_iq", **seg(3, 4))
    ikw_t = [tab(t) for t in _rope_lane_tables(pos, LANES, IDX_DIM // 4, ROPE_THETA)]
    out["ikw"], = _matmul(xn, w_pad, functools.partial(_ep_rope_lanes, half=8, blocked=False), ikw_t,
                          [_nat(m, LANES, F32, tm, LANES)], tm=tm, tn=LANES, n=LANES, col0=OFFS[4] // LANES,
                          name="proj_ikw")
    for nm, grp, dec, scale in (("rq", 6, q_dec, 1.0), ("rk", 7, k_dec, RET_DK ** -0.5)):
        res = _matmul(xn, w_pad, functools.partial(_ep_ret_qk, scale=scale),
                      [tab(r_cos), tab(r_sin), tab(dec, tn // 2, by_col=True)],
                      [_nat(m, RET_QK_W, wdt, tm, tn), _nat(m, RET_QK_W, wdt, tm, tn)],
                      tm=tm, name="proj_" + nm, **seg(grp, grp + 1))
        out[nm], out[nm + "d"] = res
    out["rv"], = _matmul(xn, w_pad, _ep_plain, [], [_nat(m, RET_V_W, wdt, tm, tn)], tm=tm, name="proj_rv",
                         **seg(8, 9))
    out["gates"], = _matmul(xn, w_pad, _ep_plain, [], [_nat(m, 3 * D_MODEL, F32, tm, tn)], tm=tm,
                            name="proj_gates", **seg(9, 12))
    return out


def _sortable_key(score):
    kb = lax.bitcast_convert_type(score, I32)
    kb = jnp.where(kb == INT_MIN, 0, kb)
    return jnp.where(kb < 0, kb ^ 0x7FFFFFFF, kb)


def _tile_reduce(x, op):
    tiles = [x[i * SUBLANES:(i + 1) * SUBLANES] for i in range(x.shape[0] // SUBLANES)]
    while len(tiles) > 1:
        nxt = [op(tiles[i], tiles[i + 1]) for i in range(0, len(tiles) - 1, 2)]
        if len(tiles) % 2:
            nxt.append(tiles[-1])
        tiles = nxt
    return tiles[0]


def _topk_mask(sc_ref, key_ref, bias_ref, q_pos, n_chunks):
    def rows(c):
        r0 = pl.multiple_of(c * KEY_CHUNK, KEY_CHUNK)
        return pl.ds(r0, KEY_CHUNK), r0 + lax.broadcasted_iota(I32, (KEY_CHUNK, LANES), 0)

    def build(c, carry):
        sl, row = rows(c)
        key_ref[sl, :] = jnp.where(row <= q_pos, _sortable_key(sc_ref[sl, :]), INT_MIN)
        return carry

    lax.fori_loop(0, n_chunks, build, 0)

    def count(pred):
        def body(c, acc):
            sl, row = rows(c)
            return acc + _tile_reduce(jnp.where(pred(key_ref[sl, :], row), 1.0, 0.0), jnp.add)

        part = lax.fori_loop(0, n_chunks, body, jnp.zeros((SUBLANES, LANES), F32))
        return jnp.sum(part, axis=0, keepdims=True)

    t0 = jnp.where(count(lambda k, row: k >= 0) >= TOPK, 0, INT_MIN).astype(I32)

    def search(i, t):
        cand = t | lax.shift_left(jnp.int32(1), jnp.int32(30) - i)
        return jnp.where(count(lambda k, row: k >= cand) >= TOPK, cand, t)

    thr = lax.fori_loop(0, 31, search, t0)
    n_eq = count(lambda k, row: (row <= q_pos) & (k == thr))
    need = TOPK - count(lambda k, row: k > thr)

    def write(c, carry):
        sl, row = rows(c)
        k = key_ref[sl, :]
        bias_ref[sl, :] = jnp.where(k > thr, 0.0, jnp.where((row <= q_pos) & (k == thr), 0.0, NEG_BIG))
        return carry

    lax.fori_loop(0, n_chunks, write, 0)

    @pl.when(jnp.max(n_eq - need) > 0)
    def _():
        r_i = lax.broadcasted_iota(I32, (LANES, LANES), 0)
        c_i = lax.broadcasted_iota(I32, (LANES, LANES), 1)
        tri = jnp.where(c_i < r_i, 1.0, 0.0).astype(BF16)

        def chunk(c, off):
            r0 = pl.multiple_of(c * LANES, LANES)
            kc = key_ref[pl.ds(r0, LANES), :]
            rc = r0 + lax.broadcasted_iota(I32, (LANES, LANES), 0)
            e = (rc <= q_pos) & (kc == thr)
            ef = jnp.where(e, 1.0, 0.0)
            before = jnp.dot(tri, ef.astype(BF16), preferred_element_type=F32) + off
            keep = e & (before < need)
            bias_ref[pl.ds(r0, LANES), :] = jnp.where(kc > thr, 0.0, jnp.where(keep, 0.0, NEG_BIG))
            return off + jnp.sum(ef, axis=0, keepdims=True)

        lax.fori_loop(0, n_chunks * (KEY_CHUNK // LANES), chunk, jnp.zeros((1, LANES), F32))


def _split_bf16(x):
    hi = x.astype(BF16).astype(F32)
    return hi, x - hi


def _dsa_prompt_kernel(iq_ref, ikw_all_ref, ikw_q_ref, q_ref, k_ref, vt_ref, att_ref,
                       lhs_ref, wt_ref, sc_ref, key_ref, bias_ref):
    qb = pl.program_id(1)
    n_ck = (qb * Q_BLOCK + Q_BLOCK + KEY_CHUNK - 1) // KEY_CHUNK
    lane = lax.broadcasted_iota(I32, (1, LANES), 1)
    low = lane < IDX_DIM

    def chunk_rows(c):
        return pl.ds(pl.multiple_of(c * KEY_CHUNK, KEY_CHUNK), KEY_CHUNK)

    @pl.when(qb == 0)
    def _():
        x = jnp.where(low, ikw_all_ref[...], 0.0)
        hi, lo = _split_bf16(x)
        lhs_ref[:, :LANES] = (hi + pltpu.roll(lo, IDX_DIM, 1)).astype(BF16)
        lhs_ref[:, LANES:] = hi.astype(BF16)

    wt_ref[...] = ikw_q_ref[...].T * IDX_SCALE
    sc_ref[...] = jnp.zeros_like(sc_ref)

    def pairs_body(pq, carry):
        rhs, wts = [], []
        for u in range(PAIRS_PER_STEP):
            p = pq * PAIRS_PER_STEP + u
            hi, lo = _split_bf16(iq_ref[p])
            rhi = pltpu.roll(hi, IDX_DIM, 1)
            rlo = pltpu.roll(lo, IDX_DIM, 1)
            ra = jnp.concatenate([jnp.where(low, hi, rhi), jnp.where(low, lo, 0.0)], axis=1)
            rb = jnp.concatenate([jnp.where(low, rhi, hi), jnp.where(low, rlo, 0.0)], axis=1)
            rhs.append(jnp.concatenate([ra, rb], axis=0).astype(BF16))
            wts.append((wt_ref[pl.ds(IDX_DIM + 2 * p, 1), :], wt_ref[pl.ds(IDX_DIM + 2 * p + 1, 1), :]))

        def ck_body(c, inner):
            sl = chunk_rows(c)
            lhs = lhs_ref[sl, :]
            acc = sc_ref[sl, :]
            for rhs_t, (wa, wb) in zip(rhs, wts):
                d = lax.dot_general(lhs, rhs_t, _NT, preferred_element_type=F32)
                acc = acc + wa * jnp.maximum(d[:, :LANES], 0.0) + wb * jnp.maximum(d[:, LANES:], 0.0)
            sc_ref[sl, :] = acc
            return inner

        lax.fori_loop(0, n_ck, ck_body, 0)
        return carry

    lax.fori_loop(0, IDX_HEADS // 2 // PAIRS_PER_STEP, pairs_body, 0)

    _topk_mask(sc_ref, key_ref, bias_ref, qb * Q_BLOCK + lane, n_ck)

    c_exp = ATT_SCALE * math.log2(math.e)

    def groups_body(gq, carry):
        gs = [gq * GROUPS_PER_STEP + u for u in range(GROUPS_PER_STEP)]
        qqs = [jnp.concatenate([q_ref[2 * g], q_ref[2 * g + 1]], axis=0) for g in gs]

        def ck_body(c, state):
            sl = chunk_rows(c)
            bias = bias_ref[sl, :]
            new = []
            for g, qq, (m_old, l_old, acc) in zip(gs, qqs, state):
                lg = lax.dot_general(k_ref[g, sl, :], qq, _NT, preferred_element_type=F32)
                ls = [lg[:, hh * LANES:(hh + 1) * LANES] + bias for hh in range(2)]
                m_ck = jnp.concatenate(
                    [jnp.max(_tile_reduce(l, jnp.maximum), axis=0, keepdims=True) for l in ls], axis=1)
                m_new = jnp.maximum(m_old, m_ck)
                alpha = jnp.exp2((m_old - m_new) * c_exp)
                ps = [jnp.exp2((l - m_new[:, hh * LANES:(hh + 1) * LANES]) * c_exp) for hh, l in enumerate(ls)]
                l_ck = jnp.concatenate(
                    [jnp.sum(_tile_reduce(p, jnp.add), axis=0, keepdims=True) for p in ps], axis=1)
                pt = jnp.concatenate(ps, axis=1).astype(BF16)
                pv = jnp.dot(vt_ref[g, c], pt, preferred_element_type=F32)
                new.append((m_new, alpha * l_old + l_ck, alpha * acc + pv))
            return tuple(new)

        init = (jnp.full((1, 2 * LANES), NEG_BIG, F32), jnp.zeros((1, 2 * LANES), F32),
                jnp.zeros((ATT_HEAD_DIM, 2 * LANES), F32))
        final = lax.fori_loop(0, n_ck, ck_body, (init,) * GROUPS_PER_STEP)
        for g, (_, l_fin, acc) in zip(gs, final):
            o = acc / l_fin
            for hh in range(2):
                att_ref[2 * g + hh] = o[:, hh * LANES:(hh + 1) * LANES].T.astype(att_ref.dtype)
        return carry

    lax.fori_loop(0, ATT_KV_HEADS // GROUPS_PER_STEP, groups_body, 0)


def _dsa_prompt(iq, ikw, q, k_heads, vt):
    nqb = SEQ // Q_BLOCK
    n_pair = IDX_Q_W // LANES
    return pl.pallas_call(
        _dsa_prompt_kernel,
        grid=(BATCH, nqb),
        in_specs=[
            pl.BlockSpec((None, n_pair, Q_BLOCK, LANES), lambda b, i: (b * nqb + i, 0, 0, 0)),
            pl.BlockSpec((SEQ, LANES), lambda b, i: (b, 0)),
            pl.BlockSpec((Q_BLOCK, LANES), lambda b, i: (b * nqb + i, 0)),
            pl.BlockSpec((None, ATT_HEADS, Q_BLOCK, LANES), lambda b, i: (b * nqb + i, 0, 0, 0)),
            pl.BlockSpec((None, ATT_KV_HEADS, SEQ, LANES), lambda b, i: (b, 0, 0, 0)),
            pl.BlockSpec((None, ATT_KV_HEADS, SEQ // KEY_CHUNK, LANES, KEY_CHUNK), lambda b, i: (b, 0, 0, 0, 0)),
        ],
        out_specs=pl.BlockSpec((None, ATT_HEADS, Q_BLOCK, LANES), lambda b, i: (b * nqb + i, 0, 0, 0)),
        out_shape=jax.ShapeDtypeStruct((BATCH * nqb, ATT_HEADS, Q_BLOCK, LANES), BF16),
        scratch_shapes=[
            pltpu.VMEM((SEQ, 2 * LANES), BF16),
            pltpu.VMEM((LANES, LANES), F32),
            pltpu.VMEM((SEQ, LANES), F32),
            pltpu.VMEM((SEQ, LANES), I32),
            pltpu.VMEM((SEQ, LANES), F32),
        ],
        compiler_params=_params(("parallel", "arbitrary")),
        name="dsa_prompt",
    )(iq, ikw, ikw, q, k_heads, vt)


def _group_norm_gate(o, gn_w, rg):
    of = o * lax.rsqrt(jnp.mean(o * o, axis=-1, keepdims=True) + NORM_EPS) * gn_w
    return of * (rg * _sigmoid(rg))


def _ret_prompt_kernel(q_ref, qd_ref, k_ref, kd_ref, v_ref, rg_ref, decay_ref, cdec_ref, gnw_ref,
                       ret_ref, s_ref):
    c = pl.program_id(2)

    @pl.when(c == 0)
    def _():
        s_ref[...] = jnp.zeros_like(s_ref)

    s = s_ref[...]
    v = v_ref[...]
    scores = lax.dot_general(q_ref[...], k_ref[...], _NT, preferred_element_type=F32) * decay_ref[...]
    o = (jnp.dot(scores.astype(BF16), v, preferred_element_type=F32)
         + jnp.dot(qd_ref[...], s.astype(BF16), preferred_element_type=F32))
    kv = lax.dot_general(kd_ref[...], v, (((0,), (0,)), ((), ())), preferred_element_type=F32)
    s_ref[...] = cdec_ref[...] * s + kv
    ret_ref[...] = _group_norm_gate(o, gnw_ref[...], rg_ref[...]).astype(ret_ref.dtype)


def _ret_prompt(pr, decay, cdec, gn_w):
    nc = SEQ // RET_CHUNK
    qk = pl.BlockSpec((RET_CHUNK, RET_DK), lambda b, h, c: (b * nc + c, h))
    vv = pl.BlockSpec((RET_CHUNK, RET_DV), lambda b, h, c: (b * nc + c, h))
    return pl.pallas_call(
        _ret_prompt_kernel,
        grid=(BATCH, RET_HEADS, nc),
        in_specs=[qk, qk, qk, qk, vv, vv,
                  pl.BlockSpec((None, RET_CHUNK, RET_CHUNK), lambda b, h, c: (h, 0, 0)),
                  pl.BlockSpec((None, 1, RET_DV), lambda b, h, c: (h, 0, 0)),
                  pl.BlockSpec((1, RET_DV), lambda b, h, c: (0, h))],
        out_specs=[vv, pl.BlockSpec((None, None, RET_DK, RET_DV), lambda b, h, c: (b, h, 0, 0))],
        out_shape=[jax.ShapeDtypeStruct((BATCH * SEQ, RET_V_W), BF16),
                   jax.ShapeDtypeStruct((BATCH, RET_HEADS, RET_DK, RET_DV), F32)],
        compiler_params=_params(("parallel", "parallel", "arbitrary")),
        name="retention_prompt",
    )(pr["rq"], pr["rqd"], pr["rk"], pr["rkd"], pr["rv"], pr["gates"], decay, cdec, gn_w.reshape(1, RET_V_W))


def _ret_sample_kernel(q_ref, qd_ref, kt_ref, k_ref, v_ref, rg_ref, cdec_ref, gnw_ref, s_ref, ret_ref, so_ref):
    b = pl.program_id(1)
    nb = kt_ref.shape[1]
    s = s_ref[...]
    v = v_ref[pl.ds(b, 1), :]
    q = q_ref[pl.ds(b, 1), :]
    k = k_ref[pl.ds(b, 1), :]
    qk = jnp.sum(q.astype(BF16).astype(F32) * k.astype(BF16).astype(F32), axis=-1, keepdims=True)
    qd8 = jnp.broadcast_to(qd_ref[pl.ds(b, 1), :], (16, RET_DK)).astype(BF16)
    o = qk.astype(BF16).astype(F32) * v.astype(BF16).astype(F32) \
        + jnp.dot(qd8, s.astype(BF16), preferred_element_type=F32)[0:1]
    onehot = lax.broadcasted_iota(I32, (1, nb), 1) == b
    k_col = jnp.sum(jnp.where(onehot, kt_ref[...], 0.0), axis=1, keepdims=True)
    so_ref[...] = cdec_ref[...] * s + k_col * v
    ret_ref[...] = _group_norm_gate(o, gnw_ref[...], rg_ref[pl.ds(b, 1), :])


def _ret_sample(ps, state, cdec, gn_w):
    nb = DEC_BATCH
    qk = pl.BlockSpec((nb, RET_DK), lambda h, b: (0, h))
    vv = pl.BlockSpec((nb, RET_DV), lambda h, b: (0, h))
    st = pl.BlockSpec((None, None, RET_DK, RET_DV), lambda h, b: (b, h, 0, 0))
    return pl.pallas_call(
        _ret_sample_kernel,
        grid=(RET_HEADS, nb),
        in_specs=[qk, qk, pl.BlockSpec((RET_DK, nb), lambda h, b: (h, 0)), qk, vv, vv,
                  pl.BlockSpec((None, 1, RET_DV), lambda h, b: (h, 0, 0)),
                  pl.BlockSpec((1, RET_DV), lambda h, b: (0, h)), st],
        out_specs=[pl.BlockSpec((None, 1, RET_DV), lambda h, b: (b, 0, h)), st],
        out_shape=[jax.ShapeDtypeStruct((nb, 1, RET_V_W), F32),
                   jax.ShapeDtypeStruct((nb, RET_HEADS, RET_DK, RET_DV), F32)],
        compiler_params=_params(("parallel", "arbitrary")),
        name="retention_sample",
    )(ps["rq"], ps["rqd"], ps["rkd"].T, ps["rk"], ps["rv"], ps["gates"], cdec, gn_w.reshape(1, RET_V_W), state)


PAGES_PER_STEP = 8


def _idx_sample_kernel(pt_ref, iq_ref, w_ref, new_ref, *rest):
    pages = rest[:PAGES_PER_STEP]
    sc_ref, scn_ref = rest[PAGES_PER_STEP], rest[PAGES_PER_STEP + 1]
    j = pl.program_id(1)
    hi, lo = _split_bf16(iq_ref[...])
    lhs = jnp.concatenate([hi + pltpu.roll(hi, IDX_DIM, 1), lo], axis=1).astype(BF16)
    w = w_ref[...] * IDX_SCALE

    def page_scores(x):
        xp = jnp.concatenate([x, jnp.zeros_like(x)], axis=1)
        khi, klo = _split_bf16(xp)
        rhs = jnp.concatenate([khi + pltpu.roll(klo, IDX_DIM, 1), khi], axis=1).astype(BF16)
        d = lax.dot_general(lhs, rhs, _NT, preferred_element_type=F32)
        return jnp.sum(w * jnp.maximum(d, 0.0), axis=0, keepdims=True)

    for i in range(PAGES_PER_STEP):
        sc_ref[:, i * PAGE_SIZE:(i + 1) * PAGE_SIZE] = page_scores(pages[i][...])

    @pl.when(j == pl.num_programs(1) - 1)
    def _():
        scn_ref[...] = page_scores(new_ref[...])


def _idx_sample(page_table, iq3, w3, new_pages, cache_idx):
    nsteps = N_PAGES // PAGES_PER_STEP

    def page_spec(i):
        return pl.BlockSpec((None, PAGE_SIZE, IDX_DIM),
                            lambda b, j, pt: (pt[b, j * PAGES_PER_STEP + i], 0, 0))

    grid_spec = pltpu.PrefetchScalarGridSpec(
        num_scalar_prefetch=1,
        grid=(DEC_BATCH, nsteps),
        in_specs=[pl.BlockSpec((None, IDX_HEADS, LANES), lambda b, j, pt: (b, 0, 0)),
                  pl.BlockSpec((None, IDX_HEADS, 1), lambda b, j, pt: (b, 0, 0)),
                  pl.BlockSpec((None, PAGE_SIZE, IDX_DIM), lambda b, j, pt: (b, 0, 0))]
                 + [page_spec(i) for i in range(PAGES_PER_STEP)],
        out_specs=[pl.BlockSpec((None, 1, PAGES_PER_STEP * PAGE_SIZE), lambda b, j, pt: (b, 0, j)),
                   pl.BlockSpec((None, 1, PAGE_SIZE), lambda b, j, pt: (b, 0, 0))],
    )
    return pl.pallas_call(
        _idx_sample_kernel,
        grid_spec=grid_spec,
        out_shape=[jax.ShapeDtypeStruct((DEC_BATCH, 1, PAST_LEN), F32),
                   jax.ShapeDtypeStruct((DEC_BATCH, 1, PAGE_SIZE), F32)],
        compiler_params=_params(("parallel", "arbitrary")),
        name="indexer_sample",
    )(page_table, iq3, w3, new_pages, *([cache_idx] * PAGES_PER_STEP))


def _select_sample_kernel(sc_ref, bias_ref, key_ref):
    _topk_mask(sc_ref, key_ref, bias_ref, jnp.full((1, LANES), PAST_LEN, I32), sc_ref.shape[0] // KEY_CHUNK)


def _select_sample(scores_t):
    return pl.pallas_call(
        _select_sample_kernel,
        out_shape=jax.ShapeDtypeStruct(scores_t.shape, F32),
        scratch_shapes=[pltpu.VMEM(scores_t.shape, I32)],
        compiler_params=pltpu.CompilerParams(vmem_limit_bytes=VMEM_LIMIT),
        name="select_sample",
    )(scores_t)


PAGE_ROWS = PAGE_SIZE * ATT_KV_HEADS


def _attn_sample_kernel(pt_ref, q_ref, bias_ref, biasn_ref, kn_ref, vn_ref, *rest):
    kp = rest[:PAGES_PER_STEP]
    vp = rest[PAGES_PER_STEP:2 * PAGES_PER_STEP]
    o_ref = rest[2 * PAGES_PER_STEP]
    m_ref, l_ref, acc_ref = rest[2 * PAGES_PER_STEP + 1:]
    j = pl.program_id(1)
    col = lax.broadcasted_iota(I32, (ATT_HEADS, PAGE_ROWS), 1)
    head = lax.broadcasted_iota(I32, (ATT_HEADS, PAGE_ROWS), 0)
    own = (col % ATT_KV_HEADS) == (head // (ATT_HEADS // ATT_KV_HEADS))
    c_exp = ATT_SCALE * math.log2(math.e)
    q = q_ref[...].astype(BF16)

    @pl.when(j == 0)
    def _():
        m_ref[...] = jnp.full_like(m_ref, NEG_BIG)
        l_ref[...] = jnp.zeros_like(l_ref)
        acc_ref[...] = jnp.zeros_like(acc_ref)

    logits = []
    for i in range(PAGES_PER_STEP):
        lg = lax.dot_general(q, kp[i][...].astype(BF16), _NT, preferred_element_type=F32)
        logits.append(jnp.where(own, lg + bias_ref[:, i * PAGE_ROWS:(i + 1) * PAGE_ROWS], NEG_BIG))
    m_old = m_ref[...]
    m_new = m_old
    for lg in logits:
        m_new = jnp.maximum(m_new, jnp.max(lg, axis=1, keepdims=True))
    alpha = jnp.exp2((m_old - m_new) * c_exp)
    l_new = alpha * l_ref[...]
    acc = alpha * acc_ref[...]
    for i in range(PAGES_PER_STEP):
        p = jnp.exp2((logits[i] - m_new) * c_exp)
        l_new = l_new + jnp.sum(p, axis=1, keepdims=True)
        acc = acc + jnp.dot(p.astype(BF16), vp[i][...].astype(BF16), preferred_element_type=F32)
    m_ref[...] = m_new
    l_ref[...] = l_new
    acc_ref[...] = acc

    @pl.when(j == pl.num_programs(1) - 1)
    def _():
        kn = kn_ref[...].astype(BF16).astype(F32)
        vn = vn_ref[...].astype(BF16).astype(F32)
        lgn = jnp.sum(q.astype(F32) * kn, axis=1, keepdims=True) + biasn_ref[:, 0:1]
        m_f = jnp.maximum(m_new, lgn)
        a = jnp.exp2((m_new - m_f) * c_exp)
        pn = jnp.exp2((lgn - m_f) * c_exp)
        o_ref[...] = (a * acc + pn.astype(BF16).astype(F32) * vn) / (a * l_new + pn)


def _attn_sample(page_table, q3, bias_rows, bias_new, k_new, v_new, cache_k, cache_v):
    nsteps = N_PAGES // PAGES_PER_STEP

    def page_spec(i):
        return pl.BlockSpec((None, PAGE_ROWS, ATT_HEAD_DIM),
                            lambda b, j, pt: (pt[b, j * PAGES_PER_STEP + i], 0, 0))

    head_rows = pl.BlockSpec((None, ATT_HEADS, ATT_HEAD_DIM), lambda b, j, pt: (b, 0, 0))
    grid_spec = pltpu.PrefetchScalarGridSpec(
        num_scalar_prefetch=1,
        grid=(DEC_BATCH, nsteps),
        in_specs=[head_rows,
                  pl.BlockSpec((None, 1, PAGES_PER_STEP * PAGE_ROWS), lambda b, j, pt: (b, 0, j)),
                  pl.BlockSpec((None, 1, PAGE_SIZE), lambda b, j, pt: (b, 0, 0)),
                  head_rows, head_rows]
                 + [page_spec(i) for i in range(PAGES_PER_STEP)] * 2,
        out_specs=head_rows,
        scratch_shapes=[pltpu.VMEM((ATT_HEADS, 1), F32), pltpu.VMEM((ATT_HEADS, 1), F32),
                        pltpu.VMEM((ATT_HEADS, ATT_HEAD_DIM), F32)],
    )
    return pl.pallas_call(
        _attn_sample_kernel,
        grid_spec=grid_spec,
        out_shape=jax.ShapeDtypeStruct((DEC_BATCH, ATT_HEADS, ATT_HEAD_DIM), F32),
        compiler_params=_params(("parallel", "arbitrary")),
        name="attention_sample",
    )(page_table, q3, bias_rows, bias_new, k_new, v_new,
      *([cache_k] * PAGES_PER_STEP), *([cache_v] * PAGES_PER_STEP))


def _finish(x, att, ret, gates, w, *, tm, tr):
    m = x.shape[0]
    mg = _merge(att, ret, w["att_proj"], w["ret_proj"], gates, tm, 512)
    y, = _matmul(mg, w["out"], _ep_plain, [], [_nat(m, D_MODEL, F32, tm, 512)], tm=tm, tn=512, name="w_out")
    h, hn = _post_attn(x, y, w["n_attn_post"], w["n_mlp_pre"], tr)
    u, = _matmul(hn, w["mlp_up"], _ep_relu2, [], [_nat(m, D_FF, BF16, tm, 512)], tm=tm, tn=512, name="mlp_up")
    d, = _matmul(u, w["mlp_down"], _ep_plain, [], [_nat(m, D_MODEL, F32, tm, 512)], tm=tm, tn=512, tk=4096,
                 name="mlp_down")
    return _post_mlp(h, d, w["n_mlp_post"], tr)


def kernel(x_prompt, x_sample, cache_k, cache_v, cache_idx_k, state_ret, page_table, norm_attn_pre,
           norm_attn_post, w_in, ret_gn_w, w_att_proj, w_ret_proj, w_out, norm_mlp_pre, w_mlp_up,
           w_mlp_down, norm_mlp_post):
    log_gamma = jnp.log1p(-jnp.exp2(-5.0 - jnp.arange(RET_HEADS, dtype=F32)))
    w_in0 = _pad_cast_w_in(w_in[0])
    w = {
        "att_proj": w_att_proj[0].astype(BF16), "ret_proj": w_ret_proj[0].astype(BF16),
        "out": w_out[0].astype(BF16), "mlp_up": w_mlp_up[0].astype(BF16), "mlp_down": w_mlp_down[0].astype(BF16),
        "n_attn_post": norm_attn_post[0], "n_mlp_pre": norm_mlp_pre[0], "n_mlp_post": norm_mlp_post[0],
    }
    gn_w = ret_gn_w[0]

    m_p = BATCH * SEQ
    xp = x_prompt.reshape(m_p, D_MODEL)
    xn = _rmsnorm_cast(xp, norm_attn_pre[0], 256)
    pos_p = jnp.arange(SEQ, dtype=I32)
    pr = _project(xn, w_in0, pos_p, pos_p % RET_CHUNK, float(RET_CHUNK), log_gamma, tm=1024, prompt=True)
    vt = pr["v_bf"].reshape(BATCH, SEQ // KEY_CHUNK, KEY_CHUNK, ATT_KV_HEADS, ATT_HEAD_DIM).transpose(0, 3, 1, 4, 2)
    att4 = _dsa_prompt(pr["iq"], pr["ikw"], pr["q"], pr["k_heads"], vt)
    att = att4.reshape(m_p // Q_BLOCK, ATT_HEADS, Q_BLOCK, ATT_HEAD_DIM).transpose(0, 2, 1, 3).reshape(m_p, ATT_Q_W)
    ci = jnp.arange(RET_CHUNK, dtype=F32)
    diff = ci[:, None] - ci[None, :]
    decay = jnp.where(diff >= 0, jnp.exp(log_gamma[:, None, None] * jnp.maximum(diff, 0.0)), 0.0)
    cdec_p = jnp.broadcast_to(jnp.exp(log_gamma * RET_CHUNK)[:, None, None], (RET_HEADS, 1, RET_DV))
    ret, s_prompt = _ret_prompt(pr, decay, cdec_p, gn_w)
    y_prompt = _finish(xp, att, ret, pr["gates"], w, tm=1024, tr=256).reshape(BATCH, SEQ, D_MODEL)

    nb = DEC_BATCH
    xs = x_sample.reshape(nb, D_MODEL)
    xns = _rmsnorm_cast(xs, norm_attn_pre[0], nb)
    pos_s = jnp.full((nb,), PAST_LEN, I32)
    ps = _project(xns, w_in0, pos_s, jnp.zeros((nb,), I32), 1.0, log_gamma, tm=nb, prompt=False)
    ik_new = ps["ikw"][:, :IDX_DIM]
    iq3 = jnp.pad(ps["iq"].reshape(nb, IDX_HEADS, IDX_DIM), ((0, 0), (0, 0), (0, LANES - IDX_DIM)))
    w3 = ps["ikw"][:, IDX_DIM:IDX_DIM + IDX_HEADS].reshape(nb, IDX_HEADS, 1)
    new_pages = jnp.pad(ik_new[:, None, :], ((0, 0), (0, PAGE_SIZE - 1), (0, 0)))
    sc_past, sc_new = _idx_sample(page_table, iq3, w3, new_pages, cache_idx_k[0])
    scores = jnp.concatenate([sc_past.reshape(nb, PAST_LEN), sc_new.reshape(nb, PAGE_SIZE)], axis=1)
    n_rows = -(-(PAST_LEN + PAGE_SIZE) // KEY_CHUNK) * KEY_CHUNK
    scores_t = jnp.pad(scores.T, ((0, n_rows - PAST_LEN - PAGE_SIZE), (0, LANES - nb)))
    bias = _select_sample(scores_t)[:PAST_LEN + PAGE_SIZE, :nb].T
    n_phys = cache_k.shape[1]
    group = ATT_HEADS // ATT_KV_HEADS
    att_s = _attn_sample(page_table, ps["q"].reshape(nb, ATT_HEADS, ATT_HEAD_DIM),
                         jnp.repeat(bias[:, :PAST_LEN], ATT_KV_HEADS, axis=1).reshape(nb, 1, N_PAGES * PAGE_ROWS),
                         bias[:, PAST_LEN:].reshape(nb, 1, PAGE_SIZE),
                         jnp.repeat(ps["k"].reshape(nb, ATT_KV_HEADS, ATT_HEAD_DIM), group, axis=1),
                         jnp.repeat(ps["v"].reshape(nb, ATT_KV_HEADS, ATT_HEAD_DIM), group, axis=1),
                         cache_k[0].reshape(n_phys, PAGE_ROWS, ATT_HEAD_DIM),
                         cache_v[0].reshape(n_phys, PAGE_ROWS, ATT_HEAD_DIM))
    cdec_s = jnp.broadcast_to(jnp.exp(log_gamma)[:, None, None], (RET_HEADS, 1, RET_DV))
    ret_s, s_sample = _ret_sample(ps, state_ret[0], cdec_s, gn_w)
    y_sample = _finish(xs, att_s.reshape(nb, ATT_Q_W).astype(BF16), ret_s.reshape(nb, RET_V_W).astype(BF16),
                       ps["gates"], w, tm=nb, tr=nb).reshape(nb, 1, D_MODEL)

    return (y_prompt, y_sample,
            pr["k"].reshape(1, BATCH, SEQ, ATT_KV_HEADS, ATT_HEAD_DIM),
            pr["v"].reshape(1, BATCH, SEQ, ATT_KV_HEADS, ATT_HEAD_DIM),
            pr["ikw"][:, :IDX_DIM].reshape(1, BATCH, SEQ, IDX_DIM),
            s_prompt[None],
            ps["k"].reshape(1, nb, 1, ATT_KV_HEADS, ATT_HEAD_DIM),
            ps["v"].reshape(1, nb, 1, ATT_KV_HEADS, ATT_HEAD_DIM),
            ik_new.reshape(1, nb, 1, IDX_DIM),
            s_sample[None])
```

```python
import functools
import math

import jax
import jax.numpy as jnp
import numpy as np
from jax import lax
from jax.experimental import pallas as pl
from jax.experimental.pallas import tpu as pltpu

F32 = jnp.float32
BF16 = jnp.bfloat16
I32 = jnp.int32

D_MODEL = 4096
BATCH = 4
SEQ = 2048
DEC_BATCH = 32
PAST_LEN = 8192
PAGE_SIZE = 128
N_PAGES = PAST_LEN // PAGE_SIZE
ATT_HEADS = 16
ATT_KV_HEADS = 8
ATT_HEAD_DIM = 128
ROPE_THETA = 500000.0
IDX_HEADS = 32
IDX_DIM = 64
TOPK = 256
RET_HEADS = 8
RET_DK = 256
RET_DV = 512
RET_THETA = 10000.0
RET_CHUNK = 128
D_FF = 4 * D_MODEL
NORM_EPS = 1e-6

ATT_Q_W = ATT_HEADS * ATT_HEAD_DIM
ATT_KV_W = ATT_KV_HEADS * ATT_HEAD_DIM
IDX_Q_W = IDX_HEADS * IDX_DIM
RET_QK_W = RET_HEADS * RET_DK
RET_V_W = RET_HEADS * RET_DV
SPLITS = (ATT_Q_W, ATT_KV_W, ATT_KV_W, IDX_Q_W, IDX_DIM, IDX_HEADS,
          RET_QK_W, RET_QK_W, RET_V_W, RET_V_W, D_MODEL, D_MODEL)
OFFS = tuple(int(v) for v in np.concatenate([[0], np.cumsum(SPLITS)]))

LANES = 128
SUBLANES = 8
Q_BLOCK = 128
KEY_CHUNK = 512
PAIRS_PER_STEP = 4
GROUPS_PER_STEP = 4
VMEM_LIMIT = 56 * 1024 * 1024

INT_MIN = -2 ** 31
NEG_BIG = -1e30
ATT_SCALE = ATT_HEAD_DIM ** -0.5
IDX_SCALE = (IDX_DIM ** -0.5) * (IDX_HEADS ** -0.5)

_NT = (((1,), (1,)), ((), ()))


def _params(sem):
    return pltpu.CompilerParams(dimension_semantics=sem, vmem_limit_bytes=VMEM_LIMIT)


def _sigmoid(x):
    return 1.0 / (1.0 + jnp.exp(-x))


def _rmsnorm_cast_kernel(x_ref, w_ref, o_ref):
    x = x_ref[...]
    y = x * lax.rsqrt(jnp.mean(x * x, axis=-1, keepdims=True) + NORM_EPS)
    o_ref[...] = (y * w_ref[...]).astype(o_ref.dtype)


def _rmsnorm_cast(x, w, tr):
    m, d = x.shape
    return pl.pallas_call(
        _rmsnorm_cast_kernel,
        grid=(m // tr,),
        in_specs=[pl.BlockSpec((tr, d), lambda i: (i, 0)), pl.BlockSpec((1, d), lambda i: (0, 0))],
        out_specs=pl.BlockSpec((tr, d), lambda i: (i, 0)),
        out_shape=jax.ShapeDtypeStruct((m, d), BF16),
        compiler_params=_params(("parallel",)),
        name="rmsnorm_cast",
    )(x, w.reshape(1, d))


def _post_attn_kernel(x_ref, y_ref, w1_ref, w2_ref, h_ref, hn_ref):
    y = y_ref[...]
    yn = y * lax.rsqrt(jnp.mean(y * y, axis=-1, keepdims=True) + NORM_EPS) * w1_ref[...]
    h = x_ref[...] + yn
    h_ref[...] = h
    hn = h * lax.rsqrt(jnp.mean(h * h, axis=-1, keepdims=True) + NORM_EPS) * w2_ref[...]
    hn_ref[...] = hn.astype(hn_ref.dtype)


def _post_attn(x, y, w1, w2, tr):
    m, d = x.shape
    row = pl.BlockSpec((tr, d), lambda i: (i, 0))
    vec = pl.BlockSpec((1, d), lambda i: (0, 0))
    return pl.pallas_call(
        _post_attn_kernel,
        grid=(m // tr,),
        in_specs=[row, row, vec, vec],
        out_specs=[row, row],
        out_shape=[jax.ShapeDtypeStruct((m, d), F32), jax.ShapeDtypeStruct((m, d), BF16)],
        compiler_params=_params(("parallel",)),
        name="post_attn_norm",
    )(x, y, w1.reshape(1, d), w2.reshape(1, d))


def _post_mlp_kernel(h_ref, d_ref, w_ref, o_ref):
    d = d_ref[...]
    dn = d * lax.rsqrt(jnp.mean(d * d, axis=-1, keepdims=True) + NORM_EPS) * w_ref[...]
    o_ref[...] = h_ref[...] + dn


def _post_mlp(h, d, w, tr):
    m, dm = h.shape
    row = pl.BlockSpec((tr, dm), lambda i: (i, 0))
    return pl.pallas_call(
        _post_mlp_kernel,
        grid=(m // tr,),
        in_specs=[row, row, pl.BlockSpec((1, dm), lambda i: (0, 0))],
        out_specs=row,
        out_shape=jax.ShapeDtypeStruct((m, dm), F32),
        compiler_params=_params(("parallel",)),
        name="post_mlp_norm",
    )(h, d, w.reshape(1, dm))


def _mm_kernel(*refs, n_extra, n_out, nk, epilogue):
    a_ref, b_ref = refs[0], refs[1]
    extra = refs[2:2 + n_extra]
    outs = refs[2 + n_extra:2 + n_extra + n_out]
    if nk == 1:
        epilogue(jnp.dot(a_ref[...], b_ref[...], preferred_element_type=F32), extra, outs)
        return
    acc_ref = refs[-1]
    k = pl.program_id(2)
    d = jnp.dot(a_ref[...], b_ref[...], preferred_element_type=F32)

    @pl.when(k == 0)
    def _():
        acc_ref[...] = d

    @pl.when((k > 0) & (k < nk - 1))
    def _():
        acc_ref[...] += d

    @pl.when(k == nk - 1)
    def _():
        epilogue(acc_ref[...] + d, extra, outs)


def _matmul(a, b, epilogue, extras, outs, *, tm, tn, tk=None, n=None, col0=0, name):
    m, kd = a.shape
    n = b.shape[1] if n is None else n
    tk = kd if tk is None else tk
    nk = kd // tk
    grid = (m // tm, n // tn, nk)

    def lift(f):
        return lambda i, j, k: f(i, j)

    in_specs = [pl.BlockSpec((tm, tk), lambda i, j, k: (i, k)),
                pl.BlockSpec((tk, tn), lambda i, j, k: (k, col0 + j))]
    in_specs += [pl.BlockSpec(bs, lift(im)) for _, bs, im in extras]
    out_specs = [pl.BlockSpec(bs, lift(im)) for _, _, bs, im in outs]
    out_shape = [jax.ShapeDtypeStruct(s, dt) for s, dt, _, _ in outs]
    scratch = [pltpu.VMEM((tm, tn), F32)] if nk > 1 else []
    res = pl.pallas_call(
        functools.partial(_mm_kernel, n_extra=len(extras), n_out=len(outs), nk=nk, epilogue=epilogue),
        grid=grid,
        in_specs=in_specs,
        out_specs=out_specs,
        out_shape=out_shape,
        scratch_shapes=scratch,
        compiler_params=_params(("parallel", "parallel", "arbitrary")),
        name=name,
    )(a, b, *[e[0] for e in extras])
    return res


def _mm_w32_kernel(*refs, n_extra, n_out, shift, valid_cols, epilogue):
    a_ref, w_ref = refs[0], refs[1]
    first = 3 if shift else 2
    extra = refs[first:first + n_extra]
    outs = refs[first + n_extra:first + n_extra + n_out]
    wbf_ref = refs[first + n_extra + n_out]
    tn = wbf_ref.shape[1]

    @pl.when(pl.program_id(1) == 0)
    def _():
        lane = lax.broadcasted_iota(I32, (1, LANES), 1)
        if shift == 0:
            w = w_ref[...]
            if valid_cols < tn:
                w = jnp.where(lax.broadcasted_iota(I32, (1, tn), 1) < valid_cols, w, 0.0)
            wbf_ref[...] = w.astype(BF16)
        else:
            r = LANES - shift
            blocks = [pltpu.roll(w_ref[:, c * LANES:(c + 1) * LANES], r, 1) for c in range(tn // LANES)]
            blocks.append(pltpu.roll(refs[2][...], r, 1))
            for c in range(tn // LANES):
                wbf_ref[:, c * LANES:(c + 1) * LANES] = jnp.where(lane < r, blocks[c], blocks[c + 1]).astype(BF16)

    epilogue(jnp.dot(a_ref[...], wbf_ref[...], preferred_element_type=F32), extra, outs)


def _matmul_w32(a, w, epilogue, extras, outs, *, tm, tn, n, col_off, valid_cols=None, name):
    m, kd = a.shape
    shift = col_off % LANES
    base = col_off - shift
    assert base % tn == 0 and n % tn == 0
    valid_cols = tn if valid_cols is None else valid_cols
    grid = (n // tn, m // tm)

    def lift(f):
        return lambda j, i: f(i, j)

    in_specs = [pl.BlockSpec((tm, kd), lambda j, i: (i, 0)),
                pl.BlockSpec((kd, tn), lambda j, i: (0, base // tn + j))]
    args = [a, w]
    if shift:
        in_specs.append(pl.BlockSpec((kd, LANES), lambda j, i: (0, base // LANES + (j + 1) * (tn // LANES))))
        args.append(w)
    in_specs += [pl.BlockSpec(bs, lift(im)) for _, bs, im in extras]
    out_specs = [pl.BlockSpec(bs, lift(im)) for _, _, bs, im in outs]
    out_specs.append(pl.BlockSpec((kd, tn), lambda j, i: (0, j)))
    out_shape = [jax.ShapeDtypeStruct(s, dt) for s, dt, _, _ in outs] + [jax.ShapeDtypeStruct((kd, n), BF16)]
    return pl.pallas_call(
        functools.partial(_mm_w32_kernel, n_extra=len(extras), n_out=len(outs), shift=shift,
                          valid_cols=valid_cols, epilogue=epilogue),
        grid=grid,
        in_specs=in_specs,
        out_specs=out_specs,
        out_shape=out_shape,
        compiler_params=_params(("arbitrary", "arbitrary")),
        name=name,
    )(*args, *[e[0] for e in extras])


def _nat(m, n, dt, tm, tn):
    return ((m, n), dt, (tm, tn), lambda i, j: (i, j))


def _rope_lanes(y, c, sm, sp, half):
    n = y.shape[-1]
    return y * c + pltpu.roll(y, n - half, 1) * sm + pltpu.roll(y, half, 1) * sp


def _ep_plain(acc, extra, outs):
    for o in outs:
        o[...] = acc.astype(o.dtype)


def _ep_rope_lanes(acc, extra, outs, *, half, blocked):
    c, sm, sp = extra[0][...], extra[1][...], extra[2][...]
    tm, tn = acc.shape
    for jj in range(tn // LANES):
        y = _rope_lanes(acc[:, jj * LANES:(jj + 1) * LANES], c, sm, sp, half)
        if blocked:
            for r in range(tm // Q_BLOCK):
                outs[0][r, jj] = y[r * Q_BLOCK:(r + 1) * Q_BLOCK].astype(outs[0].dtype)
        else:
            outs[0][:, jj * LANES:(jj + 1) * LANES] = y.astype(outs[0].dtype)
        if len(outs) > 1:
            outs[1][jj] = y.astype(outs[1].dtype)


def _ep_ret_qk(acc, extra, outs, *, scale):
    cos, sin, dec = extra[0][...], extra[1][...], extra[2]
    tn = acc.shape[1]
    for hh in range(tn // RET_DK):
        lo = hh * RET_DK
        x1 = acc[:, lo:lo + LANES]
        x2 = acc[:, lo + LANES:lo + 2 * LANES]
        o1 = x1 * cos - x2 * sin
        o2 = x2 * cos + x1 * sin
        if scale != 1.0:
            o1 = o1 * scale
            o2 = o2 * scale
        d = dec[:, hh * LANES:(hh + 1) * LANES]
        outs[0][:, lo:lo + LANES] = o1.astype(outs[0].dtype)
        outs[0][:, lo + LANES:lo + 2 * LANES] = o2.astype(outs[0].dtype)
        outs[1][:, lo:lo + LANES] = (o1 * d).astype(outs[1].dtype)
        outs[1][:, lo + LANES:lo + 2 * LANES] = (o2 * d).astype(outs[1].dtype)


def _merge_kernel(att_ref, ret_ref, wa_ref, wr_ref, ga_ref, gr_ref, o_ref):
    a = jnp.dot(att_ref[...], wa_ref[...], preferred_element_type=F32)
    r = jnp.dot(ret_ref[...], wr_ref[...], preferred_element_type=F32)
    o_ref[...] = (_sigmoid(ga_ref[...]) * a + _sigmoid(gr_ref[...]) * r).astype(o_ref.dtype)


def _merge(att, ret, wa, wr, gates, tm, tn):
    m = att.shape[0]
    nb = D_MODEL // tn
    return pl.pallas_call(
        _merge_kernel,
        grid=(m // tm, nb),
        in_specs=[
            pl.BlockSpec((tm, ATT_Q_W), lambda i, j: (i, 0)),
            pl.BlockSpec((tm, RET_V_W), lambda i, j: (i, 0)),
            pl.BlockSpec((ATT_Q_W, tn), lambda i, j: (0, j)),
            pl.BlockSpec((RET_V_W, tn), lambda i, j: (0, j)),
            pl.BlockSpec((tm, tn), lambda i, j: (i, nb + j)),
            pl.BlockSpec((tm, tn), lambda i, j: (i, 2 * nb + j)),
        ],
        out_specs=pl.BlockSpec((tm, tn), lambda i, j: (i, j)),
        out_shape=jax.ShapeDtypeStruct((m, D_MODEL), BF16),
        compiler_params=_params(("parallel", "parallel")),
        name="merge_proj",
    )(att, ret, wa, wr, gates, gates)


def _ep_relu2(acc, extra, outs):
    u = jnp.maximum(acc, 0.0)
    outs[0][...] = (u * u).astype(outs[0].dtype)


def _rope_lane_tables(pos, head_w, rot, theta):
    half = rot // 2
    inv_freq = jnp.exp(-math.log(theta) * jnp.arange(half, dtype=F32) / half)
    ang = pos.astype(F32)[:, None] * inv_freq[None, :]
    cos, sin = jnp.cos(ang), jnp.sin(ang)
    n = pos.shape[0]
    z_half = jnp.zeros((n, half), F32)
    rest1 = jnp.ones((n, head_w - rot), F32)
    rest0 = jnp.zeros((n, head_w - rot), F32)
    c = jnp.concatenate([cos, cos, rest1], axis=1)
    sm = jnp.concatenate([-sin, z_half, rest0], axis=1)
    sp = jnp.concatenate([z_half, sin, rest0], axis=1)
    rep = LANES // head_w
    return [jnp.tile(t, (1, rep)) for t in (c, sm, sp)]


def _ret_tables(pos, chunk_pos, chunk_len, log_gamma):
    half = RET_DK // 2
    inv_freq = jnp.exp(-math.log(RET_THETA) * jnp.arange(half, dtype=F32) / half)
    ang = pos.astype(F32)[:, None] * inv_freq[None, :]
    i = chunk_pos.astype(F32)[:, None]
    q_dec = jnp.exp(log_gamma[None, :] * (i + 1.0))
    k_dec = jnp.exp(log_gamma[None, :] * (chunk_len - 1.0 - i))
    return jnp.cos(ang), jnp.sin(ang), jnp.repeat(q_dec, LANES, axis=1), jnp.repeat(k_dec, LANES, axis=1)


PROJ_TN = 512


def _project(xn, w, pos, chunk_pos, chunk_len, log_gamma, *, tm, prompt):
    m = xn.shape[0]
    nrep = max(pos.shape[0] // tm, 1)
    wdt = BF16 if prompt else F32
    out = {"w_bf16": {}}

    def mm(name, a, b, epilogue, extras, outs, tn=PROJ_TN, valid_cols=None):
        n = -(-(OFFS[b] - OFFS[a]) // tn) * tn
        if prompt:
            *res, wb = _matmul_w32(xn, w, epilogue, extras, outs, tm=tm, tn=tn, n=n, col_off=OFFS[a],
                                   valid_cols=valid_cols, name=name)
            out["w_bf16"][name] = wb
            return res
        return _matmul(xn, w[name], epilogue, extras, outs, tm=tm, tn=tn, name=name)

    def tab(t, width=LANES, by_col=False):
        if by_col:
            return (t, (tm, width), lambda i, j: (i % nrep, j))
        return (t, (tm, width), lambda i, j: (i % nrep, 0))

    att_t = [tab(t) for t in _rope_lane_tables(pos, ATT_HEAD_DIM, ATT_HEAD_DIM // 4, ROPE_THETA)]
    idx_t = [tab(t) for t in _rope_lane_tables(pos, IDX_DIM, IDX_DIM // 4, ROPE_THETA)]
    r_cos, r_sin, q_dec, k_dec = _ret_tables(pos, chunk_pos, chunk_len, log_gamma)
    tn = PROJ_TN
    nqb = m // Q_BLOCK

    if prompt:
        o = [((nqb, ATT_HEADS, Q_BLOCK, LANES), BF16, (tm // Q_BLOCK, tn // LANES, Q_BLOCK, LANES),
              lambda i, j: (i, j, 0, 0))]
    else:
        o = [_nat(m, ATT_Q_W, F32, tm, tn)]
    out["q"], = mm("proj_q", 0, 1, functools.partial(_ep_rope_lanes, half=16, blocked=prompt), att_t, o)
    o = [_nat(m, ATT_KV_W, F32, tm, tn)]
    if prompt:
        per_b = SEQ // tm
        o.append(((BATCH, ATT_KV_HEADS, SEQ, LANES), BF16, (None, tn // LANES, tm, LANES),
                  lambda i, j: (i // per_b, j, i % per_b, 0)))
    res = mm("proj_k", 1, 2, functools.partial(_ep_rope_lanes, half=16, blocked=False), att_t, o)
    out["k"] = res[0]
    if prompt:
        out["k_heads"] = res[1]
    o = [_nat(m, ATT_KV_W, F32, tm, tn)]
    if prompt:
        o.append(_nat(m, ATT_KV_W, BF16, tm, tn))
    res = mm("proj_v", 2, 3, _ep_plain, [], o)
    out["v"] = res[0]
    if prompt:
        out["v_bf"] = res[1]
    if prompt:
        o = [((nqb, IDX_Q_W // LANES, Q_BLOCK, LANES), F32, (tm // Q_BLOCK, tn // LANES, Q_BLOCK, LANES),
              lambda i, j: (i, j, 0, 0))]
    else:
        o = [_nat(m, IDX_Q_W, F32, tm, tn)]
    out["iq"], = mm("proj_iq", 3, 4, functools.partial(_ep_rope_lanes, half=8, blocked=prompt), idx_t, o)
    ikw_t = [tab(t) for t in _rope_lane_tables(pos, LANES, IDX_DIM // 4, ROPE_THETA)]
    out["ikw"], = mm("proj_ikw", 4, 6, functools.partial(_ep_rope_lanes, half=8, blocked=False), ikw_t,
                     [_nat(m, LANES, F32, tm, LANES)], tn=LANES, valid_cols=IDX_DIM + IDX_HEADS)
    for nm, grp, dec, scale in (("rq", 6, q_dec, 1.0), ("rk", 7, k_dec, RET_DK ** -0.5)):
        res = mm("proj_" + nm, grp, grp + 1, functools.partial(_ep_ret_qk, scale=scale),
                 [tab(r_cos), tab(r_sin), tab(dec, tn // 2, by_col=True)],
                 [_nat(m, RET_QK_W, wdt, tm, tn), _nat(m, RET_QK_W, wdt, tm, tn)])
        out[nm], out[nm + "d"] = res
    out["rv"], = mm("proj_rv", 8, 9, _ep_plain, [], [_nat(m, RET_V_W, wdt, tm, tn)])
    out["gates"], = mm("proj_gates", 9, 12, _ep_plain, [], [_nat(m, 3 * D_MODEL, F32, tm, tn)])
    return out


def _sortable_key(score):
    kb = lax.bitcast_convert_type(score, I32)
    kb = jnp.where(kb == INT_MIN, 0, kb)
    return jnp.where(kb < 0, kb ^ 0x7FFFFFFF, kb)


def _tile_reduce(x, op):
    tiles = [x[i * SUBLANES:(i + 1) * SUBLANES] for i in range(x.shape[0] // SUBLANES)]
    while len(tiles) > 1:
        nxt = [op(tiles[i], tiles[i + 1]) for i in range(0, len(tiles) - 1, 2)]
        if len(tiles) % 2:
            nxt.append(tiles[-1])
        tiles = nxt
    return tiles[0]


def _topk_mask(sc_ref, key_ref, bias_ref, q_pos, n_chunks):
    def rows(c):
        r0 = pl.multiple_of(c * KEY_CHUNK, KEY_CHUNK)
        return pl.ds(r0, KEY_CHUNK), r0 + lax.broadcasted_iota(I32, (KEY_CHUNK, LANES), 0)

    def build(c, carry):
        sl, row = rows(c)
        key_ref[sl, :] = jnp.where(row <= q_pos, _sortable_key(sc_ref[sl, :]), INT_MIN)
        return carry

    lax.fori_loop(0, n_chunks, build, 0)

    def count(pred):
        def body(c, acc):
            sl, row = rows(c)
            return acc + _tile_reduce(jnp.where(pred(key_ref[sl, :], row), 1.0, 0.0), jnp.add)

        part = lax.fori_loop(0, n_chunks, body, jnp.zeros((SUBLANES, LANES), F32))
        return jnp.sum(part, axis=0, keepdims=True)

    t0 = jnp.where(count(lambda k, row: k >= 0) >= TOPK, 0, INT_MIN).astype(I32)

    def search(i, t):
        cand = t | lax.shift_left(jnp.int32(1), jnp.int32(30) - i)
        return jnp.where(count(lambda k, row: k >= cand) >= TOPK, cand, t)

    thr = lax.fori_loop(0, 31, search, t0)
    n_eq = count(lambda k, row: (row <= q_pos) & (k == thr))
    need = TOPK - count(lambda k, row: k > thr)

    def write(c, carry):
        sl, row = rows(c)
        k = key_ref[sl, :]
        bias_ref[sl, :] = jnp.where(k > thr, 0.0, jnp.where((row <= q_pos) & (k == thr), 0.0, NEG_BIG))
        return carry

    lax.fori_loop(0, n_chunks, write, 0)

    @pl.when(jnp.max(n_eq - need) > 0)
    def _():
        r_i = lax.broadcasted_iota(I32, (LANES, LANES), 0)
        c_i = lax.broadcasted_iota(I32, (LANES, LANES), 1)
        tri = jnp.where(c_i < r_i, 1.0, 0.0).astype(BF16)

        def chunk(c, off):
            r0 = pl.multiple_of(c * LANES, LANES)
            kc = key_ref[pl.ds(r0, LANES), :]
            rc = r0 + lax.broadcasted_iota(I32, (LANES, LANES), 0)
            e = (rc <= q_pos) & (kc == thr)
            ef = jnp.where(e, 1.0, 0.0)
            before = jnp.dot(tri, ef.astype(BF16), preferred_element_type=F32) + off
            keep = e & (before < need)
            bias_ref[pl.ds(r0, LANES), :] = jnp.where(kc > thr, 0.0, jnp.where(keep, 0.0, NEG_BIG))
            return off + jnp.sum(ef, axis=0, keepdims=True)

        lax.fori_loop(0, n_chunks * (KEY_CHUNK // LANES), chunk, jnp.zeros((1, LANES), F32))


def _split_bf16(x):
    hi = x.astype(BF16).astype(F32)
    return hi, x - hi


def _dsa_prompt_kernel(iq_ref, ikw_all_ref, ikw_q_ref, q_ref, k_ref, vt_ref, att_ref,
                       lhs_ref, wt_ref, sc_ref, key_ref, bias_ref):
    qb = pl.program_id(1)
    n_ck = (qb * Q_BLOCK + Q_BLOCK + KEY_CHUNK - 1) // KEY_CHUNK
    lane = lax.broadcasted_iota(I32, (1, LANES), 1)
    low = lane < IDX_DIM

    def chunk_rows(c):
        return pl.ds(pl.multiple_of(c * KEY_CHUNK, KEY_CHUNK), KEY_CHUNK)

    @pl.when(qb == 0)
    def _():
        x = jnp.where(low, ikw_all_ref[...], 0.0)
        hi, lo = _split_bf16(x)
        lhs_ref[:, :LANES] = (hi + pltpu.roll(lo, IDX_DIM, 1)).astype(BF16)
        lhs_ref[:, LANES:] = hi.astype(BF16)

    wt_ref[...] = ikw_q_ref[...].T * IDX_SCALE
    sc_ref[...] = jnp.zeros_like(sc_ref)

    def pairs_body(pq, carry):
        rhs, wts = [], []
        for u in range(PAIRS_PER_STEP):
            p = pq * PAIRS_PER_STEP + u
            hi, lo = _split_bf16(iq_ref[p])
            rhi = pltpu.roll(hi, IDX_DIM, 1)
            rlo = pltpu.roll(lo, IDX_DIM, 1)
            ra = jnp.concatenate([jnp.where(low, hi, rhi), jnp.where(low, lo, 0.0)], axis=1)
            rb = jnp.concatenate([jnp.where(low, rhi, hi), jnp.where(low, rlo, 0.0)], axis=1)
            rhs.append(jnp.concatenate([ra, rb], axis=0).astype(BF16))
            wts.append((wt_ref[pl.ds(IDX_DIM + 2 * p, 1), :], wt_ref[pl.ds(IDX_DIM + 2 * p + 1, 1), :]))

        def ck_body(c, inner):
            sl = chunk_rows(c)
            lhs = lhs_ref[sl, :]
            acc = sc_ref[sl, :]
            for rhs_t, (wa, wb) in zip(rhs, wts):
                d = lax.dot_general(lhs, rhs_t, _NT, preferred_element_type=F32)
                acc = acc + wa * jnp.maximum(d[:, :LANES], 0.0) + wb * jnp.maximum(d[:, LANES:], 0.0)
            sc_ref[sl, :] = acc
            return inner

        lax.fori_loop(0, n_ck, ck_body, 0)
        return carry

    lax.fori_loop(0, IDX_HEADS // 2 // PAIRS_PER_STEP, pairs_body, 0)

    _topk_mask(sc_ref, key_ref, bias_ref, qb * Q_BLOCK + lane, n_ck)

    c_exp = ATT_SCALE * math.log2(math.e)

    def groups_body(gq, carry):
        gs = [gq * GROUPS_PER_STEP + u for u in range(GROUPS_PER_STEP)]
        qqs = [jnp.concatenate([q_ref[2 * g], q_ref[2 * g + 1]], axis=0) for g in gs]

        def ck_body(c, state):
            sl = chunk_rows(c)
            bias = bias_ref[sl, :]
            new = []
            for g, qq, (m_old, l_old, acc) in zip(gs, qqs, state):
                lg = lax.dot_general(k_ref[g, sl, :], qq, _NT, preferred_element_type=F32)
                ls = [lg[:, hh * LANES:(hh + 1) * LANES] + bias for hh in range(2)]
                m_ck = jnp.concatenate(
                    [jnp.max(_tile_reduce(l, jnp.maximum), axis=0, keepdims=True) for l in ls], axis=1)
                m_new = jnp.maximum(m_old, m_ck)
                alpha = jnp.exp2((m_old - m_new) * c_exp)
                ps = [jnp.exp2((l - m_new[:, hh * LANES:(hh + 1) * LANES]) * c_exp) for hh, l in enumerate(ls)]
                l_ck = jnp.concatenate(
                    [jnp.sum(_tile_reduce(p, jnp.add), axis=0, keepdims=True) for p in ps], axis=1)
                pt = jnp.concatenate(ps, axis=1).astype(BF16)
                pv = jnp.dot(vt_ref[g, c], pt, preferred_element_type=F32)
                new.append((m_new, alpha * l_old + l_ck, alpha * acc + pv))
            return tuple(new)

        init = (jnp.full((1, 2 * LANES), NEG_BIG, F32), jnp.zeros((1, 2 * LANES), F32),
                jnp.zeros((ATT_HEAD_DIM, 2 * LANES), F32))
        final = lax.fori_loop(0, n_ck, ck_body, (init,) * GROUPS_PER_STEP)
        for g, (_, l_fin, acc) in zip(gs, final):
            o = acc / l_fin
            for hh in range(2):
                att_ref[2 * g + hh] = o[:, hh * LANES:(hh + 1) * LANES].T.astype(att_ref.dtype)
        return carry

    lax.fori_loop(0, ATT_KV_HEADS // GROUPS_PER_STEP, groups_body, 0)


def _dsa_prompt(iq, ikw, q, k_heads, vt):
    nqb = SEQ // Q_BLOCK
    n_pair = IDX_Q_W // LANES
    return pl.pallas_call(
        _dsa_prompt_kernel,
        grid=(BATCH, nqb),
        in_specs=[
            pl.BlockSpec((None, n_pair, Q_BLOCK, LANES), lambda b, i: (b * nqb + i, 0, 0, 0)),
            pl.BlockSpec((SEQ, LANES), lambda b, i: (b, 0)),
            pl.BlockSpec((Q_BLOCK, LANES), lambda b, i: (b * nqb + i, 0)),
            pl.BlockSpec((None, ATT_HEADS, Q_BLOCK, LANES), lambda b, i: (b * nqb + i, 0, 0, 0)),
            pl.BlockSpec((None, ATT_KV_HEADS, SEQ, LANES), lambda b, i: (b, 0, 0, 0)),
            pl.BlockSpec((None, ATT_KV_HEADS, SEQ // KEY_CHUNK, LANES, KEY_CHUNK), lambda b, i: (b, 0, 0, 0, 0)),
        ],
        out_specs=pl.BlockSpec((None, ATT_HEADS, Q_BLOCK, LANES), lambda b, i: (b * nqb + i, 0, 0, 0)),
        out_shape=jax.ShapeDtypeStruct((BATCH * nqb, ATT_HEADS, Q_BLOCK, LANES), BF16),
        scratch_shapes=[
            pltpu.VMEM((SEQ, 2 * LANES), BF16),
            pltpu.VMEM((LANES, LANES), F32),
            pltpu.VMEM((SEQ, LANES), F32),
            pltpu.VMEM((SEQ, LANES), I32),
            pltpu.VMEM((SEQ, LANES), F32),
        ],
        compiler_params=_params(("parallel", "arbitrary")),
        name="dsa_prompt",
    )(iq, ikw, ikw, q, k_heads, vt)


def _group_norm_gate(o, gn_w, rg):
    of = o * lax.rsqrt(jnp.mean(o * o, axis=-1, keepdims=True) + NORM_EPS) * gn_w
    return of * (rg * _sigmoid(rg))


def _ret_prompt_kernel(q_ref, qd_ref, k_ref, kd_ref, v_ref, rg_ref, decay_ref, cdec_ref, gnw_ref,
                       ret_ref, s_ref):
    c = pl.program_id(1)

    @pl.when(c == 0)
    def _():
        s_ref[...] = jnp.zeros_like(s_ref)

    for h in range(RET_HEADS):
        qk = slice(h * RET_DK, (h + 1) * RET_DK)
        vv = slice(h * RET_DV, (h + 1) * RET_DV)
        s = s_ref[h]
        v = v_ref[:, vv]
        scores = lax.dot_general(q_ref[:, qk], k_ref[:, qk], _NT, preferred_element_type=F32) * decay_ref[h]
        o = (jnp.dot(scores.astype(BF16), v, preferred_element_type=F32)
             + jnp.dot(qd_ref[:, qk], s.astype(BF16), preferred_element_type=F32))
        kv = lax.dot_general(kd_ref[:, qk], v, (((0,), (0,)), ((), ())), preferred_element_type=F32)
        s_ref[h] = cdec_ref[h] * s + kv
        ret_ref[:, vv] = _group_norm_gate(o, gnw_ref[:, vv], rg_ref[:, vv]).astype(ret_ref.dtype)


def _ret_prompt(pr, decay, cdec, gn_w):
    nc = SEQ // RET_CHUNK
    qk = pl.BlockSpec((RET_CHUNK, RET_QK_W), lambda b, c: (b * nc + c, 0))
    vv = pl.BlockSpec((RET_CHUNK, RET_V_W), lambda b, c: (b * nc + c, 0))
    return pl.pallas_call(
        _ret_prompt_kernel,
        grid=(BATCH, nc),
        in_specs=[qk, qk, qk, qk, vv, vv,
                  pl.BlockSpec((RET_HEADS, RET_CHUNK, RET_CHUNK), lambda b, c: (0, 0, 0)),
                  pl.BlockSpec((RET_HEADS, 1, RET_DV), lambda b, c: (0, 0, 0)),
                  pl.BlockSpec((1, RET_V_W), lambda b, c: (0, 0))],
        out_specs=[vv, pl.BlockSpec((None, RET_HEADS, RET_DK, RET_DV), lambda b, c: (b, 0, 0, 0))],
        out_shape=[jax.ShapeDtypeStruct((BATCH * SEQ, RET_V_W), BF16),
                   jax.ShapeDtypeStruct((BATCH, RET_HEADS, RET_DK, RET_DV), F32)],
        compiler_params=_params(("parallel", "arbitrary")),
        name="retention_prompt",
    )(pr["rq"], pr["rqd"], pr["rk"], pr["rkd"], pr["rv"], pr["gates"], decay, cdec, gn_w.reshape(1, RET_V_W))


def _ret_sample_kernel(q_ref, qd_ref, kt_ref, k_ref, v_ref, rg_ref, cdec_ref, gnw_ref, s_ref, ret_ref, so_ref):
    b = pl.program_id(0)
    nb = kt_ref.shape[1]
    onehot = lax.broadcasted_iota(I32, (1, nb), 1) == b
    for h in range(RET_HEADS):
        qk_sl = slice(h * RET_DK, (h + 1) * RET_DK)
        vv = slice(h * RET_DV, (h + 1) * RET_DV)
        s = s_ref[h]
        v = v_ref[pl.ds(b, 1), vv]
        q = q_ref[pl.ds(b, 1), qk_sl]
        k = k_ref[pl.ds(b, 1), qk_sl]
        qk = jnp.sum(q.astype(BF16).astype(F32) * k.astype(BF16).astype(F32), axis=-1, keepdims=True)
        qd8 = jnp.broadcast_to(qd_ref[pl.ds(b, 1), qk_sl], (16, RET_DK)).astype(BF16)
        o = qk.astype(BF16).astype(F32) * v.astype(BF16).astype(F32) \
            + jnp.dot(qd8, s.astype(BF16), preferred_element_type=F32)[0:1]
        k_col = jnp.sum(jnp.where(onehot, kt_ref[qk_sl, :], 0.0), axis=1, keepdims=True)
        so_ref[h] = cdec_ref[h] * s + k_col * v
        ret_ref[:, vv] = _group_norm_gate(o, gnw_ref[:, vv], rg_ref[pl.ds(b, 1), vv])


def _ret_sample(ps, state, cdec, gn_w):
    nb = DEC_BATCH
    qk = pl.BlockSpec((nb, RET_QK_W), lambda b: (0, 0))
    vv = pl.BlockSpec((nb, RET_V_W), lambda b: (0, 0))
    st = pl.BlockSpec((None, RET_HEADS, RET_DK, RET_DV), lambda b: (b, 0, 0, 0))
    return pl.pallas_call(
        _ret_sample_kernel,
        grid=(nb,),
        in_specs=[qk, qk, pl.BlockSpec((RET_QK_W, nb), lambda b: (0, 0)), qk, vv, vv,
                  pl.BlockSpec((RET_HEADS, 1, RET_DV), lambda b: (0, 0, 0)),
                  pl.BlockSpec((1, RET_V_W), lambda b: (0, 0)), st],
        out_specs=[pl.BlockSpec((None, 1, RET_V_W), lambda b: (b, 0, 0)), st],
        out_shape=[jax.ShapeDtypeStruct((nb, 1, RET_V_W), F32),
                   jax.ShapeDtypeStruct((nb, RET_HEADS, RET_DK, RET_DV), F32)],
        compiler_params=_params(("arbitrary",)),
        name="retention_sample",
    )(ps["rq"], ps["rqd"], ps["rkd"].T, ps["rk"], ps["rv"], ps["gates"], cdec, gn_w.reshape(1, RET_V_W), state)


PAGES_PER_STEP = 8
IDX_PAGES_PER_STEP = 32


def _idx_sample_kernel(pt_ref, iq_ref, w_ref, new_ref, *rest):
    pages = rest[:IDX_PAGES_PER_STEP]
    sc_ref, scn_ref = rest[IDX_PAGES_PER_STEP], rest[IDX_PAGES_PER_STEP + 1]
    j = pl.program_id(1)
    hi, lo = _split_bf16(iq_ref[...])
    lhs = jnp.concatenate([hi + pltpu.roll(hi, IDX_DIM, 1), lo], axis=1).astype(BF16)
    w = w_ref[...] * IDX_SCALE

    def page_scores(x):
        xp = jnp.concatenate([x, jnp.zeros_like(x)], axis=1)
        khi, klo = _split_bf16(xp)
        rhs = jnp.concatenate([khi + pltpu.roll(klo, IDX_DIM, 1), khi], axis=1).astype(BF16)
        d = lax.dot_general(lhs, rhs, _NT, preferred_element_type=F32)
        return jnp.sum(w * jnp.maximum(d, 0.0), axis=0, keepdims=True)

    for i in range(IDX_PAGES_PER_STEP):
        sc_ref[:, i * PAGE_SIZE:(i + 1) * PAGE_SIZE] = page_scores(pages[i][...])

    @pl.when(j == pl.num_programs(1) - 1)
    def _():
        scn_ref[...] = page_scores(new_ref[...])


def _idx_sample(page_table, iq3, w3, new_pages, cache_idx):
    nsteps = N_PAGES // IDX_PAGES_PER_STEP

    def page_spec(i):
        return pl.BlockSpec((None, PAGE_SIZE, IDX_DIM),
                            lambda b, j, pt: (pt[b, j * IDX_PAGES_PER_STEP + i], 0, 0))

    grid_spec = pltpu.PrefetchScalarGridSpec(
        num_scalar_prefetch=1,
        grid=(DEC_BATCH, nsteps),
        in_specs=[pl.BlockSpec((None, IDX_HEADS, LANES), lambda b, j, pt: (b, 0, 0)),
                  pl.BlockSpec((None, IDX_HEADS, 1), lambda b, j, pt: (b, 0, 0)),
                  pl.BlockSpec((None, PAGE_SIZE, IDX_DIM), lambda b, j, pt: (b, 0, 0))]
                 + [page_spec(i) for i in range(IDX_PAGES_PER_STEP)],
        out_specs=[pl.BlockSpec((None, 1, IDX_PAGES_PER_STEP * PAGE_SIZE), lambda b, j, pt: (b, 0, j)),
                   pl.BlockSpec((None, 1, PAGE_SIZE), lambda b, j, pt: (b, 0, 0))],
    )
    return pl.pallas_call(
        _idx_sample_kernel,
        grid_spec=grid_spec,
        out_shape=[jax.ShapeDtypeStruct((DEC_BATCH, 1, PAST_LEN), F32),
                   jax.ShapeDtypeStruct((DEC_BATCH, 1, PAGE_SIZE), F32)],
        compiler_params=_params(("parallel", "arbitrary")),
        name="indexer_sample",
    )(page_table, iq3, w3, new_pages, *([cache_idx] * IDX_PAGES_PER_STEP))


def _select_sample_kernel(sc_ref, bias_ref, key_ref):
    _topk_mask(sc_ref, key_ref, bias_ref, jnp.full((1, LANES), PAST_LEN, I32), sc_ref.shape[0] // KEY_CHUNK)


def _select_sample(scores_t):
    return pl.pallas_call(
        _select_sample_kernel,
        out_shape=jax.ShapeDtypeStruct(scores_t.shape, F32),
        scratch_shapes=[pltpu.VMEM(scores_t.shape, I32)],
        compiler_params=pltpu.CompilerParams(vmem_limit_bytes=VMEM_LIMIT),
        name="select_sample",
    )(scores_t)


PAGE_ROWS = PAGE_SIZE * ATT_KV_HEADS


def _attn_sample_kernel(pt_ref, q_ref, bias_ref, biasn_ref, kn_ref, vn_ref, *rest):
    kp = rest[:PAGES_PER_STEP]
    vp = rest[PAGES_PER_STEP:2 * PAGES_PER_STEP]
    o_ref = rest[2 * PAGES_PER_STEP]
    m_ref, l_ref, acc_ref = rest[2 * PAGES_PER_STEP + 1:]
    j = pl.program_id(1)
    col = lax.broadcasted_iota(I32, (ATT_HEADS, PAGE_ROWS), 1)
    head = lax.broadcasted_iota(I32, (ATT_HEADS, PAGE_ROWS), 0)
    own = (col % ATT_KV_HEADS) == (head // (ATT_HEADS // ATT_KV_HEADS))
    c_exp = ATT_SCALE * math.log2(math.e)
    q = q_ref[...].astype(BF16)

    @pl.when(j == 0)
    def _():
        m_ref[...] = jnp.full_like(m_ref, NEG_BIG)
        l_ref[...] = jnp.zeros_like(l_ref)
        acc_ref[...] = jnp.zeros_like(acc_ref)

    logits = []
    for i in range(PAGES_PER_STEP):
        lg = lax.dot_general(q, kp[i][...].astype(BF16), _NT, preferred_element_type=F32)
        logits.append(jnp.where(own, lg + bias_ref[:, i * PAGE_ROWS:(i + 1) * PAGE_ROWS], NEG_BIG))
    m_old = m_ref[...]
    m_new = m_old
    for lg in logits:
        m_new = jnp.maximum(m_new, jnp.max(lg, axis=1, keepdims=True))
    alpha = jnp.exp2((m_old - m_new) * c_exp)
    l_new = alpha * l_ref[...]
    acc = alpha * acc_ref[...]
    for i in range(PAGES_PER_STEP):
        p = jnp.exp2((logits[i] - m_new) * c_exp)
        l_new = l_new + jnp.sum(p, axis=1, keepdims=True)
        acc = acc + jnp.dot(p.astype(BF16), vp[i][...].astype(BF16), preferred_element_type=F32)
    m_ref[...] = m_new
    l_ref[...] = l_new
    acc_ref[...] = acc

    @pl.when(j == pl.num_programs(1) - 1)
    def _():
        kn = kn_ref[...].astype(BF16).astype(F32)
        vn = vn_ref[...].astype(BF16).astype(F32)
        lgn = jnp.sum(q.astype(F32) * kn, axis=1, keepdims=True) + biasn_ref[:, 0:1]
        m_f = jnp.maximum(m_new, lgn)
        a = jnp.exp2((m_new - m_f) * c_exp)
        pn = jnp.exp2((lgn - m_f) * c_exp)
        o_ref[...] = (a * acc + pn.astype(BF16).astype(F32) * vn) / (a * l_new + pn)


def _attn_sample(page_table, q3, bias_rows, bias_new, k_new, v_new, cache_k, cache_v):
    nsteps = N_PAGES // PAGES_PER_STEP

    def page_spec(i):
        return pl.BlockSpec((None, PAGE_ROWS, ATT_HEAD_DIM),
                            lambda b, j, pt: (pt[b, j * PAGES_PER_STEP + i], 0, 0))

    head_rows = pl.BlockSpec((None, ATT_HEADS, ATT_HEAD_DIM), lambda b, j, pt: (b, 0, 0))
    grid_spec = pltpu.PrefetchScalarGridSpec(
        num_scalar_prefetch=1,
        grid=(DEC_BATCH, nsteps),
        in_specs=[head_rows,
                  pl.BlockSpec((None, 1, PAGES_PER_STEP * PAGE_ROWS), lambda b, j, pt: (b, 0, j)),
                  pl.BlockSpec((None, 1, PAGE_SIZE), lambda b, j, pt: (b, 0, 0)),
                  head_rows, head_rows]
                 + [page_spec(i) for i in range(PAGES_PER_STEP)] * 2,
        out_specs=head_rows,
        scratch_shapes=[pltpu.VMEM((ATT_HEADS, 1), F32), pltpu.VMEM((ATT_HEADS, 1), F32),
                        pltpu.VMEM((ATT_HEADS, ATT_HEAD_DIM), F32)],
    )
    return pl.pallas_call(
        _attn_sample_kernel,
        grid_spec=grid_spec,
        out_shape=jax.ShapeDtypeStruct((DEC_BATCH, ATT_HEADS, ATT_HEAD_DIM), F32),
        compiler_params=_params(("parallel", "arbitrary")),
        name="attention_sample",
    )(page_table, q3, bias_rows, bias_new, k_new, v_new,
      *([cache_k] * PAGES_PER_STEP), *([cache_v] * PAGES_PER_STEP))


def _finish(x, att, ret, gates, w, *, tm, tr):
    m = x.shape[0]
    mg = _merge(att, ret, w["att_proj"], w["ret_proj"], gates, tm, 512)
    y, = _matmul(mg, w["out"], _ep_plain, [], [_nat(m, D_MODEL, F32, tm, 512)], tm=tm, tn=512, name="w_out")
    h, hn = _post_attn(x, y, w["n_attn_post"], w["n_mlp_pre"], tr)
    up_out = [_nat(m, D_FF, BF16, tm, 512)]
    if w["mlp_up"].dtype == F32:
        u, w["mlp_up"] = _matmul_w32(hn, w["mlp_up"], _ep_relu2, [], up_out, tm=tm, tn=512, n=D_FF, col_off=0,
                                     name="mlp_up")
    else:
        u, = _matmul(hn, w["mlp_up"], _ep_relu2, [], up_out, tm=tm, tn=512, name="mlp_up")
    d, = _matmul(u, w["mlp_down"], _ep_plain, [], [_nat(m, D_MODEL, F32, tm, 512)], tm=tm, tn=512, tk=4096,
                 name="mlp_down")
    return _post_mlp(h, d, w["n_mlp_post"], tr)


def kernel(x_prompt, x_sample, cache_k, cache_v, cache_idx_k, state_ret, page_table, norm_attn_pre,
           norm_attn_post, w_in, ret_gn_w, w_att_proj, w_ret_proj, w_out, norm_mlp_pre, w_mlp_up,
           w_mlp_down, norm_mlp_post):
    log_gamma = jnp.log1p(-jnp.exp2(-5.0 - jnp.arange(RET_HEADS, dtype=F32)))
    w_in0 = w_in[0]
    w = {
        "att_proj": w_att_proj[0].astype(BF16), "ret_proj": w_ret_proj[0].astype(BF16),
        "out": w_out[0].astype(BF16), "mlp_up": w_mlp_up[0], "mlp_down": w_mlp_down[0].astype(BF16),
        "n_attn_post": norm_attn_post[0], "n_mlp_pre": norm_mlp_pre[0], "n_mlp_post": norm_mlp_post[0],
    }
    gn_w = ret_gn_w[0]

    m_p = BATCH * SEQ
    xp = x_prompt.reshape(m_p, D_MODEL)
    xn = _rmsnorm_cast(xp, norm_attn_pre[0], 256)
    pos_p = jnp.arange(SEQ, dtype=I32)
    pr = _project(xn, w_in0, pos_p, pos_p % RET_CHUNK, float(RET_CHUNK), log_gamma, tm=1024, prompt=True)
    vt = pr["v_bf"].reshape(BATCH, SEQ // KEY_CHUNK, KEY_CHUNK, ATT_KV_HEADS, ATT_HEAD_DIM).transpose(0, 3, 1, 4, 2)
    att4 = _dsa_prompt(pr["iq"], pr["ikw"], pr["q"], pr["k_heads"], vt)
    att = att4.reshape(m_p // Q_BLOCK, ATT_HEADS, Q_BLOCK, ATT_HEAD_DIM).transpose(0, 2, 1, 3).reshape(m_p, ATT_Q_W)
    ci = jnp.arange(RET_CHUNK, dtype=F32)
    diff = ci[:, None] - ci[None, :]
    decay = jnp.where(diff >= 0, jnp.exp(log_gamma[:, None, None] * jnp.maximum(diff, 0.0)), 0.0)
    cdec_p = jnp.broadcast_to(jnp.exp(log_gamma * RET_CHUNK)[:, None, None], (RET_HEADS, 1, RET_DV))
    ret, s_prompt = _ret_prompt(pr, decay, cdec_p, gn_w)
    y_prompt = _finish(xp, att, ret, pr["gates"], w, tm=1024, tr=256).reshape(BATCH, SEQ, D_MODEL)

    nb = DEC_BATCH
    xs = x_sample.reshape(nb, D_MODEL)
    xns = _rmsnorm_cast(xs, norm_attn_pre[0], nb)
    pos_s = jnp.full((nb,), PAST_LEN, I32)
    ps = _project(xns, pr["w_bf16"], pos_s, jnp.zeros((nb,), I32), 1.0, log_gamma, tm=nb, prompt=False)
    ik_new = ps["ikw"][:, :IDX_DIM]
    iq3 = jnp.pad(ps["iq"].reshape(nb, IDX_HEADS, IDX_DIM), ((0, 0), (0, 0), (0, LANES - IDX_DIM)))
    w3 = ps["ikw"][:, IDX_DIM:IDX_DIM + IDX_HEADS].reshape(nb, IDX_HEADS, 1)
    new_pages = jnp.pad(ik_new[:, None, :], ((0, 0), (0, PAGE_SIZE - 1), (0, 0)))
    sc_past, sc_new = _idx_sample(page_table, iq3, w3, new_pages, cache_idx_k[0])
    scores = jnp.concatenate([sc_past.reshape(nb, PAST_LEN), sc_new.reshape(nb, PAGE_SIZE)], axis=1)
    n_rows = -(-(PAST_LEN + PAGE_SIZE) // KEY_CHUNK) * KEY_CHUNK
    scores_t = jnp.pad(scores.T, ((0, n_rows - PAST_LEN - PAGE_SIZE), (0, LANES - nb)))
    bias = _select_sample(scores_t)[:PAST_LEN + PAGE_SIZE, :nb].T
    n_phys = cache_k.shape[1]
    group = ATT_HEADS // ATT_KV_HEADS
    att_s = _attn_sample(page_table, ps["q"].reshape(nb, ATT_HEADS, ATT_HEAD_DIM),
                         jnp.repeat(bias[:, :PAST_LEN], ATT_KV_HEADS, axis=1).reshape(nb, 1, N_PAGES * PAGE_ROWS),
                         bias[:, PAST_LEN:].reshape(nb, 1, PAGE_SIZE),
                         jnp.repeat(ps["k"].reshape(nb, ATT_KV_HEADS, ATT_HEAD_DIM), group, axis=1),
                         jnp.repeat(ps["v"].reshape(nb, ATT_KV_HEADS, ATT_HEAD_DIM), group, axis=1),
                         cache_k[0].reshape(n_phys, PAGE_ROWS, ATT_HEAD_DIM),
                         cache_v[0].reshape(n_phys, PAGE_ROWS, ATT_HEAD_DIM))
    cdec_s = jnp.broadcast_to(jnp.exp(log_gamma)[:, None, None], (RET_HEADS, 1, RET_DV))
    ret_s, s_sample = _ret_sample(ps, state_ret[0], cdec_s, gn_w)
    y_sample = _finish(xs, att_s.reshape(nb, ATT_Q_W).astype(BF16), ret_s.reshape(nb, RET_V_W).astype(BF16),
                       ps["gates"], w, tm=nb, tr=nb).reshape(nb, 1, D_MODEL)

    return (y_prompt, y_sample,
            pr["k"].reshape(1, BATCH, SEQ, ATT_KV_HEADS, ATT_HEAD_DIM),
            pr["v"].reshape(1, BATCH, SEQ, ATT_KV_HEADS, ATT_HEAD_DIM),
            pr["ikw"][:, :IDX_DIM].reshape(1, BATCH, SEQ, IDX_DIM),
            s_prompt[None],
            ps["k"].reshape(1, nb, 1, ATT_KV_HEADS, ATT_HEAD_DIM),
            ps["v"].reshape(1, nb, 1, ATT_KV_HEADS, ATT_HEAD_DIM),
            ik_new.reshape(1, nb, 1, IDX_DIM),
            s_sample[None])
```

```python
import functools
import math

import jax
import jax.numpy as jnp
import numpy as np
from jax import lax
from jax.experimental import pallas as pl
from jax.experimental.pallas import tpu as pltpu

F32 = jnp.float32
BF16 = jnp.bfloat16
I32 = jnp.int32

D_MODEL = 4096
BATCH = 4
SEQ = 2048
DEC_BATCH = 32
PAST_LEN = 8192
PAGE_SIZE = 128
N_PAGES = PAST_LEN // PAGE_SIZE
ATT_HEADS = 16
ATT_KV_HEADS = 8
ATT_HEAD_DIM = 128
ROPE_THETA = 500000.0
IDX_HEADS = 32
IDX_DIM = 64
TOPK = 256
RET_HEADS = 8
RET_DK = 256
RET_DV = 512
RET_THETA = 10000.0
RET_CHUNK = 128
D_FF = 4 * D_MODEL
NORM_EPS = 1e-6

ATT_Q_W = ATT_HEADS * ATT_HEAD_DIM
ATT_KV_W = ATT_KV_HEADS * ATT_HEAD_DIM
IDX_Q_W = IDX_HEADS * IDX_DIM
RET_QK_W = RET_HEADS * RET_DK
RET_V_W = RET_HEADS * RET_DV
SPLITS = (ATT_Q_W, ATT_KV_W, ATT_KV_W, IDX_Q_W, IDX_DIM, IDX_HEADS,
          RET_QK_W, RET_QK_W, RET_V_W, RET_V_W, D_MODEL, D_MODEL)
OFFS = tuple(int(v) for v in np.concatenate([[0], np.cumsum(SPLITS)]))

LANES = 128
SUBLANES = 8
Q_BLOCK = 128
KEY_CHUNK = 512
PAIRS_PER_STEP = 4
GROUPS_PER_STEP = 4
VMEM_LIMIT = 56 * 1024 * 1024

INT_MIN = -2 ** 31
NEG_BIG = -1e30
ATT_SCALE = ATT_HEAD_DIM ** -0.5
IDX_SCALE = (IDX_DIM ** -0.5) * (IDX_HEADS ** -0.5)

_NT = (((1,), (1,)), ((), ()))


def _params(sem):
    return pltpu.CompilerParams(dimension_semantics=sem, vmem_limit_bytes=VMEM_LIMIT)


def _sigmoid(x):
    return 1.0 / (1.0 + jnp.exp(-x))


def _rmsnorm_cast_kernel(x_ref, w_ref, o_ref):
    x = x_ref[...]
    y = x * lax.rsqrt(jnp.mean(x * x, axis=-1, keepdims=True) + NORM_EPS)
    o_ref[...] = (y * w_ref[...]).astype(o_ref.dtype)


def _rmsnorm_cast(x, w, tr):
    m, d = x.shape
    return pl.pallas_call(
        _rmsnorm_cast_kernel,
        grid=(m // tr,),
        in_specs=[pl.BlockSpec((tr, d), lambda i: (i, 0)), pl.BlockSpec((1, d), lambda i: (0, 0))],
        out_specs=pl.BlockSpec((tr, d), lambda i: (i, 0)),
        out_shape=jax.ShapeDtypeStruct((m, d), BF16),
        compiler_params=_params(("parallel",)),
        name="rmsnorm_cast",
    )(x, w.reshape(1, d))


def _post_attn_kernel(x_ref, y_ref, w1_ref, w2_ref, h_ref, hn_ref):
    y = y_ref[...]
    yn = y * lax.rsqrt(jnp.mean(y * y, axis=-1, keepdims=True) + NORM_EPS) * w1_ref[...]
    h = x_ref[...] + yn
    h_ref[...] = h
    hn = h * lax.rsqrt(jnp.mean(h * h, axis=-1, keepdims=True) + NORM_EPS) * w2_ref[...]
    hn_ref[...] = hn.astype(hn_ref.dtype)


def _post_attn(x, y, w1, w2, tr):
    m, d = x.shape
    row = pl.BlockSpec((tr, d), lambda i: (i, 0))
    vec = pl.BlockSpec((1, d), lambda i: (0, 0))
    return pl.pallas_call(
        _post_attn_kernel,
        grid=(m // tr,),
        in_specs=[row, row, vec, vec],
        out_specs=[row, row],
        out_shape=[jax.ShapeDtypeStruct((m, d), F32), jax.ShapeDtypeStruct((m, d), BF16)],
        compiler_params=_params(("parallel",)),
        name="post_attn_norm",
    )(x, y, w1.reshape(1, d), w2.reshape(1, d))


def _post_mlp_kernel(h_ref, d_ref, w_ref, o_ref):
    d = d_ref[...]
    dn = d * lax.rsqrt(jnp.mean(d * d, axis=-1, keepdims=True) + NORM_EPS) * w_ref[...]
    o_ref[...] = h_ref[...] + dn


def _post_mlp(h, d, w, tr):
    m, dm = h.shape
    row = pl.BlockSpec((tr, dm), lambda i: (i, 0))
    return pl.pallas_call(
        _post_mlp_kernel,
        grid=(m // tr,),
        in_specs=[row, row, pl.BlockSpec((1, dm), lambda i: (0, 0))],
        out_specs=row,
        out_shape=jax.ShapeDtypeStruct((m, dm), F32),
        compiler_params=_params(("parallel",)),
        name="post_mlp_norm",
    )(h, d, w.reshape(1, dm))


def _dot(a, b, b_t):
    if b_t:
        return lax.dot_general(a, b, _NT, preferred_element_type=F32)
    return jnp.dot(a, b, preferred_element_type=F32)


def _mm_kernel(*refs, n_extra, n_out, nk, b_t, epilogue):
    a_ref, b_ref = refs[0], refs[1]
    extra = refs[2:2 + n_extra]
    outs = refs[2 + n_extra:2 + n_extra + n_out]
    if nk == 1:
        epilogue(_dot(a_ref[...], b_ref[...], b_t), extra, outs)
        return
    acc_ref = refs[-1]
    k = pl.program_id(2)
    d = _dot(a_ref[...], b_ref[...], b_t)

    @pl.when(k == 0)
    def _():
        acc_ref[...] = d

    @pl.when((k > 0) & (k < nk - 1))
    def _():
        acc_ref[...] += d

    @pl.when(k == nk - 1)
    def _():
        epilogue(acc_ref[...] + d, extra, outs)


def _matmul(a, b, epilogue, extras, outs, *, tm, tn, tk=None, b_t=False, name):
    m, kd = a.shape
    n = b.shape[0] if b_t else b.shape[1]
    tk = kd if tk is None else tk
    nk = kd // tk
    grid = (m // tm, n // tn, nk)

    def lift(f):
        return lambda i, j, k: f(i, j)

    b_spec = (pl.BlockSpec((tn, tk), lambda i, j, k: (j, k)) if b_t
              else pl.BlockSpec((tk, tn), lambda i, j, k: (k, j)))
    in_specs = [pl.BlockSpec((tm, tk), lambda i, j, k: (i, k)), b_spec]
    in_specs += [pl.BlockSpec(bs, lift(im)) for _, bs, im in extras]
    out_specs = [pl.BlockSpec(bs, lift(im)) for _, _, bs, im in outs]
    out_shape = [jax.ShapeDtypeStruct(s, dt) for s, dt, _, _ in outs]
    scratch = [pltpu.VMEM((tm, tn), F32)] if nk > 1 else []
    res = pl.pallas_call(
        functools.partial(_mm_kernel, n_extra=len(extras), n_out=len(outs), nk=nk, b_t=b_t, epilogue=epilogue),
        grid=grid,
        in_specs=in_specs,
        out_specs=out_specs,
        out_shape=out_shape,
        scratch_shapes=scratch,
        compiler_params=_params(("parallel", "parallel", "arbitrary")),
        name=name,
    )(a, b, *[e[0] for e in extras])
    return res


def _mm_w32_kernel(*refs, n_extra, n_out, w_t, valid, epilogue):
    a_ref, w_ref = refs[0], refs[1]
    extra = refs[2:2 + n_extra]
    outs = refs[2 + n_extra:2 + n_extra + n_out]
    wbf_ref = refs[2 + n_extra + n_out]

    @pl.when(pl.program_id(1) == 0)
    def _():
        w = w_ref[...]
        n_axis = 0 if w_t else 1
        if valid < w.shape[n_axis]:
            w = jnp.where(lax.broadcasted_iota(I32, w.shape, n_axis) < valid, w, 0.0)
        wbf_ref[...] = w.astype(BF16)

    epilogue(_dot(a_ref[...], wbf_ref[...], w_t), extra, outs)


def _matmul_w32(a, w, epilogue, extras, outs, *, tm, tn, n, col_off=0, w_t=False, valid=None, name):
    m, kd = a.shape
    assert n % tn == 0 and col_off % (SUBLANES if w_t else tn) == 0
    valid = tn if valid is None else valid
    grid = (n // tn, m // tm)

    def lift(f):
        return lambda j, i: f(i, j)

    if w_t:
        w_spec = pl.BlockSpec((pl.Element(tn), pl.Element(kd)),
                              lambda j, i: (pl.multiple_of(col_off + j * tn, SUBLANES), 0))
        wbf_spec, wbf_shape = pl.BlockSpec((tn, kd), lambda j, i: (j, 0)), (n, kd)
    else:
        w_spec = pl.BlockSpec((kd, tn), lambda j, i: (0, col_off // tn + j))
        wbf_spec, wbf_shape = pl.BlockSpec((kd, tn), lambda j, i: (0, j)), (kd, n)
    in_specs = [pl.BlockSpec((tm, kd), lambda j, i: (i, 0)), w_spec]
    in_specs += [pl.BlockSpec(bs, lift(im)) for _, bs, im in extras]
    out_specs = [pl.BlockSpec(bs, lift(im)) for _, _, bs, im in outs] + [wbf_spec]
    out_shape = [jax.ShapeDtypeStruct(s, dt) for s, dt, _, _ in outs] + [jax.ShapeDtypeStruct(wbf_shape, BF16)]
    return pl.pallas_call(
        functools.partial(_mm_w32_kernel, n_extra=len(extras), n_out=len(outs), w_t=w_t, valid=valid,
                          epilogue=epilogue),
        grid=grid,
        in_specs=in_specs,
        out_specs=out_specs,
        out_shape=out_shape,
        compiler_params=_params(("arbitrary", "arbitrary")),
        name=name,
    )(a, w, *[e[0] for e in extras])


def _nat(m, n, dt, tm, tn):
    return ((m, n), dt, (tm, tn), lambda i, j: (i, j))


def _rope_lanes(y, c, sm, sp, half):
    n = y.shape[-1]
    return y * c + pltpu.roll(y, n - half, 1) * sm + pltpu.roll(y, half, 1) * sp


def _ep_plain(acc, extra, outs):
    for o in outs:
        o[...] = acc.astype(o.dtype)


def _ep_rope_lanes(acc, extra, outs, *, half, blocked):
    c, sm, sp = extra[0][...], extra[1][...], extra[2][...]
    tm, tn = acc.shape
    for jj in range(tn // LANES):
        y = _rope_lanes(acc[:, jj * LANES:(jj + 1) * LANES], c, sm, sp, half)
        if blocked:
            for r in range(tm // Q_BLOCK):
                outs[0][r, jj] = y[r * Q_BLOCK:(r + 1) * Q_BLOCK].astype(outs[0].dtype)
        else:
            outs[0][:, jj * LANES:(jj + 1) * LANES] = y.astype(outs[0].dtype)
        if len(outs) > 1:
            outs[1][jj] = y.astype(outs[1].dtype)


def _ep_ret_qk(acc, extra, outs, *, scale):
    cos, sin, dec = extra[0][...], extra[1][...], extra[2]
    tn = acc.shape[1]
    for hh in range(tn // RET_DK):
        lo = hh * RET_DK
        x1 = acc[:, lo:lo + LANES]
        x2 = acc[:, lo + LANES:lo + 2 * LANES]
        o1 = x1 * cos - x2 * sin
        o2 = x2 * cos + x1 * sin
        if scale != 1.0:
            o1 = o1 * scale
            o2 = o2 * scale
        d = dec[:, hh * LANES:(hh + 1) * LANES]
        outs[0][:, lo:lo + LANES] = o1.astype(outs[0].dtype)
        outs[0][:, lo + LANES:lo + 2 * LANES] = o2.astype(outs[0].dtype)
        outs[1][:, lo:lo + LANES] = (o1 * d).astype(outs[1].dtype)
        outs[1][:, lo + LANES:lo + 2 * LANES] = (o2 * d).astype(outs[1].dtype)


def _merge_kernel(att_ref, ret_ref, wa_ref, wr_ref, ga_ref, gr_ref, o_ref):
    a = jnp.dot(att_ref[...], wa_ref[...], preferred_element_type=F32)
    r = jnp.dot(ret_ref[...], wr_ref[...], preferred_element_type=F32)
    o_ref[...] = (_sigmoid(ga_ref[...]) * a + _sigmoid(gr_ref[...]) * r).astype(o_ref.dtype)


def _merge(att, ret, wa, wr, gates, tm, tn):
    m = att.shape[0]
    nb = D_MODEL // tn
    return pl.pallas_call(
        _merge_kernel,
        grid=(m // tm, nb),
        in_specs=[
            pl.BlockSpec((tm, ATT_Q_W), lambda i, j: (i, 0)),
            pl.BlockSpec((tm, RET_V_W), lambda i, j: (i, 0)),
            pl.BlockSpec((ATT_Q_W, tn), lambda i, j: (0, j)),
            pl.BlockSpec((RET_V_W, tn), lambda i, j: (0, j)),
            pl.BlockSpec((tm, tn), lambda i, j: (i, nb + j)),
            pl.BlockSpec((tm, tn), lambda i, j: (i, 2 * nb + j)),
        ],
        out_specs=pl.BlockSpec((tm, tn), lambda i, j: (i, j)),
        out_shape=jax.ShapeDtypeStruct((m, D_MODEL), BF16),
        compiler_params=_params(("parallel", "parallel")),
        name="merge_proj",
    )(att, ret, wa, wr, gates, gates)


def _ep_relu2(acc, extra, outs):
    u = jnp.maximum(acc, 0.0)
    outs[0][...] = (u * u).astype(outs[0].dtype)


def _rope_lane_tables(pos, head_w, rot, theta):
    half = rot // 2
    inv_freq = jnp.exp(-math.log(theta) * jnp.arange(half, dtype=F32) / half)
    ang = pos.astype(F32)[:, None] * inv_freq[None, :]
    cos, sin = jnp.cos(ang), jnp.sin(ang)
    n = pos.shape[0]
    z_half = jnp.zeros((n, half), F32)
    rest1 = jnp.ones((n, head_w - rot), F32)
    rest0 = jnp.zeros((n, head_w - rot), F32)
    c = jnp.concatenate([cos, cos, rest1], axis=1)
    sm = jnp.concatenate([-sin, z_half, rest0], axis=1)
    sp = jnp.concatenate([z_half, sin, rest0], axis=1)
    rep = LANES // head_w
    return [jnp.tile(t, (1, rep)) for t in (c, sm, sp)]


def _ret_tables(pos, chunk_pos, chunk_len, log_gamma):
    half = RET_DK // 2
    inv_freq = jnp.exp(-math.log(RET_THETA) * jnp.arange(half, dtype=F32) / half)
    ang = pos.astype(F32)[:, None] * inv_freq[None, :]
    i = chunk_pos.astype(F32)[:, None]
    q_dec = jnp.exp(log_gamma[None, :] * (i + 1.0))
    k_dec = jnp.exp(log_gamma[None, :] * (chunk_len - 1.0 - i))
    return jnp.cos(ang), jnp.sin(ang), jnp.repeat(q_dec, LANES, axis=1), jnp.repeat(k_dec, LANES, axis=1)


PROJ_TN = 512


def _project(xn, w, pos, chunk_pos, chunk_len, log_gamma, *, tm, prompt):
    m = xn.shape[0]
    nrep = max(pos.shape[0] // tm, 1)
    wdt = BF16 if prompt else F32
    out = {"w_bf16": {}}

    def mm(name, a, b, epilogue, extras, outs, tn=PROJ_TN, valid_cols=None):
        n = -(-(OFFS[b] - OFFS[a]) // tn) * tn
        if prompt:
            *res, wb = _matmul_w32(xn, w, epilogue, extras, outs, tm=tm, tn=tn, n=n, col_off=OFFS[a], w_t=True,
                                   valid=valid_cols, name=name)
            out["w_bf16"][name] = wb
            return res
        return _matmul(xn, w[name], epilogue, extras, outs, tm=tm, tn=tn, b_t=True, name=name)

    def tab(t, width=LANES, by_col=False):
        if by_col:
            return (t, (tm, width), lambda i, j: (i % nrep, j))
        return (t, (tm, width), lambda i, j: (i % nrep, 0))

    att_t = [tab(t) for t in _rope_lane_tables(pos, ATT_HEAD_DIM, ATT_HEAD_DIM // 4, ROPE_THETA)]
    idx_t = [tab(t) for t in _rope_lane_tables(pos, IDX_DIM, IDX_DIM // 4, ROPE_THETA)]
    r_cos, r_sin, q_dec, k_dec = _ret_tables(pos, chunk_pos, chunk_len, log_gamma)
    tn = PROJ_TN
    nqb = m // Q_BLOCK

    if prompt:
        o = [((nqb, ATT_HEADS, Q_BLOCK, LANES), BF16, (tm // Q_BLOCK, tn // LANES, Q_BLOCK, LANES),
              lambda i, j: (i, j, 0, 0))]
    else:
        o = [_nat(m, ATT_Q_W, F32, tm, tn)]
    out["q"], = mm("proj_q", 0, 1, functools.partial(_ep_rope_lanes, half=16, blocked=prompt), att_t, o)
    o = [_nat(m, ATT_KV_W, F32, tm, tn)]
    if prompt:
        per_b = SEQ // tm
        o.append(((BATCH, ATT_KV_HEADS, SEQ, LANES), BF16, (None, tn // LANES, tm, LANES),
                  lambda i, j: (i // per_b, j, i % per_b, 0)))
    res = mm("proj_k", 1, 2, functools.partial(_ep_rope_lanes, half=16, blocked=False), att_t, o)
    out["k"] = res[0]
    if prompt:
        out["k_heads"] = res[1]
    o = [_nat(m, ATT_KV_W, F32, tm, tn)]
    if prompt:
        o.append(_nat(m, ATT_KV_W, BF16, tm, tn))
    res = mm("proj_v", 2, 3, _ep_plain, [], o)
    out["v"] = res[0]
    if prompt:
        out["v_bf"] = res[1]
    if prompt:
        o = [((nqb, IDX_Q_W // LANES, Q_BLOCK, LANES), F32, (tm // Q_BLOCK, tn // LANES, Q_BLOCK, LANES),
              lambda i, j: (i, j, 0, 0))]
    else:
        o = [_nat(m, IDX_Q_W, F32, tm, tn)]
    out["iq"], = mm("proj_iq", 3, 4, functools.partial(_ep_rope_lanes, half=8, blocked=prompt), idx_t, o)
    ikw_t = [tab(t) for t in _rope_lane_tables(pos, LANES, IDX_DIM // 4, ROPE_THETA)]
    out["ikw"], = mm("proj_ikw", 4, 6, functools.partial(_ep_rope_lanes, half=8, blocked=False), ikw_t,
                     [_nat(m, LANES, F32, tm, LANES)], tn=LANES, valid_cols=IDX_DIM + IDX_HEADS)
    for nm, grp, dec, scale in (("rq", 6, q_dec, 1.0), ("rk", 7, k_dec, RET_DK ** -0.5)):
        res = mm("proj_" + nm, grp, grp + 1, functools.partial(_ep_ret_qk, scale=scale),
                 [tab(r_cos), tab(r_sin), tab(dec, tn // 2, by_col=True)],
                 [_nat(m, RET_QK_W, wdt, tm, tn), _nat(m, RET_QK_W, wdt, tm, tn)])
        out[nm], out[nm + "d"] = res
    out["rv"], = mm("proj_rv", 8, 9, _ep_plain, [], [_nat(m, RET_V_W, wdt, tm, tn)])
    out["gates"], = mm("proj_gates", 9, 12, _ep_plain, [], [_nat(m, 3 * D_MODEL, F32, tm, tn)])
    return out


def _sortable_key(score):
    kb = lax.bitcast_convert_type(score, I32)
    kb = jnp.where(kb == INT_MIN, 0, kb)
    return jnp.where(kb < 0, kb ^ 0x7FFFFFFF, kb)


def _tile_reduce(x, op):
    tiles = [x[i * SUBLANES:(i + 1) * SUBLANES] for i in range(x.shape[0] // SUBLANES)]
    while len(tiles) > 1:
        nxt = [op(tiles[i], tiles[i + 1]) for i in range(0, len(tiles) - 1, 2)]
        if len(tiles) % 2:
            nxt.append(tiles[-1])
        tiles = nxt
    return tiles[0]


def _topk_mask(sc_ref, key_ref, bias_ref, q_pos, n_chunks):
    def rows(c):
        r0 = pl.multiple_of(c * KEY_CHUNK, KEY_CHUNK)
        return pl.ds(r0, KEY_CHUNK), r0 + lax.broadcasted_iota(I32, (KEY_CHUNK, LANES), 0)

    def build(c, carry):
        sl, row = rows(c)
        key_ref[sl, :] = jnp.where(row <= q_pos, _sortable_key(sc_ref[sl, :]), INT_MIN)
        return carry

    lax.fori_loop(0, n_chunks, build, 0)

    def count(pred):
        def body(c, acc):
            sl, row = rows(c)
            return acc + _tile_reduce(jnp.where(pred(key_ref[sl, :], row), 1.0, 0.0), jnp.add)

        part = lax.fori_loop(0, n_chunks, body, jnp.zeros((SUBLANES, LANES), F32))
        return jnp.sum(part, axis=0, keepdims=True)

    t0 = jnp.where(count(lambda k, row: k >= 0) >= TOPK, 0, INT_MIN).astype(I32)

    def search(i, t):
        cand = t | lax.shift_left(jnp.int32(1), jnp.int32(30) - i)
        return jnp.where(count(lambda k, row: k >= cand) >= TOPK, cand, t)

    thr = lax.fori_loop(0, 31, search, t0)
    n_eq = count(lambda k, row: (row <= q_pos) & (k == thr))
    need = TOPK - count(lambda k, row: k > thr)

    def write(c, carry):
        sl, row = rows(c)
        k = key_ref[sl, :]
        bias_ref[sl, :] = jnp.where(k > thr, 0.0, jnp.where((row <= q_pos) & (k == thr), 0.0, NEG_BIG))
        return carry

    lax.fori_loop(0, n_chunks, write, 0)

    @pl.when(jnp.max(n_eq - need) > 0)
    def _():
        r_i = lax.broadcasted_iota(I32, (LANES, LANES), 0)
        c_i = lax.broadcasted_iota(I32, (LANES, LANES), 1)
        tri = jnp.where(c_i < r_i, 1.0, 0.0).astype(BF16)

        def chunk(c, off):
            r0 = pl.multiple_of(c * LANES, LANES)
            kc = key_ref[pl.ds(r0, LANES), :]
            rc = r0 + lax.broadcasted_iota(I32, (LANES, LANES), 0)
            e = (rc <= q_pos) & (kc == thr)
            ef = jnp.where(e, 1.0, 0.0)
            before = jnp.dot(tri, ef.astype(BF16), preferred_element_type=F32) + off
            keep = e & (before < need)
            bias_ref[pl.ds(r0, LANES), :] = jnp.where(kc > thr, 0.0, jnp.where(keep, 0.0, NEG_BIG))
            return off + jnp.sum(ef, axis=0, keepdims=True)

        lax.fori_loop(0, n_chunks * (KEY_CHUNK // LANES), chunk, jnp.zeros((1, LANES), F32))


def _split_bf16(x):
    hi = x.astype(BF16).astype(F32)
    return hi, x - hi


def _dsa_prompt_kernel(iq_ref, ikw_all_ref, ikw_q_ref, q_ref, k_ref, vt_ref, att_ref,
                       lhs_ref, wt_ref, sc_ref, key_ref, bias_ref):
    qb = pl.program_id(1)
    n_ck = (qb * Q_BLOCK + Q_BLOCK + KEY_CHUNK - 1) // KEY_CHUNK
    lane = lax.broadcasted_iota(I32, (1, LANES), 1)
    low = lane < IDX_DIM

    def chunk_rows(c):
        return pl.ds(pl.multiple_of(c * KEY_CHUNK, KEY_CHUNK), KEY_CHUNK)

    @pl.when(qb == 0)
    def _():
        x = jnp.where(low, ikw_all_ref[...], 0.0)
        hi, lo = _split_bf16(x)
        lhs_ref[:, :LANES] = (hi + pltpu.roll(lo, IDX_DIM, 1)).astype(BF16)
        lhs_ref[:, LANES:] = hi.astype(BF16)

    wt_ref[...] = ikw_q_ref[...].T * IDX_SCALE
    sc_ref[...] = jnp.zeros_like(sc_ref)

    def pairs_body(pq, carry):
        rhs, wts = [], []
        for u in range(PAIRS_PER_STEP):
            p = pq * PAIRS_PER_STEP + u
            hi, lo = _split_bf16(iq_ref[p])
            rhi = pltpu.roll(hi, IDX_DIM, 1)
            rlo = pltpu.roll(lo, IDX_DIM, 1)
            ra = jnp.concatenate([jnp.where(low, hi, rhi), jnp.where(low, lo, 0.0)], axis=1)
            rb = jnp.concatenate([jnp.where(low, rhi, hi), jnp.where(low, rlo, 0.0)], axis=1)
            rhs.append(jnp.concatenate([ra, rb], axis=0).astype(BF16))
            wts.append((wt_ref[pl.ds(IDX_DIM + 2 * p, 1), :], wt_ref[pl.ds(IDX_DIM + 2 * p + 1, 1), :]))

        def ck_body(c, inner):
            sl = chunk_rows(c)
            lhs = lhs_ref[sl, :]
            acc = sc_ref[sl, :]
            for rhs_t, (wa, wb) in zip(rhs, wts):
                d = lax.dot_general(lhs, rhs_t, _NT, preferred_element_type=F32)
                acc = acc + wa * jnp.maximum(d[:, :LANES], 0.0) + wb * jnp.maximum(d[:, LANES:], 0.0)
            sc_ref[sl, :] = acc
            return inner

        lax.fori_loop(0, n_ck, ck_body, 0)
        return carry

    lax.fori_loop(0, IDX_HEADS // 2 // PAIRS_PER_STEP, pairs_body, 0)

    _topk_mask(sc_ref, key_ref, bias_ref, qb * Q_BLOCK + lane, n_ck)

    c_exp = ATT_SCALE * math.log2(math.e)

    def groups_body(gq, carry):
        gs = [gq * GROUPS_PER_STEP + u for u in range(GROUPS_PER_STEP)]
        qqs = [jnp.concatenate([q_ref[2 * g], q_ref[2 * g + 1]], axis=0) for g in gs]

        def ck_body(c, state):
            sl = chunk_rows(c)
            bias = bias_ref[sl, :]
            new = []
            for g, qq, (m_old, l_old, acc) in zip(gs, qqs, state):
                lg = lax.dot_general(k_ref[g, sl, :], qq, _NT, preferred_element_type=F32)
                ls = [lg[:, hh * LANES:(hh + 1) * LANES] + bias for hh in range(2)]
                m_ck = jnp.concatenate(
                    [jnp.max(_tile_reduce(l, jnp.maximum), axis=0, keepdims=True) for l in ls], axis=1)
                m_new = jnp.maximum(m_old, m_ck)
                alpha = jnp.exp2((m_old - m_new) * c_exp)
                ps = [jnp.exp2((l - m_new[:, hh * LANES:(hh + 1) * LANES]) * c_exp) for hh, l in enumerate(ls)]
                l_ck = jnp.concatenate(
                    [jnp.sum(_tile_reduce(p, jnp.add), axis=0, keepdims=True) for p in ps], axis=1)
                pt = jnp.concatenate(ps, axis=1).astype(BF16)
                pv = jnp.dot(vt_ref[g, c], pt, preferred_element_type=F32)
                new.append((m_new, alpha * l_old + l_ck, alpha * acc + pv))
            return tuple(new)

        init = (jnp.full((1, 2 * LANES), NEG_BIG, F32), jnp.zeros((1, 2 * LANES), F32),
                jnp.zeros((ATT_HEAD_DIM, 2 * LANES), F32))
        final = lax.fori_loop(0, n_ck, ck_body, (init,) * GROUPS_PER_STEP)
        for g, (_, l_fin, acc) in zip(gs, final):
            o = acc / l_fin
            for hh in range(2):
                att_ref[2 * g + hh] = o[:, hh * LANES:(hh + 1) * LANES].T.astype(att_ref.dtype)
        return carry

    lax.fori_loop(0, ATT_KV_HEADS // GROUPS_PER_STEP, groups_body, 0)


def _dsa_prompt(iq, ikw, q, k_heads, vt):
    nqb = SEQ // Q_BLOCK
    n_pair = IDX_Q_W // LANES
    return pl.pallas_call(
        _dsa_prompt_kernel,
        grid=(BATCH, nqb),
        in_specs=[
            pl.BlockSpec((None, n_pair, Q_BLOCK, LANES), lambda b, i: (b * nqb + i, 0, 0, 0)),
            pl.BlockSpec((SEQ, LANES), lambda b, i: (b, 0)),
            pl.BlockSpec((Q_BLOCK, LANES), lambda b, i: (b * nqb + i, 0)),
            pl.BlockSpec((None, ATT_HEADS, Q_BLOCK, LANES), lambda b, i: (b * nqb + i, 0, 0, 0)),
            pl.BlockSpec((None, ATT_KV_HEADS, SEQ, LANES), lambda b, i: (b, 0, 0, 0)),
            pl.BlockSpec((None, ATT_KV_HEADS, SEQ // KEY_CHUNK, LANES, KEY_CHUNK), lambda b, i: (b, 0, 0, 0, 0)),
        ],
        out_specs=pl.BlockSpec((None, ATT_HEADS, Q_BLOCK, LANES), lambda b, i: (b * nqb + i, 0, 0, 0)),
        out_shape=jax.ShapeDtypeStruct((BATCH * nqb, ATT_HEADS, Q_BLOCK, LANES), BF16),
        scratch_shapes=[
            pltpu.VMEM((SEQ, 2 * LANES), BF16),
            pltpu.VMEM((LANES, LANES), F32),
            pltpu.VMEM((SEQ, LANES), F32),
            pltpu.VMEM((SEQ, LANES), I32),
            pltpu.VMEM((SEQ, LANES), F32),
        ],
        compiler_params=_params(("parallel", "arbitrary")),
        name="dsa_prompt",
    )(iq, ikw, ikw, q, k_heads, vt)


def _group_norm_gate(o, gn_w, rg):
    of = o * lax.rsqrt(jnp.mean(o * o, axis=-1, keepdims=True) + NORM_EPS) * gn_w
    return of * (rg * _sigmoid(rg))


def _ret_prompt_kernel(q_ref, qd_ref, k_ref, kd_ref, v_ref, rg_ref, decay_ref, cdec_ref, gnw_ref,
                       ret_ref, s_ref):
    c = pl.program_id(1)

    @pl.when(c == 0)
    def _():
        s_ref[...] = jnp.zeros_like(s_ref)

    for h in range(RET_HEADS):
        qk = slice(h * RET_DK, (h + 1) * RET_DK)
        vv = slice(h * RET_DV, (h + 1) * RET_DV)
        s = s_ref[h]
        v = v_ref[:, vv]
        scores = lax.dot_general(q_ref[:, qk], k_ref[:, qk], _NT, preferred_element_type=F32) * decay_ref[h]
        o = (jnp.dot(scores.astype(BF16), v, preferred_element_type=F32)
             + jnp.dot(qd_ref[:, qk], s.astype(BF16), preferred_element_type=F32))
        kv = lax.dot_general(kd_ref[:, qk], v, (((0,), (0,)), ((), ())), preferred_element_type=F32)
        s_ref[h] = cdec_ref[h] * s + kv
        ret_ref[:, vv] = _group_norm_gate(o, gnw_ref[:, vv], rg_ref[:, vv]).astype(ret_ref.dtype)


def _ret_prompt(pr, decay, cdec, gn_w):
    nc = SEQ // RET_CHUNK
    qk = pl.BlockSpec((RET_CHUNK, RET_QK_W), lambda b, c: (b * nc + c, 0))
    vv = pl.BlockSpec((RET_CHUNK, RET_V_W), lambda b, c: (b * nc + c, 0))
    return pl.pallas_call(
        _ret_prompt_kernel,
        grid=(BATCH, nc),
        in_specs=[qk, qk, qk, qk, vv, vv,
                  pl.BlockSpec((RET_HEADS, RET_CHUNK, RET_CHUNK), lambda b, c: (0, 0, 0)),
                  pl.BlockSpec((RET_HEADS, 1, RET_DV), lambda b, c: (0, 0, 0)),
                  pl.BlockSpec((1, RET_V_W), lambda b, c: (0, 0))],
        out_specs=[vv, pl.BlockSpec((None, RET_HEADS, RET_DK, RET_DV), lambda b, c: (b, 0, 0, 0))],
        out_shape=[jax.ShapeDtypeStruct((BATCH * SEQ, RET_V_W), BF16),
                   jax.ShapeDtypeStruct((BATCH, RET_HEADS, RET_DK, RET_DV), F32)],
        compiler_params=_params(("parallel", "arbitrary")),
        name="retention_prompt",
    )(pr["rq"], pr["rqd"], pr["rk"], pr["rkd"], pr["rv"], pr["gates"], decay, cdec, gn_w.reshape(1, RET_V_W))


def _ret_sample_kernel(q_ref, qd_ref, kt_ref, k_ref, v_ref, rg_ref, cdec_ref, gnw_ref, s_ref, ret_ref, so_ref):
    b = pl.program_id(0)
    nb = kt_ref.shape[1]
    onehot = lax.broadcasted_iota(I32, (1, nb), 1) == b
    for h in range(RET_HEADS):
        qk_sl = slice(h * RET_DK, (h + 1) * RET_DK)
        vv = slice(h * RET_DV, (h + 1) * RET_DV)
        s = s_ref[h]
        v = v_ref[pl.ds(b, 1), vv]
        q = q_ref[pl.ds(b, 1), qk_sl]
        k = k_ref[pl.ds(b, 1), qk_sl]
        qk = jnp.sum(q.astype(BF16).astype(F32) * k.astype(BF16).astype(F32), axis=-1, keepdims=True)
        qd8 = jnp.broadcast_to(qd_ref[pl.ds(b, 1), qk_sl], (16, RET_DK)).astype(BF16)
        o = qk.astype(BF16).astype(F32) * v.astype(BF16).astype(F32) \
            + jnp.dot(qd8, s.astype(BF16), preferred_element_type=F32)[0:1]
        k_col = jnp.sum(jnp.where(onehot, kt_ref[qk_sl, :], 0.0), axis=1, keepdims=True)
        so_ref[h] = cdec_ref[h] * s + k_col * v
        ret_ref[:, vv] = _group_norm_gate(o, gnw_ref[:, vv], rg_ref[pl.ds(b, 1), vv])


def _ret_sample(ps, state, cdec, gn_w):
    nb = DEC_BATCH
    qk = pl.BlockSpec((nb, RET_QK_W), lambda b: (0, 0))
    vv = pl.BlockSpec((nb, RET_V_W), lambda b: (0, 0))
    st = pl.BlockSpec((None, RET_HEADS, RET_DK, RET_DV), lambda b: (b, 0, 0, 0))
    return pl.pallas_call(
        _ret_sample_kernel,
        grid=(nb,),
        in_specs=[qk, qk, pl.BlockSpec((RET_QK_W, nb), lambda b: (0, 0)), qk, vv, vv,
                  pl.BlockSpec((RET_HEADS, 1, RET_DV), lambda b: (0, 0, 0)),
                  pl.BlockSpec((1, RET_V_W), lambda b: (0, 0)), st],
        out_specs=[pl.BlockSpec((None, 1, RET_V_W), lambda b: (b, 0, 0)), st],
        out_shape=[jax.ShapeDtypeStruct((nb, 1, RET_V_W), F32),
                   jax.ShapeDtypeStruct((nb, RET_HEADS, RET_DK, RET_DV), F32)],
        compiler_params=_params(("arbitrary",)),
        name="retention_sample",
    )(ps["rq"], ps["rqd"], ps["rkd"].T, ps["rk"], ps["rv"], ps["gates"], cdec, gn_w.reshape(1, RET_V_W), state)


PAGES_PER_STEP = 8
IDX_PAGES_PER_STEP = 32


def _idx_sample_kernel(pt_ref, iq_ref, w_ref, new_ref, *rest):
    pages = rest[:IDX_PAGES_PER_STEP]
    sc_ref, scn_ref = rest[IDX_PAGES_PER_STEP], rest[IDX_PAGES_PER_STEP + 1]
    j = pl.program_id(1)
    hi, lo = _split_bf16(iq_ref[...])
    lhs = jnp.concatenate([hi + pltpu.roll(hi, IDX_DIM, 1), lo], axis=1).astype(BF16)
    w = w_ref[...] * IDX_SCALE

    def page_scores(xt):
        khi, klo = _split_bf16(xt)
        rhs = jnp.concatenate([khi, klo, khi, jnp.zeros_like(khi)], axis=0).astype(BF16)
        d = jnp.dot(lhs, rhs, preferred_element_type=F32)
        return jnp.sum(w * jnp.maximum(d, 0.0), axis=0, keepdims=True)

    for i in range(IDX_PAGES_PER_STEP):
        sc_ref[:, i * PAGE_SIZE:(i + 1) * PAGE_SIZE] = page_scores(pages[i][...])

    @pl.when(j == pl.num_programs(1) - 1)
    def _():
        scn_ref[...] = page_scores(new_ref[...])


def _idx_sample(page_table, iq3, w3, new_pages, cache_idx):
    nsteps = N_PAGES // IDX_PAGES_PER_STEP

    def page_spec(i):
        return pl.BlockSpec((None, IDX_DIM, PAGE_SIZE),
                            lambda b, j, pt: (pt[b, j * IDX_PAGES_PER_STEP + i], 0, 0))

    grid_spec = pltpu.PrefetchScalarGridSpec(
        num_scalar_prefetch=1,
        grid=(DEC_BATCH, nsteps),
        in_specs=[pl.BlockSpec((None, IDX_HEADS, LANES), lambda b, j, pt: (b, 0, 0)),
                  pl.BlockSpec((None, IDX_HEADS, 1), lambda b, j, pt: (b, 0, 0)),
                  pl.BlockSpec((None, IDX_DIM, PAGE_SIZE), lambda b, j, pt: (b, 0, 0))]
                 + [page_spec(i) for i in range(IDX_PAGES_PER_STEP)],
        out_specs=[pl.BlockSpec((None, 1, IDX_PAGES_PER_STEP * PAGE_SIZE), lambda b, j, pt: (b, 0, j)),
                   pl.BlockSpec((None, 1, PAGE_SIZE), lambda b, j, pt: (b, 0, 0))],
    )
    return pl.pallas_call(
        _idx_sample_kernel,
        grid_spec=grid_spec,
        out_shape=[jax.ShapeDtypeStruct((DEC_BATCH, 1, PAST_LEN), F32),
                   jax.ShapeDtypeStruct((DEC_BATCH, 1, PAGE_SIZE), F32)],
        compiler_params=_params(("parallel", "arbitrary")),
        name="indexer_sample",
    )(page_table, iq3, w3, new_pages, *([cache_idx] * IDX_PAGES_PER_STEP))


def _select_sample_kernel(sc_ref, bias_ref, key_ref):
    _topk_mask(sc_ref, key_ref, bias_ref, jnp.full((1, LANES), PAST_LEN, I32), sc_ref.shape[0] // KEY_CHUNK)


def _select_sample(scores_t):
    return pl.pallas_call(
        _select_sample_kernel,
        out_shape=jax.ShapeDtypeStruct(scores_t.shape, F32),
        scratch_shapes=[pltpu.VMEM(scores_t.shape, I32)],
        compiler_params=pltpu.CompilerParams(vmem_limit_bytes=VMEM_LIMIT),
        name="select_sample",
    )(scores_t)


PAGE_ROWS = PAGE_SIZE * ATT_KV_HEADS


def _attn_sample_kernel(pt_ref, q_ref, bias_ref, biasn_ref, kn_ref, vn_ref, *rest):
    kp = rest[:PAGES_PER_STEP]
    vp = rest[PAGES_PER_STEP:2 * PAGES_PER_STEP]
    o_ref = rest[2 * PAGES_PER_STEP]
    m_ref, l_ref, acc_ref = rest[2 * PAGES_PER_STEP + 1:]
    j = pl.program_id(1)
    col = lax.broadcasted_iota(I32, (ATT_HEADS, PAGE_ROWS), 1)
    head = lax.broadcasted_iota(I32, (ATT_HEADS, PAGE_ROWS), 0)
    own = (col % ATT_KV_HEADS) == (head // (ATT_HEADS // ATT_KV_HEADS))
    c_exp = ATT_SCALE * math.log2(math.e)
    q = q_ref[...].astype(BF16)

    @pl.when(j == 0)
    def _():
        m_ref[...] = jnp.full_like(m_ref, NEG_BIG)
        l_ref[...] = jnp.zeros_like(l_ref)
        acc_ref[...] = jnp.zeros_like(acc_ref)

    logits = []
    for i in range(PAGES_PER_STEP):
        lg = lax.dot_general(q, kp[i][...].astype(BF16), _NT, preferred_element_type=F32)
        logits.append(jnp.where(own, lg + bias_ref[:, i * PAGE_ROWS:(i + 1) * PAGE_ROWS], NEG_BIG))
    m_old = m_ref[...]
    m_new = m_old
    for lg in logits:
        m_new = jnp.maximum(m_new, jnp.max(lg, axis=1, keepdims=True))
    alpha = jnp.exp2((m_old - m_new) * c_exp)
    l_new = alpha * l_ref[...]
    acc = alpha * acc_ref[...]
    for i in range(PAGES_PER_STEP):
        p = jnp.exp2((logits[i] - m_new) * c_exp)
        l_new = l_new + jnp.sum(p, axis=1, keepdims=True)
        acc = acc + jnp.dot(p.astype(BF16), vp[i][...].astype(BF16), preferred_element_type=F32)
    m_ref[...] = m_new
    l_ref[...] = l_new
    acc_ref[...] = acc

    @pl.when(j == pl.num_programs(1) - 1)
    def _():
        kn = kn_ref[...].astype(BF16).astype(F32)
        vn = vn_ref[...].astype(BF16).astype(F32)
        lgn = jnp.sum(q.astype(F32) * kn, axis=1, keepdims=True) + biasn_ref[:, 0:1]
        m_f = jnp.maximum(m_new, lgn)
        a = jnp.exp2((m_new - m_f) * c_exp)
        pn = jnp.exp2((lgn - m_f) * c_exp)
        o_ref[...] = (a * acc + pn.astype(BF16).astype(F32) * vn) / (a * l_new + pn)


def _attn_sample(page_table, q3, bias_rows, bias_new, k_new, v_new, cache_k, cache_v):
    nsteps = N_PAGES // PAGES_PER_STEP

    def page_spec(i):
        return pl.BlockSpec((None, PAGE_ROWS, ATT_HEAD_DIM),
                            lambda b, j, pt: (pt[b, j * PAGES_PER_STEP + i], 0, 0))

    head_rows = pl.BlockSpec((None, ATT_HEADS, ATT_HEAD_DIM), lambda b, j, pt: (b, 0, 0))
    grid_spec = pltpu.PrefetchScalarGridSpec(
        num_scalar_prefetch=1,
        grid=(DEC_BATCH, nsteps),
        in_specs=[head_rows,
                  pl.BlockSpec((None, 1, PAGES_PER_STEP * PAGE_ROWS), lambda b, j, pt: (b, 0, j)),
                  pl.BlockSpec((None, 1, PAGE_SIZE), lambda b, j, pt: (b, 0, 0)),
                  head_rows, head_rows]
                 + [page_spec(i) for i in range(PAGES_PER_STEP)] * 2,
        out_specs=head_rows,
        scratch_shapes=[pltpu.VMEM((ATT_HEADS, 1), F32), pltpu.VMEM((ATT_HEADS, 1), F32),
                        pltpu.VMEM((ATT_HEADS, ATT_HEAD_DIM), F32)],
    )
    return pl.pallas_call(
        _attn_sample_kernel,
        grid_spec=grid_spec,
        out_shape=jax.ShapeDtypeStruct((DEC_BATCH, ATT_HEADS, ATT_HEAD_DIM), F32),
        compiler_params=_params(("parallel", "arbitrary")),
        name="attention_sample",
    )(page_table, q3, bias_rows, bias_new, k_new, v_new,
      *([cache_k] * PAGES_PER_STEP), *([cache_v] * PAGES_PER_STEP))


def _finish(x, att, ret, gates, w, *, tm, tr):
    m = x.shape[0]
    mg = _merge(att, ret, w["att_proj"], w["ret_proj"], gates, tm, 512)
    y, = _matmul(mg, w["out"], _ep_plain, [], [_nat(m, D_MODEL, F32, tm, 512)], tm=tm, tn=512, name="w_out")
    h, hn = _post_attn(x, y, w["n_attn_post"], w["n_mlp_pre"], tr)
    up_out = [_nat(m, D_FF, BF16, tm, 512)]
    if w["mlp_up"].dtype == F32:
        u, w["mlp_up"] = _matmul_w32(hn, w["mlp_up"], _ep_relu2, [], up_out, tm=tm, tn=512, n=D_FF, col_off=0,
                                     name="mlp_up")
    else:
        u, = _matmul(hn, w["mlp_up"], _ep_relu2, [], up_out, tm=tm, tn=512, name="mlp_up")
    d, = _matmul(u, w["mlp_down"], _ep_plain, [], [_nat(m, D_MODEL, F32, tm, 512)], tm=tm, tn=512, tk=4096,
                 name="mlp_down")
    return _post_mlp(h, d, w["n_mlp_post"], tr)


def kernel(x_prompt, x_sample, cache_k, cache_v, cache_idx_k, state_ret, page_table, norm_attn_pre,
           norm_attn_post, w_in, ret_gn_w, w_att_proj, w_ret_proj, w_out, norm_mlp_pre, w_mlp_up,
           w_mlp_down, norm_mlp_post):
    log_gamma = jnp.log1p(-jnp.exp2(-5.0 - jnp.arange(RET_HEADS, dtype=F32)))
    w_in0 = w_in[0].T
    w = {
        "att_proj": w_att_proj[0].astype(BF16), "ret_proj": w_ret_proj[0].astype(BF16),
        "out": w_out[0].astype(BF16), "mlp_up": w_mlp_up[0], "mlp_down": w_mlp_down[0].astype(BF16),
        "n_attn_post": norm_attn_post[0], "n_mlp_pre": norm_mlp_pre[0], "n_mlp_post": norm_mlp_post[0],
    }
    gn_w = ret_gn_w[0]

    m_p = BATCH * SEQ
    xp = x_prompt.reshape(m_p, D_MODEL)
    xn = _rmsnorm_cast(xp, norm_attn_pre[0], 256)
    pos_p = jnp.arange(SEQ, dtype=I32)
    pr = _project(xn, w_in0, pos_p, pos_p % RET_CHUNK, float(RET_CHUNK), log_gamma, tm=1024, prompt=True)
    vt = pr["v_bf"].reshape(BATCH, SEQ // KEY_CHUNK, KEY_CHUNK, ATT_KV_HEADS, ATT_HEAD_DIM).transpose(0, 3, 1, 4, 2)
    att4 = _dsa_prompt(pr["iq"], pr["ikw"], pr["q"], pr["k_heads"], vt)
    att = att4.reshape(m_p // Q_BLOCK, ATT_HEADS, Q_BLOCK, ATT_HEAD_DIM).transpose(0, 2, 1, 3).reshape(m_p, ATT_Q_W)
    ci = jnp.arange(RET_CHUNK, dtype=F32)
    diff = ci[:, None] - ci[None, :]
    decay = jnp.where(diff >= 0, jnp.exp(log_gamma[:, None, None] * jnp.maximum(diff, 0.0)), 0.0)
    cdec_p = jnp.broadcast_to(jnp.exp(log_gamma * RET_CHUNK)[:, None, None], (RET_HEADS, 1, RET_DV))
    ret, s_prompt = _ret_prompt(pr, decay, cdec_p, gn_w)
    y_prompt = _finish(xp, att, ret, pr["gates"], w, tm=1024, tr=256).reshape(BATCH, SEQ, D_MODEL)

    nb = DEC_BATCH
    xs = x_sample.reshape(nb, D_MODEL)
    xns = _rmsnorm_cast(xs, norm_attn_pre[0], nb)
    pos_s = jnp.full((nb,), PAST_LEN, I32)
    ps = _project(xns, pr["w_bf16"], pos_s, jnp.zeros((nb,), I32), 1.0, log_gamma, tm=nb, prompt=False)
    ik_new = ps["ikw"][:, :IDX_DIM]
    iq3 = jnp.pad(ps["iq"].reshape(nb, IDX_HEADS, IDX_DIM), ((0, 0), (0, 0), (0, LANES - IDX_DIM)))
    w3 = ps["ikw"][:, IDX_DIM:IDX_DIM + IDX_HEADS].reshape(nb, IDX_HEADS, 1)
    new_pages = jnp.pad(ik_new[:, :, None], ((0, 0), (0, 0), (0, PAGE_SIZE - 1)))
    sc_past, sc_new = _idx_sample(page_table, iq3, w3, new_pages, cache_idx_k[0].transpose(0, 2, 1))
    scores = jnp.concatenate([sc_past.reshape(nb, PAST_LEN), sc_new.reshape(nb, PAGE_SIZE)], axis=1)
    n_rows = -(-(PAST_LEN + PAGE_SIZE) // KEY_CHUNK) * KEY_CHUNK
    scores_t = jnp.pad(scores.T, ((0, n_rows - PAST_LEN - PAGE_SIZE), (0, LANES - nb)))
    bias = _select_sample(scores_t)[:PAST_LEN + PAGE_SIZE, :nb].T
    n_phys = cache_k.shape[1]
    group = ATT_HEADS // ATT_KV_HEADS
    att_s = _attn_sample(page_table, ps["q"].reshape(nb, ATT_HEADS, ATT_HEAD_DIM),
                         jnp.repeat(bias[:, :PAST_LEN], ATT_KV_HEADS, axis=1).reshape(nb, 1, N_PAGES * PAGE_ROWS),
                         bias[:, PAST_LEN:].reshape(nb, 1, PAGE_SIZE),
                         jnp.repeat(ps["k"].reshape(nb, ATT_KV_HEADS, ATT_HEAD_DIM), group, axis=1),
                         jnp.repeat(ps["v"].reshape(nb, ATT_KV_HEADS, ATT_HEAD_DIM), group, axis=1),
                         cache_k[0].reshape(n_phys, PAGE_ROWS, ATT_HEAD_DIM),
                         cache_v[0].reshape(n_phys, PAGE_ROWS, ATT_HEAD_DIM))
    cdec_s = jnp.broadcast_to(jnp.exp(log_gamma)[:, None, None], (RET_HEADS, 1, RET_DV))
    ret_s, s_sample = _ret_sample(ps, state_ret[0], cdec_s, gn_w)
    y_sample = _finish(xs, att_s.reshape(nb, ATT_Q_W).astype(BF16), ret_s.reshape(nb, RET_V_W).astype(BF16),
                       ps["gates"], w, tm=nb, tr=nb).reshape(nb, 1, D_MODEL)

    return (y_prompt, y_sample,
            pr["k"].reshape(1, BATCH, SEQ, ATT_KV_HEADS, ATT_HEAD_DIM),
            pr["v"].reshape(1, BATCH, SEQ, ATT_KV_HEADS, ATT_HEAD_DIM),
            pr["ikw"][:, :IDX_DIM].reshape(1, BATCH, SEQ, IDX_DIM),
            s_prompt[None],
            ps["k"].reshape(1, nb, 1, ATT_KV_HEADS, ATT_HEAD_DIM),
            ps["v"].reshape(1, nb, 1, ATT_KV_HEADS, ATT_HEAD_DIM),
            ik_new.reshape(1, nb, 1, IDX_DIM),
            s_sample[None])
```

```python
import functools
import math

import jax
import jax.numpy as jnp
import numpy as np
from jax import lax
from jax.experimental import pallas as pl
from jax.experimental.pallas import tpu as pltpu

F32 = jnp.float32
BF16 = jnp.bfloat16
I32 = jnp.int32

D_MODEL = 4096
BATCH = 4
SEQ = 2048
DEC_BATCH = 32
PAST_LEN = 8192
PAGE_SIZE = 128
N_PAGES = PAST_LEN // PAGE_SIZE
ATT_HEADS = 16
ATT_KV_HEADS = 8
ATT_HEAD_DIM = 128
ROPE_THETA = 500000.0
IDX_HEADS = 32
IDX_DIM = 64
TOPK = 256
RET_HEADS = 8
RET_DK = 256
RET_DV = 512
RET_THETA = 10000.0
RET_CHUNK = 128
D_FF = 4 * D_MODEL
NORM_EPS = 1e-6

ATT_Q_W = ATT_HEADS * ATT_HEAD_DIM
ATT_KV_W = ATT_KV_HEADS * ATT_HEAD_DIM
IDX_Q_W = IDX_HEADS * IDX_DIM
RET_QK_W = RET_HEADS * RET_DK
RET_V_W = RET_HEADS * RET_DV
SPLITS = (ATT_Q_W, ATT_KV_W, ATT_KV_W, IDX_Q_W, IDX_DIM, IDX_HEADS,
          RET_QK_W, RET_QK_W, RET_V_W, RET_V_W, D_MODEL, D_MODEL)
OFFS = tuple(int(v) for v in np.concatenate([[0], np.cumsum(SPLITS)]))

LANES = 128
SUBLANES = 8
Q_BLOCK = 128
KEY_CHUNK = 512
PAIRS_PER_STEP = 16
GROUPS_PER_STEP = 8
VT_ONES = 16
VMEM_LIMIT = 56 * 1024 * 1024

INT_MIN = -2 ** 31
NEG_BIG = -1e30
ATT_SCALE = ATT_HEAD_DIM ** -0.5
IDX_SCALE = (IDX_DIM ** -0.5) * (IDX_HEADS ** -0.5)

_NT = (((1,), (1,)), ((), ()))


def _params(sem):
    return pltpu.CompilerParams(dimension_semantics=sem, vmem_limit_bytes=VMEM_LIMIT)


def _sigmoid(x):
    return 1.0 / (1.0 + jnp.exp(-x))


def _rmsnorm_cast_kernel(x_ref, w_ref, o_ref):
    x = x_ref[...]
    y = x * lax.rsqrt(jnp.mean(x * x, axis=-1, keepdims=True) + NORM_EPS)
    o_ref[...] = (y * w_ref[...]).astype(o_ref.dtype)


def _rmsnorm_cast(x, w, tr):
    m, d = x.shape
    return pl.pallas_call(
        _rmsnorm_cast_kernel,
        grid=(m // tr,),
        in_specs=[pl.BlockSpec((tr, d), lambda i: (i, 0)), pl.BlockSpec((1, d), lambda i: (0, 0))],
        out_specs=pl.BlockSpec((tr, d), lambda i: (i, 0)),
        out_shape=jax.ShapeDtypeStruct((m, d), BF16),
        compiler_params=_params(("parallel",)),
        name="rmsnorm_cast",
    )(x, w.reshape(1, d))


def _post_attn_kernel(x_ref, y_ref, w1_ref, w2_ref, h_ref, hn_ref):
    y = y_ref[...]
    yn = y * lax.rsqrt(jnp.mean(y * y, axis=-1, keepdims=True) + NORM_EPS) * w1_ref[...]
    h = x_ref[...] + yn
    h_ref[...] = h
    hn = h * lax.rsqrt(jnp.mean(h * h, axis=-1, keepdims=True) + NORM_EPS) * w2_ref[...]
    hn_ref[...] = hn.astype(hn_ref.dtype)


def _post_attn(x, y, w1, w2, tr):
    m, d = x.shape
    row = pl.BlockSpec((tr, d), lambda i: (i, 0))
    vec = pl.BlockSpec((1, d), lambda i: (0, 0))
    return pl.pallas_call(
        _post_attn_kernel,
        grid=(m // tr,),
        in_specs=[row, row, vec, vec],
        out_specs=[row, row],
        out_shape=[jax.ShapeDtypeStruct((m, d), F32), jax.ShapeDtypeStruct((m, d), BF16)],
        compiler_params=_params(("parallel",)),
        name="post_attn_norm",
    )(x, y, w1.reshape(1, d), w2.reshape(1, d))


def _post_mlp_kernel(h_ref, d_ref, w_ref, o_ref):
    d = d_ref[...]
    dn = d * lax.rsqrt(jnp.mean(d * d, axis=-1, keepdims=True) + NORM_EPS) * w_ref[...]
    o_ref[...] = h_ref[...] + dn


def _post_mlp(h, d, w, tr):
    m, dm = h.shape
    row = pl.BlockSpec((tr, dm), lambda i: (i, 0))
    return pl.pallas_call(
        _post_mlp_kernel,
        grid=(m // tr,),
        in_specs=[row, row, pl.BlockSpec((1, dm), lambda i: (0, 0))],
        out_specs=row,
        out_shape=jax.ShapeDtypeStruct((m, dm), F32),
        compiler_params=_params(("parallel",)),
        name="post_mlp_norm",
    )(h, d, w.reshape(1, dm))


def _dot(a, b, b_t):
    if b_t:
        return lax.dot_general(a, b, _NT, preferred_element_type=F32)
    return jnp.dot(a, b, preferred_element_type=F32)


def _mm_kernel(*refs, n_extra, n_out, nk, b_t, epilogue):
    a_ref, b_ref = refs[0], refs[1]
    extra = refs[2:2 + n_extra]
    outs = refs[2 + n_extra:2 + n_extra + n_out]
    if nk == 1:
        epilogue(_dot(a_ref[...], b_ref[...], b_t), extra, outs)
        return
    acc_ref = refs[-1]
    k = pl.program_id(2)
    d = _dot(a_ref[...], b_ref[...], b_t)

    @pl.when(k == 0)
    def _():
        acc_ref[...] = d

    @pl.when((k > 0) & (k < nk - 1))
    def _():
        acc_ref[...] += d

    @pl.when(k == nk - 1)
    def _():
        epilogue(acc_ref[...] + d, extra, outs)


def _matmul(a, b, epilogue, extras, outs, *, tm, tn, tk=None, b_t=False, name):
    m, kd = a.shape
    n = b.shape[0] if b_t else b.shape[1]
    tk = kd if tk is None else tk
    nk = kd // tk
    grid = (m // tm, n // tn, nk)

    def lift(f):
        return lambda i, j, k: f(i, j)

    b_spec = (pl.BlockSpec((tn, tk), lambda i, j, k: (j, k)) if b_t
              else pl.BlockSpec((tk, tn), lambda i, j, k: (k, j)))
    in_specs = [pl.BlockSpec((tm, tk), lambda i, j, k: (i, k)), b_spec]
    in_specs += [pl.BlockSpec(bs, lift(im)) for _, bs, im in extras]
    out_specs = [pl.BlockSpec(bs, lift(im)) for _, _, bs, im in outs]
    out_shape = [jax.ShapeDtypeStruct(s, dt) for s, dt, _, _ in outs]
    scratch = [pltpu.VMEM((tm, tn), F32)] if nk > 1 else []
    res = pl.pallas_call(
        functools.partial(_mm_kernel, n_extra=len(extras), n_out=len(outs), nk=nk, b_t=b_t, epilogue=epilogue),
        grid=grid,
        in_specs=in_specs,
        out_specs=out_specs,
        out_shape=out_shape,
        scratch_shapes=scratch,
        compiler_params=_params(("parallel", "parallel", "arbitrary")),
        name=name,
    )(a, b, *[e[0] for e in extras])
    return res


def _mm_w32_kernel(*refs, n_extra, n_out, w_t, valid, epilogue):
    a_ref, w_ref = refs[0], refs[1]
    extra = refs[2:2 + n_extra]
    outs = refs[2 + n_extra:2 + n_extra + n_out]
    wbf_ref = refs[2 + n_extra + n_out]

    @pl.when(pl.program_id(1) == 0)
    def _():
        w = w_ref[...]
        n_axis = 0 if w_t else 1
        if valid < w.shape[n_axis]:
            w = jnp.where(lax.broadcasted_iota(I32, w.shape, n_axis) < valid, w, 0.0)
        wbf_ref[...] = w.astype(BF16)

    epilogue(_dot(a_ref[...], wbf_ref[...], w_t), extra, outs)


def _matmul_w32(a, w, epilogue, extras, outs, *, tm, tn, n, col_off=0, w_t=False, valid=None, name):
    m, kd = a.shape
    assert n % tn == 0 and col_off % (SUBLANES if w_t else tn) == 0
    valid = tn if valid is None else valid
    grid = (n // tn, m // tm)

    def lift(f):
        return lambda j, i: f(i, j)

    if w_t:
        w_spec = pl.BlockSpec((pl.Element(tn), pl.Element(kd)),
                              lambda j, i: (pl.multiple_of(col_off + j * tn, SUBLANES), 0))
        wbf_spec, wbf_shape = pl.BlockSpec((tn, kd), lambda j, i: (j, 0)), (n, kd)
    else:
        w_spec = pl.BlockSpec((kd, tn), lambda j, i: (0, col_off // tn + j))
        wbf_spec, wbf_shape = pl.BlockSpec((kd, tn), lambda j, i: (0, j)), (kd, n)
    in_specs = [pl.BlockSpec((tm, kd), lambda j, i: (i, 0)), w_spec]
    in_specs += [pl.BlockSpec(bs, lift(im)) for _, bs, im in extras]
    out_specs = [pl.BlockSpec(bs, lift(im)) for _, _, bs, im in outs] + [wbf_spec]
    out_shape = [jax.ShapeDtypeStruct(s, dt) for s, dt, _, _ in outs] + [jax.ShapeDtypeStruct(wbf_shape, BF16)]
    return pl.pallas_call(
        functools.partial(_mm_w32_kernel, n_extra=len(extras), n_out=len(outs), w_t=w_t, valid=valid,
                          epilogue=epilogue),
        grid=grid,
        in_specs=in_specs,
        out_specs=out_specs,
        out_shape=out_shape,
        compiler_params=_params(("arbitrary", "arbitrary")),
        name=name,
    )(a, w, *[e[0] for e in extras])


def _nat(m, n, dt, tm, tn):
    return ((m, n), dt, (tm, tn), lambda i, j: (i, j))


def _rope_lanes(y, c, sm, sp, half):
    n = y.shape[-1]
    return y * c + pltpu.roll(y, n - half, 1) * sm + pltpu.roll(y, half, 1) * sp


def _ep_plain(acc, extra, outs):
    for o in outs:
        o[...] = acc.astype(o.dtype)


def _ep_rope_lanes(acc, extra, outs, *, half, blocked):
    c, sm, sp = extra[0][...], extra[1][...], extra[2][...]
    tm, tn = acc.shape
    for jj in range(tn // LANES):
        y = _rope_lanes(acc[:, jj * LANES:(jj + 1) * LANES], c, sm, sp, half)
        if blocked:
            for r in range(tm // Q_BLOCK):
                outs[0][r, jj] = y[r * Q_BLOCK:(r + 1) * Q_BLOCK].astype(outs[0].dtype)
        else:
            outs[0][:, jj * LANES:(jj + 1) * LANES] = y.astype(outs[0].dtype)
        if len(outs) > 1:
            outs[1][jj] = y.astype(outs[1].dtype)


def _ep_ret_qk(acc, extra, outs, *, scale):
    cos, sin, dec = extra[0][...], extra[1][...], extra[2]
    tn = acc.shape[1]
    for hh in range(tn // RET_DK):
        lo = hh * RET_DK
        x1 = acc[:, lo:lo + LANES]
        x2 = acc[:, lo + LANES:lo + 2 * LANES]
        o1 = x1 * cos - x2 * sin
        o2 = x2 * cos + x1 * sin
        if scale != 1.0:
            o1 = o1 * scale
            o2 = o2 * scale
        d = dec[:, hh * LANES:(hh + 1) * LANES]
        outs[0][:, lo:lo + LANES] = o1.astype(outs[0].dtype)
        outs[0][:, lo + LANES:lo + 2 * LANES] = o2.astype(outs[0].dtype)
        outs[1][:, lo:lo + LANES] = (o1 * d).astype(outs[1].dtype)
        outs[1][:, lo + LANES:lo + 2 * LANES] = (o2 * d).astype(outs[1].dtype)


def _merge_kernel(att_ref, ret_ref, wa_ref, wr_ref, ga_ref, gr_ref, o_ref):
    a = jnp.dot(att_ref[...], wa_ref[...], preferred_element_type=F32)
    r = jnp.dot(ret_ref[...], wr_ref[...], preferred_element_type=F32)
    o_ref[...] = (_sigmoid(ga_ref[...]) * a + _sigmoid(gr_ref[...]) * r).astype(o_ref.dtype)


def _merge(att, ret, wa, wr, gates, tm, tn):
    m = att.shape[0]
    nb = D_MODEL // tn
    return pl.pallas_call(
        _merge_kernel,
        grid=(m // tm, nb),
        in_specs=[
            pl.BlockSpec((tm, ATT_Q_W), lambda i, j: (i, 0)),
            pl.BlockSpec((tm, RET_V_W), lambda i, j: (i, 0)),
            pl.BlockSpec((ATT_Q_W, tn), lambda i, j: (0, j)),
            pl.BlockSpec((RET_V_W, tn), lambda i, j: (0, j)),
            pl.BlockSpec((tm, tn), lambda i, j: (i, nb + j)),
            pl.BlockSpec((tm, tn), lambda i, j: (i, 2 * nb + j)),
        ],
        out_specs=pl.BlockSpec((tm, tn), lambda i, j: (i, j)),
        out_shape=jax.ShapeDtypeStruct((m, D_MODEL), BF16),
        compiler_params=_params(("parallel", "parallel")),
        name="merge_proj",
    )(att, ret, wa, wr, gates, gates)


def _ep_relu2(acc, extra, outs):
    u = jnp.maximum(acc, 0.0)
    outs[0][...] = (u * u).astype(outs[0].dtype)


def _rope_lane_tables(pos, head_w, rot, theta):
    half = rot // 2
    inv_freq = jnp.exp(-math.log(theta) * jnp.arange(half, dtype=F32) / half)
    ang = pos.astype(F32)[:, None] * inv_freq[None, :]
    cos, sin = jnp.cos(ang), jnp.sin(ang)
    n = pos.shape[0]
    z_half = jnp.zeros((n, half), F32)
    rest1 = jnp.ones((n, head_w - rot), F32)
    rest0 = jnp.zeros((n, head_w - rot), F32)
    c = jnp.concatenate([cos, cos, rest1], axis=1)
    sm = jnp.concatenate([-sin, z_half, rest0], axis=1)
    sp = jnp.concatenate([z_half, sin, rest0], axis=1)
    rep = LANES // head_w
    return [jnp.tile(t, (1, rep)) for t in (c, sm, sp)]


def _ret_tables(pos, chunk_pos, chunk_len, log_gamma):
    half = RET_DK // 2
    inv_freq = jnp.exp(-math.log(RET_THETA) * jnp.arange(half, dtype=F32) / half)
    ang = pos.astype(F32)[:, None] * inv_freq[None, :]
    i = chunk_pos.astype(F32)[:, None]
    q_dec = jnp.exp(log_gamma[None, :] * (i + 1.0))
    k_dec = jnp.exp(log_gamma[None, :] * (chunk_len - 1.0 - i))
    return jnp.cos(ang), jnp.sin(ang), jnp.repeat(q_dec, LANES, axis=1), jnp.repeat(k_dec, LANES, axis=1)


PROJ_TN = 512


def _project(xn, w, pos, chunk_pos, chunk_len, log_gamma, *, tm, prompt):
    m = xn.shape[0]
    nrep = max(pos.shape[0] // tm, 1)
    wdt = BF16 if prompt else F32
    out = {"w_bf16": {}}

    def mm(name, a, b, epilogue, extras, outs, tn=PROJ_TN, valid_cols=None):
        n = -(-(OFFS[b] - OFFS[a]) // tn) * tn
        if prompt:
            *res, wb = _matmul_w32(xn, w, epilogue, extras, outs, tm=tm, tn=tn, n=n, col_off=OFFS[a], w_t=True,
                                   valid=valid_cols, name=name)
            out["w_bf16"][name] = wb
            return res
        return _matmul(xn, w[name], epilogue, extras, outs, tm=tm, tn=tn, b_t=True, name=name)

    def tab(t, width=LANES, by_col=False):
        if by_col:
            return (t, (tm, width), lambda i, j: (i % nrep, j))
        return (t, (tm, width), lambda i, j: (i % nrep, 0))

    att_t = [tab(t) for t in _rope_lane_tables(pos, ATT_HEAD_DIM, ATT_HEAD_DIM // 4, ROPE_THETA)]
    idx_t = [tab(t) for t in _rope_lane_tables(pos, IDX_DIM, IDX_DIM // 4, ROPE_THETA)]
    r_cos, r_sin, q_dec, k_dec = _ret_tables(pos, chunk_pos, chunk_len, log_gamma)
    tn = PROJ_TN
    nqb = m // Q_BLOCK

    if prompt:
        o = [((nqb, ATT_HEADS, Q_BLOCK, LANES), BF16, (tm // Q_BLOCK, tn // LANES, Q_BLOCK, LANES),
              lambda i, j: (i, j, 0, 0))]
    else:
        o = [_nat(m, ATT_Q_W, F32, tm, tn)]
    out["q"], = mm("proj_q", 0, 1, functools.partial(_ep_rope_lanes, half=16, blocked=prompt), att_t, o)
    o = [_nat(m, ATT_KV_W, F32, tm, tn)]
    if prompt:
        per_b = SEQ // tm
        o.append(((BATCH, ATT_KV_HEADS, SEQ, LANES), BF16, (None, tn // LANES, tm, LANES),
                  lambda i, j: (i // per_b, j, i % per_b, 0)))
    res = mm("proj_k", 1, 2, functools.partial(_ep_rope_lanes, half=16, blocked=False), att_t, o)
    out["k"] = res[0]
    if prompt:
        out["k_heads"] = res[1]
    o = [_nat(m, ATT_KV_W, F32, tm, tn)]
    if prompt:
        o.append(_nat(m, ATT_KV_W, BF16, tm, tn))
    res = mm("proj_v", 2, 3, _ep_plain, [], o)
    out["v"] = res[0]
    if prompt:
        out["v_bf"] = res[1]
    if prompt:
        o = [((nqb, IDX_Q_W // LANES, Q_BLOCK, LANES), F32, (tm // Q_BLOCK, tn // LANES, Q_BLOCK, LANES),
              lambda i, j: (i, j, 0, 0))]
    else:
        o = [_nat(m, IDX_Q_W, F32, tm, tn)]
    out["iq"], = mm("proj_iq", 3, 4, functools.partial(_ep_rope_lanes, half=8, blocked=prompt), idx_t, o)
    ikw_t = [tab(t) for t in _rope_lane_tables(pos, LANES, IDX_DIM // 4, ROPE_THETA)]
    out["ikw"], = mm("proj_ikw", 4, 6, functools.partial(_ep_rope_lanes, half=8, blocked=False), ikw_t,
                     [_nat(m, LANES, F32, tm, LANES)], tn=LANES, valid_cols=IDX_DIM + IDX_HEADS)
    for nm, grp, dec, scale in (("rq", 6, q_dec, 1.0), ("rk", 7, k_dec, RET_DK ** -0.5)):
        res = mm("proj_" + nm, grp, grp + 1, functools.partial(_ep_ret_qk, scale=scale),
                 [tab(r_cos), tab(r_sin), tab(dec, tn // 2, by_col=True)],
                 [_nat(m, RET_QK_W, wdt, tm, tn), _nat(m, RET_QK_W, wdt, tm, tn)])
        out[nm], out[nm + "d"] = res
    out["rv"], = mm("proj_rv", 8, 9, _ep_plain, [], [_nat(m, RET_V_W, wdt, tm, tn)])
    out["gates"], = mm("proj_gates", 9, 12, _ep_plain, [], [_nat(m, 3 * D_MODEL, F32, tm, tn)])
    return out


def _sortable_key(score):
    kb = lax.bitcast_convert_type(score, I32)
    kb = jnp.where(kb == INT_MIN, 0, kb)
    return jnp.where(kb < 0, kb ^ 0x7FFFFFFF, kb)


def _tile_reduce(x, op):
    tiles = [x[i * SUBLANES:(i + 1) * SUBLANES] for i in range(x.shape[0] // SUBLANES)]
    while len(tiles) > 1:
        nxt = [op(tiles[i], tiles[i + 1]) for i in range(0, len(tiles) - 1, 2)]
        if len(tiles) % 2:
            nxt.append(tiles[-1])
        tiles = nxt
    return tiles[0]


def _topk_mask(sc_ref, key_ref, bias_ref, q_pos, n_chunks):
    def rows(c):
        r0 = pl.multiple_of(c * KEY_CHUNK, KEY_CHUNK)
        return pl.ds(r0, KEY_CHUNK), r0 + lax.broadcasted_iota(I32, (KEY_CHUNK, LANES), 0)

    def build(c, carry):
        sl, row = rows(c)
        key_ref[sl, :] = jnp.where(row <= q_pos, _sortable_key(sc_ref[sl, :]), INT_MIN)
        return carry

    lax.fori_loop(0, n_chunks, build, 0)

    def count(pred):
        def body(c, acc):
            sl, row = rows(c)
            return acc + _tile_reduce(jnp.where(pred(key_ref[sl, :], row), 1.0, 0.0), jnp.add)

        part = lax.fori_loop(0, n_chunks, body, jnp.zeros((SUBLANES, LANES), F32))
        return jnp.sum(part, axis=0, keepdims=True)

    t0 = jnp.where(count(lambda k, row: k >= 0) >= TOPK, 0, INT_MIN).astype(I32)

    def search(i, t):
        cand = t | lax.shift_left(jnp.int32(1), jnp.int32(30) - i)
        return jnp.where(count(lambda k, row: k >= cand) >= TOPK, cand, t)

    thr = lax.fori_loop(0, 31, search, t0)
    n_eq = count(lambda k, row: (row <= q_pos) & (k == thr))
    need = TOPK - count(lambda k, row: k > thr)

    def write(c, carry):
        sl, row = rows(c)
        k = key_ref[sl, :]
        sel = jnp.where(k > thr, 0.0, jnp.where((row <= q_pos) & (k == thr), 0.0, NEG_BIG))
        bias_ref[sl, :] = sel.astype(bias_ref.dtype)
        return carry

    lax.fori_loop(0, n_chunks, write, 0)

    @pl.when(jnp.max(n_eq - need) > 0)
    def _():
        r_i = lax.broadcasted_iota(I32, (LANES, LANES), 0)
        c_i = lax.broadcasted_iota(I32, (LANES, LANES), 1)
        tri = jnp.where(c_i < r_i, 1.0, 0.0).astype(BF16)

        def chunk(c, off):
            r0 = pl.multiple_of(c * LANES, LANES)
            kc = key_ref[pl.ds(r0, LANES), :]
            rc = r0 + lax.broadcasted_iota(I32, (LANES, LANES), 0)
            e = (rc <= q_pos) & (kc == thr)
            ef = jnp.where(e, 1.0, 0.0)
            before = jnp.dot(tri, ef.astype(BF16), preferred_element_type=F32) + off
            keep = e & (before < need)
            sel = jnp.where(kc > thr, 0.0, jnp.where(keep, 0.0, NEG_BIG))
            bias_ref[pl.ds(r0, LANES), :] = sel.astype(bias_ref.dtype)
            return off + jnp.sum(ef, axis=0, keepdims=True)

        lax.fori_loop(0, n_chunks * (KEY_CHUNK // LANES), chunk, jnp.zeros((1, LANES), F32))


def _split_bf16(x):
    hi = x.astype(BF16).astype(F32)
    return hi, x - hi


def _dsa_prompt_kernel(iq_ref, ikw_all_ref, ikw_q_ref, q_ref, k_ref, vt_ref, att_ref,
                       lhs_ref, wt_ref, sc_ref, key_ref, bias_ref):
    qb = pl.program_id(1)
    n_ck = (qb * Q_BLOCK + Q_BLOCK + KEY_CHUNK - 1) // KEY_CHUNK
    lane = lax.broadcasted_iota(I32, (1, LANES), 1)
    low = lane < IDX_DIM

    def chunk_rows(c):
        return pl.ds(pl.multiple_of(c * KEY_CHUNK, KEY_CHUNK), KEY_CHUNK)

    @pl.when(qb == 0)
    def _():
        x = jnp.where(low, ikw_all_ref[...], 0.0)
        hi, lo = _split_bf16(x)
        lhs_ref[:, :LANES] = (hi + pltpu.roll(lo, IDX_DIM, 1)).astype(BF16)
        lhs_ref[:, LANES:] = hi.astype(BF16)

    wt_ref[...] = ikw_q_ref[...].T * IDX_SCALE
    sc_ref[...] = jnp.zeros_like(sc_ref)

    def pairs_body(pq, carry):
        rhs, wts = [], []
        for u in range(PAIRS_PER_STEP):
            p = pq * PAIRS_PER_STEP + u
            hi, lo = _split_bf16(iq_ref[p])
            rhi = pltpu.roll(hi, IDX_DIM, 1)
            rlo = pltpu.roll(lo, IDX_DIM, 1)
            ra = jnp.concatenate([jnp.where(low, hi, rhi), jnp.where(low, lo, 0.0)], axis=1)
            rb = jnp.concatenate([jnp.where(low, rhi, hi), jnp.where(low, rlo, 0.0)], axis=1)
            rhs.append(jnp.concatenate([ra, rb], axis=0).astype(BF16))
            wts.append((wt_ref[pl.ds(IDX_DIM + 2 * p, 1), :], wt_ref[pl.ds(IDX_DIM + 2 * p + 1, 1), :]))

        def ck_body(c, inner):
            sl = chunk_rows(c)
            lhs = lhs_ref[sl, :]
            acc = sc_ref[sl, :]
            for rhs_t, (wa, wb) in zip(rhs, wts):
                d = lax.dot_general(lhs, rhs_t, _NT, preferred_element_type=F32)
                acc = acc + wa * jnp.maximum(d[:, :LANES], 0.0) + wb * jnp.maximum(d[:, LANES:], 0.0)
            sc_ref[sl, :] = acc
            return inner

        lax.fori_loop(0, n_ck, ck_body, 0)
        return carry

    lax.fori_loop(0, IDX_HEADS // 2 // PAIRS_PER_STEP, pairs_body, 0)

    _topk_mask(sc_ref, key_ref, bias_ref, qb * Q_BLOCK + lane, n_ck)

    c_exp = ATT_SCALE * math.log2(math.e)

    r_i = lax.broadcasted_iota(I32, (2 * Q_BLOCK, LANES), 0)
    c_i = lax.broadcasted_iota(I32, (2 * Q_BLOCK, LANES), 1)
    onehot_q = jnp.where((r_i % Q_BLOCK) == c_i, 1.0, 0.0).astype(BF16)
    n_acc = vt_ref.shape[2]

    def groups_body(gq, carry):
        gs = [gq * GROUPS_PER_STEP + u for u in range(GROUPS_PER_STEP)]
        qqs = [jnp.concatenate([jnp.concatenate([q_ref[2 * g], q_ref[2 * g + 1]], axis=0), onehot_q], axis=1)
               for g in gs]

        def ck_body(c, state):
            sl = chunk_rows(c)
            mask = bias_ref[sl, :]
            lgs = [lax.dot_general(jnp.concatenate([k_ref[g, sl, :], mask], axis=1), qq, _NT,
                                   preferred_element_type=F32)
                   for g, qq in zip(gs, qqs)]
            soft = []
            for lg, (m_old, acc) in zip(lgs, state):
                m_new = jnp.maximum(m_old, jnp.max(_tile_reduce(lg, jnp.maximum), axis=0, keepdims=True))
                alpha = jnp.exp2((m_old - m_new) * c_exp)
                soft.append((m_new, alpha, jnp.exp2((lg - m_new) * c_exp).astype(BF16)))
            return tuple((m_new, alpha * acc + jnp.dot(vt_ref[g, c], pt, preferred_element_type=F32))
                         for g, (m_new, alpha, pt), (_, acc) in zip(gs, soft, state))

        init = (jnp.full((1, 2 * LANES), NEG_BIG, F32), jnp.zeros((n_acc, 2 * LANES), F32))
        final = lax.fori_loop(0, n_ck, ck_body, (init,) * GROUPS_PER_STEP)
        for g, (_, acc) in zip(gs, final):
            o = acc[:ATT_HEAD_DIM] / acc[ATT_HEAD_DIM:ATT_HEAD_DIM + 1]
            for hh in range(2):
                att_ref[2 * g + hh] = o[:, hh * LANES:(hh + 1) * LANES].T.astype(att_ref.dtype)
        return carry

    lax.fori_loop(0, ATT_KV_HEADS // GROUPS_PER_STEP, groups_body, 0)


def _dsa_prompt(iq, ikw, q, k_heads, vt):
    nqb = SEQ // Q_BLOCK
    n_pair = IDX_Q_W // LANES
    return pl.pallas_call(
        _dsa_prompt_kernel,
        grid=(BATCH, nqb),
        in_specs=[
            pl.BlockSpec((None, n_pair, Q_BLOCK, LANES), lambda b, i: (b * nqb + i, 0, 0, 0)),
            pl.BlockSpec((SEQ, LANES), lambda b, i: (b, 0)),
            pl.BlockSpec((Q_BLOCK, LANES), lambda b, i: (b * nqb + i, 0)),
            pl.BlockSpec((None, ATT_HEADS, Q_BLOCK, LANES), lambda b, i: (b * nqb + i, 0, 0, 0)),
            pl.BlockSpec((None, ATT_KV_HEADS, SEQ, LANES), lambda b, i: (b, 0, 0, 0)),
            pl.BlockSpec((None, ATT_KV_HEADS, SEQ // KEY_CHUNK, ATT_HEAD_DIM + VT_ONES, KEY_CHUNK),
                         lambda b, i: (b, 0, 0, 0, 0)),
        ],
        out_specs=pl.BlockSpec((None, ATT_HEADS, Q_BLOCK, LANES), lambda b, i: (b * nqb + i, 0, 0, 0)),
        out_shape=jax.ShapeDtypeStruct((BATCH * nqb, ATT_HEADS, Q_BLOCK, LANES), BF16),
        scratch_shapes=[
            pltpu.VMEM((SEQ, 2 * LANES), BF16),
            pltpu.VMEM((LANES, LANES), F32),
            pltpu.VMEM((SEQ, LANES), F32),
            pltpu.VMEM((SEQ, LANES), I32),
            pltpu.VMEM((SEQ, LANES), BF16),
        ],
        compiler_params=_params(("parallel", "arbitrary")),
        name="dsa_prompt",
    )(iq, ikw, ikw, q, k_heads, vt)


def _group_norm_gate(o, gn_w, rg):
    of = o * lax.rsqrt(jnp.mean(o * o, axis=-1, keepdims=True) + NORM_EPS) * gn_w
    return of * (rg * _sigmoid(rg))


def _ret_prompt_kernel(q_ref, qd_ref, k_ref, kd_ref, v_ref, rg_ref, decay_ref, cdec_ref, gnw_ref,
                       ret_ref, s_ref):
    c = pl.program_id(1)

    @pl.when(c == 0)
    def _():
        s_ref[...] = jnp.zeros_like(s_ref)

    for h in range(RET_HEADS):
        qk = slice(h * RET_DK, (h + 1) * RET_DK)
        vv = slice(h * RET_DV, (h + 1) * RET_DV)
        s = s_ref[h]
        v = v_ref[:, vv]
        scores = lax.dot_general(q_ref[:, qk], k_ref[:, qk], _NT, preferred_element_type=F32) * decay_ref[h]
        o = (jnp.dot(scores.astype(BF16), v, preferred_element_type=F32)
             + jnp.dot(qd_ref[:, qk], s.astype(BF16), preferred_element_type=F32))
        kv = lax.dot_general(kd_ref[:, qk], v, (((0,), (0,)), ((), ())), preferred_element_type=F32)
        s_ref[h] = cdec_ref[h] * s + kv
        ret_ref[:, vv] = _group_norm_gate(o, gnw_ref[:, vv], rg_ref[:, vv]).astype(ret_ref.dtype)


def _ret_prompt(pr, decay, cdec, gn_w):
    nc = SEQ // RET_CHUNK
    qk = pl.BlockSpec((RET_CHUNK, RET_QK_W), lambda b, c: (b * nc + c, 0))
    vv = pl.BlockSpec((RET_CHUNK, RET_V_W), lambda b, c: (b * nc + c, 0))
    return pl.pallas_call(
        _ret_prompt_kernel,
        grid=(BATCH, nc),
        in_specs=[qk, qk, qk, qk, vv, vv,
                  pl.BlockSpec((RET_HEADS, RET_CHUNK, RET_CHUNK), lambda b, c: (0, 0, 0)),
                  pl.BlockSpec((RET_HEADS, 1, RET_DV), lambda b, c: (0, 0, 0)),
                  pl.BlockSpec((1, RET_V_W), lambda b, c: (0, 0))],
        out_specs=[vv, pl.BlockSpec((None, RET_HEADS, RET_DK, RET_DV), lambda b, c: (b, 0, 0, 0))],
        out_shape=[jax.ShapeDtypeStruct((BATCH * SEQ, RET_V_W), BF16),
                   jax.ShapeDtypeStruct((BATCH, RET_HEADS, RET_DK, RET_DV), F32)],
        compiler_params=_params(("parallel", "arbitrary")),
        name="retention_prompt",
    )(pr["rq"], pr["rqd"], pr["rk"], pr["rkd"], pr["rv"], pr["gates"], decay, cdec, gn_w.reshape(1, RET_V_W))


def _ret_sample_kernel(q_ref, qd_ref, kt_ref, k_ref, v_ref, rg_ref, cdec_ref, gnw_ref, s_ref, ret_ref, so_ref):
    b = pl.program_id(0)
    nb = kt_ref.shape[1]
    onehot = lax.broadcasted_iota(I32, (1, nb), 1) == b
    for h in range(RET_HEADS):
        qk_sl = slice(h * RET_DK, (h + 1) * RET_DK)
        vv = slice(h * RET_DV, (h + 1) * RET_DV)
        s = s_ref[h]
        v = v_ref[pl.ds(b, 1), vv]
        q = q_ref[pl.ds(b, 1), qk_sl]
        k = k_ref[pl.ds(b, 1), qk_sl]
        qk = jnp.sum(q.astype(BF16).astype(F32) * k.astype(BF16).astype(F32), axis=-1, keepdims=True)
        qd8 = jnp.broadcast_to(qd_ref[pl.ds(b, 1), qk_sl], (16, RET_DK)).astype(BF16)
        o = qk.astype(BF16).astype(F32) * v.astype(BF16).astype(F32) \
            + jnp.dot(qd8, s.astype(BF16), preferred_element_type=F32)[0:1]
        k_col = jnp.sum(jnp.where(onehot, kt_ref[qk_sl, :], 0.0), axis=1, keepdims=True)
        so_ref[h] = cdec_ref[h] * s + k_col * v
        ret_ref[:, vv] = _group_norm_gate(o, gnw_ref[:, vv], rg_ref[pl.ds(b, 1), vv])


def _ret_sample(ps, state, cdec, gn_w):
    nb = DEC_BATCH
    qk = pl.BlockSpec((nb, RET_QK_W), lambda b: (0, 0))
    vv = pl.BlockSpec((nb, RET_V_W), lambda b: (0, 0))
    st = pl.BlockSpec((None, RET_HEADS, RET_DK, RET_DV), lambda b: (b, 0, 0, 0))
    return pl.pallas_call(
        _ret_sample_kernel,
        grid=(nb,),
        in_specs=[qk, qk, pl.BlockSpec((RET_QK_W, nb), lambda b: (0, 0)), qk, vv, vv,
                  pl.BlockSpec((RET_HEADS, 1, RET_DV), lambda b: (0, 0, 0)),
                  pl.BlockSpec((1, RET_V_W), lambda b: (0, 0)), st],
        out_specs=[pl.BlockSpec((None, 1, RET_V_W), lambda b: (b, 0, 0)), st],
        out_shape=[jax.ShapeDtypeStruct((nb, 1, RET_V_W), F32),
                   jax.ShapeDtypeStruct((nb, RET_HEADS, RET_DK, RET_DV), F32)],
        compiler_params=_params(("arbitrary",)),
        name="retention_sample",
    )(ps["rq"], ps["rqd"], ps["rkd"].T, ps["rk"], ps["rv"], ps["gates"], cdec, gn_w.reshape(1, RET_V_W), state)


PAGES_PER_STEP = 8
IDX_PAGES_PER_STEP = 32


def _idx_sample_kernel(pt_ref, iq_ref, w_ref, new_ref, *rest):
    pages = rest[:IDX_PAGES_PER_STEP]
    sc_ref, scn_ref = rest[IDX_PAGES_PER_STEP], rest[IDX_PAGES_PER_STEP + 1]
    j = pl.program_id(1)
    hi, lo = _split_bf16(iq_ref[...])
    lhs = jnp.concatenate([hi + pltpu.roll(hi, IDX_DIM, 1), lo], axis=1).astype(BF16)
    w = w_ref[...] * IDX_SCALE

    def page_scores(xt):
        khi, klo = _split_bf16(xt)
        rhs = jnp.concatenate([khi, klo, khi, jnp.zeros_like(khi)], axis=0).astype(BF16)
        d = jnp.dot(lhs, rhs, preferred_element_type=F32)
        return jnp.sum(w * jnp.maximum(d, 0.0), axis=0, keepdims=True)

    for i in range(IDX_PAGES_PER_STEP):
        sc_ref[:, i * PAGE_SIZE:(i + 1) * PAGE_SIZE] = page_scores(pages[i][...])

    @pl.when(j == pl.num_programs(1) - 1)
    def _():
        scn_ref[...] = page_scores(new_ref[...])


def _idx_sample(page_table, iq3, w3, new_pages, cache_idx):
    nsteps = N_PAGES // IDX_PAGES_PER_STEP

    def page_spec(i):
        return pl.BlockSpec((None, IDX_DIM, PAGE_SIZE),
                            lambda b, j, pt: (pt[b, j * IDX_PAGES_PER_STEP + i], 0, 0))

    grid_spec = pltpu.PrefetchScalarGridSpec(
        num_scalar_prefetch=1,
        grid=(DEC_BATCH, nsteps),
        in_specs=[pl.BlockSpec((None, IDX_HEADS, LANES), lambda b, j, pt: (b, 0, 0)),
                  pl.BlockSpec((None, IDX_HEADS, 1), lambda b, j, pt: (b, 0, 0)),
                  pl.BlockSpec((None, IDX_DIM, PAGE_SIZE), lambda b, j, pt: (b, 0, 0))]
                 + [page_spec(i) for i in range(IDX_PAGES_PER_STEP)],
        out_specs=[pl.BlockSpec((None, 1, IDX_PAGES_PER_STEP * PAGE_SIZE), lambda b, j, pt: (b, 0, j)),
                   pl.BlockSpec((None, 1, PAGE_SIZE), lambda b, j, pt: (b, 0, 0))],
    )
    return pl.pallas_call(
        _idx_sample_kernel,
        grid_spec=grid_spec,
        out_shape=[jax.ShapeDtypeStruct((DEC_BATCH, 1, PAST_LEN), F32),
                   jax.ShapeDtypeStruct((DEC_BATCH, 1, PAGE_SIZE), F32)],
        compiler_params=_params(("parallel", "arbitrary")),
        name="indexer_sample",
    )(page_table, iq3, w3, new_pages, *([cache_idx] * IDX_PAGES_PER_STEP))


def _select_sample_kernel(sc_ref, bias_ref, key_ref):
    _topk_mask(sc_ref, key_ref, bias_ref, jnp.full((1, LANES), PAST_LEN, I32), sc_ref.shape[0] // KEY_CHUNK)


def _select_sample(scores_t):
    return pl.pallas_call(
        _select_sample_kernel,
        out_shape=jax.ShapeDtypeStruct(scores_t.shape, F32),
        scratch_shapes=[pltpu.VMEM(scores_t.shape, I32)],
        compiler_params=pltpu.CompilerParams(vmem_limit_bytes=VMEM_LIMIT),
        name="select_sample",
    )(scores_t)


PAGE_ROWS = PAGE_SIZE * ATT_KV_HEADS


def _attn_sample_kernel(pt_ref, q_ref, bias_ref, biasn_ref, kn_ref, vn_ref, *rest):
    kp = rest[:PAGES_PER_STEP]
    vp = rest[PAGES_PER_STEP:2 * PAGES_PER_STEP]
    o_ref = rest[2 * PAGES_PER_STEP]
    m_ref, l_ref, acc_ref = rest[2 * PAGES_PER_STEP + 1:]
    j = pl.program_id(1)
    col = lax.broadcasted_iota(I32, (ATT_HEADS, PAGE_ROWS), 1)
    head = lax.broadcasted_iota(I32, (ATT_HEADS, PAGE_ROWS), 0)
    own = (col % ATT_KV_HEADS) == (head // (ATT_HEADS // ATT_KV_HEADS))
    c_exp = ATT_SCALE * math.log2(math.e)
    q = q_ref[...].astype(BF16)

    @pl.when(j == 0)
    def _():
        m_ref[...] = jnp.full_like(m_ref, NEG_BIG)
        l_ref[...] = jnp.zeros_like(l_ref)
        acc_ref[...] = jnp.zeros_like(acc_ref)

    logits = []
    for i in range(PAGES_PER_STEP):
        lg = lax.dot_general(q, kp[i][...].astype(BF16), _NT, preferred_element_type=F32)
        logits.append(jnp.where(own, lg + bias_ref[:, i * PAGE_ROWS:(i + 1) * PAGE_ROWS], NEG_BIG))
    m_old = m_ref[...]
    m_new = m_old
    for lg in logits:
        m_new = jnp.maximum(m_new, jnp.max(lg, axis=1, keepdims=True))
    alpha = jnp.exp2((m_old - m_new) * c_exp)
    l_new = alpha * l_ref[...]
    acc = alpha * acc_ref[...]
    for i in range(PAGES_PER_STEP):
        p = jnp.exp2((logits[i] - m_new) * c_exp)
        l_new = l_new + jnp.sum(p, axis=1, keepdims=True)
        acc = acc + jnp.dot(p.astype(BF16), vp[i][...].astype(BF16), preferred_element_type=F32)
    m_ref[...] = m_new
    l_ref[...] = l_new
    acc_ref[...] = acc

    @pl.when(j == pl.num_programs(1) - 1)
    def _():
        kn = kn_ref[...].astype(BF16).astype(F32)
        vn = vn_ref[...].astype(BF16).astype(F32)
        lgn = jnp.sum(q.astype(F32) * kn, axis=1, keepdims=True) + biasn_ref[:, 0:1]
        m_f = jnp.maximum(m_new, lgn)
        a = jnp.exp2((m_new - m_f) * c_exp)
        pn = jnp.exp2((lgn - m_f) * c_exp)
        o_ref[...] = (a * acc + pn.astype(BF16).astype(F32) * vn) / (a * l_new + pn)


def _attn_sample(page_table, q3, bias_rows, bias_new, k_new, v_new, cache_k, cache_v):
    nsteps = N_PAGES // PAGES_PER_STEP

    def page_spec(i):
        return pl.BlockSpec((None, PAGE_ROWS, ATT_HEAD_DIM),
                            lambda b, j, pt: (pt[b, j * PAGES_PER_STEP + i], 0, 0))

    head_rows = pl.BlockSpec((None, ATT_HEADS, ATT_HEAD_DIM), lambda b, j, pt: (b, 0, 0))
    grid_spec = pltpu.PrefetchScalarGridSpec(
        num_scalar_prefetch=1,
        grid=(DEC_BATCH, nsteps),
        in_specs=[head_rows,
                  pl.BlockSpec((None, 1, PAGES_PER_STEP * PAGE_ROWS), lambda b, j, pt: (b, 0, j)),
                  pl.BlockSpec((None, 1, PAGE_SIZE), lambda b, j, pt: (b, 0, 0)),
                  head_rows, head_rows]
                 + [page_spec(i) for i in range(PAGES_PER_STEP)] * 2,
        out_specs=head_rows,
        scratch_shapes=[pltpu.VMEM((ATT_HEADS, 1), F32), pltpu.VMEM((ATT_HEADS, 1), F32),
                        pltpu.VMEM((ATT_HEADS, ATT_HEAD_DIM), F32)],
    )
    return pl.pallas_call(
        _attn_sample_kernel,
        grid_spec=grid_spec,
        out_shape=jax.ShapeDtypeStruct((DEC_BATCH, ATT_HEADS, ATT_HEAD_DIM), F32),
        compiler_params=_params(("parallel", "arbitrary")),
        name="attention_sample",
    )(page_table, q3, bias_rows, bias_new, k_new, v_new,
      *([cache_k] * PAGES_PER_STEP), *([cache_v] * PAGES_PER_STEP))


def _finish(x, att, ret, gates, w, *, tm, tr):
    m = x.shape[0]
    mg = _merge(att, ret, w["att_proj"], w["ret_proj"], gates, tm, 512)
    y, = _matmul(mg, w["out"], _ep_plain, [], [_nat(m, D_MODEL, F32, tm, 512)], tm=tm, tn=512, name="w_out")
    h, hn = _post_attn(x, y, w["n_attn_post"], w["n_mlp_pre"], tr)
    up_out = [_nat(m, D_FF, BF16, tm, 512)]
    if w["mlp_up"].dtype == F32:
        u, w["mlp_up"] = _matmul_w32(hn, w["mlp_up"], _ep_relu2, [], up_out, tm=tm, tn=512, n=D_FF, col_off=0,
                                     name="mlp_up")
    else:
        u, = _matmul(hn, w["mlp_up"], _ep_relu2, [], up_out, tm=tm, tn=512, name="mlp_up")
    d, = _matmul(u, w["mlp_down"], _ep_plain, [], [_nat(m, D_MODEL, F32, tm, 512)], tm=tm, tn=512, tk=4096,
                 name="mlp_down")
    return _post_mlp(h, d, w["n_mlp_post"], tr)


def kernel(x_prompt, x_sample, cache_k, cache_v, cache_idx_k, state_ret, page_table, norm_attn_pre,
           norm_attn_post, w_in, ret_gn_w, w_att_proj, w_ret_proj, w_out, norm_mlp_pre, w_mlp_up,
           w_mlp_down, norm_mlp_post):
    log_gamma = jnp.log1p(-jnp.exp2(-5.0 - jnp.arange(RET_HEADS, dtype=F32)))
    w_in0 = w_in[0].T
    w = {
        "att_proj": w_att_proj[0].astype(BF16), "ret_proj": w_ret_proj[0].astype(BF16),
        "out": w_out[0].astype(BF16), "mlp_up": w_mlp_up[0], "mlp_down": w_mlp_down[0].astype(BF16),
        "n_attn_post": norm_attn_post[0], "n_mlp_pre": norm_mlp_pre[0], "n_mlp_post": norm_mlp_post[0],
    }
    gn_w = ret_gn_w[0]

    m_p = BATCH * SEQ
    xp = x_prompt.reshape(m_p, D_MODEL)
    xn = _rmsnorm_cast(xp, norm_attn_pre[0], 256)
    pos_p = jnp.arange(SEQ, dtype=I32)
    pr = _project(xn, w_in0, pos_p, pos_p % RET_CHUNK, float(RET_CHUNK), log_gamma, tm=1024, prompt=True)
    vt = pr["v_bf"].reshape(BATCH, SEQ // KEY_CHUNK, KEY_CHUNK, ATT_KV_HEADS, ATT_HEAD_DIM).transpose(0, 3, 1, 4, 2)
    vt = jnp.concatenate([vt, jnp.ones(vt.shape[:3] + (VT_ONES, KEY_CHUNK), BF16)], axis=3)
    att4 = _dsa_prompt(pr["iq"], pr["ikw"], pr["q"], pr["k_heads"], vt)
    att = att4.reshape(m_p // Q_BLOCK, ATT_HEADS, Q_BLOCK, ATT_HEAD_DIM).transpose(0, 2, 1, 3).reshape(m_p, ATT_Q_W)
    ci = jnp.arange(RET_CHUNK, dtype=F32)
    diff = ci[:, None] - ci[None, :]
    decay = jnp.where(diff >= 0, jnp.exp(log_gamma[:, None, None] * jnp.maximum(diff, 0.0)), 0.0)
    cdec_p = jnp.broadcast_to(jnp.exp(log_gamma * RET_CHUNK)[:, None, None], (RET_HEADS, 1, RET_DV))
    ret, s_prompt = _ret_prompt(pr, decay, cdec_p, gn_w)
    y_prompt = _finish(xp, att, ret, pr["gates"], w, tm=1024, tr=256).reshape(BATCH, SEQ, D_MODEL)

    nb = DEC_BATCH
    xs = x_sample.reshape(nb, D_MODEL)
    xns = _rmsnorm_cast(xs, norm_attn_pre[0], nb)
    pos_s = jnp.full((nb,), PAST_LEN, I32)
    ps = _project(xns, pr["w_bf16"], pos_s, jnp.zeros((nb,), I32), 1.0, log_gamma, tm=nb, prompt=False)
    ik_new = ps["ikw"][:, :IDX_DIM]
    iq3 = jnp.pad(ps["iq"].reshape(nb, IDX_HEADS, IDX_DIM), ((0, 0), (0, 0), (0, LANES - IDX_DIM)))
    w3 = ps["ikw"][:, IDX_DIM:IDX_DIM + IDX_HEADS].reshape(nb, IDX_HEADS, 1)
    new_pages = jnp.pad(ik_new[:, :, None], ((0, 0), (0, 0), (0, PAGE_SIZE - 1)))
    sc_past, sc_new = _idx_sample(page_table, iq3, w3, new_pages, cache_idx_k[0].transpose(0, 2, 1))
    scores = jnp.concatenate([sc_past.reshape(nb, PAST_LEN), sc_new.reshape(nb, PAGE_SIZE)], axis=1)
    n_rows = -(-(PAST_LEN + PAGE_SIZE) // KEY_CHUNK) * KEY_CHUNK
    scores_t = jnp.pad(scores.T, ((0, n_rows - PAST_LEN - PAGE_SIZE), (0, LANES - nb)))
    bias = _select_sample(scores_t)[:PAST_LEN + PAGE_SIZE, :nb].T
    n_phys = cache_k.shape[1]
    group = ATT_HEADS // ATT_KV_HEADS
    att_s = _attn_sample(page_table, ps["q"].reshape(nb, ATT_HEADS, ATT_HEAD_DIM),
                         jnp.repeat(bias[:, :PAST_LEN], ATT_KV_HEADS, axis=1).reshape(nb, 1, N_PAGES * PAGE_ROWS),
                         bias[:, PAST_LEN:].reshape(nb, 1, PAGE_SIZE),
                         jnp.repeat(ps["k"].reshape(nb, ATT_KV_HEADS, ATT_HEAD_DIM), group, axis=1),
                         jnp.repeat(ps["v"].reshape(nb, ATT_KV_HEADS, ATT_HEAD_DIM), group, axis=1),
                         cache_k[0].reshape(n_phys, PAGE_ROWS, ATT_HEAD_DIM),
                         cache_v[0].reshape(n_phys, PAGE_ROWS, ATT_HEAD_DIM))
    cdec_s = jnp.broadcast_to(jnp.exp(log_gamma)[:, None, None], (RET_HEADS, 1, RET_DV))
    ret_s, s_sample = _ret_sample(ps, state_ret[0], cdec_s, gn_w)
    y_sample = _finish(xs, att_s.reshape(nb, ATT_Q_W).astype(BF16), ret_s.reshape(nb, RET_V_W).astype(BF16),
                       ps["gates"], w, tm=nb, tr=nb).reshape(nb, 1, D_MODEL)

    return (y_prompt, y_sample,
            pr["k"].reshape(1, BATCH, SEQ, ATT_KV_HEADS, ATT_HEAD_DIM),
            pr["v"].reshape(1, BATCH, SEQ, ATT_KV_HEADS, ATT_HEAD_DIM),
            pr["ikw"][:, :IDX_DIM].reshape(1, BATCH, SEQ, IDX_DIM),
            s_prompt[None],
            ps["k"].reshape(1, nb, 1, ATT_KV_HEADS, ATT_HEAD_DIM),
            ps["v"].reshape(1, nb, 1, ATT_KV_HEADS, ATT_HEAD_DIM),
            ik_new.reshape(1, nb, 1, IDX_DIM),
            s_sample[None])
```

```python
import functools
import math

import jax
import jax.numpy as jnp
import numpy as np
from jax import lax
from jax.experimental import pallas as pl
from jax.experimental.pallas import tpu as pltpu

F32 = jnp.float32
BF16 = jnp.bfloat16
I32 = jnp.int32

D_MODEL = 4096
BATCH = 4
SEQ = 2048
DEC_BATCH = 32
PAST_LEN = 8192
PAGE_SIZE = 128
N_PAGES = PAST_LEN // PAGE_SIZE
ATT_HEADS = 16
ATT_KV_HEADS = 8
ATT_HEAD_DIM = 128
ROPE_THETA = 500000.0
IDX_HEADS = 32
IDX_DIM = 64
TOPK = 256
RET_HEADS = 8
RET_DK = 256
RET_DV = 512
RET_THETA = 10000.0
RET_CHUNK = 128
D_FF = 4 * D_MODEL
NORM_EPS = 1e-6

ATT_Q_W = ATT_HEADS * ATT_HEAD_DIM
ATT_KV_W = ATT_KV_HEADS * ATT_HEAD_DIM
IDX_Q_W = IDX_HEADS * IDX_DIM
RET_QK_W = RET_HEADS * RET_DK
RET_V_W = RET_HEADS * RET_DV
SPLITS = (ATT_Q_W, ATT_KV_W, ATT_KV_W, IDX_Q_W, IDX_DIM, IDX_HEADS,
          RET_QK_W, RET_QK_W, RET_V_W, RET_V_W, D_MODEL, D_MODEL)
OFFS = tuple(int(v) for v in np.concatenate([[0], np.cumsum(SPLITS)]))

LANES = 128
SUBLANES = 8
Q_BLOCK = 128
KEY_CHUNK = 512
PAIRS_PER_STEP = 16
GROUPS_PER_STEP = 8
VT_ONES = 16
VMEM_LIMIT = 56 * 1024 * 1024

INT_MIN = -2 ** 31
NEG_BIG = -1e30
ATT_SCALE = ATT_HEAD_DIM ** -0.5
IDX_SCALE = (IDX_DIM ** -0.5) * (IDX_HEADS ** -0.5)

_NT = (((1,), (1,)), ((), ()))


def _params(sem):
    return pltpu.CompilerParams(dimension_semantics=sem, vmem_limit_bytes=VMEM_LIMIT)


def _sigmoid(x):
    return 1.0 / (1.0 + jnp.exp(-x))


def _rmsnorm_cast_kernel(x_ref, w_ref, o_ref):
    x = x_ref[...]
    y = x * lax.rsqrt(jnp.mean(x * x, axis=-1, keepdims=True) + NORM_EPS)
    o_ref[...] = (y * w_ref[...]).astype(o_ref.dtype)


def _rmsnorm_cast(x, w, tr):
    m, d = x.shape
    return pl.pallas_call(
        _rmsnorm_cast_kernel,
        grid=(m // tr,),
        in_specs=[pl.BlockSpec((tr, d), lambda i: (i, 0)), pl.BlockSpec((1, d), lambda i: (0, 0))],
        out_specs=pl.BlockSpec((tr, d), lambda i: (i, 0)),
        out_shape=jax.ShapeDtypeStruct((m, d), BF16),
        compiler_params=_params(("parallel",)),
        name="rmsnorm_cast",
    )(x, w.reshape(1, d))


def _post_attn_kernel(x_ref, y_ref, w1_ref, w2_ref, h_ref, hn_ref):
    y = y_ref[...]
    yn = y * lax.rsqrt(jnp.mean(y * y, axis=-1, keepdims=True) + NORM_EPS) * w1_ref[...]
    h = x_ref[...] + yn
    h_ref[...] = h
    hn = h * lax.rsqrt(jnp.mean(h * h, axis=-1, keepdims=True) + NORM_EPS) * w2_ref[...]
    hn_ref[...] = hn.astype(hn_ref.dtype)


def _post_attn(x, y, w1, w2, tr):
    m, d = x.shape
    row = pl.BlockSpec((tr, d), lambda i: (i, 0))
    vec = pl.BlockSpec((1, d), lambda i: (0, 0))
    return pl.pallas_call(
        _post_attn_kernel,
        grid=(m // tr,),
        in_specs=[row, row, vec, vec],
        out_specs=[row, row],
        out_shape=[jax.ShapeDtypeStruct((m, d), F32), jax.ShapeDtypeStruct((m, d), BF16)],
        compiler_params=_params(("parallel",)),
        name="post_attn_norm",
    )(x, y, w1.reshape(1, d), w2.reshape(1, d))


def _post_mlp_kernel(h_ref, d_ref, w_ref, o_ref):
    d = d_ref[...]
    dn = d * lax.rsqrt(jnp.mean(d * d, axis=-1, keepdims=True) + NORM_EPS) * w_ref[...]
    o_ref[...] = h_ref[...] + dn


def _post_mlp(h, d, w, tr):
    m, dm = h.shape
    row = pl.BlockSpec((tr, dm), lambda i: (i, 0))
    return pl.pallas_call(
        _post_mlp_kernel,
        grid=(m // tr,),
        in_specs=[row, row, pl.BlockSpec((1, dm), lambda i: (0, 0))],
        out_specs=row,
        out_shape=jax.ShapeDtypeStruct((m, dm), F32),
        compiler_params=_params(("parallel",)),
        name="post_mlp_norm",
    )(h, d, w.reshape(1, dm))


def _dot(a, b, b_t):
    if b_t:
        return lax.dot_general(a, b, _NT, preferred_element_type=F32)
    return jnp.dot(a, b, preferred_element_type=F32)


def _mm_kernel(*refs, n_extra, n_out, nk, b_t, epilogue):
    a_ref, b_ref = refs[0], refs[1]
    extra = refs[2:2 + n_extra]
    outs = refs[2 + n_extra:2 + n_extra + n_out]
    if nk == 1:
        epilogue(_dot(a_ref[...], b_ref[...], b_t), extra, outs)
        return
    acc_ref = refs[-1]
    k = pl.program_id(2)
    d = _dot(a_ref[...], b_ref[...], b_t)

    @pl.when(k == 0)
    def _():
        acc_ref[...] = d

    @pl.when((k > 0) & (k < nk - 1))
    def _():
        acc_ref[...] += d

    @pl.when(k == nk - 1)
    def _():
        epilogue(acc_ref[...] + d, extra, outs)


def _matmul(a, b, epilogue, extras, outs, *, tm, tn, tk=None, b_t=False, name):
    m, kd = a.shape
    n = b.shape[0] if b_t else b.shape[1]
    tk = kd if tk is None else tk
    nk = kd // tk
    grid = (m // tm, n // tn, nk)

    def lift(f):
        return lambda i, j, k: f(i, j)

    b_spec = (pl.BlockSpec((tn, tk), lambda i, j, k: (j, k)) if b_t
              else pl.BlockSpec((tk, tn), lambda i, j, k: (k, j)))
    in_specs = [pl.BlockSpec((tm, tk), lambda i, j, k: (i, k)), b_spec]
    in_specs += [pl.BlockSpec(bs, lift(im)) for _, bs, im in extras]
    out_specs = [pl.BlockSpec(bs, lift(im)) for _, _, bs, im in outs]
    out_shape = [jax.ShapeDtypeStruct(s, dt) for s, dt, _, _ in outs]
    scratch = [pltpu.VMEM((tm, tn), F32)] if nk > 1 else []
    res = pl.pallas_call(
        functools.partial(_mm_kernel, n_extra=len(extras), n_out=len(outs), nk=nk, b_t=b_t, epilogue=epilogue),
        grid=grid,
        in_specs=in_specs,
        out_specs=out_specs,
        out_shape=out_shape,
        scratch_shapes=scratch,
        compiler_params=_params(("parallel", "parallel", "arbitrary")),
        name=name,
    )(a, b, *[e[0] for e in extras])
    return res


def _mm_w32_kernel(*refs, n_extra, n_out, valid, epilogue):
    a_ref, w_ref = refs[0], refs[1]
    extra = refs[2:2 + n_extra]
    outs = refs[2 + n_extra:2 + n_extra + n_out]
    wbf_ref = refs[2 + n_extra + n_out]

    @pl.when(pl.program_id(1) == 0)
    def _():
        w = w_ref[...]
        if valid < w.shape[0]:
            w = jnp.where(lax.broadcasted_iota(I32, w.shape, 0) < valid, w, 0.0)
        wbf_ref[...] = w.astype(BF16)

    epilogue(_dot(a_ref[...], wbf_ref[...], True), extra, outs)


def _matmul_w32(a, w_t, epilogue, extras, outs, *, tm, tn, n, col_off, valid=None, name):
    m, kd = a.shape
    assert n % tn == 0 and col_off % SUBLANES == 0
    valid = tn if valid is None else valid
    grid = (n // tn, m // tm)

    def lift(f):
        return lambda j, i: f(i, j)

    w_spec = pl.BlockSpec((pl.Element(tn), pl.Element(kd)),
                          lambda j, i: (pl.multiple_of(col_off + j * tn, SUBLANES), 0))
    in_specs = [pl.BlockSpec((tm, kd), lambda j, i: (i, 0)), w_spec]
    in_specs += [pl.BlockSpec(bs, lift(im)) for _, bs, im in extras]
    out_specs = [pl.BlockSpec(bs, lift(im)) for _, _, bs, im in outs] + [pl.BlockSpec((tn, kd), lambda j, i: (j, 0))]
    out_shape = [jax.ShapeDtypeStruct(s, dt) for s, dt, _, _ in outs] + [jax.ShapeDtypeStruct((n, kd), BF16)]
    return pl.pallas_call(
        functools.partial(_mm_w32_kernel, n_extra=len(extras), n_out=len(outs), valid=valid, epilogue=epilogue),
        grid=grid,
        in_specs=in_specs,
        out_specs=out_specs,
        out_shape=out_shape,
        compiler_params=_params(("arbitrary", "arbitrary")),
        name=name,
    )(a, w_t, *[e[0] for e in extras])


def _nat(m, n, dt, tm, tn):
    return ((m, n), dt, (tm, tn), lambda i, j: (i, j))


def _rope_lanes(y, c, sm, sp, half):
    n = y.shape[-1]
    return y * c + pltpu.roll(y, n - half, 1) * sm + pltpu.roll(y, half, 1) * sp


def _ep_plain(acc, extra, outs):
    for o in outs:
        o[...] = acc.astype(o.dtype)


def _ep_rope_lanes(acc, extra, outs, *, half, blocked):
    c, sm, sp = extra[0][...], extra[1][...], extra[2][...]
    tm, tn = acc.shape
    for jj in range(tn // LANES):
        y = _rope_lanes(acc[:, jj * LANES:(jj + 1) * LANES], c, sm, sp, half)
        if blocked:
            for r in range(tm // Q_BLOCK):
                outs[0][r, jj] = y[r * Q_BLOCK:(r + 1) * Q_BLOCK].astype(outs[0].dtype)
        else:
            outs[0][:, jj * LANES:(jj + 1) * LANES] = y.astype(outs[0].dtype)
        if len(outs) > 1:
            outs[1][jj] = y.astype(outs[1].dtype)


def _ep_ret_qk(acc, extra, outs, *, scale):
    cos, sin, dec = extra[0][...], extra[1][...], extra[2]
    tn = acc.shape[1]
    for hh in range(tn // RET_DK):
        lo = hh * RET_DK
        x1 = acc[:, lo:lo + LANES]
        x2 = acc[:, lo + LANES:lo + 2 * LANES]
        o1 = x1 * cos - x2 * sin
        o2 = x2 * cos + x1 * sin
        if scale != 1.0:
            o1 = o1 * scale
            o2 = o2 * scale
        d = dec[:, hh * LANES:(hh + 1) * LANES]
        outs[0][:, lo:lo + LANES] = o1.astype(outs[0].dtype)
        outs[0][:, lo + LANES:lo + 2 * LANES] = o2.astype(outs[0].dtype)
        outs[1][:, lo:lo + LANES] = (o1 * d).astype(outs[1].dtype)
        outs[1][:, lo + LANES:lo + 2 * LANES] = (o2 * d).astype(outs[1].dtype)


def _merge_kernel(att_ref, ret_ref, wa_ref, wr_ref, ga_ref, gr_ref, o_ref):
    a = jnp.dot(att_ref[...], wa_ref[...], preferred_element_type=F32)
    r = jnp.dot(ret_ref[...], wr_ref[...], preferred_element_type=F32)
    o_ref[...] = (_sigmoid(ga_ref[...]) * a + _sigmoid(gr_ref[...]) * r).astype(o_ref.dtype)


def _merge(att, ret, wa, wr, gates, tm, tn):
    m = att.shape[0]
    nb = D_MODEL // tn
    return pl.pallas_call(
        _merge_kernel,
        grid=(m // tm, nb),
        in_specs=[
            pl.BlockSpec((tm, ATT_Q_W), lambda i, j: (i, 0)),
            pl.BlockSpec((tm, RET_V_W), lambda i, j: (i, 0)),
            pl.BlockSpec((ATT_Q_W, tn), lambda i, j: (0, j)),
            pl.BlockSpec((RET_V_W, tn), lambda i, j: (0, j)),
            pl.BlockSpec((tm, tn), lambda i, j: (i, nb + j)),
            pl.BlockSpec((tm, tn), lambda i, j: (i, 2 * nb + j)),
        ],
        out_specs=pl.BlockSpec((tm, tn), lambda i, j: (i, j)),
        out_shape=jax.ShapeDtypeStruct((m, D_MODEL), BF16),
        compiler_params=_params(("parallel", "parallel")),
        name="merge_proj",
    )(att, ret, wa, wr, gates, gates)


def _ep_relu2(acc, extra, outs):
    u = jnp.maximum(acc, 0.0)
    outs[0][...] = (u * u).astype(outs[0].dtype)


def _rope_lane_tables(pos, head_w, rot, theta):
    half = rot // 2
    inv_freq = jnp.exp(-math.log(theta) * jnp.arange(half, dtype=F32) / half)
    ang = pos.astype(F32)[:, None] * inv_freq[None, :]
    cos, sin = jnp.cos(ang), jnp.sin(ang)
    n = pos.shape[0]
    z_half = jnp.zeros((n, half), F32)
    rest1 = jnp.ones((n, head_w - rot), F32)
    rest0 = jnp.zeros((n, head_w - rot), F32)
    c = jnp.concatenate([cos, cos, rest1], axis=1)
    sm = jnp.concatenate([-sin, z_half, rest0], axis=1)
    sp = jnp.concatenate([z_half, sin, rest0], axis=1)
    rep = LANES // head_w
    return [jnp.tile(t, (1, rep)) for t in (c, sm, sp)]


def _ret_tables(pos, chunk_pos, chunk_len, log_gamma):
    half = RET_DK // 2
    inv_freq = jnp.exp(-math.log(RET_THETA) * jnp.arange(half, dtype=F32) / half)
    ang = pos.astype(F32)[:, None] * inv_freq[None, :]
    i = chunk_pos.astype(F32)[:, None]
    q_dec = jnp.exp(log_gamma[None, :] * (i + 1.0))
    k_dec = jnp.exp(log_gamma[None, :] * (chunk_len - 1.0 - i))
    return jnp.cos(ang), jnp.sin(ang), jnp.repeat(q_dec, LANES, axis=1), jnp.repeat(k_dec, LANES, axis=1)


PROJ_TN = 512


def _project(xn, w, pos, chunk_pos, chunk_len, log_gamma, *, tm, prompt):
    m = xn.shape[0]
    nrep = max(pos.shape[0] // tm, 1)
    wdt = BF16 if prompt else F32
    out = {"w_bf16": {}}

    def mm(name, a, b, epilogue, extras, outs, tn=PROJ_TN, valid_cols=None):
        n = -(-(OFFS[b] - OFFS[a]) // tn) * tn
        if prompt:
            *res, wb = _matmul_w32(xn, w, epilogue, extras, outs, tm=tm, tn=tn, n=n, col_off=OFFS[a],
                                   valid=valid_cols, name=name)
            out["w_bf16"][name] = wb
            return res
        return _matmul(xn, w[name], epilogue, extras, outs, tm=tm, tn=tn, b_t=True, name=name)

    def tab(t, width=LANES, by_col=False):
        if by_col:
            return (t, (tm, width), lambda i, j: (i % nrep, j))
        return (t, (tm, width), lambda i, j: (i % nrep, 0))

    att_t = [tab(t) for t in _rope_lane_tables(pos, ATT_HEAD_DIM, ATT_HEAD_DIM // 4, ROPE_THETA)]
    idx_t = [tab(t) for t in _rope_lane_tables(pos, IDX_DIM, IDX_DIM // 4, ROPE_THETA)]
    r_cos, r_sin, q_dec, k_dec = _ret_tables(pos, chunk_pos, chunk_len, log_gamma)
    tn = PROJ_TN
    nqb = m // Q_BLOCK

    if prompt:
        o = [((nqb, ATT_HEADS, Q_BLOCK, LANES), BF16, (tm // Q_BLOCK, tn // LANES, Q_BLOCK, LANES),
              lambda i, j: (i, j, 0, 0))]
    else:
        o = [_nat(m, ATT_Q_W, F32, tm, tn)]
    out["q"], = mm("proj_q", 0, 1, functools.partial(_ep_rope_lanes, half=16, blocked=prompt), att_t, o)
    o = [_nat(m, ATT_KV_W, F32, tm, tn)]
    if prompt:
        per_b = SEQ // tm
        o.append(((BATCH, ATT_KV_HEADS, SEQ, LANES), BF16, (None, tn // LANES, tm, LANES),
                  lambda i, j: (i // per_b, j, i % per_b, 0)))
    res = mm("proj_k", 1, 2, functools.partial(_ep_rope_lanes, half=16, blocked=False), att_t, o)
    out["k"] = res[0]
    if prompt:
        out["k_heads"] = res[1]
    o = [_nat(m, ATT_KV_W, F32, tm, tn)]
    if prompt:
        o.append(_nat(m, ATT_KV_W, BF16, tm, tn))
    res = mm("proj_v", 2, 3, _ep_plain, [], o)
    out["v"] = res[0]
    if prompt:
        out["v_bf"] = res[1]
    if prompt:
        o = [((nqb, IDX_Q_W // LANES, Q_BLOCK, LANES), F32, (tm // Q_BLOCK, tn // LANES, Q_BLOCK, LANES),
              lambda i, j: (i, j, 0, 0))]
    else:
        o = [_nat(m, IDX_Q_W, F32, tm, tn)]
    out["iq"], = mm("proj_iq", 3, 4, functools.partial(_ep_rope_lanes, half=8, blocked=prompt), idx_t, o)
    ikw_t = [tab(t) for t in _rope_lane_tables(pos, LANES, IDX_DIM // 4, ROPE_THETA)]
    out["ikw"], = mm("proj_ikw", 4, 6, functools.partial(_ep_rope_lanes, half=8, blocked=False), ikw_t,
                     [_nat(m, LANES, F32, tm, LANES)], tn=LANES, valid_cols=IDX_DIM + IDX_HEADS)
    for nm, grp, dec, scale in (("rq", 6, q_dec, 1.0), ("rk", 7, k_dec, RET_DK ** -0.5)):
        res = mm("proj_" + nm, grp, grp + 1, functools.partial(_ep_ret_qk, scale=scale),
                 [tab(r_cos), tab(r_sin), tab(dec, tn // 2, by_col=True)],
                 [_nat(m, RET_QK_W, wdt, tm, tn), _nat(m, RET_QK_W, wdt, tm, tn)])
        out[nm], out[nm + "d"] = res
    out["rv"], = mm("proj_rv", 8, 9, _ep_plain, [], [_nat(m, RET_V_W, wdt, tm, tn)])
    out["gates"], = mm("proj_gates", 9, 12, _ep_plain, [], [_nat(m, 3 * D_MODEL, F32, tm, tn)])
    return out


def _sortable_key(score):
    kb = lax.bitcast_convert_type(score, I32)
    kb = jnp.where(kb == INT_MIN, 0, kb)
    return jnp.where(kb < 0, kb ^ 0x7FFFFFFF, kb)


def _tile_reduce(x, op):
    tiles = [x[i * SUBLANES:(i + 1) * SUBLANES] for i in range(x.shape[0] // SUBLANES)]
    while len(tiles) > 1:
        nxt = [op(tiles[i], tiles[i + 1]) for i in range(0, len(tiles) - 1, 2)]
        if len(tiles) % 2:
            nxt.append(tiles[-1])
        tiles = nxt
    return tiles[0]


def _topk_mask(sc_ref, key_ref, bias_ref, q_pos, n_chunks):
    def rows(c):
        r0 = pl.multiple_of(c * KEY_CHUNK, KEY_CHUNK)
        return pl.ds(r0, KEY_CHUNK), r0 + lax.broadcasted_iota(I32, (KEY_CHUNK, LANES), 0)

    def build(c, carry):
        sl, row = rows(c)
        key_ref[sl, :] = jnp.where(row <= q_pos, _sortable_key(sc_ref[sl, :]), INT_MIN)
        return carry

    lax.fori_loop(0, n_chunks, build, 0)

    def count(pred):
        def body(c, acc):
            sl, row = rows(c)
            return acc + _tile_reduce(jnp.where(pred(key_ref[sl, :], row), 1.0, 0.0), jnp.add)

        part = lax.fori_loop(0, n_chunks, body, jnp.zeros((SUBLANES, LANES), F32))
        return jnp.sum(part, axis=0, keepdims=True)

    t0 = jnp.where(count(lambda k, row: k >= 0) >= TOPK, 0, INT_MIN).astype(I32)

    def search(i, t):
        cand = t | lax.shift_left(jnp.int32(1), jnp.int32(30) - i)
        return jnp.where(count(lambda k, row: k >= cand) >= TOPK, cand, t)

    thr = lax.fori_loop(0, 31, search, t0)
    n_eq = count(lambda k, row: (row <= q_pos) & (k == thr))
    need = TOPK - count(lambda k, row: k > thr)

    def write(c, carry):
        sl, row = rows(c)
        k = key_ref[sl, :]
        sel = jnp.where(k > thr, 0.0, jnp.where((row <= q_pos) & (k == thr), 0.0, NEG_BIG))
        bias_ref[sl, :] = sel.astype(bias_ref.dtype)
        return carry

    lax.fori_loop(0, n_chunks, write, 0)

    @pl.when(jnp.max(n_eq - need) > 0)
    def _():
        r_i = lax.broadcasted_iota(I32, (LANES, LANES), 0)
        c_i = lax.broadcasted_iota(I32, (LANES, LANES), 1)
        tri = jnp.where(c_i < r_i, 1.0, 0.0).astype(BF16)

        def chunk(c, off):
            r0 = pl.multiple_of(c * LANES, LANES)
            kc = key_ref[pl.ds(r0, LANES), :]
            rc = r0 + lax.broadcasted_iota(I32, (LANES, LANES), 0)
            e = (rc <= q_pos) & (kc == thr)
            ef = jnp.where(e, 1.0, 0.0)
            before = jnp.dot(tri, ef.astype(BF16), preferred_element_type=F32) + off
            keep = e & (before < need)
            sel = jnp.where(kc > thr, 0.0, jnp.where(keep, 0.0, NEG_BIG))
            bias_ref[pl.ds(r0, LANES), :] = sel.astype(bias_ref.dtype)
            return off + jnp.sum(ef, axis=0, keepdims=True)

        lax.fori_loop(0, n_chunks * (KEY_CHUNK // LANES), chunk, jnp.zeros((1, LANES), F32))


def _split_bf16(x):
    hi = x.astype(BF16).astype(F32)
    return hi, x - hi


def _dsa_prompt_kernel(iq_ref, ikw_all_ref, ikw_q_ref, q_ref, k_ref, vt_ref, att_ref,
                       lhs_ref, wt_ref, sc_ref, key_ref, bias_ref):
    qb = pl.program_id(1)
    n_ck = (qb * Q_BLOCK + Q_BLOCK + KEY_CHUNK - 1) // KEY_CHUNK
    lane = lax.broadcasted_iota(I32, (1, LANES), 1)
    low = lane < IDX_DIM

    def chunk_rows(c):
        return pl.ds(pl.multiple_of(c * KEY_CHUNK, KEY_CHUNK), KEY_CHUNK)

    @pl.when(qb == 0)
    def _():
        x = jnp.where(low, ikw_all_ref[...], 0.0)
        hi, lo = _split_bf16(x)
        lhs_ref[:, :LANES] = (hi + pltpu.roll(lo, IDX_DIM, 1)).astype(BF16)
        lhs_ref[:, LANES:] = hi.astype(BF16)

    wt_ref[...] = ikw_q_ref[...].T * IDX_SCALE
    sc_ref[...] = jnp.zeros_like(sc_ref)

    def pairs_body(pq, carry):
        rhs, wts = [], []
        for u in range(PAIRS_PER_STEP):
            p = pq * PAIRS_PER_STEP + u
            hi, lo = _split_bf16(iq_ref[p])
            rhi = pltpu.roll(hi, IDX_DIM, 1)
            rlo = pltpu.roll(lo, IDX_DIM, 1)
            ra = jnp.concatenate([jnp.where(low, hi, rhi), jnp.where(low, lo, 0.0)], axis=1)
            rb = jnp.concatenate([jnp.where(low, rhi, hi), jnp.where(low, rlo, 0.0)], axis=1)
            rhs.append(jnp.concatenate([ra, rb], axis=0).astype(BF16))
            wts.append((wt_ref[pl.ds(IDX_DIM + 2 * p, 1), :], wt_ref[pl.ds(IDX_DIM + 2 * p + 1, 1), :]))

        def ck_body(c, inner):
            sl = chunk_rows(c)
            lhs = lhs_ref[sl, :]
            acc = sc_ref[sl, :]
            for rhs_t, (wa, wb) in zip(rhs, wts):
                d = lax.dot_general(lhs, rhs_t, _NT, preferred_element_type=F32)
                acc = acc + wa * jnp.maximum(d[:, :LANES], 0.0) + wb * jnp.maximum(d[:, LANES:], 0.0)
            sc_ref[sl, :] = acc
            return inner

        lax.fori_loop(0, n_ck, ck_body, 0)
        return carry

    lax.fori_loop(0, IDX_HEADS // 2 // PAIRS_PER_STEP, pairs_body, 0)

    _topk_mask(sc_ref, key_ref, bias_ref, qb * Q_BLOCK + lane, n_ck)

    c_exp = ATT_SCALE * math.log2(math.e)

    r_i = lax.broadcasted_iota(I32, (2 * Q_BLOCK, LANES), 0)
    c_i = lax.broadcasted_iota(I32, (2 * Q_BLOCK, LANES), 1)
    onehot_q = jnp.where((r_i % Q_BLOCK) == c_i, 1.0, 0.0).astype(BF16)
    n_acc = vt_ref.shape[2]

    def groups_body(gq, carry):
        gs = [gq * GROUPS_PER_STEP + u for u in range(GROUPS_PER_STEP)]
        qqs = [jnp.concatenate([jnp.concatenate([q_ref[2 * g], q_ref[2 * g + 1]], axis=0), onehot_q], axis=1)
               for g in gs]

        def ck_body(c, state):
            sl = chunk_rows(c)
            mask = bias_ref[sl, :]
            lgs = [lax.dot_general(jnp.concatenate([k_ref[g, sl, :], mask], axis=1), qq, _NT,
                                   preferred_element_type=F32)
                   for g, qq in zip(gs, qqs)]
            soft = []
            for lg, (m_old, acc) in zip(lgs, state):
                m_new = jnp.maximum(m_old, jnp.max(_tile_reduce(lg, jnp.maximum), axis=0, keepdims=True))
                alpha = jnp.exp2((m_old - m_new) * c_exp)
                soft.append((m_new, alpha, jnp.exp2((lg - m_new) * c_exp).astype(BF16)))
            return tuple((m_new, alpha * acc + jnp.dot(vt_ref[g, c], pt, preferred_element_type=F32))
                         for g, (m_new, alpha, pt), (_, acc) in zip(gs, soft, state))

        init = (jnp.full((1, 2 * LANES), NEG_BIG, F32), jnp.zeros((n_acc, 2 * LANES), F32))
        final = lax.fori_loop(0, n_ck, ck_body, (init,) * GROUPS_PER_STEP)
        for g, (_, acc) in zip(gs, final):
            o = acc[:ATT_HEAD_DIM] / acc[ATT_HEAD_DIM:ATT_HEAD_DIM + 1]
            for hh in range(2):
                att_ref[2 * g + hh] = o[:, hh * LANES:(hh + 1) * LANES].T.astype(att_ref.dtype)
        return carry

    lax.fori_loop(0, ATT_KV_HEADS // GROUPS_PER_STEP, groups_body, 0)


def _dsa_prompt(iq, ikw, q, k_heads, vt):
    nqb = SEQ // Q_BLOCK
    n_pair = IDX_Q_W // LANES
    return pl.pallas_call(
        _dsa_prompt_kernel,
        grid=(BATCH, nqb),
        in_specs=[
            pl.BlockSpec((None, n_pair, Q_BLOCK, LANES), lambda b, i: (b * nqb + i, 0, 0, 0)),
            pl.BlockSpec((SEQ, LANES), lambda b, i: (b, 0)),
            pl.BlockSpec((Q_BLOCK, LANES), lambda b, i: (b * nqb + i, 0)),
            pl.BlockSpec((None, ATT_HEADS, Q_BLOCK, LANES), lambda b, i: (b * nqb + i, 0, 0, 0)),
            pl.BlockSpec((None, ATT_KV_HEADS, SEQ, LANES), lambda b, i: (b, 0, 0, 0)),
            pl.BlockSpec((None, ATT_KV_HEADS, SEQ // KEY_CHUNK, ATT_HEAD_DIM + VT_ONES, KEY_CHUNK),
                         lambda b, i: (b, 0, 0, 0, 0)),
        ],
        out_specs=pl.BlockSpec((None, ATT_HEADS, Q_BLOCK, LANES), lambda b, i: (b * nqb + i, 0, 0, 0)),
        out_shape=jax.ShapeDtypeStruct((BATCH * nqb, ATT_HEADS, Q_BLOCK, LANES), BF16),
        scratch_shapes=[
            pltpu.VMEM((SEQ, 2 * LANES), BF16),
            pltpu.VMEM((LANES, LANES), F32),
            pltpu.VMEM((SEQ, LANES), F32),
            pltpu.VMEM((SEQ, LANES), I32),
            pltpu.VMEM((SEQ, LANES), BF16),
        ],
        compiler_params=_params(("parallel", "arbitrary")),
        name="dsa_prompt",
    )(iq, ikw, ikw, q, k_heads, vt)


def _group_norm_gate(o, gn_w, rg):
    of = o * lax.rsqrt(jnp.mean(o * o, axis=-1, keepdims=True) + NORM_EPS) * gn_w
    return of * (rg * _sigmoid(rg))


def _ret_prompt_kernel(q_ref, qd_ref, k_ref, kd_ref, v_ref, rg_ref, decay_ref, cdec_ref, gnw_ref,
                       ret_ref, s_ref):
    c = pl.program_id(1)

    @pl.when(c == 0)
    def _():
        s_ref[...] = jnp.zeros_like(s_ref)

    for h in range(RET_HEADS):
        qk = slice(h * RET_DK, (h + 1) * RET_DK)
        vv = slice(h * RET_DV, (h + 1) * RET_DV)
        s = s_ref[h]
        v = v_ref[:, vv]
        scores = lax.dot_general(q_ref[:, qk], k_ref[:, qk], _NT, preferred_element_type=F32) * decay_ref[h]
        o = (jnp.dot(scores.astype(BF16), v, preferred_element_type=F32)
             + jnp.dot(qd_ref[:, qk], s.astype(BF16), preferred_element_type=F32))
        kv = lax.dot_general(kd_ref[:, qk], v, (((0,), (0,)), ((), ())), preferred_element_type=F32)
        s_ref[h] = cdec_ref[h] * s + kv
        ret_ref[:, vv] = _group_norm_gate(o, gnw_ref[:, vv], rg_ref[:, vv]).astype(ret_ref.dtype)


def _ret_prompt(pr, decay, cdec, gn_w):
    nc = SEQ // RET_CHUNK
    qk = pl.BlockSpec((RET_CHUNK, RET_QK_W), lambda b, c: (b * nc + c, 0))
    vv = pl.BlockSpec((RET_CHUNK, RET_V_W), lambda b, c: (b * nc + c, 0))
    return pl.pallas_call(
        _ret_prompt_kernel,
        grid=(BATCH, nc),
        in_specs=[qk, qk, qk, qk, vv, vv,
                  pl.BlockSpec((RET_HEADS, RET_CHUNK, RET_CHUNK), lambda b, c: (0, 0, 0)),
                  pl.BlockSpec((RET_HEADS, 1, RET_DV), lambda b, c: (0, 0, 0)),
                  pl.BlockSpec((1, RET_V_W), lambda b, c: (0, 0))],
        out_specs=[vv, pl.BlockSpec((None, RET_HEADS, RET_DK, RET_DV), lambda b, c: (b, 0, 0, 0))],
        out_shape=[jax.ShapeDtypeStruct((BATCH * SEQ, RET_V_W), BF16),
                   jax.ShapeDtypeStruct((BATCH, RET_HEADS, RET_DK, RET_DV), F32)],
        compiler_params=_params(("parallel", "arbitrary")),
        name="retention_prompt",
    )(pr["rq"], pr["rqd"], pr["rk"], pr["rkd"], pr["rv"], pr["gates"], decay, cdec, gn_w.reshape(1, RET_V_W))


def _ret_sample_kernel(q_ref, qd_ref, kt_ref, k_ref, v_ref, rg_ref, cdec_ref, gnw_ref, s_ref, ret_ref, so_ref):
    b = pl.program_id(0)
    nb = kt_ref.shape[1]
    onehot = lax.broadcasted_iota(I32, (1, nb), 1) == b
    for h in range(RET_HEADS):
        qk_sl = slice(h * RET_DK, (h + 1) * RET_DK)
        vv = slice(h * RET_DV, (h + 1) * RET_DV)
        s = s_ref[h]
        v = v_ref[pl.ds(b, 1), vv]
        q = q_ref[pl.ds(b, 1), qk_sl]
        k = k_ref[pl.ds(b, 1), qk_sl]
        qk = jnp.sum(q.astype(BF16).astype(F32) * k.astype(BF16).astype(F32), axis=-1, keepdims=True)
        qd8 = jnp.broadcast_to(qd_ref[pl.ds(b, 1), qk_sl], (16, RET_DK)).astype(BF16)
        o = qk.astype(BF16).astype(F32) * v.astype(BF16).astype(F32) \
            + jnp.dot(qd8, s.astype(BF16), preferred_element_type=F32)[0:1]
        k_col = jnp.sum(jnp.where(onehot, kt_ref[qk_sl, :], 0.0), axis=1, keepdims=True)
        so_ref[h] = cdec_ref[h] * s + k_col * v
        ret_ref[:, vv] = _group_norm_gate(o, gnw_ref[:, vv], rg_ref[pl.ds(b, 1), vv])


def _ret_sample(ps, state, cdec, gn_w):
    nb = DEC_BATCH
    qk = pl.BlockSpec((nb, RET_QK_W), lambda b: (0, 0))
    vv = pl.BlockSpec((nb, RET_V_W), lambda b: (0, 0))
    st = pl.BlockSpec((None, RET_HEADS, RET_DK, RET_DV), lambda b: (b, 0, 0, 0))
    return pl.pallas_call(
        _ret_sample_kernel,
        grid=(nb,),
        in_specs=[qk, qk, pl.BlockSpec((RET_QK_W, nb), lambda b: (0, 0)), qk, vv, vv,
                  pl.BlockSpec((RET_HEADS, 1, RET_DV), lambda b: (0, 0, 0)),
                  pl.BlockSpec((1, RET_V_W), lambda b: (0, 0)), st],
        out_specs=[pl.BlockSpec((None, 1, RET_V_W), lambda b: (b, 0, 0)), st],
        out_shape=[jax.ShapeDtypeStruct((nb, 1, RET_V_W), F32),
                   jax.ShapeDtypeStruct((nb, RET_HEADS, RET_DK, RET_DV), F32)],
        compiler_params=_params(("arbitrary",)),
        name="retention_sample",
    )(ps["rq"], ps["rqd"], ps["rkd"].T, ps["rk"], ps["rv"], ps["gates"], cdec, gn_w.reshape(1, RET_V_W), state)


PAGES_PER_STEP = 8
IDX_PAGES_PER_STEP = 32


def _idx_sample_kernel(pt_ref, iq_ref, w_ref, new_ref, *rest):
    pages = rest[:IDX_PAGES_PER_STEP]
    sc_ref, scn_ref = rest[IDX_PAGES_PER_STEP], rest[IDX_PAGES_PER_STEP + 1]
    j = pl.program_id(1)
    hi, lo = _split_bf16(iq_ref[...])
    lhs = jnp.concatenate([hi + pltpu.roll(hi, IDX_DIM, 1), lo], axis=1).astype(BF16)
    w = w_ref[...] * IDX_SCALE

    def page_scores(xt):
        khi, klo = _split_bf16(xt)
        rhs = jnp.concatenate([khi, klo, khi, jnp.zeros_like(khi)], axis=0).astype(BF16)
        d = jnp.dot(lhs, rhs, preferred_element_type=F32)
        return jnp.sum(w * jnp.maximum(d, 0.0), axis=0, keepdims=True)

    for i in range(IDX_PAGES_PER_STEP):
        sc_ref[:, i * PAGE_SIZE:(i + 1) * PAGE_SIZE] = page_scores(pages[i][...])

    @pl.when(j == pl.num_programs(1) - 1)
    def _():
        scn_ref[...] = page_scores(new_ref[...])


def _idx_sample(page_table, iq3, w3, new_pages, cache_idx):
    nsteps = N_PAGES // IDX_PAGES_PER_STEP

    def page_spec(i):
        return pl.BlockSpec((None, IDX_DIM, PAGE_SIZE),
                            lambda b, j, pt: (pt[b, j * IDX_PAGES_PER_STEP + i], 0, 0))

    grid_spec = pltpu.PrefetchScalarGridSpec(
        num_scalar_prefetch=1,
        grid=(DEC_BATCH, nsteps),
        in_specs=[pl.BlockSpec((None, IDX_HEADS, LANES), lambda b, j, pt: (b, 0, 0)),
                  pl.BlockSpec((None, IDX_HEADS, 1), lambda b, j, pt: (b, 0, 0)),
                  pl.BlockSpec((None, IDX_DIM, PAGE_SIZE), lambda b, j, pt: (b, 0, 0))]
                 + [page_spec(i) for i in range(IDX_PAGES_PER_STEP)],
        out_specs=[pl.BlockSpec((None, 1, IDX_PAGES_PER_STEP * PAGE_SIZE), lambda b, j, pt: (b, 0, j)),
                   pl.BlockSpec((None, 1, PAGE_SIZE), lambda b, j, pt: (b, 0, 0))],
    )
    return pl.pallas_call(
        _idx_sample_kernel,
        grid_spec=grid_spec,
        out_shape=[jax.ShapeDtypeStruct((DEC_BATCH, 1, PAST_LEN), F32),
                   jax.ShapeDtypeStruct((DEC_BATCH, 1, PAGE_SIZE), F32)],
        compiler_params=_params(("parallel", "arbitrary")),
        name="indexer_sample",
    )(page_table, iq3, w3, new_pages, *([cache_idx] * IDX_PAGES_PER_STEP))


def _select_sample_kernel(sc_ref, bias_ref, key_ref):
    _topk_mask(sc_ref, key_ref, bias_ref, jnp.full((1, LANES), PAST_LEN, I32), sc_ref.shape[0] // KEY_CHUNK)


def _select_sample(scores_t):
    return pl.pallas_call(
        _select_sample_kernel,
        out_shape=jax.ShapeDtypeStruct(scores_t.shape, F32),
        scratch_shapes=[pltpu.VMEM(scores_t.shape, I32)],
        compiler_params=pltpu.CompilerParams(vmem_limit_bytes=VMEM_LIMIT),
        name="select_sample",
    )(scores_t)


PAGE_ROWS = PAGE_SIZE * ATT_KV_HEADS


def _attn_sample_step(j, n_steps, q_ref, bias_ref, biasn_ref, kn_ref, vn_ref, kp, vp, o_ref, m_ref, l_ref, acc_ref):
    col = lax.broadcasted_iota(I32, (ATT_HEADS, PAGE_ROWS), 1)
    head = lax.broadcasted_iota(I32, (ATT_HEADS, PAGE_ROWS), 0)
    own = (col % ATT_KV_HEADS) == (head // (ATT_HEADS // ATT_KV_HEADS))
    c_exp = ATT_SCALE * math.log2(math.e)
    q = q_ref[...].astype(BF16)

    @pl.when(j == 0)
    def _():
        m_ref[...] = jnp.full_like(m_ref, NEG_BIG)
        l_ref[...] = jnp.zeros_like(l_ref)
        acc_ref[...] = jnp.zeros_like(acc_ref)

    logits = []
    for i in range(PAGES_PER_STEP):
        lg = lax.dot_general(q, kp[i][...].astype(BF16), _NT, preferred_element_type=F32)
        logits.append(jnp.where(own, lg + bias_ref[:, i * PAGE_ROWS:(i + 1) * PAGE_ROWS], NEG_BIG))
    m_old = m_ref[...]
    m_new = m_old
    for lg in logits:
        m_new = jnp.maximum(m_new, jnp.max(lg, axis=1, keepdims=True))
    alpha = jnp.exp2((m_old - m_new) * c_exp)
    l_new = alpha * l_ref[...]
    acc = alpha * acc_ref[...]
    for i in range(PAGES_PER_STEP):
        p = jnp.exp2((logits[i] - m_new) * c_exp)
        l_new = l_new + jnp.sum(p, axis=1, keepdims=True)
        acc = acc + jnp.dot(p.astype(BF16), vp[i][...].astype(BF16), preferred_element_type=F32)
    m_ref[...] = m_new
    l_ref[...] = l_new
    acc_ref[...] = acc

    @pl.when(j == n_steps - 1)
    def _():
        kn = kn_ref[...].astype(BF16).astype(F32)
        vn = vn_ref[...].astype(BF16).astype(F32)
        lgn = jnp.sum(q.astype(F32) * kn, axis=1, keepdims=True) + biasn_ref[:, 0:1]
        m_f = jnp.maximum(m_new, lgn)
        a = jnp.exp2((m_new - m_f) * c_exp)
        pn = jnp.exp2((lgn - m_f) * c_exp)
        o_ref[...] = (a * acc + pn.astype(BF16).astype(F32) * vn) / (a * l_new + pn)


MLP_UP_TM, MLP_UP_TN = 1024, 512


def _mlp_up_attn_kernel(pt_ref, hn_ref, wup_ref, q_ref, bias_ref, biasn_ref, kn_ref, vn_ref, *rest):
    kp = rest[:PAGES_PER_STEP]
    vp = rest[PAGES_PER_STEP:2 * PAGES_PER_STEP]
    u_ref, o_ref = rest[2 * PAGES_PER_STEP], rest[2 * PAGES_PER_STEP + 1]
    m_ref, l_ref, acc_ref = rest[2 * PAGES_PER_STEP + 2:]
    _ep_relu2(jnp.dot(hn_ref[...], wup_ref[...], preferred_element_type=F32), (), (u_ref,))
    n_pg = N_PAGES // PAGES_PER_STEP
    _attn_sample_step(pl.program_id(1) % n_pg, n_pg, q_ref, bias_ref, biasn_ref, kn_ref, vn_ref, kp, vp,
                      o_ref, m_ref, l_ref, acc_ref)


def _mlp_up_with_sample_attn(hn, w_up, page_table, q3, bias_rows, bias_new, k_new, v_new, cache_k, cache_v):
    m, kd = hn.shape
    n_pg = N_PAGES // PAGES_PER_STEP
    n_row, n_col = m // MLP_UP_TM, D_FF // MLP_UP_TN
    assert n_row * n_col == DEC_BATCH * n_pg and n_col % n_pg == 0

    def row(i, j):
        return (i * n_col + j) // n_pg

    def page_spec(p):
        return pl.BlockSpec((None, PAGE_ROWS, ATT_HEAD_DIM),
                            lambda i, j, pt: (pt[row(i, j), (j % n_pg) * PAGES_PER_STEP + p], 0, 0))

    head_rows = pl.BlockSpec((None, ATT_HEADS, ATT_HEAD_DIM), lambda i, j, pt: (row(i, j), 0, 0))
    grid_spec = pltpu.PrefetchScalarGridSpec(
        num_scalar_prefetch=1,
        grid=(n_row, n_col),
        in_specs=[pl.BlockSpec((MLP_UP_TM, kd), lambda i, j, pt: (i, 0)),
                  pl.BlockSpec((kd, MLP_UP_TN), lambda i, j, pt: (0, j)),
                  head_rows,
                  pl.BlockSpec((None, 1, PAGES_PER_STEP * PAGE_ROWS), lambda i, j, pt: (row(i, j), 0, j % n_pg)),
                  pl.BlockSpec((None, 1, PAGE_SIZE), lambda i, j, pt: (row(i, j), 0, 0)),
                  head_rows, head_rows]
                 + [page_spec(p) for p in range(PAGES_PER_STEP)] * 2,
        out_specs=[pl.BlockSpec((MLP_UP_TM, MLP_UP_TN), lambda i, j, pt: (i, j)), head_rows],
        scratch_shapes=[pltpu.VMEM((ATT_HEADS, 1), F32), pltpu.VMEM((ATT_HEADS, 1), F32),
                        pltpu.VMEM((ATT_HEADS, ATT_HEAD_DIM), F32)],
    )
    return pl.pallas_call(
        _mlp_up_attn_kernel,
        grid_spec=grid_spec,
        out_shape=[jax.ShapeDtypeStruct((m, D_FF), BF16),
                   jax.ShapeDtypeStruct((DEC_BATCH, ATT_HEADS, ATT_HEAD_DIM), F32)],
        compiler_params=_params(("arbitrary", "arbitrary")),
        name="mlp_up_attention_sample",
    )(page_table, hn, w_up, q3, bias_rows, bias_new, k_new, v_new,
      *([cache_k] * PAGES_PER_STEP), *([cache_v] * PAGES_PER_STEP))


def _finish(x, att, ret, gates, w, *, tm, tr, sample_attn=None):
    m = x.shape[0]
    mg = _merge(att, ret, w["att_proj"], w["ret_proj"], gates, tm, 512)
    y, = _matmul(mg, w["out"], _ep_plain, [], [_nat(m, D_MODEL, F32, tm, 512)], tm=tm, tn=512, name="w_out")
    h, hn = _post_attn(x, y, w["n_attn_post"], w["n_mlp_pre"], tr)
    if sample_attn is None:
        u, = _matmul(hn, w["mlp_up"], _ep_relu2, [], [_nat(m, D_FF, BF16, tm, 512)], tm=tm, tn=512, name="mlp_up")
        att_s = None
    else:
        u, att_s = _mlp_up_with_sample_attn(hn, w["mlp_up"], *sample_attn)
    d, = _matmul(u, w["mlp_down"], _ep_plain, [], [_nat(m, D_MODEL, F32, tm, 512)], tm=tm, tn=512, tk=4096,
                 name="mlp_down")
    out = _post_mlp(h, d, w["n_mlp_post"], tr)
    return out if sample_attn is None else (out, att_s)


def kernel(x_prompt, x_sample, cache_k, cache_v, cache_idx_k, state_ret, page_table, norm_attn_pre,
           norm_attn_post, w_in, ret_gn_w, w_att_proj, w_ret_proj, w_out, norm_mlp_pre, w_mlp_up,
           w_mlp_down, norm_mlp_post):
    log_gamma = jnp.log1p(-jnp.exp2(-5.0 - jnp.arange(RET_HEADS, dtype=F32)))
    w_in0 = w_in[0].T
    w = {
        "att_proj": w_att_proj[0].astype(BF16), "ret_proj": w_ret_proj[0].astype(BF16),
        "out": w_out[0].astype(BF16), "mlp_up": w_mlp_up[0].astype(BF16), "mlp_down": w_mlp_down[0].astype(BF16),
        "n_attn_post": norm_attn_post[0], "n_mlp_pre": norm_mlp_pre[0], "n_mlp_post": norm_mlp_post[0],
    }
    gn_w = ret_gn_w[0]

    m_p = BATCH * SEQ
    xp = x_prompt.reshape(m_p, D_MODEL)
    xn = _rmsnorm_cast(xp, norm_attn_pre[0], 256)
    pos_p = jnp.arange(SEQ, dtype=I32)
    pr = _project(xn, w_in0, pos_p, pos_p % RET_CHUNK, float(RET_CHUNK), log_gamma, tm=1024, prompt=True)
    vt = pr["v_bf"].reshape(BATCH, SEQ // KEY_CHUNK, KEY_CHUNK, ATT_KV_HEADS, ATT_HEAD_DIM).transpose(0, 3, 1, 4, 2)
    vt = jnp.concatenate([vt, jnp.ones(vt.shape[:3] + (VT_ONES, KEY_CHUNK), BF16)], axis=3)
    att4 = _dsa_prompt(pr["iq"], pr["ikw"], pr["q"], pr["k_heads"], vt)
    att = att4.reshape(m_p // Q_BLOCK, ATT_HEADS, Q_BLOCK, ATT_HEAD_DIM).transpose(0, 2, 1, 3).reshape(m_p, ATT_Q_W)
    ci = jnp.arange(RET_CHUNK, dtype=F32)
    diff = ci[:, None] - ci[None, :]
    decay = jnp.where(diff >= 0, jnp.exp(log_gamma[:, None, None] * jnp.maximum(diff, 0.0)), 0.0)
    cdec_p = jnp.broadcast_to(jnp.exp(log_gamma * RET_CHUNK)[:, None, None], (RET_HEADS, 1, RET_DV))
    ret, s_prompt = _ret_prompt(pr, decay, cdec_p, gn_w)

    nb = DEC_BATCH
    xs = x_sample.reshape(nb, D_MODEL)
    xns = _rmsnorm_cast(xs, norm_attn_pre[0], nb)
    pos_s = jnp.full((nb,), PAST_LEN, I32)
    ps = _project(xns, pr["w_bf16"], pos_s, jnp.zeros((nb,), I32), 1.0, log_gamma, tm=nb, prompt=False)
    ik_new = ps["ikw"][:, :IDX_DIM]
    iq3 = jnp.pad(ps["iq"].reshape(nb, IDX_HEADS, IDX_DIM), ((0, 0), (0, 0), (0, LANES - IDX_DIM)))
    w3 = ps["ikw"][:, IDX_DIM:IDX_DIM + IDX_HEADS].reshape(nb, IDX_HEADS, 1)
    new_pages = jnp.pad(ik_new[:, :, None], ((0, 0), (0, 0), (0, PAGE_SIZE - 1)))
    sc_past, sc_new = _idx_sample(page_table, iq3, w3, new_pages, cache_idx_k[0].transpose(0, 2, 1))
    scores = jnp.concatenate([sc_past.reshape(nb, PAST_LEN), sc_new.reshape(nb, PAGE_SIZE)], axis=1)
    n_rows = -(-(PAST_LEN + PAGE_SIZE) // KEY_CHUNK) * KEY_CHUNK
    scores_t = jnp.pad(scores.T, ((0, n_rows - PAST_LEN - PAGE_SIZE), (0, LANES - nb)))
    bias = _select_sample(scores_t)[:PAST_LEN + PAGE_SIZE, :nb].T
    n_phys = cache_k.shape[1]
    group = ATT_HEADS // ATT_KV_HEADS
    sample_attn = (page_table, ps["q"].reshape(nb, ATT_HEADS, ATT_HEAD_DIM),
                   jnp.repeat(bias[:, :PAST_LEN], ATT_KV_HEADS, axis=1).reshape(nb, 1, N_PAGES * PAGE_ROWS),
                   bias[:, PAST_LEN:].reshape(nb, 1, PAGE_SIZE),
                   jnp.repeat(ps["k"].reshape(nb, ATT_KV_HEADS, ATT_HEAD_DIM), group, axis=1),
                   jnp.repeat(ps["v"].reshape(nb, ATT_KV_HEADS, ATT_HEAD_DIM), group, axis=1),
                   cache_k[0].reshape(n_phys, PAGE_ROWS, ATT_HEAD_DIM),
                   cache_v[0].reshape(n_phys, PAGE_ROWS, ATT_HEAD_DIM))

    y_prompt, att_s = _finish(xp, att, ret, pr["gates"], w, tm=1024, tr=256, sample_attn=sample_attn)
    y_prompt = y_prompt.reshape(BATCH, SEQ, D_MODEL)
    cdec_s = jnp.broadcast_to(jnp.exp(log_gamma)[:, None, None], (RET_HEADS, 1, RET_DV))
    ret_s, s_sample = _ret_sample(ps, state_ret[0], cdec_s, gn_w)
    y_sample = _finish(xs, att_s.reshape(nb, ATT_Q_W).astype(BF16), ret_s.reshape(nb, RET_V_W).astype(BF16),
                       ps["gates"], w, tm=nb, tr=nb).reshape(nb, 1, D_MODEL)

    return (y_prompt, y_sample,
            pr["k"].reshape(1, BATCH, SEQ, ATT_KV_HEADS, ATT_HEAD_DIM),
            pr["v"].reshape(1, BATCH, SEQ, ATT_KV_HEADS, ATT_HEAD_DIM),
            pr["ikw"][:, :IDX_DIM].reshape(1, BATCH, SEQ, IDX_DIM),
            s_prompt[None],
            ps["k"].reshape(1, nb, 1, ATT_KV_HEADS, ATT_HEAD_DIM),
            ps["v"].reshape(1, nb, 1, ATT_KV_HEADS, ATT_HEAD_DIM),
            ik_new.reshape(1, nb, 1, IDX_DIM),
            s_sample[None])
```

```python
import functools
import math

import jax
import jax.numpy as jnp
import numpy as np
from jax import lax
from jax.experimental import pallas as pl
from jax.experimental.pallas import tpu as pltpu

F32 = jnp.float32
BF16 = jnp.bfloat16
I32 = jnp.int32

D_MODEL = 4096
BATCH = 4
SEQ = 2048
DEC_BATCH = 32
PAST_LEN = 8192
PAGE_SIZE = 128
N_PAGES = PAST_LEN // PAGE_SIZE
ATT_HEADS = 16
ATT_KV_HEADS = 8
ATT_HEAD_DIM = 128
ROPE_THETA = 500000.0
IDX_HEADS = 32
IDX_DIM = 64
TOPK = 256
RET_HEADS = 8
RET_DK = 256
RET_DV = 512
RET_THETA = 10000.0
RET_CHUNK = 128
D_FF = 4 * D_MODEL
NORM_EPS = 1e-6

ATT_Q_W = ATT_HEADS * ATT_HEAD_DIM
ATT_KV_W = ATT_KV_HEADS * ATT_HEAD_DIM
IDX_Q_W = IDX_HEADS * IDX_DIM
RET_QK_W = RET_HEADS * RET_DK
RET_V_W = RET_HEADS * RET_DV
SPLITS = (ATT_Q_W, ATT_KV_W, ATT_KV_W, IDX_Q_W, IDX_DIM, IDX_HEADS,
          RET_QK_W, RET_QK_W, RET_V_W, RET_V_W, D_MODEL, D_MODEL)
OFFS = tuple(int(v) for v in np.concatenate([[0], np.cumsum(SPLITS)]))

LANES = 128
SUBLANES = 8
Q_BLOCK = 128
KEY_CHUNK = 512
PAIRS_PER_STEP = 16
GROUPS_PER_STEP = 8
VT_ONES = 16
VMEM_LIMIT = 56 * 1024 * 1024

INT_MIN = -2 ** 31
NEG_BIG = -1e30
ATT_SCALE = ATT_HEAD_DIM ** -0.5
IDX_SCALE = (IDX_DIM ** -0.5) * (IDX_HEADS ** -0.5)

_NT = (((1,), (1,)), ((), ()))


def _params(sem):
    return pltpu.CompilerParams(dimension_semantics=sem, vmem_limit_bytes=VMEM_LIMIT)


def _sigmoid(x):
    return 1.0 / (1.0 + jnp.exp(-x))


def _rmsnorm_cast_kernel(x_ref, w_ref, o_ref):
    x = x_ref[...]
    y = x * lax.rsqrt(jnp.mean(x * x, axis=-1, keepdims=True) + NORM_EPS)
    o_ref[...] = (y * w_ref[...]).astype(o_ref.dtype)


def _rmsnorm_cast(x, w, tr):
    m, d = x.shape
    return pl.pallas_call(
        _rmsnorm_cast_kernel,
        grid=(m // tr,),
        in_specs=[pl.BlockSpec((tr, d), lambda i: (i, 0)), pl.BlockSpec((1, d), lambda i: (0, 0))],
        out_specs=pl.BlockSpec((tr, d), lambda i: (i, 0)),
        out_shape=jax.ShapeDtypeStruct((m, d), BF16),
        compiler_params=_params(("parallel",)),
        name="rmsnorm_cast",
    )(x, w.reshape(1, d))


def _post_attn_kernel(x_ref, y_ref, w1_ref, w2_ref, h_ref, hn_ref):
    y = y_ref[...]
    yn = y * lax.rsqrt(jnp.mean(y * y, axis=-1, keepdims=True) + NORM_EPS) * w1_ref[...]
    h = x_ref[...] + yn
    h_ref[...] = h
    hn = h * lax.rsqrt(jnp.mean(h * h, axis=-1, keepdims=True) + NORM_EPS) * w2_ref[...]
    hn_ref[...] = hn.astype(hn_ref.dtype)


def _post_attn(x, y, w1, w2, tr):
    m, d = x.shape
    row = pl.BlockSpec((tr, d), lambda i: (i, 0))
    vec = pl.BlockSpec((1, d), lambda i: (0, 0))
    return pl.pallas_call(
        _post_attn_kernel,
        grid=(m // tr,),
        in_specs=[row, row, vec, vec],
        out_specs=[row, row],
        out_shape=[jax.ShapeDtypeStruct((m, d), F32), jax.ShapeDtypeStruct((m, d), BF16)],
        compiler_params=_params(("parallel",)),
        name="post_attn_norm",
    )(x, y, w1.reshape(1, d), w2.reshape(1, d))


def _post_mlp_kernel(h_ref, d_ref, w_ref, o_ref):
    d = d_ref[...]
    dn = d * lax.rsqrt(jnp.mean(d * d, axis=-1, keepdims=True) + NORM_EPS) * w_ref[...]
    o_ref[...] = h_ref[...] + dn


def _post_mlp(h, d, w, tr):
    m, dm = h.shape
    row = pl.BlockSpec((tr, dm), lambda i: (i, 0))
    return pl.pallas_call(
        _post_mlp_kernel,
        grid=(m // tr,),
        in_specs=[row, row, pl.BlockSpec((1, dm), lambda i: (0, 0))],
        out_specs=row,
        out_shape=jax.ShapeDtypeStruct((m, dm), F32),
        compiler_params=_params(("parallel",)),
        name="post_mlp_norm",
    )(h, d, w.reshape(1, dm))


def _dot(a, b, b_t):
    if b_t:
        return lax.dot_general(a, b, _NT, preferred_element_type=F32)
    return jnp.dot(a, b, preferred_element_type=F32)


def _mm_kernel(*refs, n_extra, n_out, nk, b_t, epilogue):
    a_ref, b_ref = refs[0], refs[1]
    extra = refs[2:2 + n_extra]
    outs = refs[2 + n_extra:2 + n_extra + n_out]
    if nk == 1:
        epilogue(_dot(a_ref[...], b_ref[...], b_t), extra, outs)
        return
    acc_ref = refs[-1]
    k = pl.program_id(2)
    d = _dot(a_ref[...], b_ref[...], b_t)

    @pl.when(k == 0)
    def _():
        acc_ref[...] = d

    @pl.when((k > 0) & (k < nk - 1))
    def _():
        acc_ref[...] += d

    @pl.when(k == nk - 1)
    def _():
        epilogue(acc_ref[...] + d, extra, outs)


def _matmul(a, b, epilogue, extras, outs, *, tm, tn, tk=None, b_t=False, name):
    m, kd = a.shape
    n = b.shape[0] if b_t else b.shape[1]
    tk = kd if tk is None else tk
    nk = kd // tk
    grid = (m // tm, n // tn, nk)

    def lift(f):
        return lambda i, j, k: f(i, j)

    b_spec = (pl.BlockSpec((tn, tk), lambda i, j, k: (j, k)) if b_t
              else pl.BlockSpec((tk, tn), lambda i, j, k: (k, j)))
    in_specs = [pl.BlockSpec((tm, tk), lambda i, j, k: (i, k)), b_spec]
    in_specs += [pl.BlockSpec(bs, lift(im)) for _, bs, im in extras]
    out_specs = [pl.BlockSpec(bs, lift(im)) for _, _, bs, im in outs]
    out_shape = [jax.ShapeDtypeStruct(s, dt) for s, dt, _, _ in outs]
    scratch = [pltpu.VMEM((tm, tn), F32)] if nk > 1 else []
    res = pl.pallas_call(
        functools.partial(_mm_kernel, n_extra=len(extras), n_out=len(outs), nk=nk, b_t=b_t, epilogue=epilogue),
        grid=grid,
        in_specs=in_specs,
        out_specs=out_specs,
        out_shape=out_shape,
        scratch_shapes=scratch,
        compiler_params=_params(("parallel", "parallel", "arbitrary")),
        name=name,
    )(a, b, *[e[0] for e in extras])
    return res


def _mm_w32_kernel(*refs, n_extra, n_out, valid, epilogue):
    a_ref, w_ref = refs[0], refs[1]
    extra = refs[2:2 + n_extra]
    outs = refs[2 + n_extra:2 + n_extra + n_out]
    wbf_ref = refs[2 + n_extra + n_out]

    @pl.when(pl.program_id(1) == 0)
    def _():
        w = w_ref[...]
        if valid < w.shape[0]:
            w = jnp.where(lax.broadcasted_iota(I32, w.shape, 0) < valid, w, 0.0)
        wbf_ref[...] = w.astype(BF16)

    epilogue(_dot(a_ref[...], wbf_ref[...], True), extra, outs)


def _matmul_w32(a, w_t, epilogue, extras, outs, *, tm, tn, n, col_off, valid=None, name):
    m, kd = a.shape
    assert n % tn == 0 and col_off % SUBLANES == 0
    valid = tn if valid is None else valid
    grid = (n // tn, m // tm)

    def lift(f):
        return lambda j, i: f(i, j)

    w_spec = pl.BlockSpec((pl.Element(tn), pl.Element(kd)),
                          lambda j, i: (pl.multiple_of(col_off + j * tn, SUBLANES), 0))
    in_specs = [pl.BlockSpec((tm, kd), lambda j, i: (i, 0)), w_spec]
    in_specs += [pl.BlockSpec(bs, lift(im)) for _, bs, im in extras]
    out_specs = [pl.BlockSpec(bs, lift(im)) for _, _, bs, im in outs] + [pl.BlockSpec((tn, kd), lambda j, i: (j, 0))]
    out_shape = [jax.ShapeDtypeStruct(s, dt) for s, dt, _, _ in outs] + [jax.ShapeDtypeStruct((n, kd), BF16)]
    return pl.pallas_call(
        functools.partial(_mm_w32_kernel, n_extra=len(extras), n_out=len(outs), valid=valid, epilogue=epilogue),
        grid=grid,
        in_specs=in_specs,
        out_specs=out_specs,
        out_shape=out_shape,
        compiler_params=_params(("arbitrary", "arbitrary")),
        name=name,
    )(a, w_t, *[e[0] for e in extras])


def _nat(m, n, dt, tm, tn):
    return ((m, n), dt, (tm, tn), lambda i, j: (i, j))


def _rope_lanes(y, c, sm, sp, half):
    n = y.shape[-1]
    return y * c + pltpu.roll(y, n - half, 1) * sm + pltpu.roll(y, half, 1) * sp


def _ep_plain(acc, extra, outs):
    for o in outs:
        o[...] = acc.astype(o.dtype)


def _ep_rope_lanes(acc, extra, outs, *, half, blocked):
    c, sm, sp = extra[0][...], extra[1][...], extra[2][...]
    tm, tn = acc.shape
    for jj in range(tn // LANES):
        y = _rope_lanes(acc[:, jj * LANES:(jj + 1) * LANES], c, sm, sp, half)
        if blocked:
            for r in range(tm // Q_BLOCK):
                outs[0][r, jj] = y[r * Q_BLOCK:(r + 1) * Q_BLOCK].astype(outs[0].dtype)
        else:
            outs[0][:, jj * LANES:(jj + 1) * LANES] = y.astype(outs[0].dtype)
        if len(outs) > 1:
            outs[1][jj] = y.astype(outs[1].dtype)


def _ep_ret_qk(acc, extra, outs, *, scale):
    cos, sin, dec = extra[0][...], extra[1][...], extra[2]
    tn = acc.shape[1]
    for hh in range(tn // RET_DK):
        lo = hh * RET_DK
        x1 = acc[:, lo:lo + LANES]
        x2 = acc[:, lo + LANES:lo + 2 * LANES]
        o1 = x1 * cos - x2 * sin
        o2 = x2 * cos + x1 * sin
        if scale != 1.0:
            o1 = o1 * scale
            o2 = o2 * scale
        d = dec[:, hh * LANES:(hh + 1) * LANES]
        outs[0][:, lo:lo + LANES] = o1.astype(outs[0].dtype)
        outs[0][:, lo + LANES:lo + 2 * LANES] = o2.astype(outs[0].dtype)
        outs[1][:, lo:lo + LANES] = (o1 * d).astype(outs[1].dtype)
        outs[1][:, lo + LANES:lo + 2 * LANES] = (o2 * d).astype(outs[1].dtype)


def _merge_kernel(att_ref, ret_ref, wa_ref, wr_ref, ga_ref, gr_ref, o_ref):
    a = jnp.dot(att_ref[...], wa_ref[...], preferred_element_type=F32)
    r = jnp.dot(ret_ref[...], wr_ref[...], preferred_element_type=F32)
    o_ref[...] = (_sigmoid(ga_ref[...]) * a + _sigmoid(gr_ref[...]) * r).astype(o_ref.dtype)


def _merge(att, ret, wa, wr, gates, tm, tn):
    m = att.shape[0]
    nb = D_MODEL // tn
    return pl.pallas_call(
        _merge_kernel,
        grid=(m // tm, nb),
        in_specs=[
            pl.BlockSpec((tm, ATT_Q_W), lambda i, j: (i, 0)),
            pl.BlockSpec((tm, RET_V_W), lambda i, j: (i, 0)),
            pl.BlockSpec((ATT_Q_W, tn), lambda i, j: (0, j)),
            pl.BlockSpec((RET_V_W, tn), lambda i, j: (0, j)),
            pl.BlockSpec((tm, tn), lambda i, j: (i, nb + j)),
            pl.BlockSpec((tm, tn), lambda i, j: (i, 2 * nb + j)),
        ],
        out_specs=pl.BlockSpec((tm, tn), lambda i, j: (i, j)),
        out_shape=jax.ShapeDtypeStruct((m, D_MODEL), BF16),
        compiler_params=_params(("parallel", "parallel")),
        name="merge_proj",
    )(att, ret, wa, wr, gates, gates)


def _ep_relu2(acc, extra, outs):
    u = jnp.maximum(acc, 0.0)
    outs[0][...] = (u * u).astype(outs[0].dtype)


def _rope_lane_tables(pos, head_w, rot, theta):
    half = rot // 2
    inv_freq = jnp.exp(-math.log(theta) * jnp.arange(half, dtype=F32) / half)
    ang = pos.astype(F32)[:, None] * inv_freq[None, :]
    cos, sin = jnp.cos(ang), jnp.sin(ang)
    n = pos.shape[0]
    z_half = jnp.zeros((n, half), F32)
    rest1 = jnp.ones((n, head_w - rot), F32)
    rest0 = jnp.zeros((n, head_w - rot), F32)
    c = jnp.concatenate([cos, cos, rest1], axis=1)
    sm = jnp.concatenate([-sin, z_half, rest0], axis=1)
    sp = jnp.concatenate([z_half, sin, rest0], axis=1)
    rep = LANES // head_w
    return [jnp.tile(t, (1, rep)) for t in (c, sm, sp)]


def _ret_tables(pos, chunk_pos, chunk_len, log_gamma):
    half = RET_DK // 2
    inv_freq = jnp.exp(-math.log(RET_THETA) * jnp.arange(half, dtype=F32) / half)
    ang = pos.astype(F32)[:, None] * inv_freq[None, :]
    i = chunk_pos.astype(F32)[:, None]
    q_dec = jnp.exp(log_gamma[None, :] * (i + 1.0))
    k_dec = jnp.exp(log_gamma[None, :] * (chunk_len - 1.0 - i))
    return jnp.cos(ang), jnp.sin(ang), jnp.repeat(q_dec, LANES, axis=1), jnp.repeat(k_dec, LANES, axis=1)


PROJ_TN = 512


def _project(xn, w, pos, chunk_pos, chunk_len, log_gamma, *, tm, prompt):
    m = xn.shape[0]
    nrep = max(pos.shape[0] // tm, 1)
    wdt = BF16 if prompt else F32
    out = {"w_bf16": {}}

    def mm(name, a, b, epilogue, extras, outs, tn=PROJ_TN, valid_cols=None):
        n = -(-(OFFS[b] - OFFS[a]) // tn) * tn
        if prompt:
            *res, wb = _matmul_w32(xn, w, epilogue, extras, outs, tm=tm, tn=tn, n=n, col_off=OFFS[a],
                                   valid=valid_cols, name=name)
            out["w_bf16"][name] = wb
            return res
        return _matmul(xn, w[name], epilogue, extras, outs, tm=tm, tn=tn, b_t=True, name=name)

    def tab(t, width=LANES, by_col=False):
        if by_col:
            return (t, (tm, width), lambda i, j: (i % nrep, j))
        return (t, (tm, width), lambda i, j: (i % nrep, 0))

    att_t = [tab(t) for t in _rope_lane_tables(pos, ATT_HEAD_DIM, ATT_HEAD_DIM // 4, ROPE_THETA)]
    idx_t = [tab(t) for t in _rope_lane_tables(pos, IDX_DIM, IDX_DIM // 4, ROPE_THETA)]
    r_cos, r_sin, q_dec, k_dec = _ret_tables(pos, chunk_pos, chunk_len, log_gamma)
    tn = PROJ_TN
    nqb = m // Q_BLOCK

    if prompt:
        o = [((nqb, ATT_HEADS, Q_BLOCK, LANES), BF16, (tm // Q_BLOCK, tn // LANES, Q_BLOCK, LANES),
              lambda i, j: (i, j, 0, 0))]
    else:
        o = [_nat(m, ATT_Q_W, F32, tm, tn)]
    out["q"], = mm("proj_q", 0, 1, functools.partial(_ep_rope_lanes, half=16, blocked=prompt), att_t, o)
    o = [_nat(m, ATT_KV_W, F32, tm, tn)]
    if prompt:
        per_b = SEQ // tm
        o.append(((BATCH, ATT_KV_HEADS, SEQ, LANES), BF16, (None, tn // LANES, tm, LANES),
                  lambda i, j: (i // per_b, j, i % per_b, 0)))
    res = mm("proj_k", 1, 2, functools.partial(_ep_rope_lanes, half=16, blocked=False), att_t, o)
    out["k"] = res[0]
    if prompt:
        out["k_heads"] = res[1]
    o = [_nat(m, ATT_KV_W, F32, tm, tn)]
    if prompt:
        o.append(_nat(m, ATT_KV_W, BF16, tm, tn))
    res = mm("proj_v", 2, 3, _ep_plain, [], o)
    out["v"] = res[0]
    if prompt:
        out["v_bf"] = res[1]
    if prompt:
        o = [((nqb, IDX_Q_W // LANES, Q_BLOCK, LANES), F32, (tm // Q_BLOCK, tn // LANES, Q_BLOCK, LANES),
              lambda i, j: (i, j, 0, 0))]
    else:
        o = [_nat(m, IDX_Q_W, F32, tm, tn)]
    out["iq"], = mm("proj_iq", 3, 4, functools.partial(_ep_rope_lanes, half=8, blocked=prompt), idx_t, o)
    ikw_t = [tab(t) for t in _rope_lane_tables(pos, LANES, IDX_DIM // 4, ROPE_THETA)]
    out["ikw"], = mm("proj_ikw", 4, 6, functools.partial(_ep_rope_lanes, half=8, blocked=False), ikw_t,
                     [_nat(m, LANES, F32, tm, LANES)], tn=LANES, valid_cols=IDX_DIM + IDX_HEADS)
    for nm, grp, dec, scale in (("rq", 6, q_dec, 1.0), ("rk", 7, k_dec, RET_DK ** -0.5)):
        res = mm("proj_" + nm, grp, grp + 1, functools.partial(_ep_ret_qk, scale=scale),
                 [tab(r_cos), tab(r_sin), tab(dec, tn // 2, by_col=True)],
                 [_nat(m, RET_QK_W, wdt, tm, tn), _nat(m, RET_QK_W, wdt, tm, tn)])
        out[nm], out[nm + "d"] = res
    out["rv"], = mm("proj_rv", 8, 9, _ep_plain, [], [_nat(m, RET_V_W, wdt, tm, tn)])
    out["gates"], = mm("proj_gates", 9, 12, _ep_plain, [], [_nat(m, 3 * D_MODEL, F32, tm, tn)])
    return out


def _sortable_key(score):
    kb = lax.bitcast_convert_type(score, I32)
    kb = jnp.where(kb == INT_MIN, 0, kb)
    return jnp.where(kb < 0, kb ^ 0x7FFFFFFF, kb)


def _tile_reduce(x, op):
    tiles = [x[i * SUBLANES:(i + 1) * SUBLANES] for i in range(x.shape[0] // SUBLANES)]
    while len(tiles) > 1:
        nxt = [op(tiles[i], tiles[i + 1]) for i in range(0, len(tiles) - 1, 2)]
        if len(tiles) % 2:
            nxt.append(tiles[-1])
        tiles = nxt
    return tiles[0]


def _topk_mask(sc_ref, key_ref, bias_ref, q_pos, n_chunks):
    def rows(c):
        r0 = pl.multiple_of(c * KEY_CHUNK, KEY_CHUNK)
        return pl.ds(r0, KEY_CHUNK), r0 + lax.broadcasted_iota(I32, (KEY_CHUNK, LANES), 0)

    def build(c, carry):
        sl, row = rows(c)
        key_ref[sl, :] = jnp.where(row <= q_pos, _sortable_key(sc_ref[sl, :]), INT_MIN)
        return carry

    lax.fori_loop(0, n_chunks, build, 0)

    def count(pred):
        def body(c, acc):
            sl, row = rows(c)
            return acc + _tile_reduce(jnp.where(pred(key_ref[sl, :], row), 1.0, 0.0), jnp.add)

        part = lax.fori_loop(0, n_chunks, body, jnp.zeros((SUBLANES, LANES), F32))
        return jnp.sum(part, axis=0, keepdims=True)

    t0 = jnp.where(count(lambda k, row: k >= 0) >= TOPK, 0, INT_MIN).astype(I32)

    def search(i, t):
        cand = t | lax.shift_left(jnp.int32(1), jnp.int32(30) - i)
        return jnp.where(count(lambda k, row: k >= cand) >= TOPK, cand, t)

    thr = lax.fori_loop(0, 31, search, t0)
    n_eq = count(lambda k, row: (row <= q_pos) & (k == thr))
    need = TOPK - count(lambda k, row: k > thr)

    def write(c, carry):
        sl, row = rows(c)
        k = key_ref[sl, :]
        sel = jnp.where(k > thr, 0.0, jnp.where((row <= q_pos) & (k == thr), 0.0, NEG_BIG))
        bias_ref[sl, :] = sel.astype(bias_ref.dtype)
        return carry

    lax.fori_loop(0, n_chunks, write, 0)

    @pl.when(jnp.max(n_eq - need) > 0)
    def _():
        r_i = lax.broadcasted_iota(I32, (LANES, LANES), 0)
        c_i = lax.broadcasted_iota(I32, (LANES, LANES), 1)
        tri = jnp.where(c_i < r_i, 1.0, 0.0).astype(BF16)

        def chunk(c, off):
            r0 = pl.multiple_of(c * LANES, LANES)
            kc = key_ref[pl.ds(r0, LANES), :]
            rc = r0 + lax.broadcasted_iota(I32, (LANES, LANES), 0)
            e = (rc <= q_pos) & (kc == thr)
            ef = jnp.where(e, 1.0, 0.0)
            before = jnp.dot(tri, ef.astype(BF16), preferred_element_type=F32) + off
            keep = e & (before < need)
            sel = jnp.where(kc > thr, 0.0, jnp.where(keep, 0.0, NEG_BIG))
            bias_ref[pl.ds(r0, LANES), :] = sel.astype(bias_ref.dtype)
            return off + jnp.sum(ef, axis=0, keepdims=True)

        lax.fori_loop(0, n_chunks * (KEY_CHUNK // LANES), chunk, jnp.zeros((1, LANES), F32))


def _split_bf16(x):
    hi = x.astype(BF16).astype(F32)
    return hi, x - hi


def _dsa_prompt_kernel(iq_ref, ikw_all_ref, ikw_q_ref, q_ref, k_ref, vt_ref, w32_ref, att_ref, wbf_ref,
                       lhs_ref, wt_ref, sc_ref, key_ref, bias_ref):
    wbf_ref[...] = w32_ref[...].astype(BF16)
    qb = pl.program_id(1)
    n_ck = (qb * Q_BLOCK + Q_BLOCK + KEY_CHUNK - 1) // KEY_CHUNK
    lane = lax.broadcasted_iota(I32, (1, LANES), 1)
    low = lane < IDX_DIM

    def chunk_rows(c):
        return pl.ds(pl.multiple_of(c * KEY_CHUNK, KEY_CHUNK), KEY_CHUNK)

    @pl.when(qb == 0)
    def _():
        x = jnp.where(low, ikw_all_ref[...], 0.0)
        hi, lo = _split_bf16(x)
        lhs_ref[:, :LANES] = (hi + pltpu.roll(lo, IDX_DIM, 1)).astype(BF16)
        lhs_ref[:, LANES:] = hi.astype(BF16)

    wt_ref[...] = ikw_q_ref[...].T * IDX_SCALE
    sc_ref[...] = jnp.zeros_like(sc_ref)

    def pairs_body(pq, carry):
        rhs, wts = [], []
        for u in range(PAIRS_PER_STEP):
            p = pq * PAIRS_PER_STEP + u
            hi, lo = _split_bf16(iq_ref[p])
            rhi = pltpu.roll(hi, IDX_DIM, 1)
            rlo = pltpu.roll(lo, IDX_DIM, 1)
            ra = jnp.concatenate([jnp.where(low, hi, rhi), jnp.where(low, lo, 0.0)], axis=1)
            rb = jnp.concatenate([jnp.where(low, rhi, hi), jnp.where(low, rlo, 0.0)], axis=1)
            rhs.append(jnp.concatenate([ra, rb], axis=0).astype(BF16))
            wts.append((wt_ref[pl.ds(IDX_DIM + 2 * p, 1), :], wt_ref[pl.ds(IDX_DIM + 2 * p + 1, 1), :]))

        def ck_body(c, inner):
            sl = chunk_rows(c)
            lhs = lhs_ref[sl, :]
            acc = sc_ref[sl, :]
            for rhs_t, (wa, wb) in zip(rhs, wts):
                d = lax.dot_general(lhs, rhs_t, _NT, preferred_element_type=F32)
                acc = acc + wa * jnp.maximum(d[:, :LANES], 0.0) + wb * jnp.maximum(d[:, LANES:], 0.0)
            sc_ref[sl, :] = acc
            return inner

        lax.fori_loop(0, n_ck, ck_body, 0)
        return carry

    lax.fori_loop(0, IDX_HEADS // 2 // PAIRS_PER_STEP, pairs_body, 0)

    _topk_mask(sc_ref, key_ref, bias_ref, qb * Q_BLOCK + lane, n_ck)

    c_exp = ATT_SCALE * math.log2(math.e)

    r_i = lax.broadcasted_iota(I32, (2 * Q_BLOCK, LANES), 0)
    c_i = lax.broadcasted_iota(I32, (2 * Q_BLOCK, LANES), 1)
    onehot_q = jnp.where((r_i % Q_BLOCK) == c_i, 1.0, 0.0).astype(BF16)
    n_acc = vt_ref.shape[2]

    def groups_body(gq, carry):
        gs = [gq * GROUPS_PER_STEP + u for u in range(GROUPS_PER_STEP)]
        qqs = [jnp.concatenate([jnp.concatenate([q_ref[2 * g], q_ref[2 * g + 1]], axis=0), onehot_q], axis=1)
               for g in gs]

        def ck_body(c, state):
            sl = chunk_rows(c)
            mask = bias_ref[sl, :]
            lgs = [lax.dot_general(jnp.concatenate([k_ref[g, sl, :], mask], axis=1), qq, _NT,
                                   preferred_element_type=F32)
                   for g, qq in zip(gs, qqs)]
            soft = []
            for lg, (m_old, acc) in zip(lgs, state):
                m_new = jnp.maximum(m_old, jnp.max(_tile_reduce(lg, jnp.maximum), axis=0, keepdims=True))
                alpha = jnp.exp2((m_old - m_new) * c_exp)
                soft.append((m_new, alpha, jnp.exp2((lg - m_new) * c_exp).astype(BF16)))
            return tuple((m_new, alpha * acc + jnp.dot(vt_ref[g, c], pt, preferred_element_type=F32))
                         for g, (m_new, alpha, pt), (_, acc) in zip(gs, soft, state))

        init = (jnp.full((1, 2 * LANES), NEG_BIG, F32), jnp.zeros((n_acc, 2 * LANES), F32))
        final = lax.fori_loop(0, n_ck, ck_body, (init,) * GROUPS_PER_STEP)
        for g, (_, acc) in zip(gs, final):
            o = acc[:ATT_HEAD_DIM] / acc[ATT_HEAD_DIM:ATT_HEAD_DIM + 1]
            for hh in range(2):
                att_ref[2 * g + hh] = o[:, hh * LANES:(hh + 1) * LANES].T.astype(att_ref.dtype)
        return carry

    lax.fori_loop(0, ATT_KV_HEADS // GROUPS_PER_STEP, groups_body, 0)


def _cast_slab_specs(w32, n_steps, step_of):
    rows, cols = w32.shape
    slab = rows // n_steps
    assert slab * n_steps == rows and slab % (2 * SUBLANES) == 0
    spec = pl.BlockSpec((slab, cols), lambda *idx: (step_of(*idx), 0))
    return spec, jax.ShapeDtypeStruct((rows, cols), BF16)


def _dsa_prompt(iq, ikw, q, k_heads, vt, w32):
    nqb = SEQ // Q_BLOCK
    n_pair = IDX_Q_W // LANES
    cast_spec, cast_shape = _cast_slab_specs(w32, BATCH * nqb, lambda b, i: b * nqb + i)
    return pl.pallas_call(
        _dsa_prompt_kernel,
        grid=(BATCH, nqb),
        in_specs=[
            pl.BlockSpec((None, n_pair, Q_BLOCK, LANES), lambda b, i: (b * nqb + i, 0, 0, 0)),
            pl.BlockSpec((SEQ, LANES), lambda b, i: (b, 0)),
            pl.BlockSpec((Q_BLOCK, LANES), lambda b, i: (b * nqb + i, 0)),
            pl.BlockSpec((None, ATT_HEADS, Q_BLOCK, LANES), lambda b, i: (b * nqb + i, 0, 0, 0)),
            pl.BlockSpec((None, ATT_KV_HEADS, SEQ, LANES), lambda b, i: (b, 0, 0, 0)),
            pl.BlockSpec((None, ATT_KV_HEADS, SEQ // KEY_CHUNK, ATT_HEAD_DIM + VT_ONES, KEY_CHUNK),
                         lambda b, i: (b, 0, 0, 0, 0)),
            cast_spec,
        ],
        out_specs=[pl.BlockSpec((None, ATT_HEADS, Q_BLOCK, LANES), lambda b, i: (b * nqb + i, 0, 0, 0)), cast_spec],
        out_shape=[jax.ShapeDtypeStruct((BATCH * nqb, ATT_HEADS, Q_BLOCK, LANES), BF16), cast_shape],
        scratch_shapes=[
            pltpu.VMEM((SEQ, 2 * LANES), BF16),
            pltpu.VMEM((LANES, LANES), F32),
            pltpu.VMEM((SEQ, LANES), F32),
            pltpu.VMEM((SEQ, LANES), I32),
            pltpu.VMEM((SEQ, LANES), BF16),
        ],
        compiler_params=_params(("parallel", "arbitrary")),
        name="dsa_prompt",
    )(iq, ikw, ikw, q, k_heads, vt, w32)


def _group_norm_gate(o, gn_w, rg):
    of = o * lax.rsqrt(jnp.mean(o * o, axis=-1, keepdims=True) + NORM_EPS) * gn_w
    return of * (rg * _sigmoid(rg))


def _ret_prompt_kernel(q_ref, qd_ref, k_ref, kd_ref, v_ref, rg_ref, decay_ref, cdec_ref, gnw_ref, w32_ref,
                       ret_ref, s_ref, wbf_ref):
    wbf_ref[...] = w32_ref[...].astype(BF16)
    c = pl.program_id(1)

    @pl.when(c == 0)
    def _():
        s_ref[...] = jnp.zeros_like(s_ref)

    def qk(h):
        return slice(h * RET_DK, (h + 1) * RET_DK)

    def vv(h):
        return slice(h * RET_DV, (h + 1) * RET_DV)

    heads = range(RET_HEADS)
    scores = [lax.dot_general(q_ref[:, qk(h)], k_ref[:, qk(h)], _NT, preferred_element_type=F32) for h in heads]
    cross = [jnp.dot(qd_ref[:, qk(h)], s_ref[h].astype(BF16), preferred_element_type=F32) for h in heads]
    kvs = [lax.dot_general(kd_ref[:, qk(h)], v_ref[:, vv(h)], (((0,), (0,)), ((), ())),
                           preferred_element_type=F32) for h in heads]
    for h in heads:
        s_ref[h] = cdec_ref[h] * s_ref[h] + kvs[h]
    for h in heads:
        o = jnp.dot((scores[h] * decay_ref[h]).astype(BF16), v_ref[:, vv(h)], preferred_element_type=F32) + cross[h]
        ret_ref[:, vv(h)] = _group_norm_gate(o, gnw_ref[:, vv(h)], rg_ref[:, vv(h)]).astype(ret_ref.dtype)


def _ret_prompt(pr, decay, cdec, gn_w, w32):
    nc = SEQ // RET_CHUNK
    qk = pl.BlockSpec((RET_CHUNK, RET_QK_W), lambda b, c: (b * nc + c, 0))
    vv = pl.BlockSpec((RET_CHUNK, RET_V_W), lambda b, c: (b * nc + c, 0))
    cast_spec, cast_shape = _cast_slab_specs(w32, BATCH * nc, lambda b, c: b * nc + c)
    return pl.pallas_call(
        _ret_prompt_kernel,
        grid=(BATCH, nc),
        in_specs=[qk, qk, qk, qk, vv, vv,
                  pl.BlockSpec((RET_HEADS, RET_CHUNK, RET_CHUNK), lambda b, c: (0, 0, 0)),
                  pl.BlockSpec((RET_HEADS, 1, RET_DV), lambda b, c: (0, 0, 0)),
                  pl.BlockSpec((1, RET_V_W), lambda b, c: (0, 0)), cast_spec],
        out_specs=[vv, pl.BlockSpec((None, RET_HEADS, RET_DK, RET_DV), lambda b, c: (b, 0, 0, 0)), cast_spec],
        out_shape=[jax.ShapeDtypeStruct((BATCH * SEQ, RET_V_W), BF16),
                   jax.ShapeDtypeStruct((BATCH, RET_HEADS, RET_DK, RET_DV), F32), cast_shape],
        compiler_params=_params(("parallel", "arbitrary")),
        name="retention_prompt",
    )(pr["rq"], pr["rqd"], pr["rk"], pr["rkd"], pr["rv"], pr["gates"], decay, cdec, gn_w.reshape(1, RET_V_W), w32)


def _ret_sample_kernel(q_ref, qd_ref, kt_ref, k_ref, v_ref, rg_ref, cdec_ref, gnw_ref, s_ref, ret_ref, so_ref):
    b = pl.program_id(0)
    nb = kt_ref.shape[1]
    onehot = lax.broadcasted_iota(I32, (1, nb), 1) == b
    for h in range(RET_HEADS):
        qk_sl = slice(h * RET_DK, (h + 1) * RET_DK)
        vv = slice(h * RET_DV, (h + 1) * RET_DV)
        s = s_ref[h]
        v = v_ref[pl.ds(b, 1), vv]
        q = q_ref[pl.ds(b, 1), qk_sl]
        k = k_ref[pl.ds(b, 1), qk_sl]
        qk = jnp.sum(q.astype(BF16).astype(F32) * k.astype(BF16).astype(F32), axis=-1, keepdims=True)
        qd8 = jnp.broadcast_to(qd_ref[pl.ds(b, 1), qk_sl], (16, RET_DK)).astype(BF16)
        o = qk.astype(BF16).astype(F32) * v.astype(BF16).astype(F32) \
            + jnp.dot(qd8, s.astype(BF16), preferred_element_type=F32)[0:1]
        k_col = jnp.sum(jnp.where(onehot, kt_ref[qk_sl, :], 0.0), axis=1, keepdims=True)
        so_ref[h] = cdec_ref[h] * s + k_col * v
        ret_ref[:, vv] = _group_norm_gate(o, gnw_ref[:, vv], rg_ref[pl.ds(b, 1), vv])


def _ret_sample(ps, state, cdec, gn_w):
    nb = DEC_BATCH
    qk = pl.BlockSpec((nb, RET_QK_W), lambda b: (0, 0))
    vv = pl.BlockSpec((nb, RET_V_W), lambda b: (0, 0))
    st = pl.BlockSpec((None, RET_HEADS, RET_DK, RET_DV), lambda b: (b, 0, 0, 0))
    return pl.pallas_call(
        _ret_sample_kernel,
        grid=(nb,),
        in_specs=[qk, qk, pl.BlockSpec((RET_QK_W, nb), lambda b: (0, 0)), qk, vv, vv,
                  pl.BlockSpec((RET_HEADS, 1, RET_DV), lambda b: (0, 0, 0)),
                  pl.BlockSpec((1, RET_V_W), lambda b: (0, 0)), st],
        out_specs=[pl.BlockSpec((None, 1, RET_V_W), lambda b: (b, 0, 0)), st],
        out_shape=[jax.ShapeDtypeStruct((nb, 1, RET_V_W), F32),
                   jax.ShapeDtypeStruct((nb, RET_HEADS, RET_DK, RET_DV), F32)],
        compiler_params=_params(("arbitrary",)),
        name="retention_sample",
    )(ps["rq"], ps["rqd"], ps["rkd"].T, ps["rk"], ps["rv"], ps["gates"], cdec, gn_w.reshape(1, RET_V_W), state)


PAGES_PER_STEP = 8
IDX_PAGES_PER_STEP = 32


def _idx_sample_kernel(pt_ref, iq_ref, w_ref, new_ref, *rest):
    pages = rest[:IDX_PAGES_PER_STEP]
    sc_ref, scn_ref = rest[IDX_PAGES_PER_STEP], rest[IDX_PAGES_PER_STEP + 1]
    j = pl.program_id(1)
    hi, lo = _split_bf16(iq_ref[...])
    lhs = jnp.concatenate([hi + pltpu.roll(hi, IDX_DIM, 1), lo], axis=1).astype(BF16)
    w = w_ref[...] * IDX_SCALE

    def page_scores(xt):
        khi, klo = _split_bf16(xt)
        rhs = jnp.concatenate([khi, klo, khi, jnp.zeros_like(khi)], axis=0).astype(BF16)
        d = jnp.dot(lhs, rhs, preferred_element_type=F32)
        return jnp.sum(w * jnp.maximum(d, 0.0), axis=0, keepdims=True)

    for i in range(IDX_PAGES_PER_STEP):
        sc_ref[:, i * PAGE_SIZE:(i + 1) * PAGE_SIZE] = page_scores(pages[i][...])

    @pl.when(j == pl.num_programs(1) - 1)
    def _():
        scn_ref[...] = page_scores(new_ref[...])


def _idx_sample(page_table, iq3, w3, new_pages, cache_idx):
    nsteps = N_PAGES // IDX_PAGES_PER_STEP

    def page_spec(i):
        return pl.BlockSpec((None, IDX_DIM, PAGE_SIZE),
                            lambda b, j, pt: (pt[b, j * IDX_PAGES_PER_STEP + i], 0, 0))

    grid_spec = pltpu.PrefetchScalarGridSpec(
        num_scalar_prefetch=1,
        grid=(DEC_BATCH, nsteps),
        in_specs=[pl.BlockSpec((None, IDX_HEADS, LANES), lambda b, j, pt: (b, 0, 0)),
                  pl.BlockSpec((None, IDX_HEADS, 1), lambda b, j, pt: (b, 0, 0)),
                  pl.BlockSpec((None, IDX_DIM, PAGE_SIZE), lambda b, j, pt: (b, 0, 0))]
                 + [page_spec(i) for i in range(IDX_PAGES_PER_STEP)],
        out_specs=[pl.BlockSpec((None, 1, IDX_PAGES_PER_STEP * PAGE_SIZE), lambda b, j, pt: (b, 0, j)),
                   pl.BlockSpec((None, 1, PAGE_SIZE), lambda b, j, pt: (b, 0, 0))],
    )
    return pl.pallas_call(
        _idx_sample_kernel,
        grid_spec=grid_spec,
        out_shape=[jax.ShapeDtypeStruct((DEC_BATCH, 1, PAST_LEN), F32),
                   jax.ShapeDtypeStruct((DEC_BATCH, 1, PAGE_SIZE), F32)],
        compiler_params=_params(("parallel", "arbitrary")),
        name="indexer_sample",
    )(page_table, iq3, w3, new_pages, *([cache_idx] * IDX_PAGES_PER_STEP))


def _select_sample_kernel(sc_ref, bias_ref, key_ref):
    _topk_mask(sc_ref, key_ref, bias_ref, jnp.full((1, LANES), PAST_LEN, I32), sc_ref.shape[0] // KEY_CHUNK)


def _select_sample(scores_t):
    return pl.pallas_call(
        _select_sample_kernel,
        out_shape=jax.ShapeDtypeStruct(scores_t.shape, F32),
        scratch_shapes=[pltpu.VMEM(scores_t.shape, I32)],
        compiler_params=pltpu.CompilerParams(vmem_limit_bytes=VMEM_LIMIT),
        name="select_sample",
    )(scores_t)


PAGE_ROWS = PAGE_SIZE * ATT_KV_HEADS


def _attn_sample_step(j, n_steps, q_ref, bias_ref, biasn_ref, kn_ref, vn_ref, kp, vp, o_ref, m_ref, l_ref, acc_ref):
    col = lax.broadcasted_iota(I32, (ATT_HEADS, PAGE_ROWS), 1)
    head = lax.broadcasted_iota(I32, (ATT_HEADS, PAGE_ROWS), 0)
    own = (col % ATT_KV_HEADS) == (head // (ATT_HEADS // ATT_KV_HEADS))
    c_exp = ATT_SCALE * math.log2(math.e)
    q = q_ref[...].astype(BF16)

    @pl.when(j == 0)
    def _():
        m_ref[...] = jnp.full_like(m_ref, NEG_BIG)
        l_ref[...] = jnp.zeros_like(l_ref)
        acc_ref[...] = jnp.zeros_like(acc_ref)

    logits = []
    for i in range(PAGES_PER_STEP):
        lg = lax.dot_general(q, kp[i][...].astype(BF16), _NT, preferred_element_type=F32)
        logits.append(jnp.where(own, lg + bias_ref[:, i * PAGE_ROWS:(i + 1) * PAGE_ROWS], NEG_BIG))
    m_old = m_ref[...]
    m_new = m_old
    for lg in logits:
        m_new = jnp.maximum(m_new, jnp.max(lg, axis=1, keepdims=True))
    alpha = jnp.exp2((m_old - m_new) * c_exp)
    l_new = alpha * l_ref[...]
    acc = alpha * acc_ref[...]
    for i in range(PAGES_PER_STEP):
        p = jnp.exp2((logits[i] - m_new) * c_exp)
        l_new = l_new + jnp.sum(p, axis=1, keepdims=True)
        acc = acc + jnp.dot(p.astype(BF16), vp[i][...].astype(BF16), preferred_element_type=F32)
    m_ref[...] = m_new
    l_ref[...] = l_new
    acc_ref[...] = acc

    @pl.when(j == n_steps - 1)
    def _():
        kn = kn_ref[...].astype(BF16).astype(F32)
        vn = vn_ref[...].astype(BF16).astype(F32)
        lgn = jnp.sum(q.astype(F32) * kn, axis=1, keepdims=True) + biasn_ref[:, 0:1]
        m_f = jnp.maximum(m_new, lgn)
        a = jnp.exp2((m_new - m_f) * c_exp)
        pn = jnp.exp2((lgn - m_f) * c_exp)
        o_ref[...] = (a * acc + pn.astype(BF16).astype(F32) * vn) / (a * l_new + pn)


MLP_UP_TM, MLP_UP_TN = 1024, 512


def _mlp_up_attn_kernel(pt_ref, hn_ref, wup_ref, q_ref, bias_ref, biasn_ref, kn_ref, vn_ref, *rest):
    kp = rest[:PAGES_PER_STEP]
    vp = rest[PAGES_PER_STEP:2 * PAGES_PER_STEP]
    u_ref, o_ref = rest[2 * PAGES_PER_STEP], rest[2 * PAGES_PER_STEP + 1]
    m_ref, l_ref, acc_ref = rest[2 * PAGES_PER_STEP + 2:]
    _ep_relu2(jnp.dot(hn_ref[...], wup_ref[...], preferred_element_type=F32), (), (u_ref,))
    n_pg = N_PAGES // PAGES_PER_STEP
    _attn_sample_step(pl.program_id(1) % n_pg, n_pg, q_ref, bias_ref, biasn_ref, kn_ref, vn_ref, kp, vp,
                      o_ref, m_ref, l_ref, acc_ref)


def _mlp_up_with_sample_attn(hn, w_up, page_table, q3, bias_rows, bias_new, k_new, v_new, cache_k, cache_v):
    m, kd = hn.shape
    n_pg = N_PAGES // PAGES_PER_STEP
    n_row, n_col = m // MLP_UP_TM, D_FF // MLP_UP_TN
    assert n_row * n_col == DEC_BATCH * n_pg and n_col % n_pg == 0

    def row(i, j):
        return (i * n_col + j) // n_pg

    def page_spec(p):
        return pl.BlockSpec((None, PAGE_ROWS, ATT_HEAD_DIM),
                            lambda i, j, pt: (pt[row(i, j), (j % n_pg) * PAGES_PER_STEP + p], 0, 0))

    head_rows = pl.BlockSpec((None, ATT_HEADS, ATT_HEAD_DIM), lambda i, j, pt: (row(i, j), 0, 0))
    grid_spec = pltpu.PrefetchScalarGridSpec(
        num_scalar_prefetch=1,
        grid=(n_row, n_col),
        in_specs=[pl.BlockSpec((MLP_UP_TM, kd), lambda i, j, pt: (i, 0)),
                  pl.BlockSpec((kd, MLP_UP_TN), lambda i, j, pt: (0, j)),
                  head_rows,
                  pl.BlockSpec((None, 1, PAGES_PER_STEP * PAGE_ROWS), lambda i, j, pt: (row(i, j), 0, j % n_pg)),
                  pl.BlockSpec((None, 1, PAGE_SIZE), lambda i, j, pt: (row(i, j), 0, 0)),
                  head_rows, head_rows]
                 + [page_spec(p) for p in range(PAGES_PER_STEP)] * 2,
        out_specs=[pl.BlockSpec((MLP_UP_TM, MLP_UP_TN), lambda i, j, pt: (i, j)), head_rows],
        scratch_shapes=[pltpu.VMEM((ATT_HEADS, 1), F32), pltpu.VMEM((ATT_HEADS, 1), F32),
                        pltpu.VMEM((ATT_HEADS, ATT_HEAD_DIM), F32)],
    )
    return pl.pallas_call(
        _mlp_up_attn_kernel,
        grid_spec=grid_spec,
        out_shape=[jax.ShapeDtypeStruct((m, D_FF), BF16),
                   jax.ShapeDtypeStruct((DEC_BATCH, ATT_HEADS, ATT_HEAD_DIM), F32)],
        compiler_params=_params(("arbitrary", "arbitrary")),
        name="mlp_up_attention_sample",
    )(page_table, hn, w_up, q3, bias_rows, bias_new, k_new, v_new,
      *([cache_k] * PAGES_PER_STEP), *([cache_v] * PAGES_PER_STEP))


def _finish(x, att, ret, gates, w, *, tm, tr, sample_attn=None):
    m = x.shape[0]
    mg = _merge(att, ret, w["att_proj"], w["ret_proj"], gates, tm, 512)
    y, = _matmul(mg, w["out"], _ep_plain, [], [_nat(m, D_MODEL, F32, tm, 512)], tm=tm, tn=512, name="w_out")
    h, hn = _post_attn(x, y, w["n_attn_post"], w["n_mlp_pre"], tr)
    if sample_attn is None:
        u, = _matmul(hn, w["mlp_up"], _ep_relu2, [], [_nat(m, D_FF, BF16, tm, 512)], tm=tm, tn=512, name="mlp_up")
        att_s = None
    else:
        u, att_s = _mlp_up_with_sample_attn(hn, w["mlp_up"], *sample_attn)
    d, = _matmul(u, w["mlp_down"], _ep_plain, [], [_nat(m, D_MODEL, F32, tm, 512)], tm=tm, tn=512, tk=4096,
                 name="mlp_down")
    out = _post_mlp(h, d, w["n_mlp_post"], tr)
    return out if sample_attn is None else (out, att_s)


def kernel(x_prompt, x_sample, cache_k, cache_v, cache_idx_k, state_ret, page_table, norm_attn_pre,
           norm_attn_post, w_in, ret_gn_w, w_att_proj, w_ret_proj, w_out, norm_mlp_pre, w_mlp_up,
           w_mlp_down, norm_mlp_post):
    log_gamma = jnp.log1p(-jnp.exp2(-5.0 - jnp.arange(RET_HEADS, dtype=F32)))
    w_in0 = w_in[0].T
    w = {
        "att_proj": w_att_proj[0].astype(BF16), "ret_proj": w_ret_proj[0].astype(BF16),
        "out": w_out[0].astype(BF16),
        "n_attn_post": norm_attn_post[0], "n_mlp_pre": norm_mlp_pre[0], "n_mlp_post": norm_mlp_post[0],
    }
    gn_w = ret_gn_w[0]

    m_p = BATCH * SEQ
    xp = x_prompt.reshape(m_p, D_MODEL)
    xn = _rmsnorm_cast(xp, norm_attn_pre[0], 256)
    pos_p = jnp.arange(SEQ, dtype=I32)
    pr = _project(xn, w_in0, pos_p, pos_p % RET_CHUNK, float(RET_CHUNK), log_gamma, tm=1024, prompt=True)
    vt = pr["v_bf"].reshape(BATCH, SEQ // KEY_CHUNK, KEY_CHUNK, ATT_KV_HEADS, ATT_HEAD_DIM).transpose(0, 3, 1, 4, 2)
    vt = jnp.concatenate([vt, jnp.ones(vt.shape[:3] + (VT_ONES, KEY_CHUNK), BF16)], axis=3)
    att4, w["mlp_down"] = _dsa_prompt(pr["iq"], pr["ikw"], pr["q"], pr["k_heads"], vt, w_mlp_down[0])
    att = att4.reshape(m_p // Q_BLOCK, ATT_HEADS, Q_BLOCK, ATT_HEAD_DIM).transpose(0, 2, 1, 3).reshape(m_p, ATT_Q_W)
    ci = jnp.arange(RET_CHUNK, dtype=F32)
    diff = ci[:, None] - ci[None, :]
    decay = jnp.where(diff >= 0, jnp.exp(log_gamma[:, None, None] * jnp.maximum(diff, 0.0)), 0.0)
    cdec_p = jnp.broadcast_to(jnp.exp(log_gamma * RET_CHUNK)[:, None, None], (RET_HEADS, 1, RET_DV))
    ret, s_prompt, w["mlp_up"] = _ret_prompt(pr, decay, cdec_p, gn_w, w_mlp_up[0])

    nb = DEC_BATCH
    xs = x_sample.reshape(nb, D_MODEL)
    xns = _rmsnorm_cast(xs, norm_attn_pre[0], nb)
    pos_s = jnp.full((nb,), PAST_LEN, I32)
    ps = _project(xns, pr["w_bf16"], pos_s, jnp.zeros((nb,), I32), 1.0, log_gamma, tm=nb, prompt=False)
    ik_new = ps["ikw"][:, :IDX_DIM]
    iq3 = jnp.pad(ps["iq"].reshape(nb, IDX_HEADS, IDX_DIM), ((0, 0), (0, 0), (0, LANES - IDX_DIM)))
    w3 = ps["ikw"][:, IDX_DIM:IDX_DIM + IDX_HEADS].reshape(nb, IDX_HEADS, 1)
    new_pages = jnp.pad(ik_new[:, :, None], ((0, 0), (0, 0), (0, PAGE_SIZE - 1)))
    sc_past, sc_new = _idx_sample(page_table, iq3, w3, new_pages, cache_idx_k[0].transpose(0, 2, 1))
    scores = jnp.concatenate([sc_past.reshape(nb, PAST_LEN), sc_new.reshape(nb, PAGE_SIZE)], axis=1)
    n_rows = -(-(PAST_LEN + PAGE_SIZE) // KEY_CHUNK) * KEY_CHUNK
    scores_t = jnp.pad(scores.T, ((0, n_rows - PAST_LEN - PAGE_SIZE), (0, LANES - nb)))
    bias = _select_sample(scores_t)[:PAST_LEN + PAGE_SIZE, :nb].T
    n_phys = cache_k.shape[1]
    group = ATT_HEADS // ATT_KV_HEADS
    sample_attn = (page_table, ps["q"].reshape(nb, ATT_HEADS, ATT_HEAD_DIM),
                   jnp.repeat(bias[:, :PAST_LEN], ATT_KV_HEADS, axis=1).reshape(nb, 1, N_PAGES * PAGE_ROWS),
                   bias[:, PAST_LEN:].reshape(nb, 1, PAGE_SIZE),
                   jnp.repeat(ps["k"].reshape(nb, ATT_KV_HEADS, ATT_HEAD_DIM), group, axis=1),
                   jnp.repeat(ps["v"].reshape(nb, ATT_KV_HEADS, ATT_HEAD_DIM), group, axis=1),
                   cache_k[0].reshape(n_phys, PAGE_ROWS, ATT_HEAD_DIM),
                   cache_v[0].reshape(n_phys, PAGE_ROWS, ATT_HEAD_DIM))

    y_prompt, att_s = _finish(xp, att, ret, pr["gates"], w, tm=1024, tr=256, sample_attn=sample_attn)
    y_prompt = y_prompt.reshape(BATCH, SEQ, D_MODEL)
    cdec_s = jnp.broadcast_to(jnp.exp(log_gamma)[:, None, None], (RET_HEADS, 1, RET_DV))
    ret_s, s_sample = _ret_sample(ps, state_ret[0], cdec_s, gn_w)
    y_sample = _finish(xs, att_s.reshape(nb, ATT_Q_W).astype(BF16), ret_s.reshape(nb, RET_V_W).astype(BF16),
                       ps["gates"], w, tm=nb, tr=nb).reshape(nb, 1, D_MODEL)

    return (y_prompt, y_sample,
            pr["k"].reshape(1, BATCH, SEQ, ATT_KV_HEADS, ATT_HEAD_DIM),
            pr["v"].reshape(1, BATCH, SEQ, ATT_KV_HEADS, ATT_HEAD_DIM),
            pr["ikw"][:, :IDX_DIM].reshape(1, BATCH, SEQ, IDX_DIM),
            s_prompt[None],
            ps["k"].reshape(1, nb, 1, ATT_KV_HEADS, ATT_HEAD_DIM),
            ps["v"].reshape(1, nb, 1, ATT_KV_HEADS, ATT_HEAD_DIM),
            ik_new.reshape(1, nb, 1, IDX_DIM),
            s_sample[None])
```

```python
import functools
import math

import jax
import jax.numpy as jnp
import numpy as np
from jax import lax
from jax.experimental import pallas as pl
from jax.experimental.pallas import tpu as pltpu

F32 = jnp.float32
BF16 = jnp.bfloat16
I32 = jnp.int32

D_MODEL = 4096
BATCH = 4
SEQ = 2048
DEC_BATCH = 32
PAST_LEN = 8192
PAGE_SIZE = 128
N_PAGES = PAST_LEN // PAGE_SIZE
ATT_HEADS = 16
ATT_KV_HEADS = 8
ATT_HEAD_DIM = 128
ROPE_THETA = 500000.0
IDX_HEADS = 32
IDX_DIM = 64
TOPK = 256
RET_HEADS = 8
RET_DK = 256
RET_DV = 512
RET_THETA = 10000.0
RET_CHUNK = 128
D_FF = 4 * D_MODEL
NORM_EPS = 1e-6

ATT_Q_W = ATT_HEADS * ATT_HEAD_DIM
ATT_KV_W = ATT_KV_HEADS * ATT_HEAD_DIM
IDX_Q_W = IDX_HEADS * IDX_DIM
RET_QK_W = RET_HEADS * RET_DK
RET_V_W = RET_HEADS * RET_DV
SPLITS = (ATT_Q_W, ATT_KV_W, ATT_KV_W, IDX_Q_W, IDX_DIM, IDX_HEADS,
          RET_QK_W, RET_QK_W, RET_V_W, RET_V_W, D_MODEL, D_MODEL)
OFFS = tuple(int(v) for v in np.concatenate([[0], np.cumsum(SPLITS)]))

LANES = 128
SUBLANES = 8
Q_BLOCK = 128
KEY_CHUNK = 512
PAIRS_PER_STEP = 16
GROUPS_PER_STEP = 8
VT_ONES = 16
VMEM_LIMIT = 56 * 1024 * 1024

INT_MIN = -2 ** 31
NEG_BIG = -1e30
ATT_SCALE = ATT_HEAD_DIM ** -0.5
IDX_SCALE = (IDX_DIM ** -0.5) * (IDX_HEADS ** -0.5)

_NT = (((1,), (1,)), ((), ()))


def _params(sem):
    return pltpu.CompilerParams(dimension_semantics=sem, vmem_limit_bytes=VMEM_LIMIT)


def _sigmoid(x):
    return 1.0 / (1.0 + jnp.exp(-x))


def _rmsnorm_cast_kernel(x_ref, w_ref, o_ref):
    x = x_ref[...]
    y = x * lax.rsqrt(jnp.mean(x * x, axis=-1, keepdims=True) + NORM_EPS)
    o_ref[...] = (y * w_ref[...]).astype(o_ref.dtype)


def _rmsnorm_cast(x, w, tr):
    m, d = x.shape
    return pl.pallas_call(
        _rmsnorm_cast_kernel,
        grid=(m // tr,),
        in_specs=[pl.BlockSpec((tr, d), lambda i: (i, 0)), pl.BlockSpec((1, d), lambda i: (0, 0))],
        out_specs=pl.BlockSpec((tr, d), lambda i: (i, 0)),
        out_shape=jax.ShapeDtypeStruct((m, d), BF16),
        compiler_params=_params(("parallel",)),
        name="rmsnorm_cast",
    )(x, w.reshape(1, d))


def _post_attn_kernel(x_ref, y_ref, w1_ref, w2_ref, h_ref, hn_ref):
    y = y_ref[...]
    yn = y * lax.rsqrt(jnp.mean(y * y, axis=-1, keepdims=True) + NORM_EPS) * w1_ref[...]
    h = x_ref[...] + yn
    h_ref[...] = h
    hn = h * lax.rsqrt(jnp.mean(h * h, axis=-1, keepdims=True) + NORM_EPS) * w2_ref[...]
    hn_ref[...] = hn.astype(hn_ref.dtype)


def _post_attn(x, y, w1, w2, tr):
    m, d = x.shape
    row = pl.BlockSpec((tr, d), lambda i: (i, 0))
    vec = pl.BlockSpec((1, d), lambda i: (0, 0))
    return pl.pallas_call(
        _post_attn_kernel,
        grid=(m // tr,),
        in_specs=[row, row, vec, vec],
        out_specs=[row, row],
        out_shape=[jax.ShapeDtypeStruct((m, d), F32), jax.ShapeDtypeStruct((m, d), BF16)],
        compiler_params=_params(("parallel",)),
        name="post_attn_norm",
    )(x, y, w1.reshape(1, d), w2.reshape(1, d))


def _post_mlp_kernel(h_ref, d_ref, w_ref, o_ref):
    d = d_ref[...]
    dn = d * lax.rsqrt(jnp.mean(d * d, axis=-1, keepdims=True) + NORM_EPS) * w_ref[...]
    o_ref[...] = h_ref[...] + dn


def _post_mlp(h, d, w, tr):
    m, dm = h.shape
    row = pl.BlockSpec((tr, dm), lambda i: (i, 0))
    return pl.pallas_call(
        _post_mlp_kernel,
        grid=(m // tr,),
        in_specs=[row, row, pl.BlockSpec((1, dm), lambda i: (0, 0))],
        out_specs=row,
        out_shape=jax.ShapeDtypeStruct((m, dm), F32),
        compiler_params=_params(("parallel",)),
        name="post_mlp_norm",
    )(h, d, w.reshape(1, dm))


def _dot(a, b, b_t):
    if b_t:
        return lax.dot_general(a, b, _NT, preferred_element_type=F32)
    return jnp.dot(a, b, preferred_element_type=F32)


def _mm_kernel(*refs, n_extra, n_out, nk, b_t, epilogue):
    a_ref, b_ref = refs[0], refs[1]
    extra = refs[2:2 + n_extra]
    outs = refs[2 + n_extra:2 + n_extra + n_out]
    if nk == 1:
        epilogue(_dot(a_ref[...], b_ref[...], b_t), extra, outs)
        return
    acc_ref = refs[-1]
    k = pl.program_id(2)
    d = _dot(a_ref[...], b_ref[...], b_t)

    @pl.when(k == 0)
    def _():
        acc_ref[...] = d

    @pl.when((k > 0) & (k < nk - 1))
    def _():
        acc_ref[...] += d

    @pl.when(k == nk - 1)
    def _():
        epilogue(acc_ref[...] + d, extra, outs)


def _matmul(a, b, epilogue, extras, outs, *, tm, tn, tk=None, b_t=False, name):
    m, kd = a.shape
    n = b.shape[0] if b_t else b.shape[1]
    tk = kd if tk is None else tk
    nk = kd // tk
    grid = (m // tm, n // tn, nk)

    def lift(f):
        return lambda i, j, k: f(i, j)

    b_spec = (pl.BlockSpec((tn, tk), lambda i, j, k: (j, k)) if b_t
              else pl.BlockSpec((tk, tn), lambda i, j, k: (k, j)))
    in_specs = [pl.BlockSpec((tm, tk), lambda i, j, k: (i, k)), b_spec]
    in_specs += [pl.BlockSpec(bs, lift(im)) for _, bs, im in extras]
    out_specs = [pl.BlockSpec(bs, lift(im)) for _, _, bs, im in outs]
    out_shape = [jax.ShapeDtypeStruct(s, dt) for s, dt, _, _ in outs]
    scratch = [pltpu.VMEM((tm, tn), F32)] if nk > 1 else []
    res = pl.pallas_call(
        functools.partial(_mm_kernel, n_extra=len(extras), n_out=len(outs), nk=nk, b_t=b_t, epilogue=epilogue),
        grid=grid,
        in_specs=in_specs,
        out_specs=out_specs,
        out_shape=out_shape,
        scratch_shapes=scratch,
        compiler_params=_params(("parallel", "parallel", "arbitrary")),
        name=name,
    )(a, b, *[e[0] for e in extras])
    return res


def _mm_w32_kernel(*refs, n_extra, n_out, valid, epilogue):
    a_ref, w_ref = refs[0], refs[1]
    extra = refs[2:2 + n_extra]
    outs = refs[2 + n_extra:2 + n_extra + n_out]
    wbf_ref = refs[2 + n_extra + n_out]

    @pl.when(pl.program_id(1) == 0)
    def _():
        w = w_ref[...]
        if valid < w.shape[0]:
            w = jnp.where(lax.broadcasted_iota(I32, w.shape, 0) < valid, w, 0.0)
        wbf_ref[...] = w.astype(BF16)

    epilogue(_dot(a_ref[...], wbf_ref[...], True), extra, outs)


def _matmul_w32(a, w_t, epilogue, extras, outs, *, tm, tn, n, col_off, valid=None, name):
    m, kd = a.shape
    assert n % tn == 0 and col_off % SUBLANES == 0
    valid = tn if valid is None else valid
    grid = (n // tn, m // tm)

    def lift(f):
        return lambda j, i: f(i, j)

    w_spec = pl.BlockSpec((pl.Element(tn), pl.Element(kd)),
                          lambda j, i: (pl.multiple_of(col_off + j * tn, SUBLANES), 0))
    in_specs = [pl.BlockSpec((tm, kd), lambda j, i: (i, 0)), w_spec]
    in_specs += [pl.BlockSpec(bs, lift(im)) for _, bs, im in extras]
    out_specs = [pl.BlockSpec(bs, lift(im)) for _, _, bs, im in outs] + [pl.BlockSpec((tn, kd), lambda j, i: (j, 0))]
    out_shape = [jax.ShapeDtypeStruct(s, dt) for s, dt, _, _ in outs] + [jax.ShapeDtypeStruct((n, kd), BF16)]
    return pl.pallas_call(
        functools.partial(_mm_w32_kernel, n_extra=len(extras), n_out=len(outs), valid=valid, epilogue=epilogue),
        grid=grid,
        in_specs=in_specs,
        out_specs=out_specs,
        out_shape=out_shape,
        compiler_params=_params(("arbitrary", "arbitrary")),
        name=name,
    )(a, w_t, *[e[0] for e in extras])


def _nat(m, n, dt, tm, tn):
    return ((m, n), dt, (tm, tn), lambda i, j: (i, j))


def _rope_lanes(y, c, sm, sp, half):
    n = y.shape[-1]
    return y * c + pltpu.roll(y, n - half, 1) * sm + pltpu.roll(y, half, 1) * sp


def _ep_plain(acc, extra, outs):
    for o in outs:
        o[...] = acc.astype(o.dtype)


def _ep_rope_lanes(acc, extra, outs, *, half, blocked):
    c, sm, sp = extra[0][...], extra[1][...], extra[2][...]
    tm, tn = acc.shape
    for jj in range(tn // LANES):
        y = _rope_lanes(acc[:, jj * LANES:(jj + 1) * LANES], c, sm, sp, half)
        if blocked:
            for r in range(tm // Q_BLOCK):
                outs[0][r, jj] = y[r * Q_BLOCK:(r + 1) * Q_BLOCK].astype(outs[0].dtype)
        else:
            outs[0][:, jj * LANES:(jj + 1) * LANES] = y.astype(outs[0].dtype)
        if len(outs) > 1:
            outs[1][jj] = y.astype(outs[1].dtype)


def _ep_ret_qk(acc, extra, outs, *, scale):
    cos, sin, dec = extra[0][...], extra[1][...], extra[2]
    tn = acc.shape[1]
    for hh in range(tn // RET_DK):
        lo = hh * RET_DK
        x1 = acc[:, lo:lo + LANES]
        x2 = acc[:, lo + LANES:lo + 2 * LANES]
        o1 = x1 * cos - x2 * sin
        o2 = x2 * cos + x1 * sin
        if scale != 1.0:
            o1 = o1 * scale
            o2 = o2 * scale
        d = dec[:, hh * LANES:(hh + 1) * LANES]
        outs[0][:, lo:lo + LANES] = o1.astype(outs[0].dtype)
        outs[0][:, lo + LANES:lo + 2 * LANES] = o2.astype(outs[0].dtype)
        outs[1][:, lo:lo + LANES] = (o1 * d).astype(outs[1].dtype)
        outs[1][:, lo + LANES:lo + 2 * LANES] = (o2 * d).astype(outs[1].dtype)


def _merge_kernel(att_ref, ret_ref, wa_ref, wr_ref, ga_ref, gr_ref, o_ref):
    a = jnp.dot(att_ref[...], wa_ref[...], preferred_element_type=F32)
    r = jnp.dot(ret_ref[...], wr_ref[...], preferred_element_type=F32)
    o_ref[...] = (_sigmoid(ga_ref[...]) * a + _sigmoid(gr_ref[...]) * r).astype(o_ref.dtype)


def _merge(att, ret, wa, wr, gates, tm, tn):
    m = att.shape[0]
    nb = D_MODEL // tn
    return pl.pallas_call(
        _merge_kernel,
        grid=(m // tm, nb),
        in_specs=[
            pl.BlockSpec((tm, ATT_Q_W), lambda i, j: (i, 0)),
            pl.BlockSpec((tm, RET_V_W), lambda i, j: (i, 0)),
            pl.BlockSpec((ATT_Q_W, tn), lambda i, j: (0, j)),
            pl.BlockSpec((RET_V_W, tn), lambda i, j: (0, j)),
            pl.BlockSpec((tm, tn), lambda i, j: (i, nb + j)),
            pl.BlockSpec((tm, tn), lambda i, j: (i, 2 * nb + j)),
        ],
        out_specs=pl.BlockSpec((tm, tn), lambda i, j: (i, j)),
        out_shape=jax.ShapeDtypeStruct((m, D_MODEL), BF16),
        compiler_params=_params(("parallel", "parallel")),
        name="merge_proj",
    )(att, ret, wa, wr, gates, gates)


def _ep_relu2(acc, extra, outs):
    u = jnp.maximum(acc, 0.0)
    outs[0][...] = (u * u).astype(outs[0].dtype)


def _rope_lane_tables(pos, head_w, rot, theta):
    half = rot // 2
    inv_freq = jnp.exp(-math.log(theta) * jnp.arange(half, dtype=F32) / half)
    ang = pos.astype(F32)[:, None] * inv_freq[None, :]
    cos, sin = jnp.cos(ang), jnp.sin(ang)
    n = pos.shape[0]
    z_half = jnp.zeros((n, half), F32)
    rest1 = jnp.ones((n, head_w - rot), F32)
    rest0 = jnp.zeros((n, head_w - rot), F32)
    c = jnp.concatenate([cos, cos, rest1], axis=1)
    sm = jnp.concatenate([-sin, z_half, rest0], axis=1)
    sp = jnp.concatenate([z_half, sin, rest0], axis=1)
    rep = LANES // head_w
    return [jnp.tile(t, (1, rep)) for t in (c, sm, sp)]


def _ret_tables(pos, chunk_pos, chunk_len, log_gamma):
    half = RET_DK // 2
    inv_freq = jnp.exp(-math.log(RET_THETA) * jnp.arange(half, dtype=F32) / half)
    ang = pos.astype(F32)[:, None] * inv_freq[None, :]
    i = chunk_pos.astype(F32)[:, None]
    q_dec = jnp.exp(log_gamma[None, :] * (i + 1.0))
    k_dec = jnp.exp(log_gamma[None, :] * (chunk_len - 1.0 - i))
    return jnp.cos(ang), jnp.sin(ang), jnp.repeat(q_dec, LANES, axis=1), jnp.repeat(k_dec, LANES, axis=1)


PROJ_TN = 512


def _project(xn, w, pos, chunk_pos, chunk_len, log_gamma, *, tm, prompt):
    m = xn.shape[0]
    nrep = max(pos.shape[0] // tm, 1)
    wdt = BF16 if prompt else F32
    out = {"w_bf16": {}}

    def mm(name, a, b, epilogue, extras, outs, tn=PROJ_TN, valid_cols=None):
        n = -(-(OFFS[b] - OFFS[a]) // tn) * tn
        if prompt:
            *res, wb = _matmul_w32(xn, w, epilogue, extras, outs, tm=tm, tn=tn, n=n, col_off=OFFS[a],
                                   valid=valid_cols, name=name)
            out["w_bf16"][name] = wb
            return res
        return _matmul(xn, w[name], epilogue, extras, outs, tm=tm, tn=tn, b_t=True, name=name)

    def tab(t, width=LANES, by_col=False):
        if by_col:
            return (t, (tm, width), lambda i, j: (i % nrep, j))
        return (t, (tm, width), lambda i, j: (i % nrep, 0))

    att_t = [tab(t) for t in _rope_lane_tables(pos, ATT_HEAD_DIM, ATT_HEAD_DIM // 4, ROPE_THETA)]
    idx_t = [tab(t) for t in _rope_lane_tables(pos, IDX_DIM, IDX_DIM // 4, ROPE_THETA)]
    r_cos, r_sin, q_dec, k_dec = _ret_tables(pos, chunk_pos, chunk_len, log_gamma)
    tn = PROJ_TN
    nqb = m // Q_BLOCK

    if prompt:
        o = [((nqb, ATT_HEADS, Q_BLOCK, LANES), BF16, (tm // Q_BLOCK, tn // LANES, Q_BLOCK, LANES),
              lambda i, j: (i, j, 0, 0))]
    else:
        o = [_nat(m, ATT_Q_W, F32, tm, tn)]
    out["q"], = mm("proj_q", 0, 1, functools.partial(_ep_rope_lanes, half=16, blocked=prompt), att_t, o)
    o = [_nat(m, ATT_KV_W, F32, tm, tn)]
    if prompt:
        per_b = SEQ // tm
        o.append(((BATCH, ATT_KV_HEADS, SEQ, LANES), BF16, (None, tn // LANES, tm, LANES),
                  lambda i, j: (i // per_b, j, i % per_b, 0)))
    res = mm("proj_k", 1, 2, functools.partial(_ep_rope_lanes, half=16, blocked=False), att_t, o)
    out["k"] = res[0]
    if prompt:
        out["k_heads"] = res[1]
    o = [_nat(m, ATT_KV_W, F32, tm, tn)]
    if prompt:
        o.append(_nat(m, ATT_KV_W, BF16, tm, tn))
    res = mm("proj_v", 2, 3, _ep_plain, [], o)
    out["v"] = res[0]
    if prompt:
        out["v_bf"] = res[1]
    if prompt:
        o = [((nqb, IDX_Q_W // LANES, Q_BLOCK, LANES), F32, (tm // Q_BLOCK, tn // LANES, Q_BLOCK, LANES),
              lambda i, j: (i, j, 0, 0))]
    else:
        o = [_nat(m, IDX_Q_W, F32, tm, tn)]
    out["iq"], = mm("proj_iq", 3, 4, functools.partial(_ep_rope_lanes, half=8, blocked=prompt), idx_t, o)
    ikw_t = [tab(t) for t in _rope_lane_tables(pos, LANES, IDX_DIM // 4, ROPE_THETA)]
    out["ikw"], = mm("proj_ikw", 4, 6, functools.partial(_ep_rope_lanes, half=8, blocked=False), ikw_t,
                     [_nat(m, LANES, F32, tm, LANES)], tn=LANES, valid_cols=IDX_DIM + IDX_HEADS)
    for nm, grp, dec, scale in (("rq", 6, q_dec, 1.0), ("rk", 7, k_dec, RET_DK ** -0.5)):
        res = mm("proj_" + nm, grp, grp + 1, functools.partial(_ep_ret_qk, scale=scale),
                 [tab(r_cos), tab(r_sin), tab(dec, tn // 2, by_col=True)],
                 [_nat(m, RET_QK_W, wdt, tm, tn), _nat(m, RET_QK_W, wdt, tm, tn)])
        out[nm], out[nm + "d"] = res
    out["rv"], = mm("proj_rv", 8, 9, _ep_plain, [], [_nat(m, RET_V_W, wdt, tm, tn)])
    out["gates"], = mm("proj_gates", 9, 12, _ep_plain, [], [_nat(m, 3 * D_MODEL, F32, tm, tn)])
    return out


def _sortable_key(score):
    kb = lax.bitcast_convert_type(score, I32)
    kb = jnp.where(kb == INT_MIN, 0, kb)
    return jnp.where(kb < 0, kb ^ 0x7FFFFFFF, kb)


def _tile_reduce(x, op):
    tiles = [x[i * SUBLANES:(i + 1) * SUBLANES] for i in range(x.shape[0] // SUBLANES)]
    while len(tiles) > 1:
        nxt = [op(tiles[i], tiles[i + 1]) for i in range(0, len(tiles) - 1, 2)]
        if len(tiles) % 2:
            nxt.append(tiles[-1])
        tiles = nxt
    return tiles[0]


def _topk_mask(sc_ref, key_ref, bias_ref, q_pos, n_chunks):
    def rows(c):
        r0 = pl.multiple_of(c * KEY_CHUNK, KEY_CHUNK)
        return pl.ds(r0, KEY_CHUNK), r0 + lax.broadcasted_iota(I32, (KEY_CHUNK, LANES), 0)

    def build(c, carry):
        sl, row = rows(c)
        key_ref[sl, :] = jnp.where(row <= q_pos, _sortable_key(sc_ref[sl, :]), INT_MIN)
        return carry

    lax.fori_loop(0, n_chunks, build, 0)

    def count(pred):
        def body(c, acc):
            sl, row = rows(c)
            return acc + _tile_reduce(jnp.where(pred(key_ref[sl, :], row), 1.0, 0.0), jnp.add)

        part = lax.fori_loop(0, n_chunks, body, jnp.zeros((SUBLANES, LANES), F32))
        return jnp.sum(part, axis=0, keepdims=True)

    t0 = jnp.where(count(lambda k, row: k >= 0) >= TOPK, 0, INT_MIN).astype(I32)

    def search(i, t):
        cand = t | lax.shift_left(jnp.int32(1), jnp.int32(30) - i)
        return jnp.where(count(lambda k, row: k >= cand) >= TOPK, cand, t)

    thr = lax.fori_loop(0, 31, search, t0)
    n_eq = count(lambda k, row: (row <= q_pos) & (k == thr))
    need = TOPK - count(lambda k, row: k > thr)

    def write(c, carry):
        sl, row = rows(c)
        k = key_ref[sl, :]
        sel = jnp.where(k > thr, 0.0, jnp.where((row <= q_pos) & (k == thr), 0.0, NEG_BIG))
        bias_ref[sl, :] = sel.astype(bias_ref.dtype)
        return carry

    lax.fori_loop(0, n_chunks, write, 0)

    @pl.when(jnp.max(n_eq - need) > 0)
    def _():
        r_i = lax.broadcasted_iota(I32, (LANES, LANES), 0)
        c_i = lax.broadcasted_iota(I32, (LANES, LANES), 1)
        tri = jnp.where(c_i < r_i, 1.0, 0.0).astype(BF16)

        def chunk(c, off):
            r0 = pl.multiple_of(c * LANES, LANES)
            kc = key_ref[pl.ds(r0, LANES), :]
            rc = r0 + lax.broadcasted_iota(I32, (LANES, LANES), 0)
            e = (rc <= q_pos) & (kc == thr)
            ef = jnp.where(e, 1.0, 0.0)
            before = jnp.dot(tri, ef.astype(BF16), preferred_element_type=F32) + off
            keep = e & (before < need)
            sel = jnp.where(kc > thr, 0.0, jnp.where(keep, 0.0, NEG_BIG))
            bias_ref[pl.ds(r0, LANES), :] = sel.astype(bias_ref.dtype)
            return off + jnp.sum(ef, axis=0, keepdims=True)

        lax.fori_loop(0, n_chunks * (KEY_CHUNK // LANES), chunk, jnp.zeros((1, LANES), F32))


def _split_bf16(x):
    hi = x.astype(BF16).astype(F32)
    return hi, x - hi


def _dsa_prompt_kernel(iq_ref, ikw_all_ref, ikw_q_ref, q_ref, k_ref, vt_ref, *rest, n_cast):
    w32_refs, att_ref, wbf_refs = rest[:n_cast], rest[n_cast], rest[n_cast + 1:2 * n_cast + 1]
    lhs_ref, wt_ref, sc_ref, key_ref, bias_ref = rest[2 * n_cast + 1:]
    for src, dst in zip(w32_refs, wbf_refs):
        dst[...] = src[...].astype(BF16)
    qb = pl.program_id(1)
    n_ck = (qb * Q_BLOCK + Q_BLOCK + KEY_CHUNK - 1) // KEY_CHUNK
    lane = lax.broadcasted_iota(I32, (1, LANES), 1)
    low = lane < IDX_DIM

    def chunk_rows(c):
        return pl.ds(pl.multiple_of(c * KEY_CHUNK, KEY_CHUNK), KEY_CHUNK)

    @pl.when(qb == 0)
    def _():
        x = jnp.where(low, ikw_all_ref[...], 0.0)
        hi, lo = _split_bf16(x)
        lhs_ref[:, :LANES] = (hi + pltpu.roll(lo, IDX_DIM, 1)).astype(BF16)
        lhs_ref[:, LANES:] = hi.astype(BF16)

    wt_ref[...] = ikw_q_ref[...].T * IDX_SCALE
    sc_ref[...] = jnp.zeros_like(sc_ref)

    def pairs_body(pq, carry):
        rhs, wts = [], []
        for u in range(PAIRS_PER_STEP):
            p = pq * PAIRS_PER_STEP + u
            hi, lo = _split_bf16(iq_ref[p])
            rhi = pltpu.roll(hi, IDX_DIM, 1)
            rlo = pltpu.roll(lo, IDX_DIM, 1)
            ra = jnp.concatenate([jnp.where(low, hi, rhi), jnp.where(low, lo, 0.0)], axis=1)
            rb = jnp.concatenate([jnp.where(low, rhi, hi), jnp.where(low, rlo, 0.0)], axis=1)
            rhs.append(jnp.concatenate([ra, rb], axis=0).astype(BF16))
            wts.append((wt_ref[pl.ds(IDX_DIM + 2 * p, 1), :], wt_ref[pl.ds(IDX_DIM + 2 * p + 1, 1), :]))

        def ck_body(c, inner):
            sl = chunk_rows(c)
            lhs = lhs_ref[sl, :]
            acc = sc_ref[sl, :]
            for rhs_t, (wa, wb) in zip(rhs, wts):
                d = lax.dot_general(lhs, rhs_t, _NT, preferred_element_type=F32)
                acc = acc + wa * jnp.maximum(d[:, :LANES], 0.0) + wb * jnp.maximum(d[:, LANES:], 0.0)
            sc_ref[sl, :] = acc
            return inner

        lax.fori_loop(0, n_ck, ck_body, 0)
        return carry

    lax.fori_loop(0, IDX_HEADS // 2 // PAIRS_PER_STEP, pairs_body, 0)

    _topk_mask(sc_ref, key_ref, bias_ref, qb * Q_BLOCK + lane, n_ck)

    c_exp = ATT_SCALE * math.log2(math.e)

    r_i = lax.broadcasted_iota(I32, (2 * Q_BLOCK, LANES), 0)
    c_i = lax.broadcasted_iota(I32, (2 * Q_BLOCK, LANES), 1)
    onehot_q = jnp.where((r_i % Q_BLOCK) == c_i, 1.0, 0.0).astype(BF16)
    n_acc = vt_ref.shape[2]

    def groups_body(gq, carry):
        gs = [gq * GROUPS_PER_STEP + u for u in range(GROUPS_PER_STEP)]
        qqs = [jnp.concatenate([jnp.concatenate([q_ref[2 * g], q_ref[2 * g + 1]], axis=0), onehot_q], axis=1)
               for g in gs]

        def ck_body(c, state):
            sl = chunk_rows(c)
            mask = bias_ref[sl, :]
            lgs = [lax.dot_general(jnp.concatenate([k_ref[g, sl, :], mask], axis=1), qq, _NT,
                                   preferred_element_type=F32)
                   for g, qq in zip(gs, qqs)]
            soft = []
            for lg, (m_old, acc) in zip(lgs, state):
                m_new = jnp.maximum(m_old, jnp.max(_tile_reduce(lg, jnp.maximum), axis=0, keepdims=True))
                alpha = jnp.exp2((m_old - m_new) * c_exp)
                soft.append((m_new, alpha, jnp.exp2((lg - m_new) * c_exp).astype(BF16)))
            return tuple((m_new, alpha * acc + jnp.dot(vt_ref[g, c], pt, preferred_element_type=F32))
                         for g, (m_new, alpha, pt), (_, acc) in zip(gs, soft, state))

        init = (jnp.full((1, 2 * LANES), NEG_BIG, F32), jnp.zeros((n_acc, 2 * LANES), F32))
        final = lax.fori_loop(0, n_ck, ck_body, (init,) * GROUPS_PER_STEP)
        for g, (_, acc) in zip(gs, final):
            o = acc[:ATT_HEAD_DIM] / acc[ATT_HEAD_DIM:ATT_HEAD_DIM + 1]
            for hh in range(2):
                att_ref[2 * g + hh] = o[:, hh * LANES:(hh + 1) * LANES].T.astype(att_ref.dtype)
        return carry

    lax.fori_loop(0, ATT_KV_HEADS // GROUPS_PER_STEP, groups_body, 0)


def _cast_slab_specs(w32, n_steps, step_of):
    rows, cols = w32.shape
    slab = rows // n_steps
    assert slab * n_steps == rows and slab % (2 * SUBLANES) == 0
    spec = pl.BlockSpec((slab, cols), lambda *idx: (step_of(*idx), 0))
    return spec, jax.ShapeDtypeStruct((rows, cols), BF16)


def _dsa_prompt(iq, ikw, q, k_heads, vt, casts):
    nqb = SEQ // Q_BLOCK
    n_pair = IDX_Q_W // LANES
    cast_specs, cast_shapes = zip(*[_cast_slab_specs(w32, BATCH * nqb, lambda b, i: b * nqb + i) for w32 in casts])
    return pl.pallas_call(
        functools.partial(_dsa_prompt_kernel, n_cast=len(casts)),
        grid=(BATCH, nqb),
        in_specs=[
            pl.BlockSpec((None, n_pair, Q_BLOCK, LANES), lambda b, i: (b * nqb + i, 0, 0, 0)),
            pl.BlockSpec((SEQ, LANES), lambda b, i: (b, 0)),
            pl.BlockSpec((Q_BLOCK, LANES), lambda b, i: (b * nqb + i, 0)),
            pl.BlockSpec((None, ATT_HEADS, Q_BLOCK, LANES), lambda b, i: (b * nqb + i, 0, 0, 0)),
            pl.BlockSpec((None, ATT_KV_HEADS, SEQ, LANES), lambda b, i: (b, 0, 0, 0)),
            pl.BlockSpec((None, ATT_KV_HEADS, SEQ // KEY_CHUNK, ATT_HEAD_DIM + VT_ONES, KEY_CHUNK),
                         lambda b, i: (b, 0, 0, 0, 0)),
            *cast_specs,
        ],
        out_specs=[pl.BlockSpec((None, ATT_HEADS, Q_BLOCK, LANES), lambda b, i: (b * nqb + i, 0, 0, 0)), *cast_specs],
        out_shape=[jax.ShapeDtypeStruct((BATCH * nqb, ATT_HEADS, Q_BLOCK, LANES), BF16), *cast_shapes],
        scratch_shapes=[
            pltpu.VMEM((SEQ, 2 * LANES), BF16),
            pltpu.VMEM((LANES, LANES), F32),
            pltpu.VMEM((SEQ, LANES), F32),
            pltpu.VMEM((SEQ, LANES), I32),
            pltpu.VMEM((SEQ, LANES), BF16),
        ],
        compiler_params=_params(("parallel", "arbitrary")),
        name="dsa_prompt",
    )(iq, ikw, ikw, q, k_heads, vt, *casts)


def _group_norm_gate(o, gn_w, rg):
    of = o * lax.rsqrt(jnp.mean(o * o, axis=-1, keepdims=True) + NORM_EPS) * gn_w
    return of * (rg * _sigmoid(rg))


def _ret_prompt_kernel(q_ref, qd_ref, k_ref, kd_ref, v_ref, rg_ref, decay_ref, cdec_ref, gnw_ref, *rest, n_cast):
    w32_refs, (ret_ref, s_ref), wbf_refs = rest[:n_cast], rest[n_cast:n_cast + 2], rest[n_cast + 2:]
    for src, dst in zip(w32_refs, wbf_refs):
        dst[...] = src[...].astype(BF16)
    c = pl.program_id(1)

    @pl.when(c == 0)
    def _():
        s_ref[...] = jnp.zeros_like(s_ref)

    def qk(h):
        return slice(h * RET_DK, (h + 1) * RET_DK)

    def vv(h):
        return slice(h * RET_DV, (h + 1) * RET_DV)

    heads = range(RET_HEADS)
    scores = [lax.dot_general(q_ref[:, qk(h)], k_ref[:, qk(h)], _NT, preferred_element_type=F32) for h in heads]
    cross = [jnp.dot(qd_ref[:, qk(h)], s_ref[h].astype(BF16), preferred_element_type=F32) for h in heads]
    kvs = [lax.dot_general(kd_ref[:, qk(h)], v_ref[:, vv(h)], (((0,), (0,)), ((), ())),
                           preferred_element_type=F32) for h in heads]
    for h in heads:
        s_ref[h] = cdec_ref[h] * s_ref[h] + kvs[h]
    for h in heads:
        o = jnp.dot((scores[h] * decay_ref[h]).astype(BF16), v_ref[:, vv(h)], preferred_element_type=F32) + cross[h]
        ret_ref[:, vv(h)] = _group_norm_gate(o, gnw_ref[:, vv(h)], rg_ref[:, vv(h)]).astype(ret_ref.dtype)


def _ret_prompt(pr, decay, cdec, gn_w, casts):
    nc = SEQ // RET_CHUNK
    qk = pl.BlockSpec((RET_CHUNK, RET_QK_W), lambda b, c: (b * nc + c, 0))
    vv = pl.BlockSpec((RET_CHUNK, RET_V_W), lambda b, c: (b * nc + c, 0))
    cast_specs, cast_shapes = zip(*[_cast_slab_specs(w32, BATCH * nc, lambda b, c: b * nc + c) for w32 in casts])
    return pl.pallas_call(
        functools.partial(_ret_prompt_kernel, n_cast=len(casts)),
        grid=(BATCH, nc),
        in_specs=[qk, qk, qk, qk, vv, vv,
                  pl.BlockSpec((RET_HEADS, RET_CHUNK, RET_CHUNK), lambda b, c: (0, 0, 0)),
                  pl.BlockSpec((RET_HEADS, 1, RET_DV), lambda b, c: (0, 0, 0)),
                  pl.BlockSpec((1, RET_V_W), lambda b, c: (0, 0)), *cast_specs],
        out_specs=[vv, pl.BlockSpec((None, RET_HEADS, RET_DK, RET_DV), lambda b, c: (b, 0, 0, 0)), *cast_specs],
        out_shape=[jax.ShapeDtypeStruct((BATCH * SEQ, RET_V_W), BF16),
                   jax.ShapeDtypeStruct((BATCH, RET_HEADS, RET_DK, RET_DV), F32), *cast_shapes],
        compiler_params=_params(("parallel", "arbitrary")),
        name="retention_prompt",
    )(pr["rq"], pr["rqd"], pr["rk"], pr["rkd"], pr["rv"], pr["gates"], decay, cdec, gn_w.reshape(1, RET_V_W), *casts)


def _ret_sample_kernel(q_ref, qd_ref, kt_ref, k_ref, v_ref, rg_ref, cdec_ref, gnw_ref, s_ref, ret_ref, so_ref):
    b = pl.program_id(0)
    nb = kt_ref.shape[1]
    onehot = lax.broadcasted_iota(I32, (1, nb), 1) == b
    for h in range(RET_HEADS):
        qk_sl = slice(h * RET_DK, (h + 1) * RET_DK)
        vv = slice(h * RET_DV, (h + 1) * RET_DV)
        s = s_ref[h]
        v = v_ref[pl.ds(b, 1), vv]
        q = q_ref[pl.ds(b, 1), qk_sl]
        k = k_ref[pl.ds(b, 1), qk_sl]
        qk = jnp.sum(q.astype(BF16).astype(F32) * k.astype(BF16).astype(F32), axis=-1, keepdims=True)
        qd8 = jnp.broadcast_to(qd_ref[pl.ds(b, 1), qk_sl], (16, RET_DK)).astype(BF16)
        o = qk.astype(BF16).astype(F32) * v.astype(BF16).astype(F32) \
            + jnp.dot(qd8, s.astype(BF16), preferred_element_type=F32)[0:1]
        k_col = jnp.sum(jnp.where(onehot, kt_ref[qk_sl, :], 0.0), axis=1, keepdims=True)
        so_ref[h] = cdec_ref[h] * s + k_col * v
        ret_ref[:, vv] = _group_norm_gate(o, gnw_ref[:, vv], rg_ref[pl.ds(b, 1), vv])


def _ret_sample(ps, state, cdec, gn_w):
    nb = DEC_BATCH
    qk = pl.BlockSpec((nb, RET_QK_W), lambda b: (0, 0))
    vv = pl.BlockSpec((nb, RET_V_W), lambda b: (0, 0))
    st = pl.BlockSpec((None, RET_HEADS, RET_DK, RET_DV), lambda b: (b, 0, 0, 0))
    return pl.pallas_call(
        _ret_sample_kernel,
        grid=(nb,),
        in_specs=[qk, qk, pl.BlockSpec((RET_QK_W, nb), lambda b: (0, 0)), qk, vv, vv,
                  pl.BlockSpec((RET_HEADS, 1, RET_DV), lambda b: (0, 0, 0)),
                  pl.BlockSpec((1, RET_V_W), lambda b: (0, 0)), st],
        out_specs=[pl.BlockSpec((None, 1, RET_V_W), lambda b: (b, 0, 0)), st],
        out_shape=[jax.ShapeDtypeStruct((nb, 1, RET_V_W), F32),
                   jax.ShapeDtypeStruct((nb, RET_HEADS, RET_DK, RET_DV), F32)],
        compiler_params=_params(("arbitrary",)),
        name="retention_sample",
    )(ps["rq"], ps["rqd"], ps["rkd"].T, ps["rk"], ps["rv"], ps["gates"], cdec, gn_w.reshape(1, RET_V_W), state)


PAGES_PER_STEP = 8
IDX_PAGES_PER_STEP = 32


def _idx_sample_kernel(pt_ref, iq_ref, w_ref, new_ref, *rest):
    pages = rest[:IDX_PAGES_PER_STEP]
    sc_ref, scn_ref = rest[IDX_PAGES_PER_STEP], rest[IDX_PAGES_PER_STEP + 1]
    j = pl.program_id(1)
    hi, lo = _split_bf16(iq_ref[...])
    lhs = jnp.concatenate([hi + pltpu.roll(hi, IDX_DIM, 1), lo], axis=1).astype(BF16)
    w = w_ref[...] * IDX_SCALE

    def page_scores(pages_t):
        xt = jnp.concatenate(pages_t, axis=1) if len(pages_t) > 1 else pages_t[0]
        khi, klo = _split_bf16(xt)
        rhs = jnp.concatenate([khi, klo, khi, jnp.zeros_like(khi)], axis=0).astype(BF16)
        d = jnp.dot(lhs, rhs, preferred_element_type=F32)
        return jnp.sum(w * jnp.maximum(d, 0.0), axis=0, keepdims=True)

    for i in range(0, IDX_PAGES_PER_STEP, 2):
        sc_ref[:, i * PAGE_SIZE:(i + 2) * PAGE_SIZE] = page_scores([pages[i][...], pages[i + 1][...]])

    @pl.when(j == pl.num_programs(1) - 1)
    def _():
        scn_ref[...] = page_scores([new_ref[...]])


def _idx_sample(page_table, iq3, w3, new_pages, cache_idx):
    nsteps = N_PAGES // IDX_PAGES_PER_STEP

    def page_spec(i):
        return pl.BlockSpec((None, IDX_DIM, PAGE_SIZE),
                            lambda b, j, pt: (pt[b, j * IDX_PAGES_PER_STEP + i], 0, 0))

    grid_spec = pltpu.PrefetchScalarGridSpec(
        num_scalar_prefetch=1,
        grid=(DEC_BATCH, nsteps),
        in_specs=[pl.BlockSpec((None, IDX_HEADS, LANES), lambda b, j, pt: (b, 0, 0)),
                  pl.BlockSpec((None, IDX_HEADS, 1), lambda b, j, pt: (b, 0, 0)),
                  pl.BlockSpec((None, IDX_DIM, PAGE_SIZE), lambda b, j, pt: (b, 0, 0))]
                 + [page_spec(i) for i in range(IDX_PAGES_PER_STEP)],
        out_specs=[pl.BlockSpec((None, 1, IDX_PAGES_PER_STEP * PAGE_SIZE), lambda b, j, pt: (b, 0, j)),
                   pl.BlockSpec((None, 1, PAGE_SIZE), lambda b, j, pt: (b, 0, 0))],
    )
    return pl.pallas_call(
        _idx_sample_kernel,
        grid_spec=grid_spec,
        out_shape=[jax.ShapeDtypeStruct((DEC_BATCH, 1, PAST_LEN), F32),
                   jax.ShapeDtypeStruct((DEC_BATCH, 1, PAGE_SIZE), F32)],
        compiler_params=_params(("parallel", "arbitrary")),
        name="indexer_sample",
    )(page_table, iq3, w3, new_pages, *([cache_idx] * IDX_PAGES_PER_STEP))


def _select_sample_kernel(sc_ref, bias_ref, key_ref):
    _topk_mask(sc_ref, key_ref, bias_ref, jnp.full((1, LANES), PAST_LEN, I32), sc_ref.shape[0] // KEY_CHUNK)


def _select_sample(scores_t):
    return pl.pallas_call(
        _select_sample_kernel,
        out_shape=jax.ShapeDtypeStruct(scores_t.shape, F32),
        scratch_shapes=[pltpu.VMEM(scores_t.shape, I32)],
        compiler_params=pltpu.CompilerParams(vmem_limit_bytes=VMEM_LIMIT),
        name="select_sample",
    )(scores_t)


PAGE_ROWS = PAGE_SIZE * ATT_KV_HEADS


def _attn_sample_step(j, n_steps, q_ref, bias_ref, biasn_ref, kn_ref, vn_ref, kp, vp, o_ref, m_ref, l_ref, acc_ref):
    col = lax.broadcasted_iota(I32, (ATT_HEADS, PAGE_ROWS), 1)
    head = lax.broadcasted_iota(I32, (ATT_HEADS, PAGE_ROWS), 0)
    own = (col % ATT_KV_HEADS) == (head // (ATT_HEADS // ATT_KV_HEADS))
    c_exp = ATT_SCALE * math.log2(math.e)
    q = q_ref[...].astype(BF16)

    @pl.when(j == 0)
    def _():
        m_ref[...] = jnp.full_like(m_ref, NEG_BIG)
        l_ref[...] = jnp.zeros_like(l_ref)
        acc_ref[...] = jnp.zeros_like(acc_ref)

    logits = []
    for i in range(PAGES_PER_STEP):
        lg = lax.dot_general(q, kp[i][...].astype(BF16), _NT, preferred_element_type=F32)
        logits.append(jnp.where(own, lg + bias_ref[:, i * PAGE_ROWS:(i + 1) * PAGE_ROWS], NEG_BIG))
    m_old = m_ref[...]
    m_new = m_old
    for lg in logits:
        m_new = jnp.maximum(m_new, jnp.max(lg, axis=1, keepdims=True))
    alpha = jnp.exp2((m_old - m_new) * c_exp)
    l_new = alpha * l_ref[...]
    acc = alpha * acc_ref[...]
    for i in range(PAGES_PER_STEP):
        p = jnp.exp2((logits[i] - m_new) * c_exp)
        l_new = l_new + jnp.sum(p, axis=1, keepdims=True)
        acc = acc + jnp.dot(p.astype(BF16), vp[i][...].astype(BF16), preferred_element_type=F32)
    m_ref[...] = m_new
    l_ref[...] = l_new
    acc_ref[...] = acc

    @pl.when(j == n_steps - 1)
    def _():
        kn = kn_ref[...].astype(BF16).astype(F32)
        vn = vn_ref[...].astype(BF16).astype(F32)
        lgn = jnp.sum(q.astype(F32) * kn, axis=1, keepdims=True) + biasn_ref[:, 0:1]
        m_f = jnp.maximum(m_new, lgn)
        a = jnp.exp2((m_new - m_f) * c_exp)
        pn = jnp.exp2((lgn - m_f) * c_exp)
        o_ref[...] = (a * acc + pn.astype(BF16).astype(F32) * vn) / (a * l_new + pn)


MLP_UP_TM, MLP_UP_TN = 1024, 512


def _mlp_up_attn_kernel(pt_ref, hn_ref, wup_ref, q_ref, bias_ref, biasn_ref, kn_ref, vn_ref, *rest):
    kp = rest[:PAGES_PER_STEP]
    vp = rest[PAGES_PER_STEP:2 * PAGES_PER_STEP]
    u_ref, o_ref = rest[2 * PAGES_PER_STEP], rest[2 * PAGES_PER_STEP + 1]
    m_ref, l_ref, acc_ref = rest[2 * PAGES_PER_STEP + 2:]
    _ep_relu2(jnp.dot(hn_ref[...], wup_ref[...], preferred_element_type=F32), (), (u_ref,))
    n_pg = N_PAGES // PAGES_PER_STEP
    _attn_sample_step(pl.program_id(1) % n_pg, n_pg, q_ref, bias_ref, biasn_ref, kn_ref, vn_ref, kp, vp,
                      o_ref, m_ref, l_ref, acc_ref)


def _mlp_up_with_sample_attn(hn, w_up, page_table, q3, bias_rows, bias_new, k_new, v_new, cache_k, cache_v):
    m, kd = hn.shape
    n_pg = N_PAGES // PAGES_PER_STEP
    n_row, n_col = m // MLP_UP_TM, D_FF // MLP_UP_TN
    assert n_row * n_col == DEC_BATCH * n_pg and n_col % n_pg == 0

    def row(i, j):
        return (i * n_col + j) // n_pg

    def page_spec(p):
        return pl.BlockSpec((None, PAGE_ROWS, ATT_HEAD_DIM),
                            lambda i, j, pt: (pt[row(i, j), (j % n_pg) * PAGES_PER_STEP + p], 0, 0))

    head_rows = pl.BlockSpec((None, ATT_HEADS, ATT_HEAD_DIM), lambda i, j, pt: (row(i, j), 0, 0))
    grid_spec = pltpu.PrefetchScalarGridSpec(
        num_scalar_prefetch=1,
        grid=(n_row, n_col),
        in_specs=[pl.BlockSpec((MLP_UP_TM, kd), lambda i, j, pt: (i, 0)),
                  pl.BlockSpec((kd, MLP_UP_TN), lambda i, j, pt: (0, j)),
                  head_rows,
                  pl.BlockSpec((None, 1, PAGES_PER_STEP * PAGE_ROWS), lambda i, j, pt: (row(i, j), 0, j % n_pg)),
                  pl.BlockSpec((None, 1, PAGE_SIZE), lambda i, j, pt: (row(i, j), 0, 0)),
                  head_rows, head_rows]
                 + [page_spec(p) for p in range(PAGES_PER_STEP)] * 2,
        out_specs=[pl.BlockSpec((MLP_UP_TM, MLP_UP_TN), lambda i, j, pt: (i, j)), head_rows],
        scratch_shapes=[pltpu.VMEM((ATT_HEADS, 1), F32), pltpu.VMEM((ATT_HEADS, 1), F32),
                        pltpu.VMEM((ATT_HEADS, ATT_HEAD_DIM), F32)],
    )
    return pl.pallas_call(
        _mlp_up_attn_kernel,
        grid_spec=grid_spec,
        out_shape=[jax.ShapeDtypeStruct((m, D_FF), BF16),
                   jax.ShapeDtypeStruct((DEC_BATCH, ATT_HEADS, ATT_HEAD_DIM), F32)],
        compiler_params=_params(("arbitrary", "arbitrary")),
        name="mlp_up_attention_sample",
    )(page_table, hn, w_up, q3, bias_rows, bias_new, k_new, v_new,
      *([cache_k] * PAGES_PER_STEP), *([cache_v] * PAGES_PER_STEP))


def _finish(x, att, ret, gates, w, *, tm, tr, sample_attn=None):
    m = x.shape[0]
    mg = _merge(att, ret, w["att_proj"], w["ret_proj"], gates, tm, 512)
    y, = _matmul(mg, w["out"], _ep_plain, [], [_nat(m, D_MODEL, F32, tm, 512)], tm=tm, tn=512, name="w_out")
    h, hn = _post_attn(x, y, w["n_attn_post"], w["n_mlp_pre"], tr)
    if sample_attn is None:
        u, = _matmul(hn, w["mlp_up"], _ep_relu2, [], [_nat(m, D_FF, BF16, tm, 512)], tm=tm, tn=512, name="mlp_up")
        att_s = None
    else:
        u, att_s = _mlp_up_with_sample_attn(hn, w["mlp_up"], *sample_attn)
    d, = _matmul(u, w["mlp_down"], _ep_plain, [], [_nat(m, D_MODEL, F32, tm, 512)], tm=tm, tn=512, tk=4096,
                 name="mlp_down")
    out = _post_mlp(h, d, w["n_mlp_post"], tr)
    return out if sample_attn is None else (out, att_s)


def kernel(x_prompt, x_sample, cache_k, cache_v, cache_idx_k, state_ret, page_table, norm_attn_pre,
           norm_attn_post, w_in, ret_gn_w, w_att_proj, w_ret_proj, w_out, norm_mlp_pre, w_mlp_up,
           w_mlp_down, norm_mlp_post):
    log_gamma = jnp.log1p(-jnp.exp2(-5.0 - jnp.arange(RET_HEADS, dtype=F32)))
    w_in0 = w_in[0].T
    w = {"n_attn_post": norm_attn_post[0], "n_mlp_pre": norm_mlp_pre[0], "n_mlp_post": norm_mlp_post[0]}
    gn_w = ret_gn_w[0]

    m_p = BATCH * SEQ
    xp = x_prompt.reshape(m_p, D_MODEL)
    xn = _rmsnorm_cast(xp, norm_attn_pre[0], 256)
    pos_p = jnp.arange(SEQ, dtype=I32)
    pr = _project(xn, w_in0, pos_p, pos_p % RET_CHUNK, float(RET_CHUNK), log_gamma, tm=1024, prompt=True)
    vt = pr["v_bf"].reshape(BATCH, SEQ // KEY_CHUNK, KEY_CHUNK, ATT_KV_HEADS, ATT_HEAD_DIM).transpose(0, 3, 1, 4, 2)
    vt = jnp.concatenate([vt, jnp.ones(vt.shape[:3] + (VT_ONES, KEY_CHUNK), BF16)], axis=3)
    att4, w["mlp_down"], w["out"], w["ret_proj"], w["att_proj"] = _dsa_prompt(
        pr["iq"], pr["ikw"], pr["q"], pr["k_heads"], vt, (w_mlp_down[0], w_out[0], w_ret_proj[0], w_att_proj[0]))
    att = att4.reshape(m_p // Q_BLOCK, ATT_HEADS, Q_BLOCK, ATT_HEAD_DIM).transpose(0, 2, 1, 3).reshape(m_p, ATT_Q_W)
    ci = jnp.arange(RET_CHUNK, dtype=F32)
    diff = ci[:, None] - ci[None, :]
    decay = jnp.where(diff >= 0, jnp.exp(log_gamma[:, None, None] * jnp.maximum(diff, 0.0)), 0.0)
    cdec_p = jnp.broadcast_to(jnp.exp(log_gamma * RET_CHUNK)[:, None, None], (RET_HEADS, 1, RET_DV))
    ret, s_prompt, w["mlp_up"] = _ret_prompt(pr, decay, cdec_p, gn_w, (w_mlp_up[0],))

    nb = DEC_BATCH
    xs = x_sample.reshape(nb, D_MODEL)
    xns = _rmsnorm_cast(xs, norm_attn_pre[0], nb)
    pos_s = jnp.full((nb,), PAST_LEN, I32)
    ps = _project(xns, pr["w_bf16"], pos_s, jnp.zeros((nb,), I32), 1.0, log_gamma, tm=nb, prompt=False)
    ik_new = ps["ikw"][:, :IDX_DIM]
    iq3 = jnp.pad(ps["iq"].reshape(nb, IDX_HEADS, IDX_DIM), ((0, 0), (0, 0), (0, LANES - IDX_DIM)))
    w3 = ps["ikw"][:, IDX_DIM:IDX_DIM + IDX_HEADS].reshape(nb, IDX_HEADS, 1)
    new_pages = jnp.pad(ik_new[:, :, None], ((0, 0), (0, 0), (0, PAGE_SIZE - 1)))
    sc_past, sc_new = _idx_sample(page_table, iq3, w3, new_pages, cache_idx_k[0].transpose(0, 2, 1))
    scores = jnp.concatenate([sc_past.reshape(nb, PAST_LEN), sc_new.reshape(nb, PAGE_SIZE)], axis=1)
    n_rows = -(-(PAST_LEN + PAGE_SIZE) // KEY_CHUNK) * KEY_CHUNK
    scores_t = jnp.pad(scores.T, ((0, n_rows - PAST_LEN - PAGE_SIZE), (0, LANES - nb)))
    bias = _select_sample(scores_t)[:PAST_LEN + PAGE_SIZE, :nb].T
    n_phys = cache_k.shape[1]
    group = ATT_HEADS // ATT_KV_HEADS
    sample_attn = (page_table, ps["q"].reshape(nb, ATT_HEADS, ATT_HEAD_DIM),
                   jnp.repeat(bias[:, :PAST_LEN], ATT_KV_HEADS, axis=1).reshape(nb, 1, N_PAGES * PAGE_ROWS),
                   bias[:, PAST_LEN:].reshape(nb, 1, PAGE_SIZE),
                   jnp.repeat(ps["k"].reshape(nb, ATT_KV_HEADS, ATT_HEAD_DIM), group, axis=1),
                   jnp.repeat(ps["v"].reshape(nb, ATT_KV_HEADS, ATT_HEAD_DIM), group, axis=1),
                   cache_k[0].reshape(n_phys, PAGE_ROWS, ATT_HEAD_DIM),
                   cache_v[0].reshape(n_phys, PAGE_ROWS, ATT_HEAD_DIM))

    y_prompt, att_s = _finish(xp, att, ret, pr["gates"], w, tm=1024, tr=256, sample_attn=sample_attn)
    y_prompt = y_prompt.reshape(BATCH, SEQ, D_MODEL)
    cdec_s = jnp.broadcast_to(jnp.exp(log_gamma)[:, None, None], (RET_HEADS, 1, RET_DV))
    ret_s, s_sample = _ret_sample(ps, state_ret[0], cdec_s, gn_w)
    y_sample = _finish(xs, att_s.reshape(nb, ATT_Q_W).astype(BF16), ret_s.reshape(nb, RET_V_W).astype(BF16),
                       ps["gates"], w, tm=nb, tr=nb).reshape(nb, 1, D_MODEL)

    return (y_prompt, y_sample,
            pr["k"].reshape(1, BATCH, SEQ, ATT_KV_HEADS, ATT_HEAD_DIM),
            pr["v"].reshape(1, BATCH, SEQ, ATT_KV_HEADS, ATT_HEAD_DIM),
            pr["ikw"][:, :IDX_DIM].reshape(1, BATCH, SEQ, IDX_DIM),
            s_prompt[None],
            ps["k"].reshape(1, nb, 1, ATT_KV_HEADS, ATT_HEAD_DIM),
            ps["v"].reshape(1, nb, 1, ATT_KV_HEADS, ATT_HEAD_DIM),
            ik_new.reshape(1, nb, 1, IDX_DIM),
            s_sample[None])
```

```python
import functools
import math

import jax
import jax.numpy as jnp
import numpy as np
from jax import lax
from jax.experimental import pallas as pl
from jax.experimental.pallas import tpu as pltpu

F32 = jnp.float32
BF16 = jnp.bfloat16
I32 = jnp.int32

D_MODEL = 4096
BATCH = 4
SEQ = 2048
DEC_BATCH = 32
PAST_LEN = 8192
PAGE_SIZE = 128
N_PAGES = PAST_LEN // PAGE_SIZE
ATT_HEADS = 16
ATT_KV_HEADS = 8
ATT_HEAD_DIM = 128
ROPE_THETA = 500000.0
IDX_HEADS = 32
IDX_DIM = 64
TOPK = 256
RET_HEADS = 8
RET_DK = 256
RET_DV = 512
RET_THETA = 10000.0
RET_CHUNK = 128
D_FF = 4 * D_MODEL
NORM_EPS = 1e-6

ATT_Q_W = ATT_HEADS * ATT_HEAD_DIM
ATT_KV_W = ATT_KV_HEADS * ATT_HEAD_DIM
IDX_Q_W = IDX_HEADS * IDX_DIM
RET_QK_W = RET_HEADS * RET_DK
RET_V_W = RET_HEADS * RET_DV
SPLITS = (ATT_Q_W, ATT_KV_W, ATT_KV_W, IDX_Q_W, IDX_DIM, IDX_HEADS,
          RET_QK_W, RET_QK_W, RET_V_W, RET_V_W, D_MODEL, D_MODEL)
OFFS = tuple(int(v) for v in np.concatenate([[0], np.cumsum(SPLITS)]))

LANES = 128
SUBLANES = 8
Q_BLOCK = 128
KEY_CHUNK = 512
PAIRS_PER_STEP = 16
GROUPS_PER_STEP = 8
VT_ONES = 16
VMEM_LIMIT = 56 * 1024 * 1024

INT_MIN = -2 ** 31
NEG_BIG = -1e30
ATT_SCALE = ATT_HEAD_DIM ** -0.5
IDX_SCALE = (IDX_DIM ** -0.5) * (IDX_HEADS ** -0.5)

_NT = (((1,), (1,)), ((), ()))


def _params(sem):
    return pltpu.CompilerParams(dimension_semantics=sem, vmem_limit_bytes=VMEM_LIMIT)


def _sigmoid(x):
    return 1.0 / (1.0 + jnp.exp(-x))


def _rmsnorm_cast_kernel(x_ref, w_ref, o_ref):
    x = x_ref[...]
    y = x * lax.rsqrt(jnp.mean(x * x, axis=-1, keepdims=True) + NORM_EPS)
    o_ref[...] = (y * w_ref[...]).astype(o_ref.dtype)


def _rmsnorm_cast(x, w, tr):
    m, d = x.shape
    return pl.pallas_call(
        _rmsnorm_cast_kernel,
        grid=(m // tr,),
        in_specs=[pl.BlockSpec((tr, d), lambda i: (i, 0)), pl.BlockSpec((1, d), lambda i: (0, 0))],
        out_specs=pl.BlockSpec((tr, d), lambda i: (i, 0)),
        out_shape=jax.ShapeDtypeStruct((m, d), BF16),
        compiler_params=_params(("parallel",)),
        name="rmsnorm_cast",
    )(x, w.reshape(1, d))


def _post_attn_kernel(x_ref, y_ref, w1_ref, w2_ref, h_ref, hn_ref):
    y = y_ref[...]
    yn = y * lax.rsqrt(jnp.mean(y * y, axis=-1, keepdims=True) + NORM_EPS) * w1_ref[...]
    h = x_ref[...] + yn
    h_ref[...] = h
    hn = h * lax.rsqrt(jnp.mean(h * h, axis=-1, keepdims=True) + NORM_EPS) * w2_ref[...]
    hn_ref[...] = hn.astype(hn_ref.dtype)


def _post_attn(x, y, w1, w2, tr):
    m, d = x.shape
    row = pl.BlockSpec((tr, d), lambda i: (i, 0))
    vec = pl.BlockSpec((1, d), lambda i: (0, 0))
    return pl.pallas_call(
        _post_attn_kernel,
        grid=(m // tr,),
        in_specs=[row, row, vec, vec],
        out_specs=[row, row],
        out_shape=[jax.ShapeDtypeStruct((m, d), F32), jax.ShapeDtypeStruct((m, d), BF16)],
        compiler_params=_params(("parallel",)),
        name="post_attn_norm",
    )(x, y, w1.reshape(1, d), w2.reshape(1, d))


def _post_mlp_kernel(h_ref, d_ref, w_ref, o_ref):
    d = d_ref[...]
    dn = d * lax.rsqrt(jnp.mean(d * d, axis=-1, keepdims=True) + NORM_EPS) * w_ref[...]
    o_ref[...] = h_ref[...] + dn


def _post_mlp(h, d, w, tr):
    m, dm = h.shape
    row = pl.BlockSpec((tr, dm), lambda i: (i, 0))
    return pl.pallas_call(
        _post_mlp_kernel,
        grid=(m // tr,),
        in_specs=[row, row, pl.BlockSpec((1, dm), lambda i: (0, 0))],
        out_specs=row,
        out_shape=jax.ShapeDtypeStruct((m, dm), F32),
        compiler_params=_params(("parallel",)),
        name="post_mlp_norm",
    )(h, d, w.reshape(1, dm))


def _dot(a, b, b_t):
    if b_t:
        return lax.dot_general(a, b, _NT, preferred_element_type=F32)
    return jnp.dot(a, b, preferred_element_type=F32)


def _mm_kernel(*refs, n_extra, n_out, nk, b_t, epilogue):
    a_ref, b_ref = refs[0], refs[1]
    extra = refs[2:2 + n_extra]
    outs = refs[2 + n_extra:2 + n_extra + n_out]
    if nk == 1:
        epilogue(_dot(a_ref[...], b_ref[...], b_t), extra, outs)
        return
    acc_ref = refs[-1]
    k = pl.program_id(2)
    d = _dot(a_ref[...], b_ref[...], b_t)

    @pl.when(k == 0)
    def _():
        acc_ref[...] = d

    @pl.when((k > 0) & (k < nk - 1))
    def _():
        acc_ref[...] += d

    @pl.when(k == nk - 1)
    def _():
        epilogue(acc_ref[...] + d, extra, outs)


def _matmul(a, b, epilogue, extras, outs, *, tm, tn, tk=None, b_t=False, name):
    m, kd = a.shape
    n = b.shape[0] if b_t else b.shape[1]
    tk = kd if tk is None else tk
    nk = kd // tk
    grid = (m // tm, n // tn, nk)

    def lift(f):
        return lambda i, j, k: f(i, j)

    b_spec = (pl.BlockSpec((tn, tk), lambda i, j, k: (j, k)) if b_t
              else pl.BlockSpec((tk, tn), lambda i, j, k: (k, j)))
    in_specs = [pl.BlockSpec((tm, tk), lambda i, j, k: (i, k)), b_spec]
    in_specs += [pl.BlockSpec(bs, lift(im)) for _, bs, im in extras]
    out_specs = [pl.BlockSpec(bs, lift(im)) for _, _, bs, im in outs]
    out_shape = [jax.ShapeDtypeStruct(s, dt) for s, dt, _, _ in outs]
    scratch = [pltpu.VMEM((tm, tn), F32)] if nk > 1 else []
    res = pl.pallas_call(
        functools.partial(_mm_kernel, n_extra=len(extras), n_out=len(outs), nk=nk, b_t=b_t, epilogue=epilogue),
        grid=grid,
        in_specs=in_specs,
        out_specs=out_specs,
        out_shape=out_shape,
        scratch_shapes=scratch,
        compiler_params=_params(("parallel", "parallel", "arbitrary")),
        name=name,
    )(a, b, *[e[0] for e in extras])
    return res


def _mm_w32_kernel(*refs, n_extra, n_out, valid, epilogue):
    a_ref, w_ref = refs[0], refs[1]
    extra = refs[2:2 + n_extra]
    outs = refs[2 + n_extra:2 + n_extra + n_out]
    wbf_ref = refs[2 + n_extra + n_out]

    @pl.when(pl.program_id(1) == 0)
    def _():
        w = w_ref[...]
        if valid < w.shape[0]:
            w = jnp.where(lax.broadcasted_iota(I32, w.shape, 0) < valid, w, 0.0)
        wbf_ref[...] = w.astype(BF16)

    epilogue(_dot(a_ref[...], wbf_ref[...], True), extra, outs)


def _matmul_w32(a, w_t, epilogue, extras, outs, *, tm, tn, n, col_off, valid=None, name):
    m, kd = a.shape
    assert n % tn == 0 and col_off % SUBLANES == 0
    valid = tn if valid is None else valid
    grid = (n // tn, m // tm)

    def lift(f):
        return lambda j, i: f(i, j)

    w_spec = pl.BlockSpec((pl.Element(tn), pl.Element(kd)),
                          lambda j, i: (pl.multiple_of(col_off + j * tn, SUBLANES), 0))
    in_specs = [pl.BlockSpec((tm, kd), lambda j, i: (i, 0)), w_spec]
    in_specs += [pl.BlockSpec(bs, lift(im)) for _, bs, im in extras]
    out_specs = [pl.BlockSpec(bs, lift(im)) for _, _, bs, im in outs] + [pl.BlockSpec((tn, kd), lambda j, i: (j, 0))]
    out_shape = [jax.ShapeDtypeStruct(s, dt) for s, dt, _, _ in outs] + [jax.ShapeDtypeStruct((n, kd), BF16)]
    return pl.pallas_call(
        functools.partial(_mm_w32_kernel, n_extra=len(extras), n_out=len(outs), valid=valid, epilogue=epilogue),
        grid=grid,
        in_specs=in_specs,
        out_specs=out_specs,
        out_shape=out_shape,
        compiler_params=_params(("arbitrary", "arbitrary")),
        name=name,
    )(a, w_t, *[e[0] for e in extras])


def _nat(m, n, dt, tm, tn):
    return ((m, n), dt, (tm, tn), lambda i, j: (i, j))


def _rope_lanes(y, c, sm, sp, half):
    n = y.shape[-1]
    return y * c + pltpu.roll(y, n - half, 1) * sm + pltpu.roll(y, half, 1) * sp


def _ep_plain(acc, extra, outs):
    for o in outs:
        o[...] = acc.astype(o.dtype)


def _ep_rope_lanes(acc, extra, outs, *, half, blocked):
    c, sm, sp = extra[0][...], extra[1][...], extra[2][...]
    tm, tn = acc.shape
    for jj in range(tn // LANES):
        y = _rope_lanes(acc[:, jj * LANES:(jj + 1) * LANES], c, sm, sp, half)
        if blocked:
            for r in range(tm // Q_BLOCK):
                outs[0][r, jj] = y[r * Q_BLOCK:(r + 1) * Q_BLOCK].astype(outs[0].dtype)
        else:
            outs[0][:, jj * LANES:(jj + 1) * LANES] = y.astype(outs[0].dtype)
        if len(outs) > 1:
            outs[1][jj] = y.astype(outs[1].dtype)


def _ep_ret_qk(acc, extra, outs, *, scale):
    cos, sin, dec = extra[0][...], extra[1][...], extra[2]
    tn = acc.shape[1]
    for hh in range(tn // RET_DK):
        lo = hh * RET_DK
        x1 = acc[:, lo:lo + LANES]
        x2 = acc[:, lo + LANES:lo + 2 * LANES]
        o1 = x1 * cos - x2 * sin
        o2 = x2 * cos + x1 * sin
        if scale != 1.0:
            o1 = o1 * scale
            o2 = o2 * scale
        d = dec[:, hh * LANES:(hh + 1) * LANES]
        outs[0][:, lo:lo + LANES] = o1.astype(outs[0].dtype)
        outs[0][:, lo + LANES:lo + 2 * LANES] = o2.astype(outs[0].dtype)
        outs[1][:, lo:lo + LANES] = (o1 * d).astype(outs[1].dtype)
        outs[1][:, lo + LANES:lo + 2 * LANES] = (o2 * d).astype(outs[1].dtype)


def _merge_kernel(att_ref, ret_ref, wa_ref, wr_ref, ga_ref, gr_ref, o_ref):
    a = jnp.dot(att_ref[...], wa_ref[...], preferred_element_type=F32)
    r = jnp.dot(ret_ref[...], wr_ref[...], preferred_element_type=F32)
    o_ref[...] = (_sigmoid(ga_ref[...]) * a + _sigmoid(gr_ref[...]) * r).astype(o_ref.dtype)


def _merge(att, ret, wa, wr, gates, tm, tn):
    m = att.shape[0]
    nb = D_MODEL // tn
    return pl.pallas_call(
        _merge_kernel,
        grid=(m // tm, nb),
        in_specs=[
            pl.BlockSpec((tm, ATT_Q_W), lambda i, j: (i, 0)),
            pl.BlockSpec((tm, RET_V_W), lambda i, j: (i, 0)),
            pl.BlockSpec((ATT_Q_W, tn), lambda i, j: (0, j)),
            pl.BlockSpec((RET_V_W, tn), lambda i, j: (0, j)),
            pl.BlockSpec((tm, tn), lambda i, j: (i, nb + j)),
            pl.BlockSpec((tm, tn), lambda i, j: (i, 2 * nb + j)),
        ],
        out_specs=pl.BlockSpec((tm, tn), lambda i, j: (i, j)),
        out_shape=jax.ShapeDtypeStruct((m, D_MODEL), BF16),
        compiler_params=_params(("parallel", "parallel")),
        name="merge_proj",
    )(att, ret, wa, wr, gates, gates)


def _ep_relu2(acc, extra, outs):
    u = jnp.maximum(acc, 0.0)
    outs[0][...] = (u * u).astype(outs[0].dtype)


def _rope_lane_tables(pos, head_w, rot, theta):
    half = rot // 2
    inv_freq = jnp.exp(-math.log(theta) * jnp.arange(half, dtype=F32) / half)
    ang = pos.astype(F32)[:, None] * inv_freq[None, :]
    cos, sin = jnp.cos(ang), jnp.sin(ang)
    n = pos.shape[0]
    z_half = jnp.zeros((n, half), F32)
    rest1 = jnp.ones((n, head_w - rot), F32)
    rest0 = jnp.zeros((n, head_w - rot), F32)
    c = jnp.concatenate([cos, cos, rest1], axis=1)
    sm = jnp.concatenate([-sin, z_half, rest0], axis=1)
    sp = jnp.concatenate([z_half, sin, rest0], axis=1)
    rep = LANES // head_w
    return [jnp.tile(t, (1, rep)) for t in (c, sm, sp)]


def _ret_tables(pos, chunk_pos, chunk_len, log_gamma):
    half = RET_DK // 2
    inv_freq = jnp.exp(-math.log(RET_THETA) * jnp.arange(half, dtype=F32) / half)
    ang = pos.astype(F32)[:, None] * inv_freq[None, :]
    i = chunk_pos.astype(F32)[:, None]
    q_dec = jnp.exp(log_gamma[None, :] * (i + 1.0))
    k_dec = jnp.exp(log_gamma[None, :] * (chunk_len - 1.0 - i))
    return jnp.cos(ang), jnp.sin(ang), jnp.repeat(q_dec, LANES, axis=1), jnp.repeat(k_dec, LANES, axis=1)


PROJ_TN = 512


def _project(xn, w, pos, chunk_pos, chunk_len, log_gamma, *, tm, prompt):
    m = xn.shape[0]
    nrep = max(pos.shape[0] // tm, 1)
    wdt = BF16 if prompt else F32
    out = {"w_bf16": {}}

    def mm(name, a, b, epilogue, extras, outs, tn=PROJ_TN, valid_cols=None):
        n = -(-(OFFS[b] - OFFS[a]) // tn) * tn
        if prompt:
            *res, wb = _matmul_w32(xn, w, epilogue, extras, outs, tm=tm, tn=tn, n=n, col_off=OFFS[a],
                                   valid=valid_cols, name=name)
            out["w_bf16"][name] = wb
            return res
        return _matmul(xn, w[name], epilogue, extras, outs, tm=tm, tn=tn, b_t=True, name=name)

    def tab(t, width=LANES, by_col=False):
        if by_col:
            return (t, (tm, width), lambda i, j: (i % nrep, j))
        return (t, (tm, width), lambda i, j: (i % nrep, 0))

    att_t = [tab(t) for t in _rope_lane_tables(pos, ATT_HEAD_DIM, ATT_HEAD_DIM // 4, ROPE_THETA)]
    idx_t = [tab(t) for t in _rope_lane_tables(pos, IDX_DIM, IDX_DIM // 4, ROPE_THETA)]
    r_cos, r_sin, q_dec, k_dec = _ret_tables(pos, chunk_pos, chunk_len, log_gamma)
    tn = PROJ_TN
    nqb = m // Q_BLOCK

    if prompt:
        o = [((nqb, ATT_HEADS, Q_BLOCK, LANES), BF16, (tm // Q_BLOCK, tn // LANES, Q_BLOCK, LANES),
              lambda i, j: (i, j, 0, 0))]
    else:
        o = [_nat(m, ATT_Q_W, F32, tm, tn)]
    out["q"], = mm("proj_q", 0, 1, functools.partial(_ep_rope_lanes, half=16, blocked=prompt), att_t, o)
    o = [_nat(m, ATT_KV_W, F32, tm, tn)]
    if prompt:
        per_b = SEQ // tm
        o.append(((BATCH, ATT_KV_HEADS, SEQ, LANES), BF16, (None, tn // LANES, tm, LANES),
                  lambda i, j: (i // per_b, j, i % per_b, 0)))
    res = mm("proj_k", 1, 2, functools.partial(_ep_rope_lanes, half=16, blocked=False), att_t, o)
    out["k"] = res[0]
    if prompt:
        out["k_heads"] = res[1]
    o = [_nat(m, ATT_KV_W, F32, tm, tn)]
    if prompt:
        o.append(_nat(m, ATT_KV_W, BF16, tm, tn))
    res = mm("proj_v", 2, 3, _ep_plain, [], o)
    out["v"] = res[0]
    if prompt:
        out["v_bf"] = res[1]
    if prompt:
        o = [((nqb, IDX_Q_W // LANES, Q_BLOCK, LANES), F32, (tm // Q_BLOCK, tn // LANES, Q_BLOCK, LANES),
              lambda i, j: (i, j, 0, 0))]
    else:
        o = [_nat(m, IDX_Q_W, F32, tm, tn)]
    out["iq"], = mm("proj_iq", 3, 4, functools.partial(_ep_rope_lanes, half=8, blocked=prompt), idx_t, o)
    ikw_t = [tab(t) for t in _rope_lane_tables(pos, LANES, IDX_DIM // 4, ROPE_THETA)]
    out["ikw"], = mm("proj_ikw", 4, 6, functools.partial(_ep_rope_lanes, half=8, blocked=False), ikw_t,
                     [_nat(m, LANES, F32, tm, LANES)], tn=LANES, valid_cols=IDX_DIM + IDX_HEADS)
    for nm, grp, dec, scale in (("rq", 6, q_dec, 1.0), ("rk", 7, k_dec, RET_DK ** -0.5)):
        res = mm("proj_" + nm, grp, grp + 1, functools.partial(_ep_ret_qk, scale=scale),
                 [tab(r_cos), tab(r_sin), tab(dec, tn // 2, by_col=True)],
                 [_nat(m, RET_QK_W, wdt, tm, tn), _nat(m, RET_QK_W, wdt, tm, tn)])
        out[nm], out[nm + "d"] = res
    out["rv"], = mm("proj_rv", 8, 9, _ep_plain, [], [_nat(m, RET_V_W, wdt, tm, tn)])
    out["gates"], = mm("proj_gates", 9, 12, _ep_plain, [], [_nat(m, 3 * D_MODEL, F32, tm, tn)])
    return out


def _sortable_key(score):
    kb = lax.bitcast_convert_type(score, I32)
    kb = jnp.where(kb == INT_MIN, 0, kb)
    return jnp.where(kb < 0, kb ^ 0x7FFFFFFF, kb)


def _tile_reduce(x, op):
    tiles = [x[i * SUBLANES:(i + 1) * SUBLANES] for i in range(x.shape[0] // SUBLANES)]
    while len(tiles) > 1:
        nxt = [op(tiles[i], tiles[i + 1]) for i in range(0, len(tiles) - 1, 2)]
        if len(tiles) % 2:
            nxt.append(tiles[-1])
        tiles = nxt
    return tiles[0]


def _topk_mask(sc_ref, key_ref, bias_ref, q_pos, n_chunks):
    def rows(c):
        r0 = pl.multiple_of(c * KEY_CHUNK, KEY_CHUNK)
        return pl.ds(r0, KEY_CHUNK), r0 + lax.broadcasted_iota(I32, (KEY_CHUNK, LANES), 0)

    def build(c, carry):
        sl, row = rows(c)
        key_ref[sl, :] = jnp.where(row <= q_pos, _sortable_key(sc_ref[sl, :]), INT_MIN)
        return carry

    lax.fori_loop(0, n_chunks, build, 0)

    def count(pred):
        def body(c, acc):
            sl, row = rows(c)
            return acc + _tile_reduce(jnp.where(pred(key_ref[sl, :], row), 1.0, 0.0), jnp.add)

        part = lax.fori_loop(0, n_chunks, body, jnp.zeros((SUBLANES, LANES), F32))
        return jnp.sum(part, axis=0, keepdims=True)

    t0 = jnp.where(count(lambda k, row: k >= 0) >= TOPK, 0, INT_MIN).astype(I32)

    def search(i, t):
        cand = t | lax.shift_left(jnp.int32(1), jnp.int32(30) - i)
        return jnp.where(count(lambda k, row: k >= cand) >= TOPK, cand, t)

    thr = lax.fori_loop(0, 31, search, t0)
    n_eq = count(lambda k, row: (row <= q_pos) & (k == thr))
    need = TOPK - count(lambda k, row: k > thr)

    def write(c, carry):
        sl, row = rows(c)
        k = key_ref[sl, :]
        sel = jnp.where(k > thr, 0.0, jnp.where((row <= q_pos) & (k == thr), 0.0, NEG_BIG))
        bias_ref[sl, :] = sel.astype(bias_ref.dtype)
        return carry

    lax.fori_loop(0, n_chunks, write, 0)

    @pl.when(jnp.max(n_eq - need) > 0)
    def _():
        r_i = lax.broadcasted_iota(I32, (LANES, LANES), 0)
        c_i = lax.broadcasted_iota(I32, (LANES, LANES), 1)
        tri = jnp.where(c_i < r_i, 1.0, 0.0).astype(BF16)

        def chunk(c, off):
            r0 = pl.multiple_of(c * LANES, LANES)
            kc = key_ref[pl.ds(r0, LANES), :]
            rc = r0 + lax.broadcasted_iota(I32, (LANES, LANES), 0)
            e = (rc <= q_pos) & (kc == thr)
            ef = jnp.where(e, 1.0, 0.0)
            before = jnp.dot(tri, ef.astype(BF16), preferred_element_type=F32) + off
            keep = e & (before < need)
            sel = jnp.where(kc > thr, 0.0, jnp.where(keep, 0.0, NEG_BIG))
            bias_ref[pl.ds(r0, LANES), :] = sel.astype(bias_ref.dtype)
            return off + jnp.sum(ef, axis=0, keepdims=True)

        lax.fori_loop(0, n_chunks * (KEY_CHUNK // LANES), chunk, jnp.zeros((1, LANES), F32))


def _split_bf16(x):
    hi = x.astype(BF16).astype(F32)
    return hi, x - hi


def _dsa_prompt_kernel(iq_ref, ikw_all_ref, ikw_q_ref, q_ref, k_ref, vt_ref, *rest, n_cast):
    w32_refs, att_ref, wbf_refs = rest[:n_cast], rest[n_cast], rest[n_cast + 1:2 * n_cast + 1]
    lhs_ref, wt_ref, sc_ref, key_ref, bias_ref = rest[2 * n_cast + 1:]
    for src, dst in zip(w32_refs, wbf_refs):
        dst[...] = src[...].astype(BF16)
    qb = pl.program_id(1)
    n_ck = (qb * Q_BLOCK + Q_BLOCK + KEY_CHUNK - 1) // KEY_CHUNK
    lane = lax.broadcasted_iota(I32, (1, LANES), 1)
    low = lane < IDX_DIM

    def chunk_rows(c):
        return pl.ds(pl.multiple_of(c * KEY_CHUNK, KEY_CHUNK), KEY_CHUNK)

    @pl.when(qb == 0)
    def _():
        x = jnp.where(low, ikw_all_ref[...], 0.0)
        hi, lo = _split_bf16(x)
        lhs_ref[:, :LANES] = (hi + pltpu.roll(lo, IDX_DIM, 1)).astype(BF16)
        lhs_ref[:, LANES:] = hi.astype(BF16)

    wt_ref[...] = ikw_q_ref[...].T * IDX_SCALE
    sc_ref[...] = jnp.zeros_like(sc_ref)

    def pairs_body(pq, carry):
        rhs, wts = [], []
        for u in range(PAIRS_PER_STEP):
            p = pq * PAIRS_PER_STEP + u
            hi, lo = _split_bf16(iq_ref[p])
            rhi = pltpu.roll(hi, IDX_DIM, 1)
            rlo = pltpu.roll(lo, IDX_DIM, 1)
            ra = jnp.concatenate([jnp.where(low, hi, rhi), jnp.where(low, lo, 0.0)], axis=1)
            rb = jnp.concatenate([jnp.where(low, rhi, hi), jnp.where(low, rlo, 0.0)], axis=1)
            rhs.append(jnp.concatenate([ra, rb], axis=0).astype(BF16))
            wts.append((wt_ref[pl.ds(IDX_DIM + 2 * p, 1), :], wt_ref[pl.ds(IDX_DIM + 2 * p + 1, 1), :]))

        def ck_body(c, inner):
            sl = chunk_rows(c)
            lhs = lhs_ref[sl, :]
            acc = sc_ref[sl, :]
            for rhs_t, (wa, wb) in zip(rhs, wts):
                d = lax.dot_general(lhs, rhs_t, _NT, preferred_element_type=F32)
                acc = acc + wa * jnp.maximum(d[:, :LANES], 0.0) + wb * jnp.maximum(d[:, LANES:], 0.0)
            sc_ref[sl, :] = acc
            return inner

        lax.fori_loop(0, n_ck, ck_body, 0)
        return carry

    lax.fori_loop(0, IDX_HEADS // 2 // PAIRS_PER_STEP, pairs_body, 0)

    _topk_mask(sc_ref, key_ref, bias_ref, qb * Q_BLOCK + lane, n_ck)

    c_exp = ATT_SCALE * math.log2(math.e)

    r_i = lax.broadcasted_iota(I32, (2 * Q_BLOCK, LANES), 0)
    c_i = lax.broadcasted_iota(I32, (2 * Q_BLOCK, LANES), 1)
    onehot_q = jnp.where((r_i % Q_BLOCK) == c_i, 1.0, 0.0).astype(BF16)
    n_acc = vt_ref.shape[2]

    def groups_body(gq, carry):
        gs = [gq * GROUPS_PER_STEP + u for u in range(GROUPS_PER_STEP)]
        qqs = [jnp.concatenate([jnp.concatenate([q_ref[2 * g], q_ref[2 * g + 1]], axis=0), onehot_q], axis=1)
               for g in gs]

        def ck_body(c, state):
            sl = chunk_rows(c)
            mask = bias_ref[sl, :]
            lgs = [lax.dot_general(jnp.concatenate([k_ref[g, sl, :], mask], axis=1), qq, _NT,
                                   preferred_element_type=F32)
                   for g, qq in zip(gs, qqs)]
            soft = []
            for lg, (m_old, acc) in zip(lgs, state):
                m_new = jnp.maximum(m_old, jnp.max(_tile_reduce(lg, jnp.maximum), axis=0, keepdims=True))
                alpha = jnp.exp2((m_old - m_new) * c_exp)
                soft.append((m_new, alpha, jnp.exp2((lg - m_new) * c_exp).astype(BF16)))
            return tuple((m_new, alpha * acc + jnp.dot(vt_ref[g, c], pt, preferred_element_type=F32))
                         for g, (m_new, alpha, pt), (_, acc) in zip(gs, soft, state))

        init = (jnp.full((1, 2 * LANES), NEG_BIG, F32), jnp.zeros((n_acc, 2 * LANES), F32))
        final = lax.fori_loop(0, n_ck, ck_body, (init,) * GROUPS_PER_STEP)
        for g, (_, acc) in zip(gs, final):
            o = acc[:ATT_HEAD_DIM] / acc[ATT_HEAD_DIM:ATT_HEAD_DIM + 1]
            for hh in range(2):
                att_ref[2 * g + hh] = o[:, hh * LANES:(hh + 1) * LANES].T.astype(att_ref.dtype)
        return carry

    lax.fori_loop(0, ATT_KV_HEADS // GROUPS_PER_STEP, groups_body, 0)


def _cast_slab_specs(w32, n_steps, step_of):
    rows, cols = w32.shape
    slab = rows // n_steps
    assert slab * n_steps == rows and slab % (2 * SUBLANES) == 0
    spec = pl.BlockSpec((slab, cols), lambda *idx: (step_of(*idx), 0))
    return spec, jax.ShapeDtypeStruct((rows, cols), BF16)


def _dsa_prompt(iq, ikw, q, k_heads, vt, casts):
    nqb = SEQ // Q_BLOCK
    n_pair = IDX_Q_W // LANES
    cast_specs, cast_shapes = zip(*[_cast_slab_specs(w32, BATCH * nqb, lambda b, i: b * nqb + i) for w32 in casts])
    return pl.pallas_call(
        functools.partial(_dsa_prompt_kernel, n_cast=len(casts)),
        grid=(BATCH, nqb),
        in_specs=[
            pl.BlockSpec((None, n_pair, Q_BLOCK, LANES), lambda b, i: (b * nqb + i, 0, 0, 0)),
            pl.BlockSpec((SEQ, LANES), lambda b, i: (b, 0)),
            pl.BlockSpec((Q_BLOCK, LANES), lambda b, i: (b * nqb + i, 0)),
            pl.BlockSpec((None, ATT_HEADS, Q_BLOCK, LANES), lambda b, i: (b * nqb + i, 0, 0, 0)),
            pl.BlockSpec((None, ATT_KV_HEADS, SEQ, LANES), lambda b, i: (b, 0, 0, 0)),
            pl.BlockSpec((None, ATT_KV_HEADS, SEQ // KEY_CHUNK, ATT_HEAD_DIM + VT_ONES, KEY_CHUNK),
                         lambda b, i: (b, 0, 0, 0, 0)),
            *cast_specs,
        ],
        out_specs=[pl.BlockSpec((None, ATT_HEADS, Q_BLOCK, LANES), lambda b, i: (b * nqb + i, 0, 0, 0)), *cast_specs],
        out_shape=[jax.ShapeDtypeStruct((BATCH * nqb, ATT_HEADS, Q_BLOCK, LANES), BF16), *cast_shapes],
        scratch_shapes=[
            pltpu.VMEM((SEQ, 2 * LANES), BF16),
            pltpu.VMEM((LANES, LANES), F32),
            pltpu.VMEM((SEQ, LANES), F32),
            pltpu.VMEM((SEQ, LANES), I32),
            pltpu.VMEM((SEQ, LANES), BF16),
        ],
        compiler_params=_params(("parallel", "arbitrary")),
        name="dsa_prompt",
    )(iq, ikw, ikw, q, k_heads, vt, *casts)


def _group_norm_gate(o, gn_w, rg):
    of = o * lax.rsqrt(jnp.mean(o * o, axis=-1, keepdims=True) + NORM_EPS) * gn_w
    return of * (rg * _sigmoid(rg))


def _ret_prompt_kernel(q_ref, qd_ref, k_ref, kd_ref, v_ref, rg_ref, decay_ref, cdec_ref, gnw_ref, *rest, n_cast):
    w32_refs, (ret_ref, s_ref), wbf_refs = rest[:n_cast], rest[n_cast:n_cast + 2], rest[n_cast + 2:]
    for src, dst in zip(w32_refs, wbf_refs):
        dst[...] = src[...].astype(BF16)
    c = pl.program_id(1)

    @pl.when(c == 0)
    def _():
        s_ref[...] = jnp.zeros_like(s_ref)

    def qk(h):
        return slice(h * RET_DK, (h + 1) * RET_DK)

    def vv(h):
        return slice(h * RET_DV, (h + 1) * RET_DV)

    heads = range(RET_HEADS)
    scores = [lax.dot_general(q_ref[:, qk(h)], k_ref[:, qk(h)], _NT, preferred_element_type=F32) for h in heads]
    cross = [jnp.dot(qd_ref[:, qk(h)], s_ref[h].astype(BF16), preferred_element_type=F32) for h in heads]
    kvs = [lax.dot_general(kd_ref[:, qk(h)], v_ref[:, vv(h)], (((0,), (0,)), ((), ())),
                           preferred_element_type=F32) for h in heads]
    for h in heads:
        s_ref[h] = cdec_ref[h] * s_ref[h] + kvs[h]
    for h in heads:
        o = jnp.dot((scores[h] * decay_ref[h]).astype(BF16), v_ref[:, vv(h)], preferred_element_type=F32) + cross[h]
        ret_ref[:, vv(h)] = _group_norm_gate(o, gnw_ref[:, vv(h)], rg_ref[:, vv(h)]).astype(ret_ref.dtype)


def _ret_prompt(pr, decay, cdec, gn_w, casts):
    nc = SEQ // RET_CHUNK
    qk = pl.BlockSpec((RET_CHUNK, RET_QK_W), lambda b, c: (b * nc + c, 0))
    vv = pl.BlockSpec((RET_CHUNK, RET_V_W), lambda b, c: (b * nc + c, 0))
    cast_specs, cast_shapes = zip(*[_cast_slab_specs(w32, BATCH * nc, lambda b, c: b * nc + c) for w32 in casts])
    return pl.pallas_call(
        functools.partial(_ret_prompt_kernel, n_cast=len(casts)),
        grid=(BATCH, nc),
        in_specs=[qk, qk, qk, qk, vv, vv,
                  pl.BlockSpec((RET_HEADS, RET_CHUNK, RET_CHUNK), lambda b, c: (0, 0, 0)),
                  pl.BlockSpec((RET_HEADS, 1, RET_DV), lambda b, c: (0, 0, 0)),
                  pl.BlockSpec((1, RET_V_W), lambda b, c: (0, 0)), *cast_specs],
        out_specs=[vv, pl.BlockSpec((None, RET_HEADS, RET_DK, RET_DV), lambda b, c: (b, 0, 0, 0)), *cast_specs],
        out_shape=[jax.ShapeDtypeStruct((BATCH * SEQ, RET_V_W), BF16),
                   jax.ShapeDtypeStruct((BATCH, RET_HEADS, RET_DK, RET_DV), F32), *cast_shapes],
        compiler_params=_params(("parallel", "arbitrary")),
        name="retention_prompt",
    )(pr["rq"], pr["rqd"], pr["rk"], pr["rkd"], pr["rv"], pr["gates"], decay, cdec, gn_w.reshape(1, RET_V_W), *casts)


def _ret_sample_kernel(q_ref, qd_ref, kt_ref, k_ref, v_ref, rg_ref, cdec_ref, gnw_ref, s_ref, ret_ref, so_ref):
    b = pl.program_id(0)
    nb = kt_ref.shape[1]
    onehot = lax.broadcasted_iota(I32, (1, nb), 1) == b
    for h in range(RET_HEADS):
        qk_sl = slice(h * RET_DK, (h + 1) * RET_DK)
        vv = slice(h * RET_DV, (h + 1) * RET_DV)
        s = s_ref[h]
        v = v_ref[pl.ds(b, 1), vv]
        q = q_ref[pl.ds(b, 1), qk_sl]
        k = k_ref[pl.ds(b, 1), qk_sl]
        qk = jnp.sum(q.astype(BF16).astype(F32) * k.astype(BF16).astype(F32), axis=-1, keepdims=True)
        qd8 = jnp.broadcast_to(qd_ref[pl.ds(b, 1), qk_sl], (16, RET_DK)).astype(BF16)
        o = qk.astype(BF16).astype(F32) * v.astype(BF16).astype(F32) \
            + jnp.dot(qd8, s.astype(BF16), preferred_element_type=F32)[0:1]
        k_col = jnp.sum(jnp.where(onehot, kt_ref[qk_sl, :], 0.0), axis=1, keepdims=True)
        so_ref[h] = cdec_ref[h] * s + k_col * v
        ret_ref[:, vv] = _group_norm_gate(o, gnw_ref[:, vv], rg_ref[pl.ds(b, 1), vv])


def _ret_sample(ps, state, cdec, gn_w):
    nb = DEC_BATCH
    qk = pl.BlockSpec((nb, RET_QK_W), lambda b: (0, 0))
    vv = pl.BlockSpec((nb, RET_V_W), lambda b: (0, 0))
    st = pl.BlockSpec((None, RET_HEADS, RET_DK, RET_DV), lambda b: (b, 0, 0, 0))
    return pl.pallas_call(
        _ret_sample_kernel,
        grid=(nb,),
        in_specs=[qk, qk, pl.BlockSpec((RET_QK_W, nb), lambda b: (0, 0)), qk, vv, vv,
                  pl.BlockSpec((RET_HEADS, 1, RET_DV), lambda b: (0, 0, 0)),
                  pl.BlockSpec((1, RET_V_W), lambda b: (0, 0)), st],
        out_specs=[pl.BlockSpec((None, 1, RET_V_W), lambda b: (b, 0, 0)), st],
        out_shape=[jax.ShapeDtypeStruct((nb, 1, RET_V_W), F32),
                   jax.ShapeDtypeStruct((nb, RET_HEADS, RET_DK, RET_DV), F32)],
        compiler_params=_params(("arbitrary",)),
        name="retention_sample",
    )(ps["rq"], ps["rqd"], ps["rkd"].T, ps["rk"], ps["rv"], ps["gates"], cdec, gn_w.reshape(1, RET_V_W), state)


PAGES_PER_STEP = 8
IDX_PAGES_PER_STEP = 32


def _idx_sample_kernel(pt_ref, iq_ref, w_ref, new_ref, *rest):
    pages = rest[:IDX_PAGES_PER_STEP]
    sc_ref, scn_ref = rest[IDX_PAGES_PER_STEP], rest[IDX_PAGES_PER_STEP + 1]
    j = pl.program_id(1)
    hi, lo = _split_bf16(iq_ref[...])
    lhs = jnp.concatenate([hi + pltpu.roll(hi, IDX_DIM, 1), lo], axis=1).astype(BF16)
    w = w_ref[...] * IDX_SCALE

    def page_scores(pages_t):
        xt = jnp.concatenate(pages_t, axis=1) if len(pages_t) > 1 else pages_t[0]
        khi, klo = _split_bf16(xt)
        rhs = jnp.concatenate([khi, klo, khi, jnp.zeros_like(khi)], axis=0).astype(BF16)
        d = jnp.dot(lhs, rhs, preferred_element_type=F32)
        return jnp.sum(w * jnp.maximum(d, 0.0), axis=0, keepdims=True)

    for i in range(0, IDX_PAGES_PER_STEP, 2):
        sc_ref[:, i * PAGE_SIZE:(i + 2) * PAGE_SIZE] = page_scores([pages[i][...], pages[i + 1][...]])

    @pl.when(j == pl.num_programs(1) - 1)
    def _():
        scn_ref[...] = page_scores([new_ref[...]])


def _idx_sample(page_table, iq3, w3, new_pages, cache_idx):
    nsteps = N_PAGES // IDX_PAGES_PER_STEP

    def page_spec(i):
        return pl.BlockSpec((None, IDX_DIM, PAGE_SIZE),
                            lambda b, j, pt: (pt[b, j * IDX_PAGES_PER_STEP + i], 0, 0))

    grid_spec = pltpu.PrefetchScalarGridSpec(
        num_scalar_prefetch=1,
        grid=(DEC_BATCH, nsteps),
        in_specs=[pl.BlockSpec((None, IDX_HEADS, LANES), lambda b, j, pt: (b, 0, 0)),
                  pl.BlockSpec((None, IDX_HEADS, 1), lambda b, j, pt: (b, 0, 0)),
                  pl.BlockSpec((None, IDX_DIM, PAGE_SIZE), lambda b, j, pt: (b, 0, 0))]
                 + [page_spec(i) for i in range(IDX_PAGES_PER_STEP)],
        out_specs=[pl.BlockSpec((None, 1, IDX_PAGES_PER_STEP * PAGE_SIZE), lambda b, j, pt: (b, 0, j)),
                   pl.BlockSpec((None, 1, PAGE_SIZE), lambda b, j, pt: (b, 0, 0))],
    )
    return pl.pallas_call(
        _idx_sample_kernel,
        grid_spec=grid_spec,
        out_shape=[jax.ShapeDtypeStruct((DEC_BATCH, 1, PAST_LEN), F32),
                   jax.ShapeDtypeStruct((DEC_BATCH, 1, PAGE_SIZE), F32)],
        compiler_params=_params(("parallel", "arbitrary")),
        name="indexer_sample",
    )(page_table, iq3, w3, new_pages, *([cache_idx] * IDX_PAGES_PER_STEP))


def _select_sample_kernel(sc_ref, bias_ref, key_ref):
    _topk_mask(sc_ref, key_ref, bias_ref, jnp.full((1, LANES), PAST_LEN, I32), sc_ref.shape[0] // KEY_CHUNK)


def _select_sample(scores_t):
    return pl.pallas_call(
        _select_sample_kernel,
        out_shape=jax.ShapeDtypeStruct(scores_t.shape, F32),
        scratch_shapes=[pltpu.VMEM(scores_t.shape, I32)],
        compiler_params=pltpu.CompilerParams(vmem_limit_bytes=VMEM_LIMIT),
        name="select_sample",
    )(scores_t)


PAGE_ROWS = PAGE_SIZE * ATT_KV_HEADS


def _attn_sample_step(j, n_steps, q_ref, bias_ref, biasn_ref, kn_ref, vn_ref, kp, vp, o_ref, m_ref, l_ref, acc_ref,
                      host_work):
    col = lax.broadcasted_iota(I32, (ATT_HEADS, PAGE_ROWS), 1)
    head = lax.broadcasted_iota(I32, (ATT_HEADS, PAGE_ROWS), 0)
    own = (col % ATT_KV_HEADS) == (head // (ATT_HEADS // ATT_KV_HEADS))
    c_exp = ATT_SCALE * math.log2(math.e)
    q = q_ref[...].astype(BF16)

    @pl.when(j == 0)
    def _():
        m_ref[...] = jnp.full_like(m_ref, NEG_BIG)
        l_ref[...] = jnp.zeros_like(l_ref)
        acc_ref[...] = jnp.zeros_like(acc_ref)

    half = PAGE_ROWS // 2
    zq = jnp.zeros_like(q)
    q2 = jnp.concatenate([jnp.concatenate([q, zq], axis=1), jnp.concatenate([zq, q], axis=1)], axis=0)

    def halves_side_by_side(page_ref):
        x = page_ref[...]
        return jnp.concatenate([x[:half], x[half:]], axis=1).astype(BF16)

    logits = []
    for i in range(PAGES_PER_STEP):
        lg2 = lax.dot_general(q2, halves_side_by_side(kp[i]), _NT, preferred_element_type=F32)
        lg = jnp.concatenate([lg2[:ATT_HEADS], lg2[ATT_HEADS:]], axis=1)
        logits.append(jnp.where(own, lg + bias_ref[:, i * PAGE_ROWS:(i + 1) * PAGE_ROWS], NEG_BIG))
    host_work()
    m_old = m_ref[...]
    m_new = m_old
    for lg in logits:
        m_new = jnp.maximum(m_new, jnp.max(lg, axis=1, keepdims=True))
    alpha = jnp.exp2((m_old - m_new) * c_exp)
    l_new = alpha * l_ref[...]
    acc = alpha * acc_ref[...]
    for i in range(PAGES_PER_STEP):
        p = jnp.exp2((logits[i] - m_new) * c_exp)
        l_new = l_new + jnp.sum(p, axis=1, keepdims=True)
        p2 = jnp.concatenate([p[:, :half], p[:, half:]], axis=0).astype(BF16)
        o2 = jnp.dot(p2, halves_side_by_side(vp[i]), preferred_element_type=F32)
        acc = acc + o2[:ATT_HEADS, :ATT_HEAD_DIM] + o2[ATT_HEADS:, ATT_HEAD_DIM:]
    m_ref[...] = m_new
    l_ref[...] = l_new
    acc_ref[...] = acc

    @pl.when(j == n_steps - 1)
    def _():
        kn = kn_ref[...].astype(BF16).astype(F32)
        vn = vn_ref[...].astype(BF16).astype(F32)
        lgn = jnp.sum(q.astype(F32) * kn, axis=1, keepdims=True) + biasn_ref[:, 0:1]
        m_f = jnp.maximum(m_new, lgn)
        a = jnp.exp2((m_new - m_f) * c_exp)
        pn = jnp.exp2((lgn - m_f) * c_exp)
        o_ref[...] = (a * acc + pn.astype(BF16).astype(F32) * vn) / (a * l_new + pn)


MLP_UP_TM, MLP_UP_TN = 1024, 512


def _mlp_up_attn_kernel(pt_ref, hn_ref, wup_ref, q_ref, bias_ref, biasn_ref, kn_ref, vn_ref, *rest):
    kp = rest[:PAGES_PER_STEP]
    vp = rest[PAGES_PER_STEP:2 * PAGES_PER_STEP]
    u_ref, o_ref = rest[2 * PAGES_PER_STEP], rest[2 * PAGES_PER_STEP + 1]
    m_ref, l_ref, acc_ref = rest[2 * PAGES_PER_STEP + 2:]
    n_pg = N_PAGES // PAGES_PER_STEP
    _attn_sample_step(
        pl.program_id(1) % n_pg, n_pg, q_ref, bias_ref, biasn_ref, kn_ref, vn_ref, kp, vp, o_ref, m_ref, l_ref, acc_ref,
        host_work=lambda: _ep_relu2(jnp.dot(hn_ref[...], wup_ref[...], preferred_element_type=F32), (), (u_ref,)))


def _mlp_up_with_sample_attn(hn, w_up, page_table, q3, bias_rows, bias_new, k_new, v_new, cache_k, cache_v):
    m, kd = hn.shape
    n_pg = N_PAGES // PAGES_PER_STEP
    n_row, n_col = m // MLP_UP_TM, D_FF // MLP_UP_TN
    assert n_row * n_col == DEC_BATCH * n_pg and n_col % n_pg == 0

    def row(i, j):
        return (i * n_col + j) // n_pg

    def page_spec(p):
        return pl.BlockSpec((None, PAGE_ROWS, ATT_HEAD_DIM),
                            lambda i, j, pt: (pt[row(i, j), (j % n_pg) * PAGES_PER_STEP + p], 0, 0))

    head_rows = pl.BlockSpec((None, ATT_HEADS, ATT_HEAD_DIM), lambda i, j, pt: (row(i, j), 0, 0))
    grid_spec = pltpu.PrefetchScalarGridSpec(
        num_scalar_prefetch=1,
        grid=(n_row, n_col),
        in_specs=[pl.BlockSpec((MLP_UP_TM, kd), lambda i, j, pt: (i, 0)),
                  pl.BlockSpec((kd, MLP_UP_TN), lambda i, j, pt: (0, j)),
                  head_rows,
                  pl.BlockSpec((None, 1, PAGES_PER_STEP * PAGE_ROWS), lambda i, j, pt: (row(i, j), 0, j % n_pg)),
                  pl.BlockSpec((None, 1, PAGE_SIZE), lambda i, j, pt: (row(i, j), 0, 0)),
                  head_rows, head_rows]
                 + [page_spec(p) for p in range(PAGES_PER_STEP)] * 2,
        out_specs=[pl.BlockSpec((MLP_UP_TM, MLP_UP_TN), lambda i, j, pt: (i, j)), head_rows],
        scratch_shapes=[pltpu.VMEM((ATT_HEADS, 1), F32), pltpu.VMEM((ATT_HEADS, 1), F32),
                        pltpu.VMEM((ATT_HEADS, ATT_HEAD_DIM), F32)],
    )
    return pl.pallas_call(
        _mlp_up_attn_kernel,
        grid_spec=grid_spec,
        out_shape=[jax.ShapeDtypeStruct((m, D_FF), BF16),
                   jax.ShapeDtypeStruct((DEC_BATCH, ATT_HEADS, ATT_HEAD_DIM), F32)],
        compiler_params=_params(("arbitrary", "arbitrary")),
        name="mlp_up_attention_sample",
    )(page_table, hn, w_up, q3, bias_rows, bias_new, k_new, v_new,
      *([cache_k] * PAGES_PER_STEP), *([cache_v] * PAGES_PER_STEP))


def _finish(x, att, ret, gates, w, *, tm, tr, sample_attn=None):
    m = x.shape[0]
    mg = _merge(att, ret, w["att_proj"], w["ret_proj"], gates, tm, 512)
    y, = _matmul(mg, w["out"], _ep_plain, [], [_nat(m, D_MODEL, F32, tm, 512)], tm=tm, tn=512, name="w_out")
    h, hn = _post_attn(x, y, w["n_attn_post"], w["n_mlp_pre"], tr)
    if sample_attn is None:
        u, = _matmul(hn, w["mlp_up"], _ep_relu2, [], [_nat(m, D_FF, BF16, tm, 512)], tm=tm, tn=512, name="mlp_up")
        att_s = None
    else:
        u, att_s = _mlp_up_with_sample_attn(hn, w["mlp_up"], *sample_attn)
    d, = _matmul(u, w["mlp_down"], _ep_plain, [], [_nat(m, D_MODEL, F32, tm, 512)], tm=tm, tn=512, tk=4096,
                 name="mlp_down")
    out = _post_mlp(h, d, w["n_mlp_post"], tr)
    return out if sample_attn is None else (out, att_s)


def kernel(x_prompt, x_sample, cache_k, cache_v, cache_idx_k, state_ret, page_table, norm_attn_pre,
           norm_attn_post, w_in, ret_gn_w, w_att_proj, w_ret_proj, w_out, norm_mlp_pre, w_mlp_up,
           w_mlp_down, norm_mlp_post):
    log_gamma = jnp.log1p(-jnp.exp2(-5.0 - jnp.arange(RET_HEADS, dtype=F32)))
    w_in0 = w_in[0].T
    w = {"n_attn_post": norm_attn_post[0], "n_mlp_pre": norm_mlp_pre[0], "n_mlp_post": norm_mlp_post[0]}
    gn_w = ret_gn_w[0]

    m_p = BATCH * SEQ
    xp = x_prompt.reshape(m_p, D_MODEL)
    xn = _rmsnorm_cast(xp, norm_attn_pre[0], 256)
    pos_p = jnp.arange(SEQ, dtype=I32)
    pr = _project(xn, w_in0, pos_p, pos_p % RET_CHUNK, float(RET_CHUNK), log_gamma, tm=1024, prompt=True)
    vt = pr["v_bf"].reshape(BATCH, SEQ // KEY_CHUNK, KEY_CHUNK, ATT_KV_HEADS, ATT_HEAD_DIM).transpose(0, 3, 1, 4, 2)
    vt = jnp.concatenate([vt, jnp.ones(vt.shape[:3] + (VT_ONES, KEY_CHUNK), BF16)], axis=3)
    att4, w["mlp_down"], w["out"], w["ret_proj"], w["att_proj"] = _dsa_prompt(
        pr["iq"], pr["ikw"], pr["q"], pr["k_heads"], vt, (w_mlp_down[0], w_out[0], w_ret_proj[0], w_att_proj[0]))
    att = att4.reshape(m_p // Q_BLOCK, ATT_HEADS, Q_BLOCK, ATT_HEAD_DIM).transpose(0, 2, 1, 3).reshape(m_p, ATT_Q_W)
    ci = jnp.arange(RET_CHUNK, dtype=F32)
    diff = ci[:, None] - ci[None, :]
    decay = jnp.where(diff >= 0, jnp.exp(log_gamma[:, None, None] * jnp.maximum(diff, 0.0)), 0.0)
    cdec_p = jnp.broadcast_to(jnp.exp(log_gamma * RET_CHUNK)[:, None, None], (RET_HEADS, 1, RET_DV))
    ret, s_prompt, w["mlp_up"] = _ret_prompt(pr, decay, cdec_p, gn_w, (w_mlp_up[0],))

    nb = DEC_BATCH
    xs = x_sample.reshape(nb, D_MODEL)
    xns = _rmsnorm_cast(xs, norm_attn_pre[0], nb)
    pos_s = jnp.full((nb,), PAST_LEN, I32)
    ps = _project(xns, pr["w_bf16"], pos_s, jnp.zeros((nb,), I32), 1.0, log_gamma, tm=nb, prompt=False)
    ik_new = ps["ikw"][:, :IDX_DIM]
    iq3 = jnp.pad(ps["iq"].reshape(nb, IDX_HEADS, IDX_DIM), ((0, 0), (0, 0), (0, LANES - IDX_DIM)))
    w3 = ps["ikw"][:, IDX_DIM:IDX_DIM + IDX_HEADS].reshape(nb, IDX_HEADS, 1)
    new_pages = jnp.pad(ik_new[:, :, None], ((0, 0), (0, 0), (0, PAGE_SIZE - 1)))
    sc_past, sc_new = _idx_sample(page_table, iq3, w3, new_pages, cache_idx_k[0].transpose(0, 2, 1))
    scores = jnp.concatenate([sc_past.reshape(nb, PAST_LEN), sc_new.reshape(nb, PAGE_SIZE)], axis=1)
    n_rows = -(-(PAST_LEN + PAGE_SIZE) // KEY_CHUNK) * KEY_CHUNK
    scores_t = jnp.pad(scores.T, ((0, n_rows - PAST_LEN - PAGE_SIZE), (0, LANES - nb)))
    bias = _select_sample(scores_t)[:PAST_LEN + PAGE_SIZE, :nb].T
    n_phys = cache_k.shape[1]
    group = ATT_HEADS // ATT_KV_HEADS
    sample_attn = (page_table, ps["q"].reshape(nb, ATT_HEADS, ATT_HEAD_DIM),
                   jnp.repeat(bias[:, :PAST_LEN], ATT_KV_HEADS, axis=1).reshape(nb, 1, N_PAGES * PAGE_ROWS),
                   bias[:, PAST_LEN:].reshape(nb, 1, PAGE_SIZE),
                   jnp.repeat(ps["k"].reshape(nb, ATT_KV_HEADS, ATT_HEAD_DIM), group, axis=1),
                   jnp.repeat(ps["v"].reshape(nb, ATT_KV_HEADS, ATT_HEAD_DIM), group, axis=1),
                   cache_k[0].reshape(n_phys, PAGE_ROWS, ATT_HEAD_DIM),
                   cache_v[0].reshape(n_phys, PAGE_ROWS, ATT_HEAD_DIM))

    y_prompt, att_s = _finish(xp, att, ret, pr["gates"], w, tm=1024, tr=256, sample_attn=sample_attn)
    y_prompt = y_prompt.reshape(BATCH, SEQ, D_MODEL)
    cdec_s = jnp.broadcast_to(jnp.exp(log_gamma)[:, None, None], (RET_HEADS, 1, RET_DV))
    ret_s, s_sample = _ret_sample(ps, state_ret[0], cdec_s, gn_w)
    y_sample = _finish(xs, att_s.reshape(nb, ATT_Q_W).astype(BF16), ret_s.reshape(nb, RET_V_W).astype(BF16),
                       ps["gates"], w, tm=nb, tr=nb).reshape(nb, 1, D_MODEL)

    return (y_prompt, y_sample,
            pr["k"].reshape(1, BATCH, SEQ, ATT_KV_HEADS, ATT_HEAD_DIM),
            pr["v"].reshape(1, BATCH, SEQ, ATT_KV_HEADS, ATT_HEAD_DIM),
            pr["ikw"][:, :IDX_DIM].reshape(1, BATCH, SEQ, IDX_DIM),
            s_prompt[None],
            ps["k"].reshape(1, nb, 1, ATT_KV_HEADS, ATT_HEAD_DIM),
            ps["v"].reshape(1, nb, 1, ATT_KV_HEADS, ATT_HEAD_DIM),
            ik_new.reshape(1, nb, 1, IDX_DIM),
            s_sample[None])
```

```python
import functools
import math

import jax
import jax.numpy as jnp
import numpy as np
from jax import lax
from jax.experimental import pallas as pl
from jax.experimental.pallas import tpu as pltpu

F32 = jnp.float32
BF16 = jnp.bfloat16
I32 = jnp.int32

D_MODEL = 4096
BATCH = 4
SEQ = 2048
DEC_BATCH = 32
PAST_LEN = 8192
PAGE_SIZE = 128
N_PAGES = PAST_LEN // PAGE_SIZE
ATT_HEADS = 16
ATT_KV_HEADS = 8
ATT_HEAD_DIM = 128
ROPE_THETA = 500000.0
IDX_HEADS = 32
IDX_DIM = 64
TOPK = 256
RET_HEADS = 8
RET_DK = 256
RET_DV = 512
RET_THETA = 10000.0
RET_CHUNK = 128
D_FF = 4 * D_MODEL
NORM_EPS = 1e-6

ATT_Q_W = ATT_HEADS * ATT_HEAD_DIM
ATT_KV_W = ATT_KV_HEADS * ATT_HEAD_DIM
IDX_Q_W = IDX_HEADS * IDX_DIM
RET_QK_W = RET_HEADS * RET_DK
RET_V_W = RET_HEADS * RET_DV
SPLITS = (ATT_Q_W, ATT_KV_W, ATT_KV_W, IDX_Q_W, IDX_DIM, IDX_HEADS,
          RET_QK_W, RET_QK_W, RET_V_W, RET_V_W, D_MODEL, D_MODEL)
OFFS = tuple(int(v) for v in np.concatenate([[0], np.cumsum(SPLITS)]))

LANES = 128
SUBLANES = 8
Q_BLOCK = 128
KEY_CHUNK = 512
PAIRS_PER_STEP = 16
GROUPS_PER_STEP = 8
VT_ONES = 16
VMEM_LIMIT = 56 * 1024 * 1024

INT_MIN = -2 ** 31
NEG_BIG = -1e30
ATT_SCALE = ATT_HEAD_DIM ** -0.5
IDX_SCALE = (IDX_DIM ** -0.5) * (IDX_HEADS ** -0.5)

_NT = (((1,), (1,)), ((), ()))


def _params(sem):
    return pltpu.CompilerParams(dimension_semantics=sem, vmem_limit_bytes=VMEM_LIMIT)


def _sigmoid(x):
    return 1.0 / (1.0 + jnp.exp(-x))


def _rmsnorm_cast_kernel(x_ref, w_ref, o_ref):
    x = x_ref[...]
    y = x * lax.rsqrt(jnp.mean(x * x, axis=-1, keepdims=True) + NORM_EPS)
    o_ref[...] = (y * w_ref[...]).astype(o_ref.dtype)


def _rmsnorm_cast(x, w, tr):
    m, d = x.shape
    return pl.pallas_call(
        _rmsnorm_cast_kernel,
        grid=(m // tr,),
        in_specs=[pl.BlockSpec((tr, d), lambda i: (i, 0)), pl.BlockSpec((1, d), lambda i: (0, 0))],
        out_specs=pl.BlockSpec((tr, d), lambda i: (i, 0)),
        out_shape=jax.ShapeDtypeStruct((m, d), BF16),
        compiler_params=_params(("parallel",)),
        name="rmsnorm_cast",
    )(x, w.reshape(1, d))


def _post_attn_kernel(x_ref, y_ref, w1_ref, w2_ref, h_ref, hn_ref):
    y = y_ref[...]
    yn = y * lax.rsqrt(jnp.mean(y * y, axis=-1, keepdims=True) + NORM_EPS) * w1_ref[...]
    h = x_ref[...] + yn
    h_ref[...] = h
    hn = h * lax.rsqrt(jnp.mean(h * h, axis=-1, keepdims=True) + NORM_EPS) * w2_ref[...]
    hn_ref[...] = hn.astype(hn_ref.dtype)


def _post_attn(x, y, w1, w2, tr):
    m, d = x.shape
    row = pl.BlockSpec((tr, d), lambda i: (i, 0))
    vec = pl.BlockSpec((1, d), lambda i: (0, 0))
    return pl.pallas_call(
        _post_attn_kernel,
        grid=(m // tr,),
        in_specs=[row, row, vec, vec],
        out_specs=[row, row],
        out_shape=[jax.ShapeDtypeStruct((m, d), F32), jax.ShapeDtypeStruct((m, d), BF16)],
        compiler_params=_params(("parallel",)),
        name="post_attn_norm",
    )(x, y, w1.reshape(1, d), w2.reshape(1, d))


def _post_mlp_kernel(h_ref, d_ref, w_ref, o_ref):
    d = d_ref[...]
    dn = d * lax.rsqrt(jnp.mean(d * d, axis=-1, keepdims=True) + NORM_EPS) * w_ref[...]
    o_ref[...] = h_ref[...] + dn


def _post_mlp(h, d, w, tr):
    m, dm = h.shape
    row = pl.BlockSpec((tr, dm), lambda i: (i, 0))
    return pl.pallas_call(
        _post_mlp_kernel,
        grid=(m // tr,),
        in_specs=[row, row, pl.BlockSpec((1, dm), lambda i: (0, 0))],
        out_specs=row,
        out_shape=jax.ShapeDtypeStruct((m, dm), F32),
        compiler_params=_params(("parallel",)),
        name="post_mlp_norm",
    )(h, d, w.reshape(1, dm))


def _dot(a, b, b_t):
    if b_t:
        return lax.dot_general(a, b, _NT, preferred_element_type=F32)
    return jnp.dot(a, b, preferred_element_type=F32)


def _mm_kernel(*refs, n_extra, n_out, nk, b_t, epilogue):
    a_ref, b_ref = refs[0], refs[1]
    extra = refs[2:2 + n_extra]
    outs = refs[2 + n_extra:2 + n_extra + n_out]
    if nk == 1:
        epilogue(_dot(a_ref[...], b_ref[...], b_t), extra, outs)
        return
    acc_ref = refs[-1]
    k = pl.program_id(2)
    d = _dot(a_ref[...], b_ref[...], b_t)

    @pl.when(k == 0)
    def _():
        acc_ref[...] = d

    @pl.when((k > 0) & (k < nk - 1))
    def _():
        acc_ref[...] += d

    @pl.when(k == nk - 1)
    def _():
        epilogue(acc_ref[...] + d, extra, outs)


def _matmul(a, b, epilogue, extras, outs, *, tm, tn, tk=None, b_t=False, name):
    m, kd = a.shape
    n = b.shape[0] if b_t else b.shape[1]
    tk = kd if tk is None else tk
    nk = kd // tk
    grid = (m // tm, n // tn, nk)

    def lift(f):
        return lambda i, j, k: f(i, j)

    b_spec = (pl.BlockSpec((tn, tk), lambda i, j, k: (j, k)) if b_t
              else pl.BlockSpec((tk, tn), lambda i, j, k: (k, j)))
    in_specs = [pl.BlockSpec((tm, tk), lambda i, j, k: (i, k)), b_spec]
    in_specs += [pl.BlockSpec(bs, lift(im)) for _, bs, im in extras]
    out_specs = [pl.BlockSpec(bs, lift(im)) for _, _, bs, im in outs]
    out_shape = [jax.ShapeDtypeStruct(s, dt) for s, dt, _, _ in outs]
    scratch = [pltpu.VMEM((tm, tn), F32)] if nk > 1 else []
    res = pl.pallas_call(
        functools.partial(_mm_kernel, n_extra=len(extras), n_out=len(outs), nk=nk, b_t=b_t, epilogue=epilogue),
        grid=grid,
        in_specs=in_specs,
        out_specs=out_specs,
        out_shape=out_shape,
        scratch_shapes=scratch,
        compiler_params=_params(("parallel", "parallel", "arbitrary")),
        name=name,
    )(a, b, *[e[0] for e in extras])
    return res


def _mm_w32_kernel(*refs, n_extra, n_out, valid, epilogue):
    a_ref, w_ref = refs[0], refs[1]
    extra = refs[2:2 + n_extra]
    outs = refs[2 + n_extra:2 + n_extra + n_out]
    wbf_ref = refs[2 + n_extra + n_out]

    @pl.when(pl.program_id(1) == 0)
    def _():
        w = w_ref[...]
        if valid < w.shape[0]:
            w = jnp.where(lax.broadcasted_iota(I32, w.shape, 0) < valid, w, 0.0)
        wbf_ref[...] = w.astype(BF16)

    epilogue(_dot(a_ref[...], wbf_ref[...], True), extra, outs)


def _matmul_w32(a, w_t, epilogue, extras, outs, *, tm, tn, n, col_off, valid=None, name):
    m, kd = a.shape
    assert n % tn == 0 and col_off % SUBLANES == 0
    valid = tn if valid is None else valid
    grid = (n // tn, m // tm)

    def lift(f):
        return lambda j, i: f(i, j)

    w_spec = pl.BlockSpec((pl.Element(tn), pl.Element(kd)),
                          lambda j, i: (pl.multiple_of(col_off + j * tn, SUBLANES), 0))
    in_specs = [pl.BlockSpec((tm, kd), lambda j, i: (i, 0)), w_spec]
    in_specs += [pl.BlockSpec(bs, lift(im)) for _, bs, im in extras]
    out_specs = [pl.BlockSpec(bs, lift(im)) for _, _, bs, im in outs] + [pl.BlockSpec((tn, kd), lambda j, i: (j, 0))]
    out_shape = [jax.ShapeDtypeStruct(s, dt) for s, dt, _, _ in outs] + [jax.ShapeDtypeStruct((n, kd), BF16)]
    return pl.pallas_call(
        functools.partial(_mm_w32_kernel, n_extra=len(extras), n_out=len(outs), valid=valid, epilogue=epilogue),
        grid=grid,
        in_specs=in_specs,
        out_specs=out_specs,
        out_shape=out_shape,
        compiler_params=_params(("arbitrary", "arbitrary")),
        name=name,
    )(a, w_t, *[e[0] for e in extras])


def _nat(m, n, dt, tm, tn):
    return ((m, n), dt, (tm, tn), lambda i, j: (i, j))


def _rope_lanes(y, c, sm, sp, half):
    n = y.shape[-1]
    return y * c + pltpu.roll(y, n - half, 1) * sm + pltpu.roll(y, half, 1) * sp


def _ep_plain(acc, extra, outs):
    for o in outs:
        o[...] = acc.astype(o.dtype)


def _ep_rope_lanes(acc, extra, outs, *, half, blocked):
    c, sm, sp = extra[0][...], extra[1][...], extra[2][...]
    tm, tn = acc.shape
    for jj in range(tn // LANES):
        y = _rope_lanes(acc[:, jj * LANES:(jj + 1) * LANES], c, sm, sp, half)
        if blocked:
            for r in range(tm // Q_BLOCK):
                outs[0][r, jj] = y[r * Q_BLOCK:(r + 1) * Q_BLOCK].astype(outs[0].dtype)
        else:
            outs[0][:, jj * LANES:(jj + 1) * LANES] = y.astype(outs[0].dtype)
        if len(outs) > 1:
            outs[1][jj] = y.astype(outs[1].dtype)


def _ep_ret_qk(acc, extra, outs, *, scale):
    cos, sin, dec = extra[0][...], extra[1][...], extra[2]
    tn = acc.shape[1]
    for hh in range(tn // RET_DK):
        lo = hh * RET_DK
        x1 = acc[:, lo:lo + LANES]
        x2 = acc[:, lo + LANES:lo + 2 * LANES]
        o1 = x1 * cos - x2 * sin
        o2 = x2 * cos + x1 * sin
        if scale != 1.0:
            o1 = o1 * scale
            o2 = o2 * scale
        d = dec[:, hh * LANES:(hh + 1) * LANES]
        outs[0][:, lo:lo + LANES] = o1.astype(outs[0].dtype)
        outs[0][:, lo + LANES:lo + 2 * LANES] = o2.astype(outs[0].dtype)
        outs[1][:, lo:lo + LANES] = (o1 * d).astype(outs[1].dtype)
        outs[1][:, lo + LANES:lo + 2 * LANES] = (o2 * d).astype(outs[1].dtype)


def _merge_kernel(att_ref, ret_ref, wa_ref, wr_ref, ga_ref, gr_ref, o_ref):
    a = jnp.dot(att_ref[...], wa_ref[...], preferred_element_type=F32)
    r = jnp.dot(ret_ref[...], wr_ref[...], preferred_element_type=F32)
    o_ref[...] = (_sigmoid(ga_ref[...]) * a + _sigmoid(gr_ref[...]) * r).astype(o_ref.dtype)


def _merge(att, ret, wa, wr, gates, tm, tn):
    m = att.shape[0]
    nb = D_MODEL // tn
    return pl.pallas_call(
        _merge_kernel,
        grid=(m // tm, nb),
        in_specs=[
            pl.BlockSpec((tm, ATT_Q_W), lambda i, j: (i, 0)),
            pl.BlockSpec((tm, RET_V_W), lambda i, j: (i, 0)),
            pl.BlockSpec((ATT_Q_W, tn), lambda i, j: (0, j)),
            pl.BlockSpec((RET_V_W, tn), lambda i, j: (0, j)),
            pl.BlockSpec((tm, tn), lambda i, j: (i, nb + j)),
            pl.BlockSpec((tm, tn), lambda i, j: (i, 2 * nb + j)),
        ],
        out_specs=pl.BlockSpec((tm, tn), lambda i, j: (i, j)),
        out_shape=jax.ShapeDtypeStruct((m, D_MODEL), BF16),
        compiler_params=_params(("parallel", "parallel")),
        name="merge_proj",
    )(att, ret, wa, wr, gates, gates)


def _ep_relu2(acc, extra, outs):
    u = jnp.maximum(acc, 0.0)
    outs[0][...] = (u * u).astype(outs[0].dtype)


def _rope_lane_tables(pos, head_w, rot, theta):
    half = rot // 2
    inv_freq = jnp.exp(-math.log(theta) * jnp.arange(half, dtype=F32) / half)
    ang = pos.astype(F32)[:, None] * inv_freq[None, :]
    cos, sin = jnp.cos(ang), jnp.sin(ang)
    n = pos.shape[0]
    z_half = jnp.zeros((n, half), F32)
    rest1 = jnp.ones((n, head_w - rot), F32)
    rest0 = jnp.zeros((n, head_w - rot), F32)
    c = jnp.concatenate([cos, cos, rest1], axis=1)
    sm = jnp.concatenate([-sin, z_half, rest0], axis=1)
    sp = jnp.concatenate([z_half, sin, rest0], axis=1)
    rep = LANES // head_w
    return [jnp.tile(t, (1, rep)) for t in (c, sm, sp)]


def _ret_tables(pos, chunk_pos, chunk_len, log_gamma):
    half = RET_DK // 2
    inv_freq = jnp.exp(-math.log(RET_THETA) * jnp.arange(half, dtype=F32) / half)
    ang = pos.astype(F32)[:, None] * inv_freq[None, :]
    i = chunk_pos.astype(F32)[:, None]
    q_dec = jnp.exp(log_gamma[None, :] * (i + 1.0))
    k_dec = jnp.exp(log_gamma[None, :] * (chunk_len - 1.0 - i))
    return jnp.cos(ang), jnp.sin(ang), jnp.repeat(q_dec, LANES, axis=1), jnp.repeat(k_dec, LANES, axis=1)


PROJ_TN = 512


def _project(xn, w, pos, chunk_pos, chunk_len, log_gamma, *, tm, prompt):
    m = xn.shape[0]
    nrep = max(pos.shape[0] // tm, 1)
    wdt = BF16 if prompt else F32
    out = {"w_bf16": {}}

    def mm(name, a, b, epilogue, extras, outs, tn=PROJ_TN, valid_cols=None):
        n = -(-(OFFS[b] - OFFS[a]) // tn) * tn
        if prompt:
            *res, wb = _matmul_w32(xn, w, epilogue, extras, outs, tm=tm, tn=tn, n=n, col_off=OFFS[a],
                                   valid=valid_cols, name=name)
            out["w_bf16"][name] = wb
            return res
        return _matmul(xn, w[name], epilogue, extras, outs, tm=tm, tn=tn, b_t=True, name=name)

    def tab(t, width=LANES, by_col=False):
        if by_col:
            return (t, (tm, width), lambda i, j: (i % nrep, j))
        return (t, (tm, width), lambda i, j: (i % nrep, 0))

    att_t = [tab(t) for t in _rope_lane_tables(pos, ATT_HEAD_DIM, ATT_HEAD_DIM // 4, ROPE_THETA)]
    idx_t = [tab(t) for t in _rope_lane_tables(pos, IDX_DIM, IDX_DIM // 4, ROPE_THETA)]
    r_cos, r_sin, q_dec, k_dec = _ret_tables(pos, chunk_pos, chunk_len, log_gamma)
    tn = PROJ_TN
    nqb = m // Q_BLOCK

    if prompt:
        o = [((nqb, ATT_HEADS, Q_BLOCK, LANES), BF16, (tm // Q_BLOCK, tn // LANES, Q_BLOCK, LANES),
              lambda i, j: (i, j, 0, 0))]
    else:
        o = [_nat(m, ATT_Q_W, F32, tm, tn)]
    out["q"], = mm("proj_q", 0, 1, functools.partial(_ep_rope_lanes, half=16, blocked=prompt), att_t, o)
    o = [_nat(m, ATT_KV_W, F32, tm, tn)]
    if prompt:
        per_b = SEQ // tm
        o.append(((BATCH, ATT_KV_HEADS, SEQ, LANES), BF16, (None, tn // LANES, tm, LANES),
                  lambda i, j: (i // per_b, j, i % per_b, 0)))
    res = mm("proj_k", 1, 2, functools.partial(_ep_rope_lanes, half=16, blocked=False), att_t, o)
    out["k"] = res[0]
    if prompt:
        out["k_heads"] = res[1]
    o = [_nat(m, ATT_KV_W, F32, tm, tn)]
    if prompt:
        o.append(_nat(m, ATT_KV_W, BF16, tm, tn))
    res = mm("proj_v", 2, 3, _ep_plain, [], o)
    out["v"] = res[0]
    if prompt:
        out["v_bf"] = res[1]
    if prompt:
        o = [((nqb, IDX_Q_W // LANES, Q_BLOCK, LANES), F32, (tm // Q_BLOCK, tn // LANES, Q_BLOCK, LANES),
              lambda i, j: (i, j, 0, 0))]
    else:
        o = [_nat(m, IDX_Q_W, F32, tm, tn)]
    out["iq"], = mm("proj_iq", 3, 4, functools.partial(_ep_rope_lanes, half=8, blocked=prompt), idx_t, o)
    ikw_t = [tab(t) for t in _rope_lane_tables(pos, LANES, IDX_DIM // 4, ROPE_THETA)]
    out["ikw"], = mm("proj_ikw", 4, 6, functools.partial(_ep_rope_lanes, half=8, blocked=False), ikw_t,
                     [_nat(m, LANES, F32, tm, LANES)], tn=LANES, valid_cols=IDX_DIM + IDX_HEADS)
    for nm, grp, dec, scale in (("rq", 6, q_dec, 1.0), ("rk", 7, k_dec, RET_DK ** -0.5)):
        res = mm("proj_" + nm, grp, grp + 1, functools.partial(_ep_ret_qk, scale=scale),
                 [tab(r_cos), tab(r_sin), tab(dec, tn // 2, by_col=True)],
                 [_nat(m, RET_QK_W, wdt, tm, tn), _nat(m, RET_QK_W, wdt, tm, tn)])
        out[nm], out[nm + "d"] = res
    out["rv"], = mm("proj_rv", 8, 9, _ep_plain, [], [_nat(m, RET_V_W, wdt, tm, tn)])
    out["gates"], = mm("proj_gates", 9, 12, _ep_plain, [], [_nat(m, 3 * D_MODEL, F32, tm, tn)])
    return out


def _sortable_key(score):
    kb = lax.bitcast_convert_type(score, I32)
    return jnp.where(kb < 0, kb ^ 0x7FFFFFFF, kb)


def _tile_reduce(x, op):
    tiles = [x[i * SUBLANES:(i + 1) * SUBLANES] for i in range(x.shape[0] // SUBLANES)]
    while len(tiles) > 1:
        nxt = [op(tiles[i], tiles[i + 1]) for i in range(0, len(tiles) - 1, 2)]
        if len(tiles) % 2:
            nxt.append(tiles[-1])
        tiles = nxt
    return tiles[0]


def _topk_mask(sc_ref, key_ref, bias_ref, q_pos, n_chunks):
    def rows(c):
        r0 = pl.multiple_of(c * KEY_CHUNK, KEY_CHUNK)
        return pl.ds(r0, KEY_CHUNK), r0 + lax.broadcasted_iota(I32, (KEY_CHUNK, LANES), 0)

    def build(c, carry):
        sl, row = rows(c)
        key_ref[sl, :] = jnp.where(row <= q_pos, _sortable_key(sc_ref[sl, :]), INT_MIN)
        return carry

    lax.fori_loop(0, n_chunks, build, 0)

    def count(pred):
        def body(c, acc):
            sl, row = rows(c)
            return acc + _tile_reduce(jnp.where(pred(key_ref[sl, :], row), 1.0, 0.0), jnp.add)

        part = lax.fori_loop(0, n_chunks, body, jnp.zeros((SUBLANES, LANES), F32))
        return jnp.sum(part, axis=0, keepdims=True)

    t0 = jnp.where(count(lambda k, row: k >= 0) >= TOPK, 0, INT_MIN).astype(I32)

    def search(i, t):
        cand = t | lax.shift_left(jnp.int32(1), jnp.int32(30) - i)
        return jnp.where(count(lambda k, row: k >= cand) >= TOPK, cand, t)

    thr = lax.fori_loop(0, 31, search, t0)
    n_eq = count(lambda k, row: (row <= q_pos) & (k == thr))
    need = TOPK - count(lambda k, row: k > thr)

    def write(c, carry):
        sl, row = rows(c)
        k = key_ref[sl, :]
        sel = jnp.where(k > thr, 0.0, jnp.where((row <= q_pos) & (k == thr), 0.0, NEG_BIG))
        bias_ref[sl, :] = sel.astype(bias_ref.dtype)
        return carry

    lax.fori_loop(0, n_chunks, write, 0)

    @pl.when(jnp.max(n_eq - need) > 0)
    def _():
        r_i = lax.broadcasted_iota(I32, (LANES, LANES), 0)
        c_i = lax.broadcasted_iota(I32, (LANES, LANES), 1)
        tri = jnp.where(c_i < r_i, 1.0, 0.0).astype(BF16)

        def chunk(c, off):
            r0 = pl.multiple_of(c * LANES, LANES)
            kc = key_ref[pl.ds(r0, LANES), :]
            rc = r0 + lax.broadcasted_iota(I32, (LANES, LANES), 0)
            e = (rc <= q_pos) & (kc == thr)
            ef = jnp.where(e, 1.0, 0.0)
            before = jnp.dot(tri, ef.astype(BF16), preferred_element_type=F32) + off
            keep = e & (before < need)
            sel = jnp.where(kc > thr, 0.0, jnp.where(keep, 0.0, NEG_BIG))
            bias_ref[pl.ds(r0, LANES), :] = sel.astype(bias_ref.dtype)
            return off + jnp.sum(ef, axis=0, keepdims=True)

        lax.fori_loop(0, n_chunks * (KEY_CHUNK // LANES), chunk, jnp.zeros((1, LANES), F32))


def _split_bf16(x):
    hi = x.astype(BF16).astype(F32)
    return hi, x - hi


def _dsa_prompt_kernel(iq_ref, ikw_all_ref, ikw_q_ref, q_ref, k_ref, vt_ref, *rest, n_cast):
    w32_refs, att_ref, wbf_refs = rest[:n_cast], rest[n_cast], rest[n_cast + 1:2 * n_cast + 1]
    lhs_ref, wt_ref, sc_ref, key_ref, bias_ref = rest[2 * n_cast + 1:]
    for src, dst in zip(w32_refs, wbf_refs):
        dst[...] = src[...].astype(BF16)
    qb = pl.program_id(1)
    n_ck = (qb * Q_BLOCK + Q_BLOCK + KEY_CHUNK - 1) // KEY_CHUNK
    lane = lax.broadcasted_iota(I32, (1, LANES), 1)
    low = lane < IDX_DIM

    def chunk_rows(c):
        return pl.ds(pl.multiple_of(c * KEY_CHUNK, KEY_CHUNK), KEY_CHUNK)

    @pl.when(qb == 0)
    def _():
        x = jnp.where(low, ikw_all_ref[...], 0.0)
        hi, lo = _split_bf16(x)
        lhs_ref[:, :LANES] = (hi + pltpu.roll(lo, IDX_DIM, 1)).astype(BF16)
        lhs_ref[:, LANES:] = hi.astype(BF16)

    wt_ref[...] = ikw_q_ref[...].T * IDX_SCALE
    sc_ref[...] = jnp.zeros_like(sc_ref)

    def pairs_body(pq, carry):
        rhs, wts = [], []
        for u in range(PAIRS_PER_STEP):
            p = pq * PAIRS_PER_STEP + u
            hi, lo = _split_bf16(iq_ref[p])
            rhi = pltpu.roll(hi, IDX_DIM, 1)
            rlo = pltpu.roll(lo, IDX_DIM, 1)
            ra = jnp.concatenate([jnp.where(low, hi, rhi), jnp.where(low, lo, 0.0)], axis=1)
            rb = jnp.concatenate([jnp.where(low, rhi, hi), jnp.where(low, rlo, 0.0)], axis=1)
            rhs.append(jnp.concatenate([ra, rb], axis=0).astype(BF16))
            wts.append((wt_ref[pl.ds(IDX_DIM + 2 * p, 1), :], wt_ref[pl.ds(IDX_DIM + 2 * p + 1, 1), :]))

        def ck_body(c, inner):
            sl = chunk_rows(c)
            lhs = lhs_ref[sl, :]
            acc = sc_ref[sl, :]
            for rhs_t, (wa, wb) in zip(rhs, wts):
                d = lax.dot_general(lhs, rhs_t, _NT, preferred_element_type=F32)
                acc = acc + wa * jnp.maximum(d[:, :LANES], 0.0) + wb * jnp.maximum(d[:, LANES:], 0.0)
            sc_ref[sl, :] = acc
            return inner

        lax.fori_loop(0, n_ck, ck_body, 0)
        return carry

    lax.fori_loop(0, IDX_HEADS // 2 // PAIRS_PER_STEP, pairs_body, 0)

    _topk_mask(sc_ref, key_ref, bias_ref, qb * Q_BLOCK + lane, n_ck)

    c_exp = ATT_SCALE * math.log2(math.e)

    r_i = lax.broadcasted_iota(I32, (2 * Q_BLOCK, LANES), 0)
    c_i = lax.broadcasted_iota(I32, (2 * Q_BLOCK, LANES), 1)
    onehot_q = jnp.where((r_i % Q_BLOCK) == c_i, 1.0, 0.0).astype(BF16)
    n_acc = vt_ref.shape[2]

    def groups_body(gq, carry):
        gs = [gq * GROUPS_PER_STEP + u for u in range(GROUPS_PER_STEP)]
        qqs = [jnp.concatenate([jnp.concatenate([q_ref[2 * g], q_ref[2 * g + 1]], axis=0), onehot_q], axis=1)
               for g in gs]

        def ck_body(c, state):
            sl = chunk_rows(c)
            mask = bias_ref[sl, :]
            lgs = [lax.dot_general(jnp.concatenate([k_ref[g, sl, :], mask], axis=1), qq, _NT,
                                   preferred_element_type=F32)
                   for g, qq in zip(gs, qqs)]
            soft = []
            for lg, (m_old, acc) in zip(lgs, state):
                m_new = jnp.maximum(m_old, jnp.max(_tile_reduce(lg, jnp.maximum), axis=0, keepdims=True))
                alpha = jnp.exp2((m_old - m_new) * c_exp)
                soft.append((m_new, alpha, jnp.exp2((lg - m_new) * c_exp).astype(BF16)))
            return tuple((m_new, alpha * acc + jnp.dot(vt_ref[g, c], pt, preferred_element_type=F32))
                         for g, (m_new, alpha, pt), (_, acc) in zip(gs, soft, state))

        init = (jnp.full((1, 2 * LANES), NEG_BIG, F32), jnp.zeros((n_acc, 2 * LANES), F32))
        final = lax.fori_loop(0, n_ck, ck_body, (init,) * GROUPS_PER_STEP)
        for g, (_, acc) in zip(gs, final):
            o = acc[:ATT_HEAD_DIM] / acc[ATT_HEAD_DIM:ATT_HEAD_DIM + 1]
            for hh in range(2):
                att_ref[2 * g + hh] = o[:, hh * LANES:(hh + 1) * LANES].T.astype(att_ref.dtype)
        return carry

    lax.fori_loop(0, ATT_KV_HEADS // GROUPS_PER_STEP, groups_body, 0)


def _cast_slab_specs(w32, n_steps, step_of):
    rows, cols = w32.shape
    slab = rows // n_steps
    assert slab * n_steps == rows and slab % (2 * SUBLANES) == 0
    spec = pl.BlockSpec((slab, cols), lambda *idx: (step_of(*idx), 0))
    return spec, jax.ShapeDtypeStruct((rows, cols), BF16)


def _dsa_prompt(iq, ikw, q, k_heads, vt, casts):
    nqb = SEQ // Q_BLOCK
    n_pair = IDX_Q_W // LANES
    cast_specs, cast_shapes = zip(*[_cast_slab_specs(w32, BATCH * nqb, lambda b, i: b * nqb + i) for w32 in casts])
    return pl.pallas_call(
        functools.partial(_dsa_prompt_kernel, n_cast=len(casts)),
        grid=(BATCH, nqb),
        in_specs=[
            pl.BlockSpec((None, n_pair, Q_BLOCK, LANES), lambda b, i: (b * nqb + i, 0, 0, 0)),
            pl.BlockSpec((SEQ, LANES), lambda b, i: (b, 0)),
            pl.BlockSpec((Q_BLOCK, LANES), lambda b, i: (b * nqb + i, 0)),
            pl.BlockSpec((None, ATT_HEADS, Q_BLOCK, LANES), lambda b, i: (b * nqb + i, 0, 0, 0)),
            pl.BlockSpec((None, ATT_KV_HEADS, SEQ, LANES), lambda b, i: (b, 0, 0, 0)),
            pl.BlockSpec((None, ATT_KV_HEADS, SEQ // KEY_CHUNK, ATT_HEAD_DIM + VT_ONES, KEY_CHUNK),
                         lambda b, i: (b, 0, 0, 0, 0)),
            *cast_specs,
        ],
        out_specs=[pl.BlockSpec((None, ATT_HEADS, Q_BLOCK, LANES), lambda b, i: (b * nqb + i, 0, 0, 0)), *cast_specs],
        out_shape=[jax.ShapeDtypeStruct((BATCH * nqb, ATT_HEADS, Q_BLOCK, LANES), BF16), *cast_shapes],
        scratch_shapes=[
            pltpu.VMEM((SEQ, 2 * LANES), BF16),
            pltpu.VMEM((LANES, LANES), F32),
            pltpu.VMEM((SEQ, LANES), F32),
            pltpu.VMEM((SEQ, LANES), I32),
            pltpu.VMEM((SEQ, LANES), BF16),
        ],
        compiler_params=_params(("parallel", "arbitrary")),
        name="dsa_prompt",
    )(iq, ikw, ikw, q, k_heads, vt, *casts)


def _group_norm_gate(o, gn_w, rg):
    of = o * lax.rsqrt(jnp.mean(o * o, axis=-1, keepdims=True) + NORM_EPS) * gn_w
    return of * (rg * _sigmoid(rg))


def _ret_prompt_kernel(q_ref, qd_ref, k_ref, kd_ref, v_ref, rg_ref, decay_ref, cdec_ref, gnw_ref, *rest, n_cast):
    w32_refs, (ret_ref, s_ref), wbf_refs = rest[:n_cast], rest[n_cast:n_cast + 2], rest[n_cast + 2:]
    for src, dst in zip(w32_refs, wbf_refs):
        dst[...] = src[...].astype(BF16)
    c = pl.program_id(1)

    @pl.when(c == 0)
    def _():
        s_ref[...] = jnp.zeros_like(s_ref)

    def qk(h):
        return slice(h * RET_DK, (h + 1) * RET_DK)

    def vv(h):
        return slice(h * RET_DV, (h + 1) * RET_DV)

    heads = range(RET_HEADS)
    scores = [lax.dot_general(q_ref[:, qk(h)], k_ref[:, qk(h)], _NT, preferred_element_type=F32) for h in heads]
    cross = [jnp.dot(qd_ref[:, qk(h)], s_ref[h].astype(BF16), preferred_element_type=F32) for h in heads]
    kvs = [lax.dot_general(kd_ref[:, qk(h)], v_ref[:, vv(h)], (((0,), (0,)), ((), ())),
                           preferred_element_type=F32) for h in heads]
    for h in heads:
        s_ref[h] = cdec_ref[h] * s_ref[h] + kvs[h]
    for h in heads:
        o = jnp.dot((scores[h] * decay_ref[h]).astype(BF16), v_ref[:, vv(h)], preferred_element_type=F32) + cross[h]
        ret_ref[:, vv(h)] = _group_norm_gate(o, gnw_ref[:, vv(h)], rg_ref[:, vv(h)]).astype(ret_ref.dtype)


def _ret_prompt(pr, decay, cdec, gn_w, casts):
    nc = SEQ // RET_CHUNK
    qk = pl.BlockSpec((RET_CHUNK, RET_QK_W), lambda b, c: (b * nc + c, 0))
    vv = pl.BlockSpec((RET_CHUNK, RET_V_W), lambda b, c: (b * nc + c, 0))
    cast_specs, cast_shapes = zip(*[_cast_slab_specs(w32, BATCH * nc, lambda b, c: b * nc + c) for w32 in casts])
    return pl.pallas_call(
        functools.partial(_ret_prompt_kernel, n_cast=len(casts)),
        grid=(BATCH, nc),
        in_specs=[qk, qk, qk, qk, vv, vv,
                  pl.BlockSpec((RET_HEADS, RET_CHUNK, RET_CHUNK), lambda b, c: (0, 0, 0)),
                  pl.BlockSpec((RET_HEADS, 1, RET_DV), lambda b, c: (0, 0, 0)),
                  pl.BlockSpec((1, RET_V_W), lambda b, c: (0, 0)), *cast_specs],
        out_specs=[vv, pl.BlockSpec((None, RET_HEADS, RET_DK, RET_DV), lambda b, c: (b, 0, 0, 0)), *cast_specs],
        out_shape=[jax.ShapeDtypeStruct((BATCH * SEQ, RET_V_W), BF16),
                   jax.ShapeDtypeStruct((BATCH, RET_HEADS, RET_DK, RET_DV), F32), *cast_shapes],
        compiler_params=_params(("parallel", "arbitrary")),
        name="retention_prompt",
    )(pr["rq"], pr["rqd"], pr["rk"], pr["rkd"], pr["rv"], pr["gates"], decay, cdec, gn_w.reshape(1, RET_V_W), *casts)


def _ret_sample_kernel(q_ref, qd_ref, kt_ref, k_ref, v_ref, rg_ref, cdec_ref, gnw_ref, s_ref, ret_ref, so_ref):
    b = pl.program_id(0)
    nb = kt_ref.shape[1]
    onehot = lax.broadcasted_iota(I32, (1, nb), 1) == b
    for h in range(RET_HEADS):
        qk_sl = slice(h * RET_DK, (h + 1) * RET_DK)
        vv = slice(h * RET_DV, (h + 1) * RET_DV)
        s = s_ref[h]
        v = v_ref[pl.ds(b, 1), vv]
        q = q_ref[pl.ds(b, 1), qk_sl]
        k = k_ref[pl.ds(b, 1), qk_sl]
        qk = jnp.sum(q.astype(BF16).astype(F32) * k.astype(BF16).astype(F32), axis=-1, keepdims=True)
        qd8 = jnp.broadcast_to(qd_ref[pl.ds(b, 1), qk_sl], (16, RET_DK)).astype(BF16)
        o = qk.astype(BF16).astype(F32) * v.astype(BF16).astype(F32) \
            + jnp.dot(qd8, s.astype(BF16), preferred_element_type=F32)[0:1]
        k_col = jnp.sum(jnp.where(onehot, kt_ref[qk_sl, :], 0.0), axis=1, keepdims=True)
        so_ref[h] = cdec_ref[h] * s + k_col * v
        ret_ref[:, vv] = _group_norm_gate(o, gnw_ref[:, vv], rg_ref[pl.ds(b, 1), vv])


def _ret_sample(ps, state, cdec, gn_w):
    nb = DEC_BATCH
    qk = pl.BlockSpec((nb, RET_QK_W), lambda b: (0, 0))
    vv = pl.BlockSpec((nb, RET_V_W), lambda b: (0, 0))
    st = pl.BlockSpec((None, RET_HEADS, RET_DK, RET_DV), lambda b: (b, 0, 0, 0))
    return pl.pallas_call(
        _ret_sample_kernel,
        grid=(nb,),
        in_specs=[qk, qk, pl.BlockSpec((RET_QK_W, nb), lambda b: (0, 0)), qk, vv, vv,
                  pl.BlockSpec((RET_HEADS, 1, RET_DV), lambda b: (0, 0, 0)),
                  pl.BlockSpec((1, RET_V_W), lambda b: (0, 0)), st],
        out_specs=[pl.BlockSpec((None, 1, RET_V_W), lambda b: (b, 0, 0)), st],
        out_shape=[jax.ShapeDtypeStruct((nb, 1, RET_V_W), F32),
                   jax.ShapeDtypeStruct((nb, RET_HEADS, RET_DK, RET_DV), F32)],
        compiler_params=_params(("arbitrary",)),
        name="retention_sample",
    )(ps["rq"], ps["rqd"], ps["rkd"].T, ps["rk"], ps["rv"], ps["gates"], cdec, gn_w.reshape(1, RET_V_W), state)


PAGES_PER_STEP = 8
IDX_PAGES_PER_STEP = 32


def _idx_sample_kernel(pt_ref, iq_ref, w_ref, new_ref, *rest):
    pages = rest[:IDX_PAGES_PER_STEP]
    sc_ref, scn_ref = rest[IDX_PAGES_PER_STEP], rest[IDX_PAGES_PER_STEP + 1]
    j = pl.program_id(1)
    hi, lo = _split_bf16(iq_ref[...])
    lhs = jnp.concatenate([hi + pltpu.roll(hi, IDX_DIM, 1), lo], axis=1).astype(BF16)
    w = w_ref[...] * IDX_SCALE

    def page_scores(pages_t):
        xt = jnp.concatenate(pages_t, axis=1) if len(pages_t) > 1 else pages_t[0]
        khi, klo = _split_bf16(xt)
        rhs = jnp.concatenate([khi, klo, khi, jnp.zeros_like(khi)], axis=0).astype(BF16)
        d = jnp.dot(lhs, rhs, preferred_element_type=F32)
        return jnp.sum(w * jnp.maximum(d, 0.0), axis=0, keepdims=True)

    for i in range(0, IDX_PAGES_PER_STEP, 2):
        sc_ref[:, i * PAGE_SIZE:(i + 2) * PAGE_SIZE] = page_scores([pages[i][...], pages[i + 1][...]])

    @pl.when(j == pl.num_programs(1) - 1)
    def _():
        scn_ref[...] = page_scores([new_ref[...]])


def _idx_sample(page_table, iq3, w3, new_pages, cache_idx):
    nsteps = N_PAGES // IDX_PAGES_PER_STEP

    def page_spec(i):
        return pl.BlockSpec((None, IDX_DIM, PAGE_SIZE),
                            lambda b, j, pt: (pt[b, j * IDX_PAGES_PER_STEP + i], 0, 0))

    grid_spec = pltpu.PrefetchScalarGridSpec(
        num_scalar_prefetch=1,
        grid=(DEC_BATCH, nsteps),
        in_specs=[pl.BlockSpec((None, IDX_HEADS, LANES), lambda b, j, pt: (b, 0, 0)),
                  pl.BlockSpec((None, IDX_HEADS, 1), lambda b, j, pt: (b, 0, 0)),
                  pl.BlockSpec((None, IDX_DIM, PAGE_SIZE), lambda b, j, pt: (b, 0, 0))]
                 + [page_spec(i) for i in range(IDX_PAGES_PER_STEP)],
        out_specs=[pl.BlockSpec((None, 1, IDX_PAGES_PER_STEP * PAGE_SIZE), lambda b, j, pt: (b, 0, j)),
                   pl.BlockSpec((None, 1, PAGE_SIZE), lambda b, j, pt: (b, 0, 0))],
    )
    return pl.pallas_call(
        _idx_sample_kernel,
        grid_spec=grid_spec,
        out_shape=[jax.ShapeDtypeStruct((DEC_BATCH, 1, PAST_LEN), F32),
                   jax.ShapeDtypeStruct((DEC_BATCH, 1, PAGE_SIZE), F32)],
        compiler_params=_params(("parallel", "arbitrary")),
        name="indexer_sample",
    )(page_table, iq3, w3, new_pages, *([cache_idx] * IDX_PAGES_PER_STEP))


def _select_sample_kernel(sc_ref, bias_ref, key_ref):
    _topk_mask(sc_ref, key_ref, bias_ref, jnp.full((1, LANES), PAST_LEN, I32), sc_ref.shape[0] // KEY_CHUNK)


def _select_sample(scores_t):
    return pl.pallas_call(
        _select_sample_kernel,
        out_shape=jax.ShapeDtypeStruct(scores_t.shape, F32),
        scratch_shapes=[pltpu.VMEM(scores_t.shape, I32)],
        compiler_params=pltpu.CompilerParams(vmem_limit_bytes=VMEM_LIMIT),
        name="select_sample",
    )(scores_t)


PAGE_ROWS = PAGE_SIZE * ATT_KV_HEADS


def _attn_sample_step(j, n_steps, q_ref, bias_ref, biasn_ref, kn_ref, vn_ref, kp, vp, o_ref, m_ref, l_ref, acc_ref,
                      host_work):
    col = lax.broadcasted_iota(I32, (ATT_HEADS, PAGE_ROWS), 1)
    head = lax.broadcasted_iota(I32, (ATT_HEADS, PAGE_ROWS), 0)
    own = (col % ATT_KV_HEADS) == (head // (ATT_HEADS // ATT_KV_HEADS))
    c_exp = ATT_SCALE * math.log2(math.e)
    q = q_ref[...].astype(BF16)

    @pl.when(j == 0)
    def _():
        m_ref[...] = jnp.full_like(m_ref, NEG_BIG)
        l_ref[...] = jnp.zeros_like(l_ref)
        acc_ref[...] = jnp.zeros_like(acc_ref)

    half = PAGE_ROWS // 2
    zq = jnp.zeros_like(q)
    q2 = jnp.concatenate([jnp.concatenate([q, zq], axis=1), jnp.concatenate([zq, q], axis=1)], axis=0)

    def halves_side_by_side(page_ref):
        x = page_ref[...]
        return jnp.concatenate([x[:half], x[half:]], axis=1).astype(BF16)

    logits = []
    for i in range(PAGES_PER_STEP):
        lg2 = lax.dot_general(q2, halves_side_by_side(kp[i]), _NT, preferred_element_type=F32)
        lg = jnp.concatenate([lg2[:ATT_HEADS], lg2[ATT_HEADS:]], axis=1)
        logits.append(jnp.where(own, lg + bias_ref[:, i * PAGE_ROWS:(i + 1) * PAGE_ROWS], NEG_BIG))
    host_work()
    m_old = m_ref[...]
    m_new = m_old
    for lg in logits:
        m_new = jnp.maximum(m_new, jnp.max(lg, axis=1, keepdims=True))
    alpha = jnp.exp2((m_old - m_new) * c_exp)
    l_new = alpha * l_ref[...]
    acc = alpha * acc_ref[...]
    for i in range(PAGES_PER_STEP):
        p = jnp.exp2((logits[i] - m_new) * c_exp)
        l_new = l_new + jnp.sum(p, axis=1, keepdims=True)
        p2 = jnp.concatenate([p[:, :half], p[:, half:]], axis=0).astype(BF16)
        o2 = jnp.dot(p2, halves_side_by_side(vp[i]), preferred_element_type=F32)
        acc = acc + o2[:ATT_HEADS, :ATT_HEAD_DIM] + o2[ATT_HEADS:, ATT_HEAD_DIM:]
    m_ref[...] = m_new
    l_ref[...] = l_new
    acc_ref[...] = acc

    @pl.when(j == n_steps - 1)
    def _():
        kn = kn_ref[...].astype(BF16).astype(F32)
        vn = vn_ref[...].astype(BF16).astype(F32)
        lgn = jnp.sum(q.astype(F32) * kn, axis=1, keepdims=True) + biasn_ref[:, 0:1]
        m_f = jnp.maximum(m_new, lgn)
        a = jnp.exp2((m_new - m_f) * c_exp)
        pn = jnp.exp2((lgn - m_f) * c_exp)
        o_ref[...] = (a * acc + pn.astype(BF16).astype(F32) * vn) / (a * l_new + pn)


MLP_UP_TM, MLP_UP_TN = 1024, 512


def _mlp_up_attn_kernel(pt_ref, hn_ref, wup_ref, q_ref, bias_ref, biasn_ref, kn_ref, vn_ref, *rest):
    kp = rest[:PAGES_PER_STEP]
    vp = rest[PAGES_PER_STEP:2 * PAGES_PER_STEP]
    u_ref, o_ref = rest[2 * PAGES_PER_STEP], rest[2 * PAGES_PER_STEP + 1]
    m_ref, l_ref, acc_ref = rest[2 * PAGES_PER_STEP + 2:]
    n_pg = N_PAGES // PAGES_PER_STEP
    _attn_sample_step(
        pl.program_id(1) % n_pg, n_pg, q_ref, bias_ref, biasn_ref, kn_ref, vn_ref, kp, vp, o_ref, m_ref, l_ref, acc_ref,
        host_work=lambda: _ep_relu2(jnp.dot(hn_ref[...], wup_ref[...], preferred_element_type=F32), (), (u_ref,)))


def _mlp_up_with_sample_attn(hn, w_up, page_table, q3, bias_rows, bias_new, k_new, v_new, cache_k, cache_v):
    m, kd = hn.shape
    n_pg = N_PAGES // PAGES_PER_STEP
    n_row, n_col = m // MLP_UP_TM, D_FF // MLP_UP_TN
    assert n_row * n_col == DEC_BATCH * n_pg and n_col % n_pg == 0

    def row(i, j):
        return (i * n_col + j) // n_pg

    def page_spec(p):
        return pl.BlockSpec((None, PAGE_ROWS, ATT_HEAD_DIM),
                            lambda i, j, pt: (pt[row(i, j), (j % n_pg) * PAGES_PER_STEP + p], 0, 0))

    head_rows = pl.BlockSpec((None, ATT_HEADS, ATT_HEAD_DIM), lambda i, j, pt: (row(i, j), 0, 0))
    grid_spec = pltpu.PrefetchScalarGridSpec(
        num_scalar_prefetch=1,
        grid=(n_row, n_col),
        in_specs=[pl.BlockSpec((MLP_UP_TM, kd), lambda i, j, pt: (i, 0)),
                  pl.BlockSpec((kd, MLP_UP_TN), lambda i, j, pt: (0, j)),
                  head_rows,
                  pl.BlockSpec((None, 1, PAGES_PER_STEP * PAGE_ROWS), lambda i, j, pt: (row(i, j), 0, j % n_pg)),
                  pl.BlockSpec((None, 1, PAGE_SIZE), lambda i, j, pt: (row(i, j), 0, 0)),
                  head_rows, head_rows]
                 + [page_spec(p) for p in range(PAGES_PER_STEP)] * 2,
        out_specs=[pl.BlockSpec((MLP_UP_TM, MLP_UP_TN), lambda i, j, pt: (i, j)), head_rows],
        scratch_shapes=[pltpu.VMEM((ATT_HEADS, 1), F32), pltpu.VMEM((ATT_HEADS, 1), F32),
                        pltpu.VMEM((ATT_HEADS, ATT_HEAD_DIM), F32)],
    )
    return pl.pallas_call(
        _mlp_up_attn_kernel,
        grid_spec=grid_spec,
        out_shape=[jax.ShapeDtypeStruct((m, D_FF), BF16),
                   jax.ShapeDtypeStruct((DEC_BATCH, ATT_HEADS, ATT_HEAD_DIM), F32)],
        compiler_params=_params(("arbitrary", "arbitrary")),
        name="mlp_up_attention_sample",
    )(page_table, hn, w_up, q3, bias_rows, bias_new, k_new, v_new,
      *([cache_k] * PAGES_PER_STEP), *([cache_v] * PAGES_PER_STEP))


def _finish(x, att, ret, gates, w, *, tm, tr, sample_attn=None):
    m = x.shape[0]
    mg = _merge(att, ret, w["att_proj"], w["ret_proj"], gates, tm, 512)
    y_out = _nat(m, D_MODEL, F32, tm, 512)
    if w["mlp_up"].dtype == F32:
        n_col = D_MODEL // 512
        slab = w["mlp_up"].shape[0] // (m // tm * n_col)
        slab_map = lambda i, j: (i * n_col + j, 0)

        def ep_w_out(acc, extra, outs):
            outs[0][...] = acc
            outs[1][...] = extra[0][...].astype(BF16)

        y, w["mlp_up"] = _matmul(mg, w["out"], ep_w_out, [(w["mlp_up"], (slab, D_FF), slab_map)],
                                 [y_out, (w["mlp_up"].shape, BF16, (slab, D_FF), slab_map)],
                                 tm=tm, tn=512, name="w_out")
    else:
        y, = _matmul(mg, w["out"], _ep_plain, [], [y_out], tm=tm, tn=512, name="w_out")
    h, hn = _post_attn(x, y, w["n_attn_post"], w["n_mlp_pre"], tr)
    if sample_attn is None:
        u, = _matmul(hn, w["mlp_up"], _ep_relu2, [], [_nat(m, D_FF, BF16, tm, 512)], tm=tm, tn=512, name="mlp_up")
        att_s = None
    else:
        u, att_s = _mlp_up_with_sample_attn(hn, w["mlp_up"], *sample_attn)
    d, = _matmul(u, w["mlp_down"], _ep_plain, [], [_nat(m, D_MODEL, F32, tm, 512)], tm=tm, tn=512, tk=4096,
                 name="mlp_down")
    out = _post_mlp(h, d, w["n_mlp_post"], tr)
    return out if sample_attn is None else (out, att_s)


def kernel(x_prompt, x_sample, cache_k, cache_v, cache_idx_k, state_ret, page_table, norm_attn_pre,
           norm_attn_post, w_in, ret_gn_w, w_att_proj, w_ret_proj, w_out, norm_mlp_pre, w_mlp_up,
           w_mlp_down, norm_mlp_post):
    log_gamma = jnp.log1p(-jnp.exp2(-5.0 - jnp.arange(RET_HEADS, dtype=F32)))
    w_in0 = w_in[0].T
    w = {"n_attn_post": norm_attn_post[0], "n_mlp_pre": norm_mlp_pre[0], "n_mlp_post": norm_mlp_post[0]}
    gn_w = ret_gn_w[0]

    m_p = BATCH * SEQ
    xp = x_prompt.reshape(m_p, D_MODEL)
    xn = _rmsnorm_cast(xp, norm_attn_pre[0], 256)
    pos_p = jnp.arange(SEQ, dtype=I32)
    pr = _project(xn, w_in0, pos_p, pos_p % RET_CHUNK, float(RET_CHUNK), log_gamma, tm=1024, prompt=True)
    vt = pr["v_bf"].reshape(BATCH, SEQ // KEY_CHUNK, KEY_CHUNK, ATT_KV_HEADS, ATT_HEAD_DIM).transpose(0, 3, 1, 4, 2)
    vt = jnp.concatenate([vt, jnp.ones(vt.shape[:3] + (VT_ONES, KEY_CHUNK), BF16)], axis=3)
    att4, w["mlp_down"] = _dsa_prompt(pr["iq"], pr["ikw"], pr["q"], pr["k_heads"], vt, (w_mlp_down[0],))
    att = att4.reshape(m_p // Q_BLOCK, ATT_HEADS, Q_BLOCK, ATT_HEAD_DIM).transpose(0, 2, 1, 3).reshape(m_p, ATT_Q_W)
    ci = jnp.arange(RET_CHUNK, dtype=F32)
    diff = ci[:, None] - ci[None, :]
    decay = jnp.where(diff >= 0, jnp.exp(log_gamma[:, None, None] * jnp.maximum(diff, 0.0)), 0.0)
    cdec_p = jnp.broadcast_to(jnp.exp(log_gamma * RET_CHUNK)[:, None, None], (RET_HEADS, 1, RET_DV))
    ret, s_prompt, w["out"], w["ret_proj"], w["att_proj"] = _ret_prompt(
        pr, decay, cdec_p, gn_w, (w_out[0], w_ret_proj[0], w_att_proj[0]))
    w["mlp_up"] = w_mlp_up[0]

    nb = DEC_BATCH
    xs = x_sample.reshape(nb, D_MODEL)
    xns = _rmsnorm_cast(xs, norm_attn_pre[0], nb)
    pos_s = jnp.full((nb,), PAST_LEN, I32)
    ps = _project(xns, pr["w_bf16"], pos_s, jnp.zeros((nb,), I32), 1.0, log_gamma, tm=nb, prompt=False)
    ik_new = ps["ikw"][:, :IDX_DIM]
    iq3 = jnp.pad(ps["iq"].reshape(nb, IDX_HEADS, IDX_DIM), ((0, 0), (0, 0), (0, LANES - IDX_DIM)))
    w3 = ps["ikw"][:, IDX_DIM:IDX_DIM + IDX_HEADS].reshape(nb, IDX_HEADS, 1)
    new_pages = jnp.pad(ik_new[:, :, None], ((0, 0), (0, 0), (0, PAGE_SIZE - 1)))
    sc_past, sc_new = _idx_sample(page_table, iq3, w3, new_pages, cache_idx_k[0].transpose(0, 2, 1))
    scores = jnp.concatenate([sc_past.reshape(nb, PAST_LEN), sc_new.reshape(nb, PAGE_SIZE)], axis=1)
    n_rows = -(-(PAST_LEN + PAGE_SIZE) // KEY_CHUNK) * KEY_CHUNK
    scores_t = jnp.pad(scores.T, ((0, n_rows - PAST_LEN - PAGE_SIZE), (0, LANES - nb)))
    bias = _select_sample(scores_t)[:PAST_LEN + PAGE_SIZE, :nb].T
    n_phys = cache_k.shape[1]
    group = ATT_HEADS // ATT_KV_HEADS
    sample_attn = (page_table, ps["q"].reshape(nb, ATT_HEADS, ATT_HEAD_DIM),
                   jnp.repeat(bias[:, :PAST_LEN], ATT_KV_HEADS, axis=1).reshape(nb, 1, N_PAGES * PAGE_ROWS),
                   bias[:, PAST_LEN:].reshape(nb, 1, PAGE_SIZE),
                   jnp.repeat(ps["k"].reshape(nb, ATT_KV_HEADS, ATT_HEAD_DIM), group, axis=1),
                   jnp.repeat(ps["v"].reshape(nb, ATT_KV_HEADS, ATT_HEAD_DIM), group, axis=1),
                   cache_k[0].reshape(n_phys, PAGE_ROWS, ATT_HEAD_DIM),
                   cache_v[0].reshape(n_phys, PAGE_ROWS, ATT_HEAD_DIM))

    y_prompt, att_s = _finish(xp, att, ret, pr["gates"], w, tm=1024, tr=256, sample_attn=sample_attn)
    y_prompt = y_prompt.reshape(BATCH, SEQ, D_MODEL)
    cdec_s = jnp.broadcast_to(jnp.exp(log_gamma)[:, None, None], (RET_HEADS, 1, RET_DV))
    ret_s, s_sample = _ret_sample(ps, state_ret[0], cdec_s, gn_w)
    y_sample = _finish(xs, att_s.reshape(nb, ATT_Q_W).astype(BF16), ret_s.reshape(nb, RET_V_W).astype(BF16),
                       ps["gates"], w, tm=nb, tr=nb).reshape(nb, 1, D_MODEL)

    return (y_prompt, y_sample,
            pr["k"].reshape(1, BATCH, SEQ, ATT_KV_HEADS, ATT_HEAD_DIM),
            pr["v"].reshape(1, BATCH, SEQ, ATT_KV_HEADS, ATT_HEAD_DIM),
            pr["ikw"][:, :IDX_DIM].reshape(1, BATCH, SEQ, IDX_DIM),
            s_prompt[None],
            ps["k"].reshape(1, nb, 1, ATT_KV_HEADS, ATT_HEAD_DIM),
            ps["v"].reshape(1, nb, 1, ATT_KV_HEADS, ATT_HEAD_DIM),
            ik_new.reshape(1, nb, 1, IDX_DIM),
            s_sample[None])
```

```python
import functools
import math

import jax
import jax.numpy as jnp
import numpy as np
from jax import lax
from jax.experimental import pallas as pl
from jax.experimental.pallas import tpu as pltpu

F32 = jnp.float32
BF16 = jnp.bfloat16
I32 = jnp.int32

D_MODEL = 4096
BATCH = 4
SEQ = 2048
DEC_BATCH = 32
PAST_LEN = 8192
PAGE_SIZE = 128
N_PAGES = PAST_LEN // PAGE_SIZE
ATT_HEADS = 16
ATT_KV_HEADS = 8
ATT_HEAD_DIM = 128
ROPE_THETA = 500000.0
IDX_HEADS = 32
IDX_DIM = 64
TOPK = 256
RET_HEADS = 8
RET_DK = 256
RET_DV = 512
RET_THETA = 10000.0
RET_CHUNK = 128
D_FF = 4 * D_MODEL
NORM_EPS = 1e-6

ATT_Q_W = ATT_HEADS * ATT_HEAD_DIM
ATT_KV_W = ATT_KV_HEADS * ATT_HEAD_DIM
IDX_Q_W = IDX_HEADS * IDX_DIM
RET_QK_W = RET_HEADS * RET_DK
RET_V_W = RET_HEADS * RET_DV
SPLITS = (ATT_Q_W, ATT_KV_W, ATT_KV_W, IDX_Q_W, IDX_DIM, IDX_HEADS,
          RET_QK_W, RET_QK_W, RET_V_W, RET_V_W, D_MODEL, D_MODEL)
OFFS = tuple(int(v) for v in np.concatenate([[0], np.cumsum(SPLITS)]))

LANES = 128
SUBLANES = 8
Q_BLOCK = 128
KEY_CHUNK = 512
PAIRS_PER_STEP = 16
GROUPS_PER_STEP = 8
VT_ONES = 16
VMEM_LIMIT = 56 * 1024 * 1024

INT_MIN = -2 ** 31
NEG_BIG = -1e30
ATT_SCALE = ATT_HEAD_DIM ** -0.5
IDX_SCALE = (IDX_DIM ** -0.5) * (IDX_HEADS ** -0.5)

_NT = (((1,), (1,)), ((), ()))


def _params(sem):
    return pltpu.CompilerParams(dimension_semantics=sem, vmem_limit_bytes=VMEM_LIMIT)


def _sigmoid(x):
    return 1.0 / (1.0 + jnp.exp(-x))


def _rmsnorm_cast_kernel(x_ref, w_ref, o_ref):
    x = x_ref[...]
    y = x * lax.rsqrt(jnp.mean(x * x, axis=-1, keepdims=True) + NORM_EPS)
    o_ref[...] = (y * w_ref[...]).astype(o_ref.dtype)


def _rmsnorm_cast(x, w, tr):
    m, d = x.shape
    return pl.pallas_call(
        _rmsnorm_cast_kernel,
        grid=(m // tr,),
        in_specs=[pl.BlockSpec((tr, d), lambda i: (i, 0)), pl.BlockSpec((1, d), lambda i: (0, 0))],
        out_specs=pl.BlockSpec((tr, d), lambda i: (i, 0)),
        out_shape=jax.ShapeDtypeStruct((m, d), BF16),
        compiler_params=_params(("parallel",)),
        name="rmsnorm_cast",
    )(x, w.reshape(1, d))


def _post_attn_kernel(x_ref, y_ref, w1_ref, w2_ref, h_ref, hn_ref):
    y = y_ref[...]
    yn = y * lax.rsqrt(jnp.mean(y * y, axis=-1, keepdims=True) + NORM_EPS) * w1_ref[...]
    h = x_ref[...] + yn
    h_ref[...] = h
    hn = h * lax.rsqrt(jnp.mean(h * h, axis=-1, keepdims=True) + NORM_EPS) * w2_ref[...]
    hn_ref[...] = hn.astype(hn_ref.dtype)


def _post_attn(x, y, w1, w2, tr):
    m, d = x.shape
    row = pl.BlockSpec((tr, d), lambda i: (i, 0))
    vec = pl.BlockSpec((1, d), lambda i: (0, 0))
    return pl.pallas_call(
        _post_attn_kernel,
        grid=(m // tr,),
        in_specs=[row, row, vec, vec],
        out_specs=[row, row],
        out_shape=[jax.ShapeDtypeStruct((m, d), F32), jax.ShapeDtypeStruct((m, d), BF16)],
        compiler_params=_params(("parallel",)),
        name="post_attn_norm",
    )(x, y, w1.reshape(1, d), w2.reshape(1, d))


def _post_mlp_kernel(h_ref, d_ref, w_ref, o_ref):
    d = d_ref[...]
    dn = d * lax.rsqrt(jnp.mean(d * d, axis=-1, keepdims=True) + NORM_EPS) * w_ref[...]
    o_ref[...] = h_ref[...] + dn


def _post_mlp(h, d, w, tr):
    m, dm = h.shape
    row = pl.BlockSpec((tr, dm), lambda i: (i, 0))
    return pl.pallas_call(
        _post_mlp_kernel,
        grid=(m // tr,),
        in_specs=[row, row, pl.BlockSpec((1, dm), lambda i: (0, 0))],
        out_specs=row,
        out_shape=jax.ShapeDtypeStruct((m, dm), F32),
        compiler_params=_params(("parallel",)),
        name="post_mlp_norm",
    )(h, d, w.reshape(1, dm))


def _dot(a, b, b_t):
    if b_t:
        return lax.dot_general(a, b, _NT, preferred_element_type=F32)
    return jnp.dot(a, b, preferred_element_type=F32)


def _mm_kernel(*refs, n_extra, n_out, nk, b_t, epilogue):
    a_ref, b_ref = refs[0], refs[1]
    extra = refs[2:2 + n_extra]
    outs = refs[2 + n_extra:2 + n_extra + n_out]
    if nk == 1:
        epilogue(_dot(a_ref[...], b_ref[...], b_t), extra, outs)
        return
    acc_ref = refs[-1]
    k = pl.program_id(2)

    @pl.when(k == 0)
    def _():
        acc_ref[...] = _dot(a_ref[...], b_ref[...], b_t)

    @pl.when((k > 0) & (k < nk - 1))
    def _():
        acc_ref[...] += _dot(a_ref[...], b_ref[...], b_t)

    @pl.when(k == nk - 1)
    def _():
        epilogue(acc_ref[...] + _dot(a_ref[...], b_ref[...], b_t), extra, outs)


def _matmul(a, b, epilogue, extras, outs, *, tm, tn, tk=None, b_t=False, name):
    m, kd = a.shape
    n = b.shape[0] if b_t else b.shape[1]
    tk = kd if tk is None else tk
    nk = kd // tk
    grid = (m // tm, n // tn, nk)

    def lift(f):
        return lambda i, j, k: f(i, j)

    b_spec = (pl.BlockSpec((tn, tk), lambda i, j, k: (j, k)) if b_t
              else pl.BlockSpec((tk, tn), lambda i, j, k: (k, j)))
    in_specs = [pl.BlockSpec((tm, tk), lambda i, j, k: (i, k)), b_spec]
    in_specs += [pl.BlockSpec(bs, lift(im)) for _, bs, im in extras]
    out_specs = [pl.BlockSpec(bs, lift(im)) for _, _, bs, im in outs]
    out_shape = [jax.ShapeDtypeStruct(s, dt) for s, dt, _, _ in outs]
    scratch = [pltpu.VMEM((tm, tn), F32)] if nk > 1 else []
    res = pl.pallas_call(
        functools.partial(_mm_kernel, n_extra=len(extras), n_out=len(outs), nk=nk, b_t=b_t, epilogue=epilogue),
        grid=grid,
        in_specs=in_specs,
        out_specs=out_specs,
        out_shape=out_shape,
        scratch_shapes=scratch,
        compiler_params=_params(("parallel", "parallel", "arbitrary")),
        name=name,
    )(a, b, *[e[0] for e in extras])
    return res


def _mm_w32_kernel(*refs, n_extra, n_out, valid, epilogue):
    a_ref, w_ref = refs[0], refs[1]
    extra = refs[2:2 + n_extra]
    outs = refs[2 + n_extra:2 + n_extra + n_out]
    wbf_ref = refs[2 + n_extra + n_out]

    @pl.when(pl.program_id(1) == 0)
    def _():
        w = w_ref[...]
        if valid < w.shape[0]:
            w = jnp.where(lax.broadcasted_iota(I32, w.shape, 0) < valid, w, 0.0)
        wbf_ref[...] = w.astype(BF16)

    epilogue(_dot(a_ref[...], wbf_ref[...], True), extra, outs)


def _matmul_w32(a, w_t, epilogue, extras, outs, *, tm, tn, n, col_off, valid=None, name):
    m, kd = a.shape
    assert n % tn == 0 and col_off % SUBLANES == 0
    valid = tn if valid is None else valid
    grid = (n // tn, m // tm)

    def lift(f):
        return lambda j, i: f(i, j)

    w_spec = pl.BlockSpec((pl.Element(tn), pl.Element(kd)),
                          lambda j, i: (pl.multiple_of(col_off + j * tn, SUBLANES), 0))
    in_specs = [pl.BlockSpec((tm, kd), lambda j, i: (i, 0)), w_spec]
    in_specs += [pl.BlockSpec(bs, lift(im)) for _, bs, im in extras]
    out_specs = [pl.BlockSpec(bs, lift(im)) for _, _, bs, im in outs] + [pl.BlockSpec((tn, kd), lambda j, i: (j, 0))]
    out_shape = [jax.ShapeDtypeStruct(s, dt) for s, dt, _, _ in outs] + [jax.ShapeDtypeStruct((n, kd), BF16)]
    return pl.pallas_call(
        functools.partial(_mm_w32_kernel, n_extra=len(extras), n_out=len(outs), valid=valid, epilogue=epilogue),
        grid=grid,
        in_specs=in_specs,
        out_specs=out_specs,
        out_shape=out_shape,
        compiler_params=_params(("arbitrary", "arbitrary")),
        name=name,
    )(a, w_t, *[e[0] for e in extras])


def _nat(m, n, dt, tm, tn):
    return ((m, n), dt, (tm, tn), lambda i, j: (i, j))


def _rope_lanes(y, c, sm, sp, half):
    n = y.shape[-1]
    return y * c + pltpu.roll(y, n - half, 1) * sm + pltpu.roll(y, half, 1) * sp


def _ep_plain(acc, extra, outs):
    for o in outs:
        o[...] = acc.astype(o.dtype)


def _ep_rope_lanes(acc, extra, outs, *, half, blocked):
    c, sm, sp = extra[0][...], extra[1][...], extra[2][...]
    tm, tn = acc.shape
    for jj in range(tn // LANES):
        y = _rope_lanes(acc[:, jj * LANES:(jj + 1) * LANES], c, sm, sp, half)
        if blocked:
            for r in range(tm // Q_BLOCK):
                outs[0][r, jj] = y[r * Q_BLOCK:(r + 1) * Q_BLOCK].astype(outs[0].dtype)
        else:
            outs[0][:, jj * LANES:(jj + 1) * LANES] = y.astype(outs[0].dtype)
        if len(outs) > 1:
            outs[1][jj] = y.astype(outs[1].dtype)


def _ep_ret_qk(acc, extra, outs, *, scale):
    cos, sin, dec = extra[0][...], extra[1][...], extra[2]
    tn = acc.shape[1]
    for hh in range(tn // RET_DK):
        lo = hh * RET_DK
        x1 = acc[:, lo:lo + LANES]
        x2 = acc[:, lo + LANES:lo + 2 * LANES]
        o1 = x1 * cos - x2 * sin
        o2 = x2 * cos + x1 * sin
        if scale != 1.0:
            o1 = o1 * scale
            o2 = o2 * scale
        d = dec[:, hh * LANES:(hh + 1) * LANES]
        outs[0][:, lo:lo + LANES] = o1.astype(outs[0].dtype)
        outs[0][:, lo + LANES:lo + 2 * LANES] = o2.astype(outs[0].dtype)
        outs[1][:, lo:lo + LANES] = (o1 * d).astype(outs[1].dtype)
        outs[1][:, lo + LANES:lo + 2 * LANES] = (o2 * d).astype(outs[1].dtype)


def _merge_kernel(att_ref, ret_ref, wa_ref, wr_ref, ga_ref, gr_ref, o_ref):
    a = jnp.dot(att_ref[...], wa_ref[...], preferred_element_type=F32)
    r = jnp.dot(ret_ref[...], wr_ref[...], preferred_element_type=F32)
    o_ref[...] = (_sigmoid(ga_ref[...]) * a + _sigmoid(gr_ref[...]) * r).astype(o_ref.dtype)


def _merge(att, ret, wa, wr, gates, tm, tn):
    m = att.shape[0]
    nb = D_MODEL // tn
    return pl.pallas_call(
        _merge_kernel,
        grid=(m // tm, nb),
        in_specs=[
            pl.BlockSpec((tm, ATT_Q_W), lambda i, j: (i, 0)),
            pl.BlockSpec((tm, RET_V_W), lambda i, j: (i, 0)),
            pl.BlockSpec((ATT_Q_W, tn), lambda i, j: (0, j)),
            pl.BlockSpec((RET_V_W, tn), lambda i, j: (0, j)),
            pl.BlockSpec((tm, tn), lambda i, j: (i, nb + j)),
            pl.BlockSpec((tm, tn), lambda i, j: (i, 2 * nb + j)),
        ],
        out_specs=pl.BlockSpec((tm, tn), lambda i, j: (i, j)),
        out_shape=jax.ShapeDtypeStruct((m, D_MODEL), BF16),
        compiler_params=_params(("parallel", "parallel")),
        name="merge_proj",
    )(att, ret, wa, wr, gates, gates)


def _ep_relu2(acc, extra, outs):
    u = jnp.maximum(acc, 0.0)
    outs[0][...] = (u * u).astype(outs[0].dtype)


def _rope_lane_tables(pos, head_w, rot, theta):
    half = rot // 2
    inv_freq = jnp.exp(-math.log(theta) * jnp.arange(half, dtype=F32) / half)
    ang = pos.astype(F32)[:, None] * inv_freq[None, :]
    cos, sin = jnp.cos(ang), jnp.sin(ang)
    n = pos.shape[0]
    z_half = jnp.zeros((n, half), F32)
    rest1 = jnp.ones((n, head_w - rot), F32)
    rest0 = jnp.zeros((n, head_w - rot), F32)
    c = jnp.concatenate([cos, cos, rest1], axis=1)
    sm = jnp.concatenate([-sin, z_half, rest0], axis=1)
    sp = jnp.concatenate([z_half, sin, rest0], axis=1)
    rep = LANES // head_w
    return [jnp.tile(t, (1, rep)) for t in (c, sm, sp)]


def _ret_tables(pos, chunk_pos, chunk_len, log_gamma):
    half = RET_DK // 2
    inv_freq = jnp.exp(-math.log(RET_THETA) * jnp.arange(half, dtype=F32) / half)
    ang = pos.astype(F32)[:, None] * inv_freq[None, :]
    i = chunk_pos.astype(F32)[:, None]
    q_dec = jnp.exp(log_gamma[None, :] * (i + 1.0))
    k_dec = jnp.exp(log_gamma[None, :] * (chunk_len - 1.0 - i))
    return jnp.cos(ang), jnp.sin(ang), jnp.repeat(q_dec, LANES, axis=1), jnp.repeat(k_dec, LANES, axis=1)


PROJ_TN = 512


def _project(xn, w, pos, chunk_pos, chunk_len, log_gamma, *, tm, prompt):
    m = xn.shape[0]
    nrep = max(pos.shape[0] // tm, 1)
    wdt = BF16 if prompt else F32
    out = {"w_bf16": {}}

    def mm(name, a, b, epilogue, extras, outs, tn=PROJ_TN, valid_cols=None):
        n = -(-(OFFS[b] - OFFS[a]) // tn) * tn
        if prompt:
            *res, wb = _matmul_w32(xn, w, epilogue, extras, outs, tm=tm, tn=tn, n=n, col_off=OFFS[a],
                                   valid=valid_cols, name=name)
            out["w_bf16"][name] = wb
            return res
        return _matmul(xn, w[name], epilogue, extras, outs, tm=tm, tn=tn, b_t=True, name=name)

    def tab(t, width=LANES, by_col=False):
        if by_col:
            return (t, (tm, width), lambda i, j: (i % nrep, j))
        return (t, (tm, width), lambda i, j: (i % nrep, 0))

    att_t = [tab(t) for t in _rope_lane_tables(pos, ATT_HEAD_DIM, ATT_HEAD_DIM // 4, ROPE_THETA)]
    idx_t = [tab(t) for t in _rope_lane_tables(pos, IDX_DIM, IDX_DIM // 4, ROPE_THETA)]
    r_cos, r_sin, q_dec, k_dec = _ret_tables(pos, chunk_pos, chunk_len, log_gamma)
    tn = PROJ_TN
    nqb = m // Q_BLOCK

    if prompt:
        o = [((nqb, ATT_HEADS, Q_BLOCK, LANES), BF16, (tm // Q_BLOCK, tn // LANES, Q_BLOCK, LANES),
              lambda i, j: (i, j, 0, 0))]
    else:
        o = [_nat(m, ATT_Q_W, F32, tm, tn)]
    out["q"], = mm("proj_q", 0, 1, functools.partial(_ep_rope_lanes, half=16, blocked=prompt), att_t, o)
    o = [_nat(m, ATT_KV_W, F32, tm, tn)]
    if prompt:
        per_b = SEQ // tm
        o.append(((BATCH, ATT_KV_HEADS, SEQ, LANES), BF16, (None, tn // LANES, tm, LANES),
                  lambda i, j: (i // per_b, j, i % per_b, 0)))
    res = mm("proj_k", 1, 2, functools.partial(_ep_rope_lanes, half=16, blocked=False), att_t, o)
    out["k"] = res[0]
    if prompt:
        out["k_heads"] = res[1]
    o = [_nat(m, ATT_KV_W, F32, tm, tn)]
    if prompt:
        o.append(_nat(m, ATT_KV_W, BF16, tm, tn))
    res = mm("proj_v", 2, 3, _ep_plain, [], o)
    out["v"] = res[0]
    if prompt:
        out["v_bf"] = res[1]
    if prompt:
        o = [((nqb, IDX_Q_W // LANES, Q_BLOCK, LANES), F32, (tm // Q_BLOCK, tn // LANES, Q_BLOCK, LANES),
              lambda i, j: (i, j, 0, 0))]
    else:
        o = [_nat(m, IDX_Q_W, F32, tm, tn)]
    out["iq"], = mm("proj_iq", 3, 4, functools.partial(_ep_rope_lanes, half=8, blocked=prompt), idx_t, o)
    ikw_t = [tab(t) for t in _rope_lane_tables(pos, LANES, IDX_DIM // 4, ROPE_THETA)]
    out["ikw"], = mm("proj_ikw", 4, 6, functools.partial(_ep_rope_lanes, half=8, blocked=False), ikw_t,
                     [_nat(m, LANES, F32, tm, LANES)], tn=LANES, valid_cols=IDX_DIM + IDX_HEADS)
    for nm, grp, dec, scale in (("rq", 6, q_dec, 1.0), ("rk", 7, k_dec, RET_DK ** -0.5)):
        res = mm("proj_" + nm, grp, grp + 1, functools.partial(_ep_ret_qk, scale=scale),
                 [tab(r_cos), tab(r_sin), tab(dec, tn // 2, by_col=True)],
                 [_nat(m, RET_QK_W, wdt, tm, tn), _nat(m, RET_QK_W, wdt, tm, tn)])
        out[nm], out[nm + "d"] = res
    out["rv"], = mm("proj_rv", 8, 9, _ep_plain, [], [_nat(m, RET_V_W, wdt, tm, tn)])
    out["gates"], = mm("proj_gates", 9, 12, _ep_plain, [], [_nat(m, 3 * D_MODEL, F32, tm, tn)])
    return out


def _sortable_key(score):
    kb = lax.bitcast_convert_type(score, I32)
    return jnp.where(kb < 0, kb ^ 0x7FFFFFFF, kb)


def _tile_reduce(x, op):
    tiles = [x[i * SUBLANES:(i + 1) * SUBLANES] for i in range(x.shape[0] // SUBLANES)]
    while len(tiles) > 1:
        nxt = [op(tiles[i], tiles[i + 1]) for i in range(0, len(tiles) - 1, 2)]
        if len(tiles) % 2:
            nxt.append(tiles[-1])
        tiles = nxt
    return tiles[0]


def _topk_mask(sc_ref, key_ref, bias_ref, q_pos, n_chunks):
    def rows(c):
        r0 = pl.multiple_of(c * KEY_CHUNK, KEY_CHUNK)
        return pl.ds(r0, KEY_CHUNK), r0 + lax.broadcasted_iota(I32, (KEY_CHUNK, LANES), 0)

    def build(c, carry):
        sl, row = rows(c)
        key_ref[sl, :] = jnp.where(row <= q_pos, _sortable_key(sc_ref[sl, :]), INT_MIN)
        return carry

    lax.fori_loop(0, n_chunks, build, 0)

    def count(pred):
        def body(c, acc):
            sl, row = rows(c)
            return acc + _tile_reduce(jnp.where(pred(key_ref[sl, :], row), 1.0, 0.0), jnp.add)

        part = lax.fori_loop(0, n_chunks, body, jnp.zeros((SUBLANES, LANES), F32))
        return jnp.sum(part, axis=0, keepdims=True)

    t0 = jnp.where(count(lambda k, row: k >= 0) >= TOPK, 0, INT_MIN).astype(I32)

    def search(i, t):
        cand = t | lax.shift_left(jnp.int32(1), jnp.int32(30) - i)
        return jnp.where(count(lambda k, row: k >= cand) >= TOPK, cand, t)

    thr = lax.fori_loop(0, 31, search, t0)
    n_eq = count(lambda k, row: (row <= q_pos) & (k == thr))
    need = TOPK - count(lambda k, row: k > thr)

    def write(c, carry):
        sl, row = rows(c)
        k = key_ref[sl, :]
        sel = jnp.where(k > thr, 0.0, jnp.where((row <= q_pos) & (k == thr), 0.0, NEG_BIG))
        bias_ref[sl, :] = sel.astype(bias_ref.dtype)
        return carry

    lax.fori_loop(0, n_chunks, write, 0)

    @pl.when(jnp.max(n_eq - need) > 0)
    def _():
        r_i = lax.broadcasted_iota(I32, (LANES, LANES), 0)
        c_i = lax.broadcasted_iota(I32, (LANES, LANES), 1)
        tri = jnp.where(c_i < r_i, 1.0, 0.0).astype(BF16)

        def chunk(c, off):
            r0 = pl.multiple_of(c * LANES, LANES)
            kc = key_ref[pl.ds(r0, LANES), :]
            rc = r0 + lax.broadcasted_iota(I32, (LANES, LANES), 0)
            e = (rc <= q_pos) & (kc == thr)
            ef = jnp.where(e, 1.0, 0.0)
            before = jnp.dot(tri, ef.astype(BF16), preferred_element_type=F32) + off
            keep = e & (before < need)
            sel = jnp.where(kc > thr, 0.0, jnp.where(keep, 0.0, NEG_BIG))
            bias_ref[pl.ds(r0, LANES), :] = sel.astype(bias_ref.dtype)
            return off + jnp.sum(ef, axis=0, keepdims=True)

        lax.fori_loop(0, n_chunks * (KEY_CHUNK // LANES), chunk, jnp.zeros((1, LANES), F32))


def _split_bf16(x):
    hi = x.astype(BF16).astype(F32)
    return hi, x - hi


def _dsa_prompt_kernel(iq_ref, ikw_all_ref, ikw_q_ref, q_ref, k_ref, vt_ref, *rest, n_cast):
    w32_refs, att_ref, wbf_refs = rest[:n_cast], rest[n_cast], rest[n_cast + 1:2 * n_cast + 1]
    lhs_ref, wt_ref, sc_ref, key_ref, bias_ref = rest[2 * n_cast + 1:]
    for src, dst in zip(w32_refs, wbf_refs):
        dst[...] = src[...].astype(BF16)
    qb = pl.program_id(1)
    n_ck = (qb * Q_BLOCK + Q_BLOCK + KEY_CHUNK - 1) // KEY_CHUNK
    lane = lax.broadcasted_iota(I32, (1, LANES), 1)
    low = lane < IDX_DIM

    def chunk_rows(c):
        return pl.ds(pl.multiple_of(c * KEY_CHUNK, KEY_CHUNK), KEY_CHUNK)

    @pl.when(qb == 0)
    def _():
        x = jnp.where(low, ikw_all_ref[...], 0.0)
        hi, lo = _split_bf16(x)
        lhs_ref[:, :LANES] = (hi + pltpu.roll(lo, IDX_DIM, 1)).astype(BF16)
        lhs_ref[:, LANES:] = hi.astype(BF16)

    wt_ref[...] = ikw_q_ref[...].T * IDX_SCALE
    sc_ref[...] = jnp.zeros_like(sc_ref)

    def pairs_body(pq, carry):
        rhs, wts = [], []
        for u in range(PAIRS_PER_STEP):
            p = pq * PAIRS_PER_STEP + u
            hi, lo = _split_bf16(iq_ref[p])
            rhi = pltpu.roll(hi, IDX_DIM, 1)
            rlo = pltpu.roll(lo, IDX_DIM, 1)
            ra = jnp.concatenate([jnp.where(low, hi, rhi), jnp.where(low, lo, 0.0)], axis=1)
            rb = jnp.concatenate([jnp.where(low, rhi, hi), jnp.where(low, rlo, 0.0)], axis=1)
            rhs.append(jnp.concatenate([ra, rb], axis=0).astype(BF16))
            wts.append((wt_ref[pl.ds(IDX_DIM + 2 * p, 1), :], wt_ref[pl.ds(IDX_DIM + 2 * p + 1, 1), :]))

        def ck_body(c, inner):
            sl = chunk_rows(c)
            lhs = lhs_ref[sl, :]
            acc = sc_ref[sl, :]
            for rhs_t, (wa, wb) in zip(rhs, wts):
                d = lax.dot_general(lhs, rhs_t, _NT, preferred_element_type=F32)
                acc = acc + wa * jnp.maximum(d[:, :LANES], 0.0) + wb * jnp.maximum(d[:, LANES:], 0.0)
            sc_ref[sl, :] = acc
            return inner

        lax.fori_loop(0, n_ck, ck_body, 0)
        return carry

    lax.fori_loop(0, IDX_HEADS // 2 // PAIRS_PER_STEP, pairs_body, 0)

    _topk_mask(sc_ref, key_ref, bias_ref, qb * Q_BLOCK + lane, n_ck)

    c_exp = ATT_SCALE * math.log2(math.e)

    r_i = lax.broadcasted_iota(I32, (2 * Q_BLOCK, LANES), 0)
    c_i = lax.broadcasted_iota(I32, (2 * Q_BLOCK, LANES), 1)
    onehot_q = jnp.where((r_i % Q_BLOCK) == c_i, 1.0, 0.0).astype(BF16)
    n_acc = vt_ref.shape[2]

    def groups_body(gq, carry):
        gs = [gq * GROUPS_PER_STEP + u for u in range(GROUPS_PER_STEP)]
        qqs = [jnp.concatenate([jnp.concatenate([q_ref[2 * g], q_ref[2 * g + 1]], axis=0), onehot_q], axis=1)
               for g in gs]

        def ck_body(c, state):
            sl = chunk_rows(c)
            mask = bias_ref[sl, :]
            lgs = [lax.dot_general(jnp.concatenate([k_ref[g, sl, :], mask], axis=1), qq, _NT,
                                   preferred_element_type=F32)
                   for g, qq in zip(gs, qqs)]
            soft = []
            for lg, (m_old, acc) in zip(lgs, state):
                m_new = jnp.maximum(m_old, jnp.max(_tile_reduce(lg, jnp.maximum), axis=0, keepdims=True))
                alpha = jnp.exp2((m_old - m_new) * c_exp)
                soft.append((m_new, alpha, jnp.exp2((lg - m_new) * c_exp).astype(BF16)))
            return tuple((m_new, alpha * acc + jnp.dot(vt_ref[g, c], pt, preferred_element_type=F32))
                         for g, (m_new, alpha, pt), (_, acc) in zip(gs, soft, state))

        init = (jnp.full((1, 2 * LANES), NEG_BIG, F32), jnp.zeros((n_acc, 2 * LANES), F32))
        final = lax.fori_loop(0, n_ck, ck_body, (init,) * GROUPS_PER_STEP)
        for g, (_, acc) in zip(gs, final):
            o = acc[:ATT_HEAD_DIM] / acc[ATT_HEAD_DIM:ATT_HEAD_DIM + 1]
            for hh in range(2):
                att_ref[2 * g + hh] = o[:, hh * LANES:(hh + 1) * LANES].T.astype(att_ref.dtype)
        return carry

    lax.fori_loop(0, ATT_KV_HEADS // GROUPS_PER_STEP, groups_body, 0)


def _cast_slab_specs(w32, n_steps, step_of):
    rows, cols = w32.shape
    slab = rows // n_steps
    assert slab * n_steps == rows and slab % (2 * SUBLANES) == 0
    spec = pl.BlockSpec((slab, cols), lambda *idx: (step_of(*idx), 0))
    return spec, jax.ShapeDtypeStruct((rows, cols), BF16)


def _dsa_prompt(iq, ikw, q, k_heads, vt, casts):
    nqb = SEQ // Q_BLOCK
    n_pair = IDX_Q_W // LANES
    cast_specs, cast_shapes = zip(*[_cast_slab_specs(w32, BATCH * nqb, lambda b, i: b * nqb + i) for w32 in casts])
    return pl.pallas_call(
        functools.partial(_dsa_prompt_kernel, n_cast=len(casts)),
        grid=(BATCH, nqb),
        in_specs=[
            pl.BlockSpec((None, n_pair, Q_BLOCK, LANES), lambda b, i: (b * nqb + i, 0, 0, 0)),
            pl.BlockSpec((SEQ, LANES), lambda b, i: (b, 0)),
            pl.BlockSpec((Q_BLOCK, LANES), lambda b, i: (b * nqb + i, 0)),
            pl.BlockSpec((None, ATT_HEADS, Q_BLOCK, LANES), lambda b, i: (b * nqb + i, 0, 0, 0)),
            pl.BlockSpec((None, ATT_KV_HEADS, SEQ, LANES), lambda b, i: (b, 0, 0, 0)),
            pl.BlockSpec((None, ATT_KV_HEADS, SEQ // KEY_CHUNK, ATT_HEAD_DIM + VT_ONES, KEY_CHUNK),
                         lambda b, i: (b, 0, 0, 0, 0)),
            *cast_specs,
        ],
        out_specs=[pl.BlockSpec((None, ATT_HEADS, Q_BLOCK, LANES), lambda b, i: (b * nqb + i, 0, 0, 0)), *cast_specs],
        out_shape=[jax.ShapeDtypeStruct((BATCH * nqb, ATT_HEADS, Q_BLOCK, LANES), BF16), *cast_shapes],
        scratch_shapes=[
            pltpu.VMEM((SEQ, 2 * LANES), BF16),
            pltpu.VMEM((LANES, LANES), F32),
            pltpu.VMEM((SEQ, LANES), F32),
            pltpu.VMEM((SEQ, LANES), I32),
            pltpu.VMEM((SEQ, LANES), BF16),
        ],
        compiler_params=_params(("parallel", "arbitrary")),
        name="dsa_prompt",
    )(iq, ikw, ikw, q, k_heads, vt, *casts)


def _group_norm_gate(o, gn_w, rg):
    of = o * lax.rsqrt(jnp.mean(o * o, axis=-1, keepdims=True) + NORM_EPS) * gn_w
    return of * (rg * _sigmoid(rg))


def _ret_prompt_kernel(q_ref, qd_ref, k_ref, kd_ref, v_ref, rg_ref, decay_ref, cdec_ref, gnw_ref, *rest, n_cast):
    w32_refs, (ret_ref, s_ref), wbf_refs = rest[:n_cast], rest[n_cast:n_cast + 2], rest[n_cast + 2:]
    for src, dst in zip(w32_refs, wbf_refs):
        dst[...] = src[...].astype(BF16)
    c = pl.program_id(1)

    @pl.when(c == 0)
    def _():
        s_ref[...] = jnp.zeros_like(s_ref)

    def qk(h):
        return slice(h * RET_DK, (h + 1) * RET_DK)

    def vv(h):
        return slice(h * RET_DV, (h + 1) * RET_DV)

    heads = range(RET_HEADS)
    scores = [lax.dot_general(q_ref[:, qk(h)], k_ref[:, qk(h)], _NT, preferred_element_type=F32) for h in heads]
    cross = [jnp.dot(qd_ref[:, qk(h)], s_ref[h].astype(BF16), preferred_element_type=F32) for h in heads]
    kvs = [lax.dot_general(kd_ref[:, qk(h)], v_ref[:, vv(h)], (((0,), (0,)), ((), ())),
                           preferred_element_type=F32) for h in heads]
    for h in heads:
        s_ref[h] = cdec_ref[h] * s_ref[h] + kvs[h]
    for h in heads:
        o = jnp.dot((scores[h] * decay_ref[h]).astype(BF16), v_ref[:, vv(h)], preferred_element_type=F32) + cross[h]
        ret_ref[:, vv(h)] = _group_norm_gate(o, gnw_ref[:, vv(h)], rg_ref[:, vv(h)]).astype(ret_ref.dtype)


def _ret_prompt(pr, decay, cdec, gn_w, casts):
    nc = SEQ // RET_CHUNK
    qk = pl.BlockSpec((RET_CHUNK, RET_QK_W), lambda b, c: (b * nc + c, 0))
    vv = pl.BlockSpec((RET_CHUNK, RET_V_W), lambda b, c: (b * nc + c, 0))
    cast_specs, cast_shapes = zip(*[_cast_slab_specs(w32, BATCH * nc, lambda b, c: b * nc + c) for w32 in casts])
    return pl.pallas_call(
        functools.partial(_ret_prompt_kernel, n_cast=len(casts)),
        grid=(BATCH, nc),
        in_specs=[qk, qk, qk, qk, vv, vv,
                  pl.BlockSpec((RET_HEADS, RET_CHUNK, RET_CHUNK), lambda b, c: (0, 0, 0)),
                  pl.BlockSpec((RET_HEADS, 1, RET_DV), lambda b, c: (0, 0, 0)),
                  pl.BlockSpec((1, RET_V_W), lambda b, c: (0, 0)), *cast_specs],
        out_specs=[vv, pl.BlockSpec((None, RET_HEADS, RET_DK, RET_DV), lambda b, c: (b, 0, 0, 0)), *cast_specs],
        out_shape=[jax.ShapeDtypeStruct((BATCH * SEQ, RET_V_W), BF16),
                   jax.ShapeDtypeStruct((BATCH, RET_HEADS, RET_DK, RET_DV), F32), *cast_shapes],
        compiler_params=_params(("parallel", "arbitrary")),
        name="retention_prompt",
    )(pr["rq"], pr["rqd"], pr["rk"], pr["rkd"], pr["rv"], pr["gates"], decay, cdec, gn_w.reshape(1, RET_V_W), *casts)


def _ret_sample_kernel(q_ref, qd_ref, kt_ref, k_ref, v_ref, rg_ref, cdec_ref, gnw_ref, s_ref, ret_ref, so_ref):
    b = pl.program_id(0)
    nb = kt_ref.shape[1]
    onehot = lax.broadcasted_iota(I32, (1, nb), 1) == b
    for h in range(RET_HEADS):
        qk_sl = slice(h * RET_DK, (h + 1) * RET_DK)
        vv = slice(h * RET_DV, (h + 1) * RET_DV)
        s = s_ref[h]
        v = v_ref[pl.ds(b, 1), vv]
        q = q_ref[pl.ds(b, 1), qk_sl]
        k = k_ref[pl.ds(b, 1), qk_sl]
        qk = jnp.sum(q.astype(BF16).astype(F32) * k.astype(BF16).astype(F32), axis=-1, keepdims=True)
        qd8 = jnp.broadcast_to(qd_ref[pl.ds(b, 1), qk_sl], (16, RET_DK)).astype(BF16)
        o = qk.astype(BF16).astype(F32) * v.astype(BF16).astype(F32) \
            + jnp.dot(qd8, s.astype(BF16), preferred_element_type=F32)[0:1]
        k_col = jnp.sum(jnp.where(onehot, kt_ref[qk_sl, :], 0.0), axis=1, keepdims=True)
        so_ref[h] = cdec_ref[h] * s + k_col * v
        ret_ref[:, vv] = _group_norm_gate(o, gnw_ref[:, vv], rg_ref[pl.ds(b, 1), vv])


def _ret_sample(ps, state, cdec, gn_w):
    nb = DEC_BATCH
    qk = pl.BlockSpec((nb, RET_QK_W), lambda b: (0, 0))
    vv = pl.BlockSpec((nb, RET_V_W), lambda b: (0, 0))
    st = pl.BlockSpec((None, RET_HEADS, RET_DK, RET_DV), lambda b: (b, 0, 0, 0))
    return pl.pallas_call(
        _ret_sample_kernel,
        grid=(nb,),
        in_specs=[qk, qk, pl.BlockSpec((RET_QK_W, nb), lambda b: (0, 0)), qk, vv, vv,
                  pl.BlockSpec((RET_HEADS, 1, RET_DV), lambda b: (0, 0, 0)),
                  pl.BlockSpec((1, RET_V_W), lambda b: (0, 0)), st],
        out_specs=[pl.BlockSpec((None, 1, RET_V_W), lambda b: (b, 0, 0)), st],
        out_shape=[jax.ShapeDtypeStruct((nb, 1, RET_V_W), F32),
                   jax.ShapeDtypeStruct((nb, RET_HEADS, RET_DK, RET_DV), F32)],
        compiler_params=_params(("arbitrary",)),
        name="retention_sample",
    )(ps["rq"], ps["rqd"], ps["rkd"].T, ps["rk"], ps["rv"], ps["gates"], cdec, gn_w.reshape(1, RET_V_W), state)


PAGES_PER_STEP = 8
IDX_PAGES_PER_STEP = 32


def _idx_sample_kernel(pt_ref, iq_ref, w_ref, new_ref, *rest):
    pages = rest[:IDX_PAGES_PER_STEP]
    sc_ref, scn_ref = rest[IDX_PAGES_PER_STEP], rest[IDX_PAGES_PER_STEP + 1]
    j = pl.program_id(1)
    hi, lo = _split_bf16(iq_ref[...])
    lhs = jnp.concatenate([hi + pltpu.roll(hi, IDX_DIM, 1), lo], axis=1).astype(BF16)
    w = w_ref[...] * IDX_SCALE

    def page_scores(pages_t):
        xt = jnp.concatenate(pages_t, axis=1) if len(pages_t) > 1 else pages_t[0]
        khi, klo = _split_bf16(xt)
        rhs = jnp.concatenate([khi, klo, khi, jnp.zeros_like(khi)], axis=0).astype(BF16)
        d = jnp.dot(lhs, rhs, preferred_element_type=F32)
        return jnp.sum(w * jnp.maximum(d, 0.0), axis=0, keepdims=True)

    for i in range(0, IDX_PAGES_PER_STEP, 2):
        sc_ref[:, i * PAGE_SIZE:(i + 2) * PAGE_SIZE] = page_scores([pages[i][...], pages[i + 1][...]])

    @pl.when(j == pl.num_programs(1) - 1)
    def _():
        scn_ref[...] = page_scores([new_ref[...]])


def _idx_sample(page_table, iq3, w3, new_pages, cache_idx):
    nsteps = N_PAGES // IDX_PAGES_PER_STEP

    def page_spec(i):
        return pl.BlockSpec((None, IDX_DIM, PAGE_SIZE),
                            lambda b, j, pt: (pt[b, j * IDX_PAGES_PER_STEP + i], 0, 0))

    grid_spec = pltpu.PrefetchScalarGridSpec(
        num_scalar_prefetch=1,
        grid=(DEC_BATCH, nsteps),
        in_specs=[pl.BlockSpec((None, IDX_HEADS, LANES), lambda b, j, pt: (b, 0, 0)),
                  pl.BlockSpec((None, IDX_HEADS, 1), lambda b, j, pt: (b, 0, 0)),
                  pl.BlockSpec((None, IDX_DIM, PAGE_SIZE), lambda b, j, pt: (b, 0, 0))]
                 + [page_spec(i) for i in range(IDX_PAGES_PER_STEP)],
        out_specs=[pl.BlockSpec((None, 1, IDX_PAGES_PER_STEP * PAGE_SIZE), lambda b, j, pt: (b, 0, j)),
                   pl.BlockSpec((None, 1, PAGE_SIZE), lambda b, j, pt: (b, 0, 0))],
    )
    return pl.pallas_call(
        _idx_sample_kernel,
        grid_spec=grid_spec,
        out_shape=[jax.ShapeDtypeStruct((DEC_BATCH, 1, PAST_LEN), F32),
                   jax.ShapeDtypeStruct((DEC_BATCH, 1, PAGE_SIZE), F32)],
        compiler_params=_params(("parallel", "arbitrary")),
        name="indexer_sample",
    )(page_table, iq3, w3, new_pages, *([cache_idx] * IDX_PAGES_PER_STEP))


def _select_sample_kernel(sc_ref, bias_ref, key_ref):
    _topk_mask(sc_ref, key_ref, bias_ref, jnp.full((1, LANES), PAST_LEN, I32), sc_ref.shape[0] // KEY_CHUNK)


def _select_sample(scores_t):
    return pl.pallas_call(
        _select_sample_kernel,
        out_shape=jax.ShapeDtypeStruct(scores_t.shape, F32),
        scratch_shapes=[pltpu.VMEM(scores_t.shape, I32)],
        compiler_params=pltpu.CompilerParams(vmem_limit_bytes=VMEM_LIMIT),
        name="select_sample",
    )(scores_t)


PAGE_ROWS = PAGE_SIZE * ATT_KV_HEADS


def _attn_sample_step(j, n_steps, q_ref, bias_ref, biasn_ref, kn_ref, vn_ref, kp, vp, o_ref, m_ref, l_ref, acc_ref,
                      host_work):
    col = lax.broadcasted_iota(I32, (ATT_HEADS, PAGE_ROWS), 1)
    head = lax.broadcasted_iota(I32, (ATT_HEADS, PAGE_ROWS), 0)
    own = (col % ATT_KV_HEADS) == (head // (ATT_HEADS // ATT_KV_HEADS))
    c_exp = ATT_SCALE * math.log2(math.e)
    q = q_ref[...].astype(BF16)

    @pl.when(j == 0)
    def _():
        m_ref[...] = jnp.full_like(m_ref, NEG_BIG)
        l_ref[...] = jnp.zeros_like(l_ref)
        acc_ref[...] = jnp.zeros_like(acc_ref)

    half = PAGE_ROWS // 2
    zq = jnp.zeros_like(q)
    q2 = jnp.concatenate([jnp.concatenate([q, zq], axis=1), jnp.concatenate([zq, q], axis=1)], axis=0)

    def halves_side_by_side(page_ref):
        x = page_ref[...]
        return jnp.concatenate([x[:half], x[half:]], axis=1).astype(BF16)

    logits = []
    for i in range(PAGES_PER_STEP):
        lg2 = lax.dot_general(q2, halves_side_by_side(kp[i]), _NT, preferred_element_type=F32)
        lg = jnp.concatenate([lg2[:ATT_HEADS], lg2[ATT_HEADS:]], axis=1)
        logits.append(jnp.where(own, lg + bias_ref[:, i * PAGE_ROWS:(i + 1) * PAGE_ROWS], NEG_BIG))
    host_work()
    m_old = m_ref[...]
    m_new = m_old
    for lg in logits:
        m_new = jnp.maximum(m_new, jnp.max(lg, axis=1, keepdims=True))
    alpha = jnp.exp2((m_old - m_new) * c_exp)
    l_new = alpha * l_ref[...]
    acc = alpha * acc_ref[...]
    for i in range(PAGES_PER_STEP):
        p = jnp.exp2((logits[i] - m_new) * c_exp)
        l_new = l_new + jnp.sum(p, axis=1, keepdims=True)
        p2 = jnp.concatenate([p[:, :half], p[:, half:]], axis=0).astype(BF16)
        o2 = jnp.dot(p2, halves_side_by_side(vp[i]), preferred_element_type=F32)
        acc = acc + o2[:ATT_HEADS, :ATT_HEAD_DIM] + o2[ATT_HEADS:, ATT_HEAD_DIM:]
    m_ref[...] = m_new
    l_ref[...] = l_new
    acc_ref[...] = acc

    @pl.when(j == n_steps - 1)
    def _():
        kn = kn_ref[...].astype(BF16).astype(F32)
        vn = vn_ref[...].astype(BF16).astype(F32)
        lgn = jnp.sum(q.astype(F32) * kn, axis=1, keepdims=True) + biasn_ref[:, 0:1]
        m_f = jnp.maximum(m_new, lgn)
        a = jnp.exp2((m_new - m_f) * c_exp)
        pn = jnp.exp2((lgn - m_f) * c_exp)
        o_ref[...] = (a * acc + pn.astype(BF16).astype(F32) * vn) / (a * l_new + pn)


MLP_UP_TM, MLP_UP_TN = 1024, 512


def _mlp_up_attn_kernel(pt_ref, hn_ref, wup_ref, q_ref, bias_ref, biasn_ref, kn_ref, vn_ref, *rest):
    kp = rest[:PAGES_PER_STEP]
    vp = rest[PAGES_PER_STEP:2 * PAGES_PER_STEP]
    u_ref, o_ref = rest[2 * PAGES_PER_STEP], rest[2 * PAGES_PER_STEP + 1]
    m_ref, l_ref, acc_ref = rest[2 * PAGES_PER_STEP + 2:]
    n_pg = N_PAGES // PAGES_PER_STEP
    _attn_sample_step(
        pl.program_id(1) % n_pg, n_pg, q_ref, bias_ref, biasn_ref, kn_ref, vn_ref, kp, vp, o_ref, m_ref, l_ref, acc_ref,
        host_work=lambda: _ep_relu2(jnp.dot(hn_ref[...], wup_ref[...], preferred_element_type=F32), (), (u_ref,)))


def _mlp_up_with_sample_attn(hn, w_up, page_table, q3, bias_rows, bias_new, k_new, v_new, cache_k, cache_v):
    m, kd = hn.shape
    n_pg = N_PAGES // PAGES_PER_STEP
    n_row, n_col = m // MLP_UP_TM, D_FF // MLP_UP_TN
    assert n_row * n_col == DEC_BATCH * n_pg and n_col % n_pg == 0

    def row(i, j):
        return (i * n_col + j) // n_pg

    def page_spec(p):
        return pl.BlockSpec((None, PAGE_ROWS, ATT_HEAD_DIM),
                            lambda i, j, pt: (pt[row(i, j), (j % n_pg) * PAGES_PER_STEP + p], 0, 0))

    head_rows = pl.BlockSpec((None, ATT_HEADS, ATT_HEAD_DIM), lambda i, j, pt: (row(i, j), 0, 0))
    grid_spec = pltpu.PrefetchScalarGridSpec(
        num_scalar_prefetch=1,
        grid=(n_row, n_col),
        in_specs=[pl.BlockSpec((MLP_UP_TM, kd), lambda i, j, pt: (i, 0)),
                  pl.BlockSpec((kd, MLP_UP_TN), lambda i, j, pt: (0, j)),
                  head_rows,
                  pl.BlockSpec((None, 1, PAGES_PER_STEP * PAGE_ROWS), lambda i, j, pt: (row(i, j), 0, j % n_pg)),
                  pl.BlockSpec((None, 1, PAGE_SIZE), lambda i, j, pt: (row(i, j), 0, 0)),
                  head_rows, head_rows]
                 + [page_spec(p) for p in range(PAGES_PER_STEP)] * 2,
        out_specs=[pl.BlockSpec((MLP_UP_TM, MLP_UP_TN), lambda i, j, pt: (i, j)), head_rows],
        scratch_shapes=[pltpu.VMEM((ATT_HEADS, 1), F32), pltpu.VMEM((ATT_HEADS, 1), F32),
                        pltpu.VMEM((ATT_HEADS, ATT_HEAD_DIM), F32)],
    )
    return pl.pallas_call(
        _mlp_up_attn_kernel,
        grid_spec=grid_spec,
        out_shape=[jax.ShapeDtypeStruct((m, D_FF), BF16),
                   jax.ShapeDtypeStruct((DEC_BATCH, ATT_HEADS, ATT_HEAD_DIM), F32)],
        compiler_params=_params(("arbitrary", "arbitrary")),
        name="mlp_up_attention_sample",
    )(page_table, hn, w_up, q3, bias_rows, bias_new, k_new, v_new,
      *([cache_k] * PAGES_PER_STEP), *([cache_v] * PAGES_PER_STEP))


def _finish(x, att, ret, gates, w, *, tm, tr, sample_attn=None):
    m = x.shape[0]
    mg = _merge(att, ret, w["att_proj"], w["ret_proj"], gates, tm, 512)
    y_out = _nat(m, D_MODEL, F32, tm, 512)
    if w["mlp_up"].dtype == F32:
        n_col = D_MODEL // 512
        slab = w["mlp_up"].shape[0] // (m // tm * n_col)
        slab_map = lambda i, j: (i * n_col + j, 0)

        def ep_w_out(acc, extra, outs):
            outs[0][...] = acc
            outs[1][...] = extra[0][...].astype(BF16)

        y, w["mlp_up"] = _matmul(mg, w["out"], ep_w_out, [(w["mlp_up"], (slab, D_FF), slab_map)],
                                 [y_out, (w["mlp_up"].shape, BF16, (slab, D_FF), slab_map)],
                                 tm=tm, tn=512, name="w_out")
    else:
        y, = _matmul(mg, w["out"], _ep_plain, [], [y_out], tm=tm, tn=512, name="w_out")
    h, hn = _post_attn(x, y, w["n_attn_post"], w["n_mlp_pre"], tr)
    if sample_attn is None:
        u, = _matmul(hn, w["mlp_up"], _ep_relu2, [], [_nat(m, D_FF, BF16, tm, 512)], tm=tm, tn=512, name="mlp_up")
        att_s = None
    else:
        u, att_s = _mlp_up_with_sample_attn(hn, w["mlp_up"], *sample_attn)
    d, = _matmul(u, w["mlp_down"], _ep_plain, [], [_nat(m, D_MODEL, F32, tm, 1024)], tm=tm, tn=1024, tk=4096,
                 name="mlp_down")
    out = _post_mlp(h, d, w["n_mlp_post"], tr)
    return out if sample_attn is None else (out, att_s)


def kernel(x_prompt, x_sample, cache_k, cache_v, cache_idx_k, state_ret, page_table, norm_attn_pre,
           norm_attn_post, w_in, ret_gn_w, w_att_proj, w_ret_proj, w_out, norm_mlp_pre, w_mlp_up,
           w_mlp_down, norm_mlp_post):
    log_gamma = jnp.log1p(-jnp.exp2(-5.0 - jnp.arange(RET_HEADS, dtype=F32)))
    w_in0 = w_in[0].T
    w = {"n_attn_post": norm_attn_post[0], "n_mlp_pre": norm_mlp_pre[0], "n_mlp_post": norm_mlp_post[0]}
    gn_w = ret_gn_w[0]

    m_p = BATCH * SEQ
    xp = x_prompt.reshape(m_p, D_MODEL)
    xn = _rmsnorm_cast(xp, norm_attn_pre[0], 256)
    pos_p = jnp.arange(SEQ, dtype=I32)
    pr = _project(xn, w_in0, pos_p, pos_p % RET_CHUNK, float(RET_CHUNK), log_gamma, tm=1024, prompt=True)
    vt = pr["v_bf"].reshape(BATCH, SEQ // KEY_CHUNK, KEY_CHUNK, ATT_KV_HEADS, ATT_HEAD_DIM).transpose(0, 3, 1, 4, 2)
    vt = jnp.concatenate([vt, jnp.ones(vt.shape[:3] + (VT_ONES, KEY_CHUNK), BF16)], axis=3)
    att4, w["mlp_down"] = _dsa_prompt(pr["iq"], pr["ikw"], pr["q"], pr["k_heads"], vt, (w_mlp_down[0],))
    att = att4.reshape(m_p // Q_BLOCK, ATT_HEADS, Q_BLOCK, ATT_HEAD_DIM).transpose(0, 2, 1, 3).reshape(m_p, ATT_Q_W)
    ci = jnp.arange(RET_CHUNK, dtype=F32)
    diff = ci[:, None] - ci[None, :]
    decay = jnp.where(diff >= 0, jnp.exp(log_gamma[:, None, None] * jnp.maximum(diff, 0.0)), 0.0)
    cdec_p = jnp.broadcast_to(jnp.exp(log_gamma * RET_CHUNK)[:, None, None], (RET_HEADS, 1, RET_DV))
    ret, s_prompt, w["out"], w["ret_proj"], w["att_proj"] = _ret_prompt(
        pr, decay, cdec_p, gn_w, (w_out[0], w_ret_proj[0], w_att_proj[0]))
    w["mlp_up"] = w_mlp_up[0]

    nb = DEC_BATCH
    xs = x_sample.reshape(nb, D_MODEL)
    xns = _rmsnorm_cast(xs, norm_attn_pre[0], nb)
    pos_s = jnp.full((nb,), PAST_LEN, I32)
    ps = _project(xns, pr["w_bf16"], pos_s, jnp.zeros((nb,), I32), 1.0, log_gamma, tm=nb, prompt=False)
    ik_new = ps["ikw"][:, :IDX_DIM]
    iq3 = jnp.pad(ps["iq"].reshape(nb, IDX_HEADS, IDX_DIM), ((0, 0), (0, 0), (0, LANES - IDX_DIM)))
    w3 = ps["ikw"][:, IDX_DIM:IDX_DIM + IDX_HEADS].reshape(nb, IDX_HEADS, 1)
    new_pages = jnp.pad(ik_new[:, :, None], ((0, 0), (0, 0), (0, PAGE_SIZE - 1)))
    sc_past, sc_new = _idx_sample(page_table, iq3, w3, new_pages, cache_idx_k[0].transpose(0, 2, 1))
    scores = jnp.concatenate([sc_past.reshape(nb, PAST_LEN), sc_new.reshape(nb, PAGE_SIZE)], axis=1)
    n_rows = -(-(PAST_LEN + PAGE_SIZE) // KEY_CHUNK) * KEY_CHUNK
    scores_t = jnp.pad(scores.T, ((0, n_rows - PAST_LEN - PAGE_SIZE), (0, LANES - nb)))
    bias = _select_sample(scores_t)[:PAST_LEN + PAGE_SIZE, :nb].T
    n_phys = cache_k.shape[1]
    group = ATT_HEADS // ATT_KV_HEADS
    sample_attn = (page_table, ps["q"].reshape(nb, ATT_HEADS, ATT_HEAD_DIM),
                   jnp.repeat(bias[:, :PAST_LEN], ATT_KV_HEADS, axis=1).reshape(nb, 1, N_PAGES * PAGE_ROWS),
                   bias[:, PAST_LEN:].reshape(nb, 1, PAGE_SIZE),
                   jnp.repeat(ps["k"].reshape(nb, ATT_KV_HEADS, ATT_HEAD_DIM), group, axis=1),
                   jnp.repeat(ps["v"].reshape(nb, ATT_KV_HEADS, ATT_HEAD_DIM), group, axis=1),
                   cache_k[0].reshape(n_phys, PAGE_ROWS, ATT_HEAD_DIM),
                   cache_v[0].reshape(n_phys, PAGE_ROWS, ATT_HEAD_DIM))

    y_prompt, att_s = _finish(xp, att, ret, pr["gates"], w, tm=1024, tr=256, sample_attn=sample_attn)
    y_prompt = y_prompt.reshape(BATCH, SEQ, D_MODEL)
    cdec_s = jnp.broadcast_to(jnp.exp(log_gamma)[:, None, None], (RET_HEADS, 1, RET_DV))
    ret_s, s_sample = _ret_sample(ps, state_ret[0], cdec_s, gn_w)
    y_sample = _finish(xs, att_s.reshape(nb, ATT_Q_W).astype(BF16), ret_s.reshape(nb, RET_V_W).astype(BF16),
                       ps["gates"], w, tm=nb, tr=nb).reshape(nb, 1, D_MODEL)

    return (y_prompt, y_sample,
            pr["k"].reshape(1, BATCH, SEQ, ATT_KV_HEADS, ATT_HEAD_DIM),
            pr["v"].reshape(1, BATCH, SEQ, ATT_KV_HEADS, ATT_HEAD_DIM),
            pr["ikw"][:, :IDX_DIM].reshape(1, BATCH, SEQ, IDX_DIM),
            s_prompt[None],
            ps["k"].reshape(1, nb, 1, ATT_KV_HEADS, ATT_HEAD_DIM),
            ps["v"].reshape(1, nb, 1, ATT_KV_HEADS, ATT_HEAD_DIM),
            ik_new.reshape(1, nb, 1, IDX_DIM),
            s_sample[None])
```
